```python
import jax, jax.numpy as jnp
from jax import lax
import numpy as np

D_MODEL = 1024
BATCH = 4
SEQ = 4096
DEPTH = 2

GRID_W = 64
CTX_LEN = 256

HEAD_DIM = 64
ROPE_BASE = 10000.0
NORM_EPS = 1e-6
GN_EPS = 1e-5
NEG_INF = -1e30

NA_HEADS = D_MODEL // HEAD_DIM // 2
NA_ROWS = 8
NA_COLS = 16
RET_HEADS = D_MODEL // HEAD_DIM // 2
RET_CHUNK = 128
W_A = NA_HEADS * HEAD_DIM
W_B = RET_HEADS * HEAD_DIM
AB_SPLITS = (W_A, 2 * W_A, 3 * W_A, 3 * W_A + W_B, 3 * W_A + 2 * W_B, 3 * W_A + 3 * W_B)
AB_IN = 3 * W_A + 4 * W_B
AB_OUT = W_A + W_B

SWA_Q_HEADS = D_MODEL // HEAD_DIM
SWA_KV_HEADS = 4
SWA_WINDOW = 128
SWA_Q_W = SWA_Q_HEADS * HEAD_DIM
SWA_KV_W = SWA_KV_HEADS * HEAD_DIM
SWA_IN = SWA_Q_W + 2 * SWA_KV_W
SWA_OUT = SWA_Q_W

PEER_HEADS = 8
PEER_KEYS = 128
PEER_EXPERTS = PEER_KEYS * PEER_KEYS
PEER_TOPK = 16
PEER_QDIM = 256
PEER_BLOCK = 128

kernel_name = 'hybrid_na_retention_swa_peer'


def rms_norm(x, g):
    xf = x.astype(jnp.float32)
    y = xf * lax.rsqrt(jnp.mean(xf * xf, axis=-1, keepdims=True) + NORM_EPS)
    return (y * g.astype(jnp.float32)).astype(x.dtype)


def modulate(x, gain, shift, scale):
    return rms_norm(x, gain) * (1.0 + scale) + shift


def adaln(cond, w, b):
    return jnp.split(jax.nn.silu(cond) @ w + b, 6, axis=-1)


def axial_rope(n_tokens, head_dim):
    t = jnp.arange(n_tokens)
    row = (t // GRID_W).astype(jnp.float32)
    col = (t % GRID_W).astype(jnp.float32)
    n_freq = head_dim // 4
    inv_freq = jnp.power(ROPE_BASE, -jnp.arange(n_freq, dtype=jnp.float32) / n_freq)
    ang = jnp.concatenate([row[:, None] * inv_freq, col[:, None] * inv_freq], axis=-1)
    return jnp.cos(ang), jnp.sin(ang)


def apply_rope(x, cos, sin):
    half = x.shape[-1] // 2
    x1, x2 = x[..., :half], x[..., half:]
    c = cos[None, :, None, :].astype(x.dtype)
    s = sin[None, :, None, :].astype(x.dtype)
    return jnp.concatenate([x1 * c - x2 * s, x1 * s + x2 * c], axis=-1)


def context_attention(q, k, v, sink=None):
    B, L, Hq, d = q.shape
    Hkv = k.shape[2]
    G = Hq // Hkv
    s = jnp.einsum('bqkgd,blkd->bkgql', q.reshape(B, L, Hkv, G, d), k).astype(jnp.float32)
    if sink is not None:
        sink_col = jnp.broadcast_to(sink.reshape(Hkv, G, 1, 1).astype(jnp.float32), (B, Hkv, G, L, 1))
        s = jnp.concatenate([s, sink_col], axis=-1)
    p = jax.nn.softmax(s, axis=-1)[..., :L].astype(v.dtype)
    return jnp.einsum('bkgql,blkd->bqkgd', p, v).reshape(B, L, Hq, d)


def neighbourhood_attention(q, k, v, k_ctx, v_ctx, rpb):
    B, S, H, d = q.shape
    rows = S // GRID_W
    kr = min(NA_ROWS, rows)
    n_keys = kr * NA_COLS
    r = jnp.arange(rows)
    col = jnp.arange(GRID_W)
    key_rows = jnp.clip(r - kr // 2, 0, rows - kr)[:, None] + jnp.arange(kr)
    key_cols = jnp.clip(col - NA_COLS // 2, 0, GRID_W - NA_COLS)[:, None] + jnp.arange(NA_COLS)
    key_tok = (key_rows[:, None, :, None] * GRID_W + key_cols[None, :, None, :]).reshape(rows, GRID_W, n_keys)
    dr = key_rows - r[:, None] + (NA_ROWS - 1)
    dc = key_cols - col[:, None] + (NA_COLS - 1)
    bias = rpb[:, dr[:, None, :, None], dc[None, :, None, :]]
    bias = bias.reshape(H, rows, GRID_W, n_keys).transpose(1, 0, 2, 3)
    q_rows = q.reshape(B, rows, GRID_W, H, d).transpose(1, 0, 2, 3, 4)

    def one_row(args):
        q_r, tok_r, bias_r = args
        k_w = k[:, tok_r]
        v_w = v[:, tok_r]
        s_loc = jnp.einsum('bwhd,bwkhd->bhwk', q_r, k_w).astype(jnp.float32) + bias_r.astype(jnp.float32)
        s_ctx = jnp.einsum('bwhd,blhd->bhwl', q_r, k_ctx).astype(jnp.float32)
        p = jax.nn.softmax(jnp.concatenate([s_loc, s_ctx], axis=-1), axis=-1).astype(v.dtype)
        return (jnp.einsum('bhwk,bwkhd->bwhd', p[..., :n_keys], v_w)
                + jnp.einsum('bhwl,blhd->bwhd', p[..., n_keys:], v_ctx))

    out = lax.map(one_row, (q_rows, key_tok, bias))
    return out.transpose(1, 0, 2, 3, 4).reshape(B, S, H, d)


def retention_scan(q, k, v, log_gamma, state0):
    B, T, H, dk = q.shape
    dv = v.shape[-1]
    C = RET_CHUNK
    n = T // C
    to_chunks = lambda t: t.reshape(B, n, C, H, t.shape[-1]).transpose(1, 0, 3, 2, 4)
    i = jnp.arange(C, dtype=jnp.float32)
    diff = i[:, None] - i[None, :]
    lg = log_gamma[:, None, None]
    decay_intra = jnp.where(diff >= 0, jnp.exp(lg * jnp.maximum(diff, 0.0)), 0.0)
    decay_q = jnp.exp(log_gamma[:, None] * (i + 1.0))[None, :, :, None]
    decay_k = jnp.exp(log_gamma[:, None] * (C - 1.0 - i))[None, :, :, None]
    decay_chunk = jnp.exp(log_gamma * C)[None, :, None, None]

    def step(S, xs):
        qb, kb, vb = xs
        intra = jnp.einsum('bhid,bhjd->bhij', qb, kb) * decay_intra
        o = jnp.einsum('bhij,bhjv->bhiv', intra, vb) + jnp.einsum('bhid,bhdv->bhiv', qb * decay_q, S)
        S = S * decay_chunk + jnp.einsum('bhjd,bhjv->bhdv', kb * decay_k, vb)
        return S, o

    S_T, o = lax.scan(step, state0, (to_chunks(q), to_chunks(k), to_chunks(v)))
    return o.transpose(1, 0, 3, 2, 4).reshape(B, T, H, dv), S_T


def bidirectional_retention(q, k, v, log_gamma, state_fwd0, state_bwd0):
    o_f, s_f = retention_scan(q, k, v, log_gamma[0], state_fwd0)
    o_b, s_b = retention_scan(q[:, ::-1], k[:, ::-1], v[:, ::-1], log_gamma[1], state_bwd0)
    diag = jnp.einsum('bthd,bthd->bth', q, k).astype(jnp.float32)[..., None] * v.astype(jnp.float32)
    return o_f + o_b[:, ::-1] - diag, s_f, s_b


def retention_output(o, g):
    mu = jnp.mean(o, axis=-1, keepdims=True)
    oc = o - mu
    on = oc * lax.rsqrt(jnp.mean(oc * oc, axis=-1, keepdims=True) + GN_EPS)
    B, T = o.shape[0], o.shape[1]
    return (on.reshape(B, T, -1) * jax.nn.silu(g.astype(jnp.float32))).astype(g.dtype)


def mixer_na_retention(hx, hc, w_in, w_out, q_norm, k_norm, rpb, ret_log_decay, cos, sin, need_ctx):
    B, S, _ = hx.shape
    L = hc.shape[1]
    scale = HEAD_DIM ** -0.5

    def split(p, T):
        aq, ak, av, bq, bk, bv, bg = jnp.split(p, AB_SPLITS, axis=-1)
        heads = lambda t: t.reshape(B, T, -1, HEAD_DIM)
        return heads(aq), heads(ak), heads(av), heads(bq), heads(bk), heads(bv), bg

    aqx, akx, avx, bqx, bkx, bvx, bgx = split(hx @ w_in, S)
    aqc, akc, avc, bqc, bkc, bvc, bgc = split(hc @ w_in, L)

    aqx = rms_norm(aqx, q_norm) * scale
    akx = rms_norm(akx, k_norm)
    aqc = rms_norm(aqc, q_norm) * scale
    akc = rms_norm(akc, k_norm)
    oa_x = neighbourhood_attention(aqx, akx, avx, akc, avc, rpb)

    log_gamma = -jnp.exp(ret_log_decay.astype(jnp.float32))
    zeros = jnp.zeros((B, RET_HEADS, HEAD_DIM, HEAD_DIM), jnp.float32)
    ob_c, st_f, st_b = bidirectional_retention(bqc * scale, bkc, bvc, log_gamma, zeros, zeros)
    ob_x, _, _ = bidirectional_retention(apply_rope(bqx, cos, sin) * scale, apply_rope(bkx, cos, sin),
                                         bvx, log_gamma, st_f, st_b)

    yx = jnp.concatenate([oa_x.reshape(B, S, -1), retention_output(ob_x, bgx)], axis=-1) @ w_out
    if not need_ctx:
        return yx, None
    oa_c = context_attention(aqc, akc, avc)
    yc = jnp.concatenate([oa_c.reshape(B, L, -1), retention_output(ob_c, bgc)], axis=-1) @ w_out
    return yx, yc


def windowed_gqa(q, k, v, k_ctx, v_ctx, sink):
    B, S, Hq, d = q.shape
    Hkv = k.shape[2]
    G = Hq // Hkv
    L = k_ctx.shape[1]
    blk = SWA_WINDOW
    n = S // blk
    n_loc = 3 * blk

    def key_blocks(t):
        tp = jnp.pad(t, ((0, 0), (blk, blk), (0, 0), (0, 0))).reshape(B, n + 2, blk, Hkv, d)
        tb = jnp.concatenate([tp[:, :-2], tp[:, 1:-1], tp[:, 2:]], axis=2)
        return tb.transpose(1, 0, 2, 3, 4)

    kb, vb = key_blocks(k), key_blocks(v)
    qb = q.reshape(B, n, blk, Hkv, G, d).transpose(1, 0, 2, 3, 4, 5)
    i = jnp.arange(n)
    q_pos = i[:, None] * blk + jnp.arange(blk)[None, :]
    key_pos = i[:, None] * blk - blk + jnp.arange(n_loc)[None, :]
    mask = ((jnp.abs(q_pos[:, :, None] - key_pos[:, None, :]) <= SWA_WINDOW)
            & (key_pos[:, None, :] >= 0) & (key_pos[:, None, :] < S))
    sink_col = jnp.broadcast_to(sink.reshape(Hkv, G, 1, 1).astype(jnp.float32), (B, Hkv, G, blk, 1))

    def one_block(args):
        q_i, k_i, v_i, m_i = args
        s_loc = jnp.where(m_i, jnp.einsum('bqkgd,bskd->bkgqs', q_i, k_i).astype(jnp.float32), NEG_INF)
        s_ctx = jnp.einsum('bqkgd,blkd->bkgql', q_i, k_ctx).astype(jnp.float32)
        p = jax.nn.softmax(jnp.concatenate([s_loc, s_ctx, sink_col], axis=-1), axis=-1).astype(v.dtype)
        return (jnp.einsum('bkgqs,bskd->bqkgd', p[..., :n_loc], v_i)
                + jnp.einsum('bkgql,blkd->bqkgd', p[..., n_loc:n_loc + L], v_ctx))

    out = lax.map(one_block, (qb, kb, vb, mask))
    return out.transpose(1, 0, 2, 3, 4, 5).reshape(B, S, Hq, d)


def mixer_window_gqa(hx, hc, w_in, w_out, q_norm, k_norm, sink, cos, sin, need_ctx):
    B, S, _ = hx.shape
    L = hc.shape[1]
    scale = HEAD_DIM ** -0.5

    def split(p, T):
        q, k, v = jnp.split(p, (SWA_Q_W, SWA_Q_W + SWA_KV_W), axis=-1)
        return (q.reshape(B, T, SWA_Q_HEADS, HEAD_DIM), k.reshape(B, T, SWA_KV_HEADS, HEAD_DIM),
                v.reshape(B, T, SWA_KV_HEADS, HEAD_DIM))

    qx, kx, vx = split(hx @ w_in, S)
    qc, kc, vc = split(hc @ w_in, L)
    qx = apply_rope(rms_norm(qx, q_norm), cos, sin) * scale
    kx = apply_rope(rms_norm(kx, k_norm), cos, sin)
    qc = rms_norm(qc, q_norm) * scale
    kc = rms_norm(kc, k_norm)
    yx = windowed_gqa(qx, kx, vx, kc, vc, sink).reshape(B, S, -1) @ w_out
    if not need_ctx:
        return yx, None
    yc = context_attention(qc, kc, vc, sink).reshape(B, L, -1) @ w_out
    return yx, yc


def peer(h, w_q, sub_keys, u, v):
    B, T, D = h.shape
    q = (h @ w_q).reshape(B, T, PEER_HEADS, 2, PEER_QDIM // 2)
    s = jnp.einsum('bthpd,hpnd->bthpn', q, sub_keys).astype(jnp.float32)
    s1, i1 = lax.top_k(s[..., 0, :], PEER_TOPK)
    s2, i2 = lax.top_k(s[..., 1, :], PEER_TOPK)
    cand_s = (s1[..., :, None] + s2[..., None, :]).reshape(B, T, PEER_HEADS, PEER_TOPK * PEER_TOPK)
    cand_i = (i1[..., :, None] * PEER_KEYS + i2[..., None, :]).reshape(B, T, PEER_HEADS, PEER_TOPK * PEER_TOPK)
    top_s, pos = lax.top_k(cand_s, PEER_TOPK)
    idx = jnp.take_along_axis(cand_i, pos, axis=-1)
    gate = jax.nn.softmax(top_s, axis=-1)
    n_e = PEER_HEADS * PEER_TOPK
    nb = (B * T) // PEER_BLOCK
    hb = h.reshape(nb, PEER_BLOCK, D)
    ib = idx.reshape(nb, PEER_BLOCK, n_e)
    gb = gate.reshape(nb, PEER_BLOCK, n_e)

    def one_block(args):
        h_i, i_i, g_i = args
        act = jax.nn.gelu(jnp.einsum('td,ted->te', h_i, u[i_i]), approximate=False)
        return jnp.einsum('te,ted->td', (g_i * act).astype(h.dtype), v[i_i])

    return lax.map(one_block, (hb, ib, gb)).reshape(B, T, D)


def setup_inputs(seed: int = 0) -> dict:
    key = jax.random.key(seed)
    ks = jax.random.split(key, 24)
    f32 = jnp.float32
    n_even = (DEPTH + 1) // 2
    n_odd = DEPTH // 2
    nrm = lambda k, shape, s: jax.random.normal(k, shape, f32) * s
    base_decay = jnp.log(-jnp.log1p(-jnp.power(2.0, -5.0 - jnp.arange(RET_HEADS, dtype=f32))))
    return {
        'x': nrm(ks[0], (BATCH, SEQ, D_MODEL), 1.0),
        'c': nrm(ks[1], (BATCH, D_MODEL), 1.0),
        'ctx': nrm(ks[2], (BATCH, CTX_LEN, D_MODEL), 1.0),
        'c_ctx': nrm(ks[3], (D_MODEL,), 1.0),
        'mod_w': nrm(ks[4], (DEPTH, D_MODEL, 6 * D_MODEL), D_MODEL ** -0.5),
        'mod_b': nrm(ks[5], (DEPTH, 6 * D_MODEL), 0.02),
        'norm1_g': 1.0 + nrm(ks[6], (DEPTH, D_MODEL), 0.02),
        'norm2_g': 1.0 + nrm(ks[7], (DEPTH, D_MODEL), 0.02),
        'ab_w_in': nrm(ks[8], (n_even, D_MODEL, AB_IN), D_MODEL ** -0.5),
        'ab_w_out': nrm(ks[9], (n_even, AB_OUT, D_MODEL), AB_OUT ** -0.5),
        'na_q_norm': 1.0 + nrm(ks[10], (n_even, HEAD_DIM), 0.02),
        'na_k_norm': 1.0 + nrm(ks[11], (n_even, HEAD_DIM), 0.02),
        'na_rpb': nrm(ks[12], (n_even, NA_HEADS, 2 * NA_ROWS - 1, 2 * NA_COLS - 1), 0.1),
        'ret_log_decay': base_decay[None, None, :] + nrm(ks[13], (n_even, 2, RET_HEADS), 0.05),
        'swa_w_in': nrm(ks[14], (n_odd, D_MODEL, SWA_IN), D_MODEL ** -0.5),
        'swa_w_out': nrm(ks[15], (n_odd, SWA_OUT, D_MODEL), SWA_OUT ** -0.5),
        'swa_q_norm': 1.0 + nrm(ks[16], (n_odd, HEAD_DIM), 0.02),
        'swa_k_norm': 1.0 + nrm(ks[17], (n_odd, HEAD_DIM), 0.02),
        'swa_sink': nrm(ks[18], (n_odd, SWA_Q_HEADS), 0.5),
        'peer_w_q': nrm(ks[19], (DEPTH, D_MODEL, PEER_HEADS * PEER_QDIM), D_MODEL ** -0.5),
        'peer_sub_keys': nrm(ks[20], (DEPTH, PEER_HEADS, 2, PEER_KEYS, PEER_QDIM // 2), (PEER_QDIM // 2) ** -0.5),
        'peer_u': nrm(ks[21], (DEPTH, PEER_EXPERTS, D_MODEL), D_MODEL ** -0.5),
        'peer_v': nrm(ks[22], (DEPTH, PEER_EXPERTS, D_MODEL), 0.5),
    }


def reference(x, c, ctx, c_ctx, mod_w, mod_b, norm1_g, norm2_g, ab_w_in, ab_w_out, na_q_norm, na_k_norm,
              na_rpb, ret_log_decay, swa_w_in, swa_w_out, swa_q_norm, swa_k_norm, swa_sink,
              peer_w_q, peer_sub_keys, peer_u, peer_v):
    S = x.shape[1]
    cos, sin = axial_rope(S, HEAD_DIM)
    x_lat, x_ctx = x, ctx
    for l in range(DEPTH):
        need_ctx = l < DEPTH - 1
        sh1, sc1, g1, sh2, sc2, g2 = [m[:, None, :] for m in adaln(c, mod_w[l], mod_b[l])]
        ch1, cs1, cg1, ch2, cs2, cg2 = adaln(c_ctx, mod_w[l], mod_b[l])
        hx = modulate(x_lat, norm1_g[l], sh1, sc1)
        hc = modulate(x_ctx, norm1_g[l], ch1, cs1)
        j = l // 2
        if l % 2 == 0:
            yx, yc = mixer_na_retention(hx, hc, ab_w_in[j], ab_w_out[j], na_q_norm[j], na_k_norm[j],
                                        na_rpb[j], ret_log_decay[j], cos, sin, need_ctx)
        else:
            yx, yc = mixer_window_gqa(hx, hc, swa_w_in[j], swa_w_out[j], swa_q_norm[j], swa_k_norm[j],
                                      swa_sink[j], cos, sin, need_ctx)
        x_lat = x_lat + g1 * yx
        hx2 = modulate(x_lat, norm2_g[l], sh2, sc2)
        x_lat = x_lat + g2 * peer(hx2, peer_w_q[l], peer_sub_keys[l], peer_u[l], peer_v[l])
        if need_ctx:
            x_ctx = x_ctx + cg1 * yc
            hc2 = modulate(x_ctx, norm2_g[l], ch2, cs2)
            x_ctx = x_ctx + cg2 * peer(hc2, peer_w_q[l], peer_sub_keys[l], peer_u[l], peer_v[l])
    return x_lat
```

```python
import functools

import jax
import jax.numpy as jnp
from jax import lax
from jax.experimental import pallas as pl
from jax.experimental.pallas import tpu as pltpu

F32 = jnp.float32
BF16 = jnp.bfloat16

HEAD_DIM = 64
GRID_W = 64
NA_ROWS = 8
NA_COLS = 16
SWA_WINDOW = 128
PEER_TOPK = 16
PEER_KEYS = 128
ROPE_BASE = 10000.0
NORM_EPS = 1e-6
GN_EPS = 1e-5
NEG_INF = -1e30
ATTN_SCALE = HEAD_DIM ** -0.5
INV_SQRT2 = 0.7071067811865476

LANES = 128
VMEM_LIMIT = 56 * 1024 * 1024

TOKEN_TILE = 512
PROJ_N_TILE = 512
RET_CHUNK = 128
NA_Q_ROWS = 8
NA_BAND_ROWS = 16
SWA_Q_TILE = 256
PEER_TOKEN_TILE = 512
PEER_EXPERT_TILE = 1024


def _dot(a, b):
    return jnp.dot(a, b, preferred_element_type=F32)


def _dot_nt(a, b):
    return lax.dot_general(a, b, (((1,), (1,)), ((), ())), preferred_element_type=F32)


def _dot_tn(a, b):
    return lax.dot_general(a, b, (((0,), (0,)), ((), ())), preferred_element_type=F32)


def _params(*sem):
    return pltpu.CompilerParams(dimension_semantics=sem, vmem_limit_bytes=VMEM_LIMIT)


def _rms_rows(x, gain):
    ms = jnp.mean(x * x, axis=-1, keepdims=True)
    return x * lax.rsqrt(ms + NORM_EPS) * gain


def _modulate(x, gain, shift, scale):
    return _rms_rows(x, gain) * (1.0 + scale) + shift


def _head_rms(x, gain):
    lane = lax.broadcasted_iota(jnp.int32, x.shape, 1)
    lo = lane < HEAD_DIM
    ss = x * x
    s_lo = jnp.sum(jnp.where(lo, ss, 0.0), axis=-1, keepdims=True)
    s_hi = jnp.sum(jnp.where(lo, 0.0, ss), axis=-1, keepdims=True)
    ms = jnp.where(lo, s_lo, s_hi) * (1.0 / HEAD_DIM)
    return x * lax.rsqrt(ms + NORM_EPS) * gain


def _rope(x, cos2, sin2):
    lane = lax.broadcasted_iota(jnp.int32, x.shape, 1)
    first_half = (lane & (HEAD_DIM // 2)) == 0
    swapped = jnp.where(first_half, pltpu.roll(x, LANES - HEAD_DIM // 2, axis=1),
                        pltpu.roll(x, HEAD_DIM // 2, axis=1))
    return x * cos2 + swapped * sin2


def _adaln_kernel(c_ref, w_ref, b_ref, o_ref):
    c = c_ref[...]
    s = c * jax.nn.sigmoid(c)
    w = w_ref[0]
    s_hi = s.astype(BF16)
    s_lo = (s - s_hi.astype(F32)).astype(BF16)
    w_hi = w.astype(BF16)
    w_lo = (w - w_hi.astype(F32)).astype(BF16)
    acc = _dot(s_hi, w_hi) + _dot(s_lo, w_hi) + _dot(s_hi, w_lo)
    o_ref[0] = acc + b_ref[0]


def _adaln(cond, mod_w, mod_b):
    depth, d, n = mod_w.shape
    tn = n // 4
    return pl.pallas_call(
        _adaln_kernel,
        grid=(depth, n // tn),
        in_specs=[pl.BlockSpec((8, d), lambda l, j: (0, 0)),
                  pl.BlockSpec((1, d, tn), lambda l, j: (l, 0, j)),
                  pl.BlockSpec((1, 1, tn), lambda l, j: (l, 0, j))],
        out_specs=pl.BlockSpec((1, 8, tn), lambda l, j: (l, 0, j)),
        out_shape=jax.ShapeDtypeStruct((depth, 8, n), F32),
        compiler_params=_params("arbitrary", "arbitrary"),
        name="adaln",
    )(cond, mod_w, mod_b.reshape(depth, 1, n))


def _modmm_kernel(x_ref, g_ref, sh_ref, sc_ref, w_ref, o_ref, h_scr):
    @pl.when(pl.program_id(1) == 0)
    def _():
        h_scr[...] = _modulate(x_ref[...], g_ref[...], sh_ref[0], sc_ref[0]).astype(BF16)

    o_ref[...] = _dot(h_scr[...], w_ref[...])


def _modmm(x, gain, shift, scale, w, seg):
    t, d = x.shape
    n = w.shape[1]
    tm = min(TOKEN_TILE, seg)
    tn = PROJ_N_TILE
    per_seg = seg // tm
    r = shift.shape[0]
    return pl.pallas_call(
        _modmm_kernel,
        grid=(t // tm, n // tn),
        in_specs=[pl.BlockSpec((tm, d), lambda i, j: (i, 0)),
                  pl.BlockSpec((1, d), lambda i, j: (0, 0)),
                  pl.BlockSpec((1, 1, d), lambda i, j: (i // per_seg, 0, 0)),
                  pl.BlockSpec((1, 1, d), lambda i, j: (i // per_seg, 0, 0)),
                  pl.BlockSpec((d, tn), lambda i, j: (0, j))],
        out_specs=pl.BlockSpec((tm, tn), lambda i, j: (i, j)),
        out_shape=jax.ShapeDtypeStruct((t, n), F32),
        scratch_shapes=[pltpu.VMEM((tm, d), BF16)],
        compiler_params=_params("arbitrary", "arbitrary"),
        name="modulate_matmul",
    )(x, gain.reshape(1, d), shift.reshape(r, 1, d), scale.reshape(r, 1, d), w)


def _outproj_kernel(*refs, n_in):
    a_refs = refs[:n_in]
    w_refs = refs[n_in:2 * n_in]
    x_ref, gate_ref, g2_ref, sh_ref, sc_ref, xo_ref, h_ref = refs[2 * n_in:]
    acc = None
    for a_ref, w_ref in zip(a_refs, w_refs):
        part = _dot(a_ref[...].astype(BF16), w_ref[...])
        acc = part if acc is None else acc + part
    xn = x_ref[...] + gate_ref[0] * acc
    xo_ref[...] = xn
    h_ref[...] = _modulate(xn, g2_ref[...], sh_ref[0], sc_ref[0]).astype(BF16)


def _outproj(a_list, w_list, x, gate, gain2, shift2, scale2, seg):
    t, d = x.shape
    tm = min(TOKEN_TILE, seg)
    per_seg = seg // tm
    r = gate.shape[0]
    n_in = len(a_list)
    row = lambda i: (i // per_seg, 0, 0)
    in_specs = ([pl.BlockSpec((tm, a.shape[1]), lambda i: (i, 0)) for a in a_list]
                + [pl.BlockSpec(w.shape, lambda i: (0, 0)) for w in w_list]
                + [pl.BlockSpec((tm, d), lambda i: (i, 0)),
                   pl.BlockSpec((1, 1, d), row),
                   pl.BlockSpec((1, d), lambda i: (0, 0)),
                   pl.BlockSpec((1, 1, d), row),
                   pl.BlockSpec((1, 1, d), row)])
    return pl.pallas_call(
        functools.partial(_outproj_kernel, n_in=n_in),
        grid=(t // tm,),
        in_specs=in_specs,
        out_specs=[pl.BlockSpec((tm, d), lambda i: (i, 0)), pl.BlockSpec((tm, d), lambda i: (i, 0))],
        out_shape=[jax.ShapeDtypeStruct((t, d), F32), jax.ShapeDtypeStruct((t, d), BF16)],
        compiler_params=_params("arbitrary"),
        name="out_proj_residual",
    )(*a_list, *w_list, x, gate.reshape(r, 1, d), gain2.reshape(1, d),
      shift2.reshape(r, 1, d), scale2.reshape(r, 1, d))


def _softmax_pv(s_list, v_list, extra=None):
    m = s_list[0].max(axis=-1, keepdims=True)
    for s in s_list[1:]:
        m = jnp.maximum(m, s.max(axis=-1, keepdims=True))
    if extra is not None:
        m = jnp.maximum(m, extra)
    denom = None
    out = None
    for s, v in zip(s_list, v_list):
        p = jnp.exp(s - m)
        ps = p.sum(axis=-1, keepdims=True)
        denom = ps if denom is None else denom + ps
        pv = _dot(p.astype(BF16), v)
        out = pv if out is None else out + pv
    if extra is not None:
        denom = denom + jnp.exp(extra - m)
    return out / denom


def _na_kernel(q_ref, k_ref, v_ref, kc_ref, vc_ref, bias_ref, qn_ref, kn_ref, o_ref, *, n_steps):
    i = pl.program_id(2)
    band = NA_BAND_ROWS * GRID_W
    start = jnp.clip(NA_Q_ROWS * i - NA_ROWS // 2, 0, NA_Q_ROWS * n_steps - NA_BAND_ROWS) * GRID_W
    start = pl.multiple_of(start, NA_ROWS // 2 * GRID_W)
    q = (_head_rms(q_ref[0], qn_ref[...]) * ATTN_SCALE).astype(BF16)
    kb = _head_rms(k_ref[0, pl.ds(start, band), :], kn_ref[...]).astype(BF16)
    vb = v_ref[0, pl.ds(start, band), :].astype(BF16)
    kc = _head_rms(kc_ref[0], kn_ref[...]).astype(BF16)
    vc = vc_ref[0].astype(BF16)
    outs = []
    for h in range(2):
        sl = slice(h * HEAD_DIM, (h + 1) * HEAD_DIM)
        s_loc = _dot_nt(q[:, sl], kb[:, sl]) + bias_ref[0, h]
        s_ctx = _dot_nt(q[:, sl], kc[:, sl])
        outs.append(_softmax_pv([s_loc, s_ctx], [vb[:, sl], vc[:, sl]]))
    o_ref[0] = jnp.concatenate(outs, axis=-1)


def _na_bias(rpb, rows):
    n_steps = rows // NA_Q_ROWS
    blk = jnp.array([0, 1, n_steps - 1])
    band0 = jnp.clip(NA_Q_ROWS * blk - NA_ROWS // 2, 0, rows - NA_BAND_ROWS)
    r = (NA_Q_ROWS * blk)[:, None] + jnp.arange(NA_Q_ROWS)[None, :]
    r0 = jnp.clip(r - NA_ROWS // 2, 0, rows - NA_ROWS)
    rk = band0[:, None] + jnp.arange(NA_BAND_ROWS)[None, :]
    dr = rk[:, None, :] - r[:, :, None] + (NA_ROWS - 1)
    ok_r = (rk[:, None, :] >= r0[:, :, None]) & (rk[:, None, :] < r0[:, :, None] + NA_ROWS)
    col = jnp.arange(GRID_W)
    c0 = jnp.clip(col - NA_COLS // 2, 0, GRID_W - NA_COLS)
    dc = col[None, :] - col[:, None] + (NA_COLS - 1)
    ok_c = (col[None, :] >= c0[:, None]) & (col[None, :] < c0[:, None] + NA_COLS)
    dr_i = jnp.clip(dr, 0, 2 * NA_ROWS - 2)[:, :, None, :, None]
    dc_i = jnp.clip(dc, 0, 2 * NA_COLS - 2)[None, None, :, None, :]
    vals = rpb[:, dr_i, dc_i]
    ok = ok_r[:, :, None, :, None] & ok_c[None, None, :, None, :]
    bias = jnp.where(ok[None], vals, NEG_INF)
    h = rpb.shape[0]
    return bias.transpose(1, 0, 2, 3, 4, 5).reshape(3, h, NA_Q_ROWS * GRID_W, NA_BAND_ROWS * GRID_W)


def _na_attention(p_lat, p_ctx, bias, qn, kn, n_pairs):
    b, s, _ = p_lat.shape
    l = p_ctx.shape[1]
    rows = s // GRID_W
    n_steps = rows // NA_Q_ROWS
    qt = NA_Q_ROWS * GRID_W

    def cls(i):
        return jnp.where(i == 0, 0, jnp.where(i == n_steps - 1, 2, 1))

    return pl.pallas_call(
        functools.partial(_na_kernel, n_steps=n_steps),
        grid=(n_pairs, b, n_steps),
        in_specs=[pl.BlockSpec((1, qt, LANES), lambda hp, bb, i: (bb, i, hp)),
                  pl.BlockSpec((1, s, LANES), lambda hp, bb, i: (bb, 0, n_pairs + hp)),
                  pl.BlockSpec((1, s, LANES), lambda hp, bb, i: (bb, 0, 2 * n_pairs + hp)),
                  pl.BlockSpec((1, l, LANES), lambda hp, bb, i: (bb, 0, n_pairs + hp)),
                  pl.BlockSpec((1, l, LANES), lambda hp, bb, i: (bb, 0, 2 * n_pairs + hp)),
                  pl.BlockSpec((1, 2, qt, NA_BAND_ROWS * GRID_W), lambda hp, bb, i: (cls(i), hp, 0, 0)),
                  pl.BlockSpec((1, LANES), lambda hp, bb, i: (0, 0)),
                  pl.BlockSpec((1, LANES), lambda hp, bb, i: (0, 0))],
        out_specs=pl.BlockSpec((1, qt, LANES), lambda hp, bb, i: (bb, i, hp)),
        out_shape=jax.ShapeDtypeStruct((b, s, n_pairs * LANES), F32),
        compiler_params=_params("arbitrary", "arbitrary", "arbitrary"),
        name="neighbourhood_attention",
    )(p_lat, p_lat, p_lat, p_ctx, p_ctx, bias, qn, kn)


def _ctx_attn_kernel(q_ref, k_ref, v_ref, qn_ref, kn_ref, o_ref):
    q = (_head_rms(q_ref[0], qn_ref[...]) * ATTN_SCALE).astype(BF16)
    k = _head_rms(k_ref[0], kn_ref[...]).astype(BF16)
    v = v_ref[0].astype(BF16)
    outs = []
    for h in range(2):
        sl = slice(h * HEAD_DIM, (h + 1) * HEAD_DIM)
        outs.append(_softmax_pv([_dot_nt(q[:, sl], k[:, sl])], [v[:, sl]]))
    o_ref[0] = jnp.concatenate(outs, axis=-1)


def _ctx_attention(p_ctx, qn, kn, n_pairs):
    b, l, _ = p_ctx.shape
    return pl.pallas_call(
        _ctx_attn_kernel,
        grid=(n_pairs, b),
        in_specs=[pl.BlockSpec((1, l, LANES), lambda hp, bb: (bb, 0, hp)),
                  pl.BlockSpec((1, l, LANES), lambda hp, bb: (bb, 0, n_pairs + hp)),
                  pl.BlockSpec((1, l, LANES), lambda hp, bb: (bb, 0, 2 * n_pairs + hp)),
                  pl.BlockSpec((1, LANES), lambda hp, bb: (0, 0)),
                  pl.BlockSpec((1, LANES), lambda hp, bb: (0, 0))],
        out_specs=pl.BlockSpec((1, l, LANES), lambda hp, bb: (bb, 0, hp)),
        out_shape=jax.ShapeDtypeStruct((b, l, n_pairs * LANES), F32),
        compiler_params=_params("arbitrary", "arbitrary"),
        name="context_attention",
    )(p_ctx, p_ctx, p_ctx, qn, kn)


def _ret_kernel(q_ref, k_ref, v_ref, g_ref, cos_ref, sin_ref, lg_ref, s0_ref, y_ref, st_ref, sf_scr,
                *, n_chunks, use_rope):
    c = RET_CHUNK
    hd = HEAD_DIM
    lg = -jnp.exp(lg_ref[0])
    lgf, lgb = lg[0:1, :], lg[1:2, :]
    ii = lax.broadcasted_iota(jnp.int32, (c, LANES), 0).astype(F32)
    dq_f = jnp.exp(lgf * (ii + 1.0))
    dk_f = jnp.exp(lgf * (c - 1.0 - ii))
    dq_b = jnp.exp(lgb * (c - ii))
    dk_b = jnp.exp(lgb * ii)
    dc_f = jnp.exp(lgf * float(c))
    dc_b = jnp.exp(lgb * float(c))
    diff = (lax.broadcasted_iota(jnp.int32, (c, c), 0) - lax.broadcasted_iota(jnp.int32, (c, c), 1)).astype(F32)
    intra = []
    chunk_f = []
    chunk_b = []
    for h in range(2):
        lf = lgf[:, h * hd:h * hd + 1]
        lb = lgb[:, h * hd:h * hd + 1]
        intra.append(jnp.where(diff >= 0, jnp.exp(lf * jnp.maximum(diff, 0.0)),
                               jnp.exp(lb * jnp.maximum(-diff, 0.0))))
        chunk_f.append(dc_f[:, h * hd:h * hd + 1])
        chunk_b.append(dc_b[:, h * hd:h * hd + 1])

    def load(n):
        r = pl.multiple_of(n * c, c)
        q = q_ref[0, pl.ds(r, c), :]
        k = k_ref[0, pl.ds(r, c), :]
        v = v_ref[0, pl.ds(r, c), :]
        if use_rope:
            cs = cos_ref[pl.ds(r, c), :]
            sn = sin_ref[pl.ds(r, c), :]
            q = _rope(q, cs, sn)
            k = _rope(k, cs, sn)
        return r, q * ATTN_SCALE, k, v.astype(BF16)

    def fwd(n, carry):
        _, _, k, v = load(n)
        kd = (k * dk_f).astype(BF16)
        new = []
        for h in range(2):
            sl = slice(h * hd, (h + 1) * hd)
            sf_scr[n, h] = carry[h]
            new.append(carry[h] * chunk_f[h] + _dot_tn(kd[:, sl], v[:, sl]))
        return tuple(new)

    sf = lax.fori_loop(0, n_chunks, fwd, (s0_ref[0, 0, 0], s0_ref[0, 0, 1]))
    st_ref[0, 0, 0] = sf[0]
    st_ref[0, 0, 1] = sf[1]

    def bwd(jj, carry):
        n = n_chunks - 1 - jj
        r, q, k, v = load(n)
        qb = q.astype(BF16)
        kb = k.astype(BF16)
        qf = (q * dq_f).astype(BF16)
        qr = (q * dq_b).astype(BF16)
        kd = (k * dk_b).astype(BF16)
        outs = []
        new = []
        for h in range(2):
            sl = slice(h * hd, (h + 1) * hd)
            a = (_dot_nt(qb[:, sl], kb[:, sl]) * intra[h]).astype(BF16)
            o = (_dot(a, v[:, sl]) + _dot(qf[:, sl], sf_scr[n, h].astype(BF16))
                 + _dot(qr[:, sl], carry[h].astype(BF16)))
            oc = o - jnp.mean(o, axis=-1, keepdims=True)
            outs.append(oc * lax.rsqrt(jnp.mean(oc * oc, axis=-1, keepdims=True) + GN_EPS))
            new.append(carry[h] * chunk_b[h] + _dot_tn(kd[:, sl], v[:, sl]))
        g = g_ref[0, pl.ds(r, c), :]
        y_ref[0, pl.ds(r, c), :] = jnp.concatenate(outs, axis=-1) * (g * jax.nn.sigmoid(g))
        return tuple(new)

    sb = lax.fori_loop(0, n_chunks, bwd, (s0_ref[0, 0, 2], s0_ref[0, 0, 3]))
    st_ref[0, 0, 2] = sb[0]
    st_ref[0, 0, 3] = sb[1]


def _retention(p, cos2, sin2, lg, s0, col0, n_pairs, use_rope):
    b, t, _ = p.shape
    n_chunks = t // RET_CHUNK
    tab = lambda hp, bb: (0, 0)
    return pl.pallas_call(
        functools.partial(_ret_kernel, n_chunks=n_chunks, use_rope=use_rope),
        grid=(n_pairs, b),
        in_specs=[pl.BlockSpec((1, t, LANES), lambda hp, bb: (bb, 0, col0 + hp)),
                  pl.BlockSpec((1, t, LANES), lambda hp, bb: (bb, 0, col0 + n_pairs + hp)),
                  pl.BlockSpec((1, t, LANES), lambda hp, bb: (bb, 0, col0 + 2 * n_pairs + hp)),
                  pl.BlockSpec((1, t, LANES), lambda hp, bb: (bb, 0, col0 + 3 * n_pairs + hp)),
                  pl.BlockSpec(cos2.shape, tab),
                  pl.BlockSpec(sin2.shape, tab),
                  pl.BlockSpec((1, 2, LANES), lambda hp, bb: (hp, 0, 0)),
                  pl.BlockSpec((1, 1, 4, HEAD_DIM, HEAD_DIM), lambda hp, bb: (bb, hp, 0, 0, 0))],
        out_specs=[pl.BlockSpec((1, t, LANES), lambda hp, bb: (bb, 0, hp)),
                   pl.BlockSpec((1, 1, 4, HEAD_DIM, HEAD_DIM), lambda hp, bb: (bb, hp, 0, 0, 0))],
        out_shape=[jax.ShapeDtypeStruct((b, t, n_pairs * LANES), F32),
                   jax.ShapeDtypeStruct((b, n_pairs, 4, HEAD_DIM, HEAD_DIM), F32)],
        scratch_shapes=[pltpu.VMEM((n_chunks, 2, HEAD_DIM, HEAD_DIM), F32)],
        compiler_params=_params("arbitrary", "arbitrary"),
        name="retention",
    )(p, p, p, p, cos2, sin2, lg, s0)


def _swa_kernel(q_ref, k_ref, v_ref, kc_ref, vc_ref, cos_ref, sin_ref, qn_ref, kn_ref, sink_ref, o_ref,
                *, seq):
    qt = SWA_Q_TILE
    wk = qt + 2 * SWA_WINDOW
    hd = HEAD_DIM
    n = pl.program_id(2)
    q0 = pl.multiple_of(n * qt, qt)
    ws = pl.multiple_of(jnp.clip(n * qt - SWA_WINDOW, 0, seq - wk), SWA_WINDOW)
    kw = _rope(_head_rms(k_ref[0, pl.ds(ws, wk), :], kn_ref[...]),
               cos_ref[pl.ds(ws, wk), :], sin_ref[pl.ds(ws, wk), :]).astype(BF16)
    vw = v_ref[0, pl.ds(ws, wk), :].astype(BF16)
    kc = _head_rms(kc_ref[0], kn_ref[...]).astype(BF16)
    vc = vc_ref[0].astype(BF16)
    cos_q = cos_ref[pl.ds(q0, qt), :]
    sin_q = sin_ref[pl.ds(q0, qt), :]
    qs = []
    for s in range(4):
        slab = q_ref[0, :, s * LANES:(s + 1) * LANES]
        qs.append((_rope(_head_rms(slab, qn_ref[...]), cos_q, sin_q) * ATTN_SCALE).astype(BF16))
    qpos = q0 + (lax.broadcasted_iota(jnp.int32, (4 * qt, wk), 0) & (qt - 1))
    kpos = ws + lax.broadcasted_iota(jnp.int32, (4 * qt, wk), 1)
    near4 = jnp.abs(qpos - kpos) <= SWA_WINDOW
    for kh in range(2):
        sl = slice(kh * hd, (kh + 1) * hd)
        qstack = jnp.concatenate(
            [qs[kh * 2 + g // 2][:, (g % 2) * hd:(g % 2 + 1) * hd] for g in range(4)], axis=0)
        sink = jnp.concatenate(
            [jnp.broadcast_to(sink_ref[kh * 4 + g:kh * 4 + g + 1, 0:1], (qt, 1)) for g in range(4)], axis=0)
        s_loc = jnp.where(near4, _dot_nt(qstack, kw[:, sl]), NEG_INF)
        s_ctx = _dot_nt(qstack, kc[:, sl])
        o = _softmax_pv([s_loc, s_ctx], [vw[:, sl], vc[:, sl]], extra=sink)
        for pair in range(2):
            col = (kh * 2 + pair) * LANES
            o_ref[0, :, col:col + LANES] = jnp.concatenate(
                [o[(2 * pair) * qt:(2 * pair + 1) * qt], o[(2 * pair + 1) * qt:(2 * pair + 2) * qt]], axis=-1)


def _swa_attention(p_lat, p_ctx, cos2, sin2, qn, kn, sink_rows, n_q_heads, n_kv_heads):
    b, s, _ = p_lat.shape
    l = p_ctx.shape[1]
    kv_pairs = n_kv_heads // 2
    q_blocks = n_q_heads * HEAD_DIM // LANES
    q_per_pair = q_blocks // kv_pairs
    qw = q_per_pair * LANES
    tab = lambda kp, bb, n: (0, 0)
    return pl.pallas_call(
        functools.partial(_swa_kernel, seq=s),
        grid=(kv_pairs, b, s // SWA_Q_TILE),
        in_specs=[pl.BlockSpec((1, SWA_Q_TILE, qw), lambda kp, bb, n: (bb, n, kp)),
                  pl.BlockSpec((1, s, LANES), lambda kp, bb, n: (bb, 0, q_blocks + kp)),
                  pl.BlockSpec((1, s, LANES), lambda kp, bb, n: (bb, 0, q_blocks + kv_pairs + kp)),
                  pl.BlockSpec((1, l, LANES), lambda kp, bb, n: (bb, 0, q_blocks + kp)),
                  pl.BlockSpec((1, l, LANES), lambda kp, bb, n: (bb, 0, q_blocks + kv_pairs + kp)),
                  pl.BlockSpec(cos2.shape, tab),
                  pl.BlockSpec(sin2.shape, tab),
                  pl.BlockSpec((1, LANES), tab),
                  pl.BlockSpec((1, LANES), tab),
                  pl.BlockSpec((8, LANES), lambda kp, bb, n: (kp, 0))],
        out_specs=pl.BlockSpec((1, SWA_Q_TILE, qw), lambda kp, bb, n: (bb, n, kp)),
        out_shape=jax.ShapeDtypeStruct((b, s, n_q_heads * HEAD_DIM), F32),
        compiler_params=_params("arbitrary", "arbitrary", "arbitrary"),
        name="windowed_gqa",
    )(p_lat, p_lat, p_lat, p_ctx, p_ctx, cos2, sin2, qn, kn, sink_rows)


def _top_rows(x, k, scr):
    work = x
    for r in range(k):
        m = jnp.max(work, axis=0, keepdims=True)
        scr[r:r + 1, :] = m
        if r + 1 < k:
            work = jnp.where(work == m, NEG_INF, work)


def _peer_scores_kernel(h_ref, wq_ref, keys_ref, ct_ref, p1_ref, s2_ref, p2_ref, a_scr, b_scr, c_scr, v_scr,
                        *, n_heads):
    k = PEER_TOPK
    hb = h_ref[...]
    for h in range(n_heads):
        s = []
        for p in range(2):
            col = (2 * h + p) * PEER_KEYS
            qhp = _dot(hb, wq_ref[:, col:col + PEER_KEYS]).astype(BF16)
            s.append(_dot_nt(keys_ref[2 * h + p], qhp))
        _top_rows(s[0], k + 1, a_scr)
        _top_rows(s[1], k + 1, b_scr)
        c_scr[0:16, :] = a_scr[0:1, :] + b_scr[0:16, :]
        for i in range(1, 8):
            c_scr[8 + 8 * i:16 + 8 * i, :] = a_scr[i:i + 1, :] + b_scr[0:8, :]
        c_scr[72:80, :] = a_scr[8:16, :] + b_scr[0:1, :]
        c_scr[80:81, :] = a_scr[0:1, :] + b_scr[16:17, :]
        c_scr[81:82, :] = a_scr[16:17, :] + b_scr[0:1, :]
        c_scr[82:88, :] = jnp.full((6, c_scr.shape[1]), NEG_INF, F32)
        _top_rows(c_scr[...], k + 1, v_scr)
        thr = 0.5 * (v_scr[k - 1:k, :] + v_scr[k:k + 1, :])
        z = jnp.sum(jnp.exp(v_scr[0:k, :] - v_scr[0:1, :]), axis=0, keepdims=True)
        ct_ref[h] = thr - s[0]
        p1_ref[h] = jnp.exp(s[0] - a_scr[0:1, :]) / z
        s2_ref[h] = s[1]
        p2_ref[h] = jnp.exp(s[1] - b_scr[0:1, :])


def _peer_scores(h2, wq, keys, n_heads):
    t, d = h2.shape
    tt = PEER_TOKEN_TILE
    tab_shape = (n_heads, PEER_KEYS, t)
    tab_spec = pl.BlockSpec((n_heads, PEER_KEYS, tt), lambda i: (0, 0, i))
    return pl.pallas_call(
        functools.partial(_peer_scores_kernel, n_heads=n_heads),
        grid=(t // tt,),
        in_specs=[pl.BlockSpec((tt, d), lambda i: (i, 0)),
                  pl.BlockSpec(wq.shape, lambda i: (0, 0)),
                  pl.BlockSpec(keys.shape, lambda i: (0, 0, 0))],
        out_specs=[tab_spec] * 4,
        out_shape=[jax.ShapeDtypeStruct(tab_shape, F32)] * 4,
        scratch_shapes=[pltpu.VMEM((24, tt), F32), pltpu.VMEM((24, tt), F32),
                        pltpu.VMEM((88, tt), F32), pltpu.VMEM((24, tt), F32)],
        compiler_params=_params("arbitrary"),
        name="peer_scores",
    )(h2, wq, keys)


def _peer_mix_kernel(h_ref, u_ref, vt_ref, ct_ref, p1_ref, s2_ref, p2_ref, x_ref, g_ref, o_ref,
                     acc_scr, act_scr, a_scr, *, n_heads):
    j = pl.program_id(1)
    tt = h_ref.shape[0]
    rows_per_tile = PEER_EXPERT_TILE // PEER_KEYS

    @pl.when(j == 0)
    def _():
        acc_scr[...] = jnp.zeros_like(acc_scr)

    act_scr[...] = _dot_nt(u_ref[...], h_ref[...])
    for aa in range(rows_per_tile):
        rs = slice(aa * PEER_KEYS, (aa + 1) * PEER_KEYS)
        for lb in range(tt // LANES):
            ls = slice(lb * LANES, (lb + 1) * LANES)
            w = None
            for h in range(n_heads):
                cut = ct_ref[h, aa:aa + 1, ls]
                p1 = p1_ref[h, aa:aa + 1, ls]
                term = jnp.where(s2_ref[h, :, ls] >= cut, p2_ref[h, :, ls], 0.0) * p1
                w = term if w is None else w + term
            act = act_scr[rs, ls]
            gelu = 0.5 * act * (1.0 + lax.erf(act * INV_SQRT2))
            a_scr[rs, ls] = (w * gelu).astype(BF16)
    acc_scr[...] += _dot(vt_ref[...], a_scr[...])

    @pl.when(j == pl.num_programs(1) - 1)
    def _():
        o_ref[...] = x_ref[...] + g_ref[0] * acc_scr[...].T


def _peer_mix(h2, u, vt, tables, x, gate, seg, n_heads):
    t, d = h2.shape
    n_exp = u.shape[0]
    tt = PEER_TOKEN_TILE
    te = PEER_EXPERT_TILE
    per_seg = seg // tt
    r = gate.shape[0]
    tab_spec = pl.BlockSpec((n_heads, PEER_KEYS, tt), lambda i, j: (0, 0, i))
    row_spec = pl.BlockSpec((n_heads, te // PEER_KEYS, tt), lambda i, j: (0, j, i))
    return pl.pallas_call(
        functools.partial(_peer_mix_kernel, n_heads=n_heads),
        grid=(t // tt, n_exp // te),
        in_specs=[pl.BlockSpec((tt, d), lambda i, j: (i, 0)),
                  pl.BlockSpec((te, d), lambda i, j: (j, 0)),
                  pl.BlockSpec((d, te), lambda i, j: (0, j)),
                  row_spec, row_spec, tab_spec, tab_spec,
                  pl.BlockSpec((tt, d), lambda i, j: (i, 0)),
                  pl.BlockSpec((1, 1, d), lambda i, j: (i // per_seg, 0, 0))],
        out_specs=pl.BlockSpec((tt, d), lambda i, j: (i, 0)),
        out_shape=jax.ShapeDtypeStruct((t, d), F32),
        scratch_shapes=[pltpu.VMEM((d, tt), F32), pltpu.VMEM((te, tt), F32), pltpu.VMEM((te, tt), BF16)],
        compiler_params=_params("arbitrary", "arbitrary"),
        name="peer_mix",
    )(h2, u, vt, *tables, x, gate.reshape(r, 1, d))


def _peer(h2, x, gate, seg, wq, keys, u, vt):
    n_heads = keys.shape[0] // 2
    tables = _peer_scores(h2, wq, keys, n_heads)
    return _peer_mix(h2, u, vt, tables, x, gate, seg, n_heads)


def _rope_tables(n_tokens):
    t = jnp.arange(n_tokens)
    row = (t // GRID_W).astype(F32)
    col = (t % GRID_W).astype(F32)
    n_freq = HEAD_DIM // 4
    inv_freq = jnp.power(ROPE_BASE, -jnp.arange(n_freq, dtype=F32) / n_freq)
    ang = jnp.concatenate([row[:, None] * inv_freq, col[:, None] * inv_freq], axis=-1)
    cos, sin = jnp.cos(ang), jnp.sin(ang)
    cos2 = jnp.tile(jnp.concatenate([cos, cos], axis=-1), (1, 2))
    sin2 = jnp.tile(jnp.concatenate([-sin, sin], axis=-1), (1, 2))
    return cos2, sin2


def _pair_lanes(v):
    return jnp.repeat(v, HEAD_DIM, axis=-1).reshape(*v.shape[:-1], v.shape[-1] // 2, LANES)


def kernel(x, c, ctx, c_ctx, mod_w, mod_b, norm1_g, norm2_g, ab_w_in, ab_w_out, na_q_norm, na_k_norm,
           na_rpb, ret_log_decay, swa_w_in, swa_w_out, swa_q_norm, swa_k_norm, swa_sink,
           peer_w_q, peer_sub_keys, peer_u, peer_v):
    b, s, d = x.shape
    l = ctx.shape[1]
    depth = mod_w.shape[0]
    assert depth == 2 and b + 1 <= 8
    assert s % (NA_Q_ROWS * GRID_W) == 0 and s // GRID_W >= NA_BAND_ROWS
    assert s % TOKEN_TILE == 0 and (b * l) % TOKEN_TILE == 0 and l % RET_CHUNK == 0

    cond = jnp.concatenate([c, c_ctx[None, :], jnp.zeros((8 - b - 1, d), F32)], axis=0)
    mods = _adaln(cond, mod_w, mod_b).reshape(depth, 8, 6, d)
    lat = lambda layer, which: mods[layer, :b, which]
    cx = lambda layer, which: mods[layer, b:b + 1, which]

    cos2, sin2 = _rope_tables(s)
    x_lat = x.reshape(b * s, d)
    x_ctx = ctx.reshape(b * l, d)
    tile2 = lambda g: jnp.tile(g, 2).reshape(1, LANES)

    def peer_weights(layer):
        n_heads = peer_sub_keys.shape[1]
        keys = peer_sub_keys[layer].reshape(2 * n_heads, PEER_KEYS, -1).astype(BF16)
        return (peer_w_q[layer].astype(BF16), keys, peer_u[layer].astype(BF16),
                peer_v[layer].T.astype(BF16))

    n_na = na_rpb.shape[1]
    n_ret = ret_log_decay.shape[2]
    na_pairs, ret_pairs = n_na // 2, n_ret // 2
    w_in = ab_w_in[0].astype(BF16)
    w_out = ab_w_out[0].astype(BF16)
    wa = n_na * HEAD_DIM
    p_lat = _modmm(x_lat, norm1_g[0], lat(0, 0), lat(0, 1), w_in, s).reshape(b, s, -1)
    p_ctx = _modmm(x_ctx, norm1_g[0], cx(0, 0), cx(0, 1), w_in, b * l).reshape(b, l, -1)

    qn, kn = tile2(na_q_norm[0]), tile2(na_k_norm[0])
    oa_lat = _na_attention(p_lat, p_ctx, _na_bias(na_rpb[0], s // GRID_W), qn, kn, na_pairs)
    oa_ctx = _ctx_attention(p_ctx, qn, kn, na_pairs)

    lg = _pair_lanes(ret_log_decay[0]).transpose(1, 0, 2)
    ret_col0 = 3 * wa // LANES
    zeros_state = jnp.zeros((b, ret_pairs, 4, HEAD_DIM, HEAD_DIM), F32)
    ones_tab, zeros_tab = jnp.ones((l, LANES), F32), jnp.zeros((l, LANES), F32)
    ob_ctx, st_ctx = _retention(p_ctx, ones_tab, zeros_tab, lg, zeros_state, ret_col0, ret_pairs, False)
    ob_lat, _ = _retention(p_lat, cos2, sin2, lg, st_ctx, ret_col0, ret_pairs, True)

    w_list = [w_out[:wa], w_out[wa:]]
    x_lat, h_lat = _outproj([oa_lat.reshape(b * s, -1), ob_lat.reshape(b * s, -1)], w_list, x_lat,
                            lat(0, 2), norm2_g[0], lat(0, 3), lat(0, 4), s)
    x_ctx, h_ctx = _outproj([oa_ctx.reshape(b * l, -1), ob_ctx.reshape(b * l, -1)], w_list, x_ctx,
                            cx(0, 2), norm2_g[0], cx(0, 3), cx(0, 4), b * l)
    pw = peer_weights(0)
    x_lat = _peer(h_lat, x_lat, lat(0, 5), s, *pw)
    x_ctx = _peer(h_ctx, x_ctx, cx(0, 5), b * l, *pw)

    n_q = swa_sink.shape[1]
    n_kv = (swa_w_in.shape[2] // HEAD_DIM - n_q) // 2
    w_in = swa_w_in[0].astype(BF16)
    p_lat = _modmm(x_lat, norm1_g[1], lat(1, 0), lat(1, 1), w_in, s).reshape(b, s, -1)
    p_ctx = _modmm(x_ctx, norm1_g[1], cx(1, 0), cx(1, 1), w_in, b * l).reshape(b, l, -1)
    sink_rows = jnp.broadcast_to(swa_sink[0][:, None], (n_q, LANES))
    o_lat = _swa_attention(p_lat, p_ctx, cos2, sin2, tile2(swa_q_norm[0]), tile2(swa_k_norm[0]),
                           sink_rows, n_q, n_kv)
    x_lat, h_lat = _outproj([o_lat.reshape(b * s, -1)], [swa_w_out[0].astype(BF16)], x_lat,
                            lat(1, 2), norm2_g[1], lat(1, 3), lat(1, 4), s)
    x_lat = _peer(h_lat, x_lat, lat(1, 5), s, *peer_weights(1))
    return x_lat.reshape(b, s, d)
```

```python
import functools

import numpy as np
import jax
import jax.numpy as jnp
from jax import lax
from jax.experimental import pallas as pl
from jax.experimental.pallas import tpu as pltpu

F32 = jnp.float32
BF16 = jnp.bfloat16

HEAD_DIM = 64
GRID_W = 64
NA_ROWS = 8
NA_COLS = 16
SWA_WINDOW = 128
PEER_TOPK = 16
PEER_KEYS = 128
ROPE_BASE = 10000.0
NORM_EPS = 1e-6
GN_EPS = 1e-5
NEG_INF = -1e30
ATTN_SCALE = HEAD_DIM ** -0.5
INV_SQRT2 = 0.7071067811865476

LANES = 128
VMEM_LIMIT = 56 * 1024 * 1024

TOKEN_TILE = 512
PROJ_N_TILE = 512
RET_CHUNK = 128
NA_Q_ROWS = 8
NA_BAND_ROWS = 16
SWA_Q_TILE = 256
PEER_TOKEN_TILE = 512
PEER_EXPERT_TILE = 1024


def _dot(a, b):
    return jnp.dot(a, b, preferred_element_type=F32)


def _dot_nt(a, b):
    return lax.dot_general(a, b, (((1,), (1,)), ((), ())), preferred_element_type=F32)


def _dot_tn(a, b):
    return lax.dot_general(a, b, (((0,), (0,)), ((), ())), preferred_element_type=F32)


def _params(*sem):
    return pltpu.CompilerParams(dimension_semantics=sem, vmem_limit_bytes=VMEM_LIMIT)


def _rms_rows(x, gain):
    ms = jnp.mean(x * x, axis=-1, keepdims=True)
    return x * lax.rsqrt(ms + NORM_EPS) * gain


def _modulate(x, gain, shift, scale):
    return _rms_rows(x, gain) * (1.0 + scale) + shift


def _head_rms(x, gain):
    lane = lax.broadcasted_iota(jnp.int32, x.shape, 1)
    lo = lane < HEAD_DIM
    ss = x * x
    s_lo = jnp.sum(jnp.where(lo, ss, 0.0), axis=-1, keepdims=True)
    s_hi = jnp.sum(jnp.where(lo, 0.0, ss), axis=-1, keepdims=True)
    ms = jnp.where(lo, s_lo, s_hi) * (1.0 / HEAD_DIM)
    return x * lax.rsqrt(ms + NORM_EPS) * gain


def _rope(x, cos2, sin2):
    lane = lax.broadcasted_iota(jnp.int32, x.shape, 1)
    first_half = (lane & (HEAD_DIM // 2)) == 0
    swapped = jnp.where(first_half, pltpu.roll(x, LANES - HEAD_DIM // 2, axis=1),
                        pltpu.roll(x, HEAD_DIM // 2, axis=1))
    return x * cos2 + swapped * sin2


def _adaln_kernel(c_ref, w_ref, b_ref, o_ref):
    c = c_ref[...]
    s = c * jax.nn.sigmoid(c)
    w = w_ref[0]
    s_hi = s.astype(BF16)
    s_lo = (s - s_hi.astype(F32)).astype(BF16)
    w_hi = w.astype(BF16)
    w_lo = (w - w_hi.astype(F32)).astype(BF16)
    acc = _dot(s_hi, w_hi) + _dot(s_lo, w_hi) + _dot(s_hi, w_lo)
    o_ref[0] = acc + b_ref[0]


def _adaln(cond, mod_w, mod_b):
    depth, d, n = mod_w.shape
    tn = n // 4
    return pl.pallas_call(
        _adaln_kernel,
        grid=(depth, n // tn),
        in_specs=[pl.BlockSpec((8, d), lambda l, j: (0, 0)),
                  pl.BlockSpec((1, d, tn), lambda l, j: (l, 0, j)),
                  pl.BlockSpec((1, 1, tn), lambda l, j: (l, 0, j))],
        out_specs=pl.BlockSpec((1, 8, tn), lambda l, j: (l, 0, j)),
        out_shape=jax.ShapeDtypeStruct((depth, 8, n), F32),
        compiler_params=_params("arbitrary", "arbitrary"),
        name="adaln",
    )(cond, mod_w, mod_b.reshape(depth, 1, n))


def _modmm_kernel(x_ref, g_ref, sh_ref, sc_ref, w_ref, o_ref, h_scr):
    @pl.when(pl.program_id(1) == 0)
    def _():
        h_scr[...] = _modulate(x_ref[...], g_ref[...], sh_ref[0], sc_ref[0]).astype(BF16)

    o_ref[...] = _dot(h_scr[...], w_ref[...])


def _modmm(x, gain, shift, scale, w, seg):
    t, d = x.shape
    n = w.shape[1]
    tm = min(TOKEN_TILE, seg)
    tn = PROJ_N_TILE
    per_seg = seg // tm
    r = shift.shape[0]
    return pl.pallas_call(
        _modmm_kernel,
        grid=(t // tm, n // tn),
        in_specs=[pl.BlockSpec((tm, d), lambda i, j: (i, 0)),
                  pl.BlockSpec((1, d), lambda i, j: (0, 0)),
                  pl.BlockSpec((1, 1, d), lambda i, j: (i // per_seg, 0, 0)),
                  pl.BlockSpec((1, 1, d), lambda i, j: (i // per_seg, 0, 0)),
                  pl.BlockSpec((d, tn), lambda i, j: (0, j))],
        out_specs=pl.BlockSpec((tm, tn), lambda i, j: (i, j)),
        out_shape=jax.ShapeDtypeStruct((t, n), F32),
        scratch_shapes=[pltpu.VMEM((tm, d), BF16)],
        compiler_params=_params("arbitrary", "arbitrary"),
        name="modulate_matmul",
    )(x, gain.reshape(1, d), shift.reshape(r, 1, d), scale.reshape(r, 1, d), w)


def _outproj_kernel(*refs, n_in):
    a_refs = refs[:n_in]
    w_refs = refs[n_in:2 * n_in]
    x_ref, gate_ref, g2_ref, sh_ref, sc_ref, xo_ref, h_ref = refs[2 * n_in:]
    acc = None
    for a_ref, w_ref in zip(a_refs, w_refs):
        part = _dot(a_ref[...].astype(BF16), w_ref[...])
        acc = part if acc is None else acc + part
    xn = x_ref[...] + gate_ref[0] * acc
    xo_ref[...] = xn
    h_ref[...] = _modulate(xn, g2_ref[...], sh_ref[0], sc_ref[0]).astype(BF16)


def _outproj(a_list, w_list, x, gate, gain2, shift2, scale2, seg):
    t, d = x.shape
    tm = min(TOKEN_TILE, seg)
    per_seg = seg // tm
    r = gate.shape[0]
    n_in = len(a_list)
    row = lambda i: (i // per_seg, 0, 0)
    in_specs = ([pl.BlockSpec((tm, a.shape[1]), lambda i: (i, 0)) for a in a_list]
                + [pl.BlockSpec(w.shape, lambda i: (0, 0)) for w in w_list]
                + [pl.BlockSpec((tm, d), lambda i: (i, 0)),
                   pl.BlockSpec((1, 1, d), row),
                   pl.BlockSpec((1, d), lambda i: (0, 0)),
                   pl.BlockSpec((1, 1, d), row),
                   pl.BlockSpec((1, 1, d), row)])
    return pl.pallas_call(
        functools.partial(_outproj_kernel, n_in=n_in),
        grid=(t // tm,),
        in_specs=in_specs,
        out_specs=[pl.BlockSpec((tm, d), lambda i: (i, 0)), pl.BlockSpec((tm, d), lambda i: (i, 0))],
        out_shape=[jax.ShapeDtypeStruct((t, d), F32), jax.ShapeDtypeStruct((t, d), BF16)],
        compiler_params=_params("arbitrary"),
        name="out_proj_residual",
    )(*a_list, *w_list, x, gate.reshape(r, 1, d), gain2.reshape(1, d),
      shift2.reshape(r, 1, d), scale2.reshape(r, 1, d))


def _softmax_pv(s_list, v_list, extra=None):
    m = s_list[0].max(axis=-1, keepdims=True)
    for s in s_list[1:]:
        m = jnp.maximum(m, s.max(axis=-1, keepdims=True))
    if extra is not None:
        m = jnp.maximum(m, extra)
    denom = None
    out = None
    for s, v in zip(s_list, v_list):
        p = jnp.exp(s - m)
        ps = p.sum(axis=-1, keepdims=True)
        denom = ps if denom is None else denom + ps
        pv = _dot(p.astype(BF16), v)
        out = pv if out is None else out + pv
    if extra is not None:
        denom = denom + jnp.exp(extra - m)
    return out / denom


def _na_kernel(q_ref, k_ref, v_ref, kc_ref, vc_ref, bias_ref, qn_ref, kn_ref, o_ref, *, n_steps):
    i = pl.program_id(2)
    band = NA_BAND_ROWS * GRID_W
    start = jnp.clip(NA_Q_ROWS * i - NA_ROWS // 2, 0, NA_Q_ROWS * n_steps - NA_BAND_ROWS) * GRID_W
    start = pl.multiple_of(start, NA_ROWS // 2 * GRID_W)
    q = (_head_rms(q_ref[0], qn_ref[...]) * ATTN_SCALE).astype(BF16)
    kb = _head_rms(k_ref[0, pl.ds(start, band), :], kn_ref[...]).astype(BF16)
    vb = v_ref[0, pl.ds(start, band), :].astype(BF16)
    kc = _head_rms(kc_ref[0], kn_ref[...]).astype(BF16)
    vc = vc_ref[0].astype(BF16)
    outs = []
    for h in range(2):
        sl = slice(h * HEAD_DIM, (h + 1) * HEAD_DIM)
        s_loc = _dot_nt(q[:, sl], kb[:, sl]) + bias_ref[0, h]
        s_ctx = _dot_nt(q[:, sl], kc[:, sl])
        outs.append(_softmax_pv([s_loc, s_ctx], [vb[:, sl], vc[:, sl]]))
    o_ref[0] = jnp.concatenate(outs, axis=-1)


def _na_bias(rpb, rows):
    h = rpb.shape[0]
    n_steps = rows // NA_Q_ROWS
    col = np.arange(GRID_W)
    c0 = np.clip(col - NA_COLS // 2, 0, GRID_W - NA_COLS)
    dc = col[None, :] - col[:, None] + (NA_COLS - 1)
    ok_c = (col[None, :] >= c0[:, None]) & (col[None, :] < c0[:, None] + NA_COLS)
    pick = ((np.arange(2 * NA_COLS - 1)[:, None, None] == dc[None]) & ok_c[None]).astype(np.float32)
    blocks = jnp.einsum('hrd,dqk->hrqk', rpb, pick, precision=lax.Precision.HIGHEST)
    blocks = jnp.where(ok_c, blocks, NEG_INF)
    masked = jnp.full((h, GRID_W, GRID_W), NEG_INF, F32)
    classes = []
    for step in (0, 1, n_steps - 1):
        band0 = min(max(NA_Q_ROWS * step - NA_ROWS // 2, 0), rows - NA_BAND_ROWS)
        q_rows = []
        for rq in range(NA_Q_ROWS):
            r = NA_Q_ROWS * step + rq
            r0 = min(max(r - NA_ROWS // 2, 0), rows - NA_ROWS)
            row = []
            for rk in range(band0, band0 + NA_BAND_ROWS):
                row.append(blocks[:, rk - r + NA_ROWS - 1] if r0 <= rk < r0 + NA_ROWS else masked)
            q_rows.append(jnp.concatenate(row, axis=-1))
        classes.append(jnp.concatenate(q_rows, axis=-2))
    return jnp.stack(classes)


def _na_attention(p_lat, p_ctx, bias, qn, kn, n_pairs):
    b, s, _ = p_lat.shape
    l = p_ctx.shape[1]
    rows = s // GRID_W
    n_steps = rows // NA_Q_ROWS
    qt = NA_Q_ROWS * GRID_W

    def cls(i):
        return jnp.where(i == 0, 0, jnp.where(i == n_steps - 1, 2, 1))

    return pl.pallas_call(
        functools.partial(_na_kernel, n_steps=n_steps),
        grid=(n_pairs, b, n_steps),
        in_specs=[pl.BlockSpec((1, qt, LANES), lambda hp, bb, i: (bb, i, hp)),
                  pl.BlockSpec((1, s, LANES), lambda hp, bb, i: (bb, 0, n_pairs + hp)),
                  pl.BlockSpec((1, s, LANES), lambda hp, bb, i: (bb, 0, 2 * n_pairs + hp)),
                  pl.BlockSpec((1, l, LANES), lambda hp, bb, i: (bb, 0, n_pairs + hp)),
                  pl.BlockSpec((1, l, LANES), lambda hp, bb, i: (bb, 0, 2 * n_pairs + hp)),
                  pl.BlockSpec((1, 2, qt, NA_BAND_ROWS * GRID_W), lambda hp, bb, i: (cls(i), hp, 0, 0)),
                  pl.BlockSpec((1, LANES), lambda hp, bb, i: (0, 0)),
                  pl.BlockSpec((1, LANES), lambda hp, bb, i: (0, 0))],
        out_specs=pl.BlockSpec((1, qt, LANES), lambda hp, bb, i: (bb, i, hp)),
        out_shape=jax.ShapeDtypeStruct((b, s, n_pairs * LANES), F32),
        compiler_params=_params("arbitrary", "arbitrary", "arbitrary"),
        name="neighbourhood_attention",
    )(p_lat, p_lat, p_lat, p_ctx, p_ctx, bias, qn, kn)


def _ctx_attn_kernel(q_ref, k_ref, v_ref, qn_ref, kn_ref, o_ref):
    q = (_head_rms(q_ref[0], qn_ref[...]) * ATTN_SCALE).astype(BF16)
    k = _head_rms(k_ref[0], kn_ref[...]).astype(BF16)
    v = v_ref[0].astype(BF16)
    outs = []
    for h in range(2):
        sl = slice(h * HEAD_DIM, (h + 1) * HEAD_DIM)
        outs.append(_softmax_pv([_dot_nt(q[:, sl], k[:, sl])], [v[:, sl]]))
    o_ref[0] = jnp.concatenate(outs, axis=-1)


def _ctx_attention(p_ctx, qn, kn, n_pairs):
    b, l, _ = p_ctx.shape
    return pl.pallas_call(
        _ctx_attn_kernel,
        grid=(n_pairs, b),
        in_specs=[pl.BlockSpec((1, l, LANES), lambda hp, bb: (bb, 0, hp)),
                  pl.BlockSpec((1, l, LANES), lambda hp, bb: (bb, 0, n_pairs + hp)),
                  pl.BlockSpec((1, l, LANES), lambda hp, bb: (bb, 0, 2 * n_pairs + hp)),
                  pl.BlockSpec((1, LANES), lambda hp, bb: (0, 0)),
                  pl.BlockSpec((1, LANES), lambda hp, bb: (0, 0))],
        out_specs=pl.BlockSpec((1, l, LANES), lambda hp, bb: (bb, 0, hp)),
        out_shape=jax.ShapeDtypeStruct((b, l, n_pairs * LANES), F32),
        compiler_params=_params("arbitrary", "arbitrary"),
        name="context_attention",
    )(p_ctx, p_ctx, p_ctx, qn, kn)


def _ret_kernel(q_ref, k_ref, v_ref, g_ref, cos_ref, sin_ref, lg_ref, s0_ref, y_ref, st_ref, sf_scr,
                *, n_chunks, use_rope):
    c = RET_CHUNK
    hd = HEAD_DIM
    lg = -jnp.exp(lg_ref[0])
    lgf, lgb = lg[0:1, :], lg[1:2, :]
    ii = lax.broadcasted_iota(jnp.int32, (c, LANES), 0).astype(F32)
    dq_f = jnp.exp(lgf * (ii + 1.0))
    dk_f = jnp.exp(lgf * (c - 1.0 - ii))
    dq_b = jnp.exp(lgb * (c - ii))
    dk_b = jnp.exp(lgb * ii)
    dc_f = jnp.exp(lgf * float(c))
    dc_b = jnp.exp(lgb * float(c))
    diff = (lax.broadcasted_iota(jnp.int32, (c, c), 0) - lax.broadcasted_iota(jnp.int32, (c, c), 1)).astype(F32)
    intra = []
    chunk_f = []
    chunk_b = []
    for h in range(2):
        lf = lgf[:, h * hd:h * hd + 1]
        lb = lgb[:, h * hd:h * hd + 1]
        intra.append(jnp.where(diff >= 0, jnp.exp(lf * jnp.maximum(diff, 0.0)),
                               jnp.exp(lb * jnp.maximum(-diff, 0.0))))
        chunk_f.append(dc_f[:, h * hd:h * hd + 1])
        chunk_b.append(dc_b[:, h * hd:h * hd + 1])

    def load(n):
        r = pl.multiple_of(n * c, c)
        q = q_ref[0, pl.ds(r, c), :]
        k = k_ref[0, pl.ds(r, c), :]
        v = v_ref[0, pl.ds(r, c), :]
        if use_rope:
            cs = cos_ref[pl.ds(r, c), :]
            sn = sin_ref[pl.ds(r, c), :]
            q = _rope(q, cs, sn)
            k = _rope(k, cs, sn)
        return r, q * ATTN_SCALE, k, v.astype(BF16)

    def fwd(n, carry):
        _, _, k, v = load(n)
        kd = (k * dk_f).astype(BF16)
        new = []
        for h in range(2):
            sl = slice(h * hd, (h + 1) * hd)
            sf_scr[n, h] = carry[h]
            new.append(carry[h] * chunk_f[h] + _dot_tn(kd[:, sl], v[:, sl]))
        return tuple(new)

    sf = lax.fori_loop(0, n_chunks, fwd, (s0_ref[0, 0, 0], s0_ref[0, 0, 1]))
    st_ref[0, 0, 0] = sf[0]
    st_ref[0, 0, 1] = sf[1]

    def bwd(jj, carry):
        n = n_chunks - 1 - jj
        r, q, k, v = load(n)
        qb = q.astype(BF16)
        kb = k.astype(BF16)
        qf = (q * dq_f).astype(BF16)
        qr = (q * dq_b).astype(BF16)
        kd = (k * dk_b).astype(BF16)
        outs = []
        new = []
        for h in range(2):
            sl = slice(h * hd, (h + 1) * hd)
            a = (_dot_nt(qb[:, sl], kb[:, sl]) * intra[h]).astype(BF16)
            o = (_dot(a, v[:, sl]) + _dot(qf[:, sl], sf_scr[n, h].astype(BF16))
                 + _dot(qr[:, sl], carry[h].astype(BF16)))
            oc = o - jnp.mean(o, axis=-1, keepdims=True)
            outs.append(oc * lax.rsqrt(jnp.mean(oc * oc, axis=-1, keepdims=True) + GN_EPS))
            new.append(carry[h] * chunk_b[h] + _dot_tn(kd[:, sl], v[:, sl]))
        g = g_ref[0, pl.ds(r, c), :]
        y_ref[0, pl.ds(r, c), :] = jnp.concatenate(outs, axis=-1) * (g * jax.nn.sigmoid(g))
        return tuple(new)

    sb = lax.fori_loop(0, n_chunks, bwd, (s0_ref[0, 0, 2], s0_ref[0, 0, 3]))
    st_ref[0, 0, 2] = sb[0]
    st_ref[0, 0, 3] = sb[1]


def _retention(p, cos2, sin2, lg, s0, col0, n_pairs, use_rope):
    b, t, _ = p.shape
    n_chunks = t // RET_CHUNK
    tab = lambda hp, bb: (0, 0)
    return pl.pallas_call(
        functools.partial(_ret_kernel, n_chunks=n_chunks, use_rope=use_rope),
        grid=(n_pairs, b),
        in_specs=[pl.BlockSpec((1, t, LANES), lambda hp, bb: (bb, 0, col0 + hp)),
                  pl.BlockSpec((1, t, LANES), lambda hp, bb: (bb, 0, col0 + n_pairs + hp)),
                  pl.BlockSpec((1, t, LANES), lambda hp, bb: (bb, 0, col0 + 2 * n_pairs + hp)),
                  pl.BlockSpec((1, t, LANES), lambda hp, bb: (bb, 0, col0 + 3 * n_pairs + hp)),
                  pl.BlockSpec(cos2.shape, tab),
                  pl.BlockSpec(sin2.shape, tab),
                  pl.BlockSpec((1, 2, LANES), lambda hp, bb: (hp, 0, 0)),
                  pl.BlockSpec((1, 1, 4, HEAD_DIM, HEAD_DIM), lambda hp, bb: (bb, hp, 0, 0, 0))],
        out_specs=[pl.BlockSpec((1, t, LANES), lambda hp, bb: (bb, 0, hp)),
                   pl.BlockSpec((1, 1, 4, HEAD_DIM, HEAD_DIM), lambda hp, bb: (bb, hp, 0, 0, 0))],
        out_shape=[jax.ShapeDtypeStruct((b, t, n_pairs * LANES), F32),
                   jax.ShapeDtypeStruct((b, n_pairs, 4, HEAD_DIM, HEAD_DIM), F32)],
        scratch_shapes=[pltpu.VMEM((n_chunks, 2, HEAD_DIM, HEAD_DIM), F32)],
        compiler_params=_params("arbitrary", "arbitrary"),
        name="retention",
    )(p, p, p, p, cos2, sin2, lg, s0)


def _swa_kernel(q_ref, k_ref, v_ref, kc_ref, vc_ref, cos_ref, sin_ref, qn_ref, kn_ref, sink_ref, o_ref,
                *, seq):
    qt = SWA_Q_TILE
    wk = qt + 2 * SWA_WINDOW
    hd = HEAD_DIM
    n = pl.program_id(2)
    q0 = pl.multiple_of(n * qt, qt)
    ws = pl.multiple_of(jnp.clip(n * qt - SWA_WINDOW, 0, seq - wk), SWA_WINDOW)
    kw = _rope(_head_rms(k_ref[0, pl.ds(ws, wk), :], kn_ref[...]),
               cos_ref[pl.ds(ws, wk), :], sin_ref[pl.ds(ws, wk), :]).astype(BF16)
    vw = v_ref[0, pl.ds(ws, wk), :].astype(BF16)
    kc = _head_rms(kc_ref[0], kn_ref[...]).astype(BF16)
    vc = vc_ref[0].astype(BF16)
    cos_q = cos_ref[pl.ds(q0, qt), :]
    sin_q = sin_ref[pl.ds(q0, qt), :]
    qs = []
    for s in range(4):
        slab = q_ref[0, :, s * LANES:(s + 1) * LANES]
        qs.append((_rope(_head_rms(slab, qn_ref[...]), cos_q, sin_q) * ATTN_SCALE).astype(BF16))
    qpos = q0 + (lax.broadcasted_iota(jnp.int32, (4 * qt, wk), 0) & (qt - 1))
    kpos = ws + lax.broadcasted_iota(jnp.int32, (4 * qt, wk), 1)
    near4 = jnp.abs(qpos - kpos) <= SWA_WINDOW
    for kh in range(2):
        sl = slice(kh * hd, (kh + 1) * hd)
        qstack = jnp.concatenate(
            [qs[kh * 2 + g // 2][:, (g % 2) * hd:(g % 2 + 1) * hd] for g in range(4)], axis=0)
        sink = jnp.concatenate(
            [jnp.broadcast_to(sink_ref[kh * 4 + g:kh * 4 + g + 1, 0:1], (qt, 1)) for g in range(4)], axis=0)
        s_loc = jnp.where(near4, _dot_nt(qstack, kw[:, sl]), NEG_INF)
        s_ctx = _dot_nt(qstack, kc[:, sl])
        o = _softmax_pv([s_loc, s_ctx], [vw[:, sl], vc[:, sl]], extra=sink)
        for pair in range(2):
            col = (kh * 2 + pair) * LANES
            o_ref[0, :, col:col + LANES] = jnp.concatenate(
                [o[(2 * pair) * qt:(2 * pair + 1) * qt], o[(2 * pair + 1) * qt:(2 * pair + 2) * qt]], axis=-1)


def _swa_attention(p_lat, p_ctx, cos2, sin2, qn, kn, sink_rows, n_q_heads, n_kv_heads):
    b, s, _ = p_lat.shape
    l = p_ctx.shape[1]
    kv_pairs = n_kv_heads // 2
    q_blocks = n_q_heads * HEAD_DIM // LANES
    q_per_pair = q_blocks // kv_pairs
    qw = q_per_pair * LANES
    tab = lambda kp, bb, n: (0, 0)
    return pl.pallas_call(
        functools.partial(_swa_kernel, seq=s),
        grid=(kv_pairs, b, s // SWA_Q_TILE),
        in_specs=[pl.BlockSpec((1, SWA_Q_TILE, qw), lambda kp, bb, n: (bb, n, kp)),
                  pl.BlockSpec((1, s, LANES), lambda kp, bb, n: (bb, 0, q_blocks + kp)),
                  pl.BlockSpec((1, s, LANES), lambda kp, bb, n: (bb, 0, q_blocks + kv_pairs + kp)),
                  pl.BlockSpec((1, l, LANES), lambda kp, bb, n: (bb, 0, q_blocks + kp)),
                  pl.BlockSpec((1, l, LANES), lambda kp, bb, n: (bb, 0, q_blocks + kv_pairs + kp)),
                  pl.BlockSpec(cos2.shape, tab),
                  pl.BlockSpec(sin2.shape, tab),
                  pl.BlockSpec((1, LANES), tab),
                  pl.BlockSpec((1, LANES), tab),
                  pl.BlockSpec((8, LANES), lambda kp, bb, n: (kp, 0))],
        out_specs=pl.BlockSpec((1, SWA_Q_TILE, qw), lambda kp, bb, n: (bb, n, kp)),
        out_shape=jax.ShapeDtypeStruct((b, s, n_q_heads * HEAD_DIM), F32),
        compiler_params=_params("arbitrary", "arbitrary", "arbitrary"),
        name="windowed_gqa",
    )(p_lat, p_lat, p_lat, p_ctx, p_ctx, cos2, sin2, qn, kn, sink_rows)


def _top_rows(x, k, scr, want_rank=False):
    work = x
    rank = jnp.full(x.shape, float(PEER_TOPK), F32) if want_rank else None
    for r in range(k):
        m = jnp.max(work, axis=0, keepdims=True)
        scr[r:r + 1, :] = m
        hit = work == m
        if want_rank and r < PEER_TOPK:
            rank = jnp.where(hit, float(r), rank)
        if r + 1 < k:
            work = jnp.where(hit, NEG_INF, work)
    return rank


def _count_at_least(sorted_scr, y):
    row = lambda i: sorted_scr[i:i + 1, :]
    c8 = row(7) >= y
    c4 = jnp.where(c8, row(11), row(3)) >= y
    c2 = jnp.where(c8, jnp.where(c4, row(13), row(9)), jnp.where(c4, row(5), row(1))) >= y
    hi = jnp.where(c4, jnp.where(c2, row(14), row(12)), jnp.where(c2, row(10), row(8)))
    lo = jnp.where(c4, jnp.where(c2, row(6), row(4)), jnp.where(c2, row(2), row(0)))
    c1 = jnp.where(c8, hi, lo) >= y
    count = (jnp.where(c8, 8.0, 0.0) + jnp.where(c4, 4.0, 0.0)) + (jnp.where(c2, 2.0, 0.0) + jnp.where(c1, 1.0, 0.0))
    return count + jnp.where(row(15) >= y, 1.0, 0.0)


def _peer_scores_kernel(h_ref, wq_ref, keys_ref, ra_ref, p1_ref, gb_ref, p2_ref, a_scr, b_scr, c_scr, v_scr,
                        *, n_heads):
    k = PEER_TOPK
    hb = h_ref[...]
    for h in range(n_heads):
        s = []
        for p in range(2):
            col = (2 * h + p) * PEER_KEYS
            qhp = _dot(hb, wq_ref[:, col:col + PEER_KEYS]).astype(BF16)
            s.append(_dot_nt(keys_ref[2 * h + p], qhp))
        rank_a = _top_rows(s[0], k + 1, a_scr, want_rank=True)
        _top_rows(s[1], k + 1, b_scr)
        c_scr[0:16, :] = a_scr[0:1, :] + b_scr[0:16, :]
        for i in range(1, 8):
            c_scr[8 + 8 * i:16 + 8 * i, :] = a_scr[i:i + 1, :] + b_scr[0:8, :]
        c_scr[72:80, :] = a_scr[8:16, :] + b_scr[0:1, :]
        c_scr[80:81, :] = a_scr[0:1, :] + b_scr[16:17, :]
        c_scr[81:82, :] = a_scr[16:17, :] + b_scr[0:1, :]
        c_scr[82:88, :] = jnp.full((6, c_scr.shape[1]), NEG_INF, F32)
        _top_rows(c_scr[...], k + 1, v_scr)
        thr = 0.5 * (v_scr[k - 1:k, :] + v_scr[k:k + 1, :])
        z = jnp.sum(jnp.exp(v_scr[0:k, :] - v_scr[0:1, :]), axis=0, keepdims=True)
        ra_ref[h] = rank_a
        p1_ref[h] = jnp.exp(s[0] - a_scr[0:1, :]) / z
        gb_ref[h] = _count_at_least(a_scr, thr - s[1]).astype(BF16)
        p2_ref[h] = jnp.exp(s[1] - b_scr[0:1, :]).astype(BF16)


def _peer_scores(h2, wq, keys, n_heads):
    t, d = h2.shape
    tt = PEER_TOKEN_TILE
    tab_shape = (n_heads, PEER_KEYS, t)
    tab_spec = pl.BlockSpec((n_heads, PEER_KEYS, tt), lambda i: (0, 0, i))
    return pl.pallas_call(
        functools.partial(_peer_scores_kernel, n_heads=n_heads),
        grid=(t // tt,),
        in_specs=[pl.BlockSpec((tt, d), lambda i: (i, 0)),
                  pl.BlockSpec(wq.shape, lambda i: (0, 0)),
                  pl.BlockSpec(keys.shape, lambda i: (0, 0, 0))],
        out_specs=[tab_spec] * 4,
        out_shape=[jax.ShapeDtypeStruct(tab_shape, dt) for dt in (F32, F32, BF16, BF16)],
        scratch_shapes=[pltpu.VMEM((24, tt), F32), pltpu.VMEM((24, tt), F32),
                        pltpu.VMEM((88, tt), F32), pltpu.VMEM((24, tt), F32)],
        compiler_params=_params("arbitrary"),
        name="peer_scores",
    )(h2, wq, keys)


def _peer_mix_kernel(h_ref, u_ref, vt_ref, ra_ref, p1_ref, gb_ref, p2_ref, x_ref, g_ref, o_ref,
                     acc_scr, act_scr, a_scr, *, n_heads, n_tiles):
    s = pl.program_id(1)
    tt = h_ref.shape[0]
    rows_per_tile = PEER_EXPERT_TILE // PEER_KEYS
    tile = (PEER_KEYS, LANES)

    def readout():
        acc_scr[...] += _dot(vt_ref[...], a_scr[...])

    def gates():
        for aa in range(rows_per_tile):
            rs = slice(aa * PEER_KEYS, (aa + 1) * PEER_KEYS)
            for lb in range(tt // LANES):
                ls = slice(lb * LANES, (lb + 1) * LANES)
                w = None
                for h in range(n_heads):
                    rank = jnp.broadcast_to(ra_ref[h, aa:aa + 1, ls].astype(BF16), tile)
                    p1 = jnp.broadcast_to(p1_ref[h, aa:aa + 1, ls].astype(BF16), tile)
                    term = jnp.where(rank < gb_ref[h, :, ls], p2_ref[h, :, ls], jnp.zeros(tile, BF16)) * p1
                    w = term if w is None else w + term
                act = act_scr[rs, ls]
                gelu = 0.5 * act * (1.0 + lax.erf(act * INV_SQRT2))
                a_scr[rs, ls] = w * gelu.astype(BF16)

    def experts():
        act_scr[...] = _dot_nt(u_ref[...], h_ref[...])

    @pl.when(s == 0)
    def _():
        acc_scr[...] = jnp.zeros_like(acc_scr)
        act_scr[...] = jnp.zeros_like(act_scr)
        a_scr[...] = jnp.zeros_like(a_scr)

    @pl.when(s < n_tiles)
    def _():
        readout()
        gates()
        experts()

    @pl.when(s == n_tiles)
    def _():
        readout()
        gates()

    @pl.when(s == n_tiles + 1)
    def _():
        readout()
        o_ref[...] = x_ref[...] + g_ref[0] * acc_scr[...].T


def _peer_mix(h2, u, vt, tables, x, gate, seg, n_heads):
    t, d = h2.shape
    n_exp = u.shape[0]
    tt = PEER_TOKEN_TILE
    te = PEER_EXPERT_TILE
    n_tiles = n_exp // te
    per_seg = seg // tt
    r = gate.shape[0]
    last = n_tiles - 1
    tab_spec = pl.BlockSpec((n_heads, PEER_KEYS, tt), lambda i, s: (0, 0, i))
    row_spec = pl.BlockSpec((n_heads, te // PEER_KEYS, tt), lambda i, s: (0, jnp.clip(s - 1, 0, last), i))
    return pl.pallas_call(
        functools.partial(_peer_mix_kernel, n_heads=n_heads, n_tiles=n_tiles),
        grid=(t // tt, n_tiles + 2),
        in_specs=[pl.BlockSpec((tt, d), lambda i, s: (i, 0)),
                  pl.BlockSpec((te, d), lambda i, s: (jnp.minimum(s, last), 0)),
                  pl.BlockSpec((d, te), lambda i, s: (0, jnp.clip(s - 2, 0, last))),
                  row_spec, row_spec, tab_spec, tab_spec,
                  pl.BlockSpec((tt, d), lambda i, s: (i, 0)),
                  pl.BlockSpec((1, 1, d), lambda i, s: (i // per_seg, 0, 0))],
        out_specs=pl.BlockSpec((tt, d), lambda i, s: (i, 0)),
        out_shape=jax.ShapeDtypeStruct((t, d), F32),
        scratch_shapes=[pltpu.VMEM((d, tt), F32), pltpu.VMEM((te, tt), F32), pltpu.VMEM((te, tt), BF16)],
        compiler_params=_params("arbitrary", "arbitrary"),
        name="peer_mix",
    )(h2, u, vt, *tables, x, gate.reshape(r, 1, d))


def _peer(h2, x, gate, seg, wq, keys, u, vt):
    n_heads = keys.shape[0] // 2
    tables = _peer_scores(h2, wq, keys, n_heads)
    return _peer_mix(h2, u, vt, tables, x, gate, seg, n_heads)


def _rope_tables(n_tokens):
    t = jnp.arange(n_tokens)
    row = (t // GRID_W).astype(F32)
    col = (t % GRID_W).astype(F32)
    n_freq = HEAD_DIM // 4
    inv_freq = jnp.power(ROPE_BASE, -jnp.arange(n_freq, dtype=F32) / n_freq)
    ang = jnp.concatenate([row[:, None] * inv_freq, col[:, None] * inv_freq], axis=-1)
    cos, sin = jnp.cos(ang), jnp.sin(ang)
    cos2 = jnp.tile(jnp.concatenate([cos, cos], axis=-1), (1, 2))
    sin2 = jnp.tile(jnp.concatenate([-sin, sin], axis=-1), (1, 2))
    return cos2, sin2


def _pair_lanes(v):
    return jnp.repeat(v, HEAD_DIM, axis=-1).reshape(*v.shape[:-1], v.shape[-1] // 2, LANES)


def kernel(x, c, ctx, c_ctx, mod_w, mod_b, norm1_g, norm2_g, ab_w_in, ab_w_out, na_q_norm, na_k_norm,
           na_rpb, ret_log_decay, swa_w_in, swa_w_out, swa_q_norm, swa_k_norm, swa_sink,
           peer_w_q, peer_sub_keys, peer_u, peer_v):
    b, s, d = x.shape
    l = ctx.shape[1]
    depth = mod_w.shape[0]
    assert depth == 2 and b + 1 <= 8
    assert s % (NA_Q_ROWS * GRID_W) == 0 and s // GRID_W >= NA_BAND_ROWS
    assert s % TOKEN_TILE == 0 and (b * l) % TOKEN_TILE == 0 and l % RET_CHUNK == 0

    cond = jnp.concatenate([c, c_ctx[None, :], jnp.zeros((8 - b - 1, d), F32)], axis=0)
    mods = _adaln(cond, mod_w, mod_b).reshape(depth, 8, 6, d)
    lat = lambda layer, which: mods[layer, :b, which]
    cx = lambda layer, which: mods[layer, b:b + 1, which]

    cos2, sin2 = _rope_tables(s)
    x_lat = x.reshape(b * s, d)
    x_ctx = ctx.reshape(b * l, d)
    tile2 = lambda g: jnp.tile(g, 2).reshape(1, LANES)

    def peer_weights(layer):
        n_heads = peer_sub_keys.shape[1]
        keys = peer_sub_keys[layer].reshape(2 * n_heads, PEER_KEYS, -1).astype(BF16)
        return (peer_w_q[layer].astype(BF16), keys, peer_u[layer].astype(BF16),
                peer_v[layer].T.astype(BF16))

    n_na = na_rpb.shape[1]
    n_ret = ret_log_decay.shape[2]
    na_pairs, ret_pairs = n_na // 2, n_ret // 2
    w_in = ab_w_in[0].astype(BF16)
    w_out = ab_w_out[0].astype(BF16)
    wa = n_na * HEAD_DIM
    p_lat = _modmm(x_lat, norm1_g[0], lat(0, 0), lat(0, 1), w_in, s).reshape(b, s, -1)
    p_ctx = _modmm(x_ctx, norm1_g[0], cx(0, 0), cx(0, 1), w_in, b * l).reshape(b, l, -1)

    qn, kn = tile2(na_q_norm[0]), tile2(na_k_norm[0])
    oa_lat = _na_attention(p_lat, p_ctx, _na_bias(na_rpb[0], s // GRID_W), qn, kn, na_pairs)
    oa_ctx = _ctx_attention(p_ctx, qn, kn, na_pairs)

    lg = _pair_lanes(ret_log_decay[0]).transpose(1, 0, 2)
    ret_col0 = 3 * wa // LANES
    zeros_state = jnp.zeros((b, ret_pairs, 4, HEAD_DIM, HEAD_DIM), F32)
    ones_tab, zeros_tab = jnp.ones((l, LANES), F32), jnp.zeros((l, LANES), F32)
    ob_ctx, st_ctx = _retention(p_ctx, ones_tab, zeros_tab, lg, zeros_state, ret_col0, ret_pairs, False)
    ob_lat, _ = _retention(p_lat, cos2, sin2, lg, st_ctx, ret_col0, ret_pairs, True)

    w_list = [w_out[:wa], w_out[wa:]]
    x_lat, h_lat = _outproj([oa_lat.reshape(b * s, -1), ob_lat.reshape(b * s, -1)], w_list, x_lat,
                            lat(0, 2), norm2_g[0], lat(0, 3), lat(0, 4), s)
    x_ctx, h_ctx = _outproj([oa_ctx.reshape(b * l, -1), ob_ctx.reshape(b * l, -1)], w_list, x_ctx,
                            cx(0, 2), norm2_g[0], cx(0, 3), cx(0, 4), b * l)
    pw = peer_weights(0)
    x_lat = _peer(h_lat, x_lat, lat(0, 5), s, *pw)
    x_ctx = _peer(h_ctx, x_ctx, cx(0, 5), b * l, *pw)

    n_q = swa_sink.shape[1]
    n_kv = (swa_w_in.shape[2] // HEAD_DIM - n_q) // 2
    w_in = swa_w_in[0].astype(BF16)
    p_lat = _modmm(x_lat, norm1_g[1], lat(1, 0), lat(1, 1), w_in, s).reshape(b, s, -1)
    p_ctx = _modmm(x_ctx, norm1_g[1], cx(1, 0), cx(1, 1), w_in, b * l).reshape(b, l, -1)
    sink_rows = jnp.broadcast_to(swa_sink[0][:, None], (n_q, LANES))
    o_lat = _swa_attention(p_lat, p_ctx, cos2, sin2, tile2(swa_q_norm[0]), tile2(swa_k_norm[0]),
                           sink_rows, n_q, n_kv)
    x_lat, h_lat = _outproj([o_lat.reshape(b * s, -1)], [swa_w_out[0].astype(BF16)], x_lat,
                            lat(1, 2), norm2_g[1], lat(1, 3), lat(1, 4), s)
    x_lat = _peer(h_lat, x_lat, lat(1, 5), s, *peer_weights(1))
    return x_lat.reshape(b, s, d)
```

```python
import functools

import numpy as np
import jax
import jax.numpy as jnp
from jax import lax
from jax.experimental import pallas as pl
from jax.experimental.pallas import tpu as pltpu

F32 = jnp.float32
BF16 = jnp.bfloat16

HEAD_DIM = 64
GRID_W = 64
NA_ROWS = 8
NA_COLS = 16
SWA_WINDOW = 128
PEER_TOPK = 16
PEER_KEYS = 128
ROPE_BASE = 10000.0
NORM_EPS = 1e-6
GN_EPS = 1e-5
NEG_INF = -1e30
ATTN_SCALE = HEAD_DIM ** -0.5
INV_SQRT2 = 0.7071067811865476

LANES = 128
VMEM_LIMIT = 56 * 1024 * 1024

TOKEN_TILE = 512
PROJ_N_TILE = 512
RET_CHUNK = 128
NA_Q_ROWS = 8
NA_BAND_ROWS = 16
SWA_Q_TILE = 256
PEER_TOKEN_TILE = 512
PEER_EXPERT_TILE = 1024


def _dot(a, b):
    return jnp.dot(a, b, preferred_element_type=F32)


def _dot_nt(a, b):
    return lax.dot_general(a, b, (((1,), (1,)), ((), ())), preferred_element_type=F32)


def _dot_tn(a, b):
    return lax.dot_general(a, b, (((0,), (0,)), ((), ())), preferred_element_type=F32)


def _params(*sem):
    return pltpu.CompilerParams(dimension_semantics=sem, vmem_limit_bytes=VMEM_LIMIT)


def _rms_rows(x, gain):
    ms = jnp.mean(x * x, axis=-1, keepdims=True)
    return x * lax.rsqrt(ms + NORM_EPS) * gain


def _modulate(x, gain, shift, scale):
    return _rms_rows(x, gain) * (1.0 + scale) + shift


def _head_rms(x, gain):
    lane = lax.broadcasted_iota(jnp.int32, x.shape, 1)
    lo = lane < HEAD_DIM
    ss = x * x
    s_lo = jnp.sum(jnp.where(lo, ss, 0.0), axis=-1, keepdims=True)
    s_hi = jnp.sum(jnp.where(lo, 0.0, ss), axis=-1, keepdims=True)
    ms = jnp.where(lo, s_lo, s_hi) * (1.0 / HEAD_DIM)
    return x * lax.rsqrt(ms + NORM_EPS) * gain


def _rope(x, cos2, sin2):
    lane = lax.broadcasted_iota(jnp.int32, x.shape, 1)
    first_half = (lane & (HEAD_DIM // 2)) == 0
    swapped = jnp.where(first_half, pltpu.roll(x, LANES - HEAD_DIM // 2, axis=1),
                        pltpu.roll(x, HEAD_DIM // 2, axis=1))
    return x * cos2 + swapped * sin2


def _adaln_kernel(c_ref, w_ref, b_ref, o_ref):
    c = c_ref[...]
    s = c * jax.nn.sigmoid(c)
    w = w_ref[0]
    s_hi = s.astype(BF16)
    s_lo = (s - s_hi.astype(F32)).astype(BF16)
    w_hi = w.astype(BF16)
    w_lo = (w - w_hi.astype(F32)).astype(BF16)
    acc = _dot(s_hi, w_hi) + _dot(s_lo, w_hi) + _dot(s_hi, w_lo)
    o_ref[0] = acc + b_ref[0]


def _adaln(cond, mod_w, mod_b):
    depth, d, n = mod_w.shape
    tn = n // 4
    return pl.pallas_call(
        _adaln_kernel,
        grid=(depth, n // tn),
        in_specs=[pl.BlockSpec((8, d), lambda l, j: (0, 0)),
                  pl.BlockSpec((1, d, tn), lambda l, j: (l, 0, j)),
                  pl.BlockSpec((1, 1, tn), lambda l, j: (l, 0, j))],
        out_specs=pl.BlockSpec((1, 8, tn), lambda l, j: (l, 0, j)),
        out_shape=jax.ShapeDtypeStruct((depth, 8, n), F32),
        compiler_params=_params("arbitrary", "arbitrary"),
        name="adaln",
    )(cond, mod_w, mod_b.reshape(depth, 1, n))


def _modmm_kernel(x_ref, g_ref, sh_ref, sc_ref, w_ref, o_ref, h_scr):
    @pl.when(pl.program_id(1) == 0)
    def _():
        h_scr[...] = _modulate(x_ref[...], g_ref[...], sh_ref[0], sc_ref[0]).astype(BF16)

    o_ref[...] = _dot(h_scr[...], w_ref[...])


def _modmm(x, gain, shift, scale, w, seg):
    t, d = x.shape
    n = w.shape[1]
    tm = min(TOKEN_TILE, seg)
    tn = PROJ_N_TILE
    per_seg = seg // tm
    r = shift.shape[0]
    return pl.pallas_call(
        _modmm_kernel,
        grid=(t // tm, n // tn),
        in_specs=[pl.BlockSpec((tm, d), lambda i, j: (i, 0)),
                  pl.BlockSpec((1, d), lambda i, j: (0, 0)),
                  pl.BlockSpec((1, 1, d), lambda i, j: (i // per_seg, 0, 0)),
                  pl.BlockSpec((1, 1, d), lambda i, j: (i // per_seg, 0, 0)),
                  pl.BlockSpec((d, tn), lambda i, j: (0, j))],
        out_specs=pl.BlockSpec((tm, tn), lambda i, j: (i, j)),
        out_shape=jax.ShapeDtypeStruct((t, n), F32),
        scratch_shapes=[pltpu.VMEM((tm, d), BF16)],
        compiler_params=_params("arbitrary", "arbitrary"),
        name="modulate_matmul",
    )(x, gain.reshape(1, d), shift.reshape(r, 1, d), scale.reshape(r, 1, d), w)


def _outproj_kernel(*refs, n_in):
    a_refs = refs[:n_in]
    w_refs = refs[n_in:2 * n_in]
    x_ref, gate_ref, g2_ref, sh_ref, sc_ref, xo_ref, h_ref = refs[2 * n_in:]
    acc = None
    for a_ref, w_ref in zip(a_refs, w_refs):
        part = _dot(a_ref[...].astype(BF16), w_ref[...])
        acc = part if acc is None else acc + part
    xn = x_ref[...] + gate_ref[0] * acc
    xo_ref[...] = xn
    h_ref[...] = _modulate(xn, g2_ref[...], sh_ref[0], sc_ref[0]).astype(BF16)


def _outproj(a_list, w_list, x, gate, gain2, shift2, scale2, seg):
    t, d = x.shape
    tm = min(TOKEN_TILE, seg)
    per_seg = seg // tm
    r = gate.shape[0]
    n_in = len(a_list)
    row = lambda i: (i // per_seg, 0, 0)
    in_specs = ([pl.BlockSpec((tm, a.shape[1]), lambda i: (i, 0)) for a in a_list]
                + [pl.BlockSpec(w.shape, lambda i: (0, 0)) for w in w_list]
                + [pl.BlockSpec((tm, d), lambda i: (i, 0)),
                   pl.BlockSpec((1, 1, d), row),
                   pl.BlockSpec((1, d), lambda i: (0, 0)),
                   pl.BlockSpec((1, 1, d), row),
                   pl.BlockSpec((1, 1, d), row)])
    return pl.pallas_call(
        functools.partial(_outproj_kernel, n_in=n_in),
        grid=(t // tm,),
        in_specs=in_specs,
        out_specs=[pl.BlockSpec((tm, d), lambda i: (i, 0)), pl.BlockSpec((tm, d), lambda i: (i, 0))],
        out_shape=[jax.ShapeDtypeStruct((t, d), F32), jax.ShapeDtypeStruct((t, d), BF16)],
        compiler_params=_params("arbitrary"),
        name="out_proj_residual",
    )(*a_list, *w_list, x, gate.reshape(r, 1, d), gain2.reshape(1, d),
      shift2.reshape(r, 1, d), scale2.reshape(r, 1, d))


def _softmax_pv(s_list, v_list, extra=None):
    m = s_list[0].max(axis=-1, keepdims=True)
    for s in s_list[1:]:
        m = jnp.maximum(m, s.max(axis=-1, keepdims=True))
    if extra is not None:
        m = jnp.maximum(m, extra)
    denom = None
    out = None
    for s, v in zip(s_list, v_list):
        p = jnp.exp(s - m)
        ps = p.sum(axis=-1, keepdims=True)
        denom = ps if denom is None else denom + ps
        pv = _dot(p.astype(BF16), v)
        out = pv if out is None else out + pv
    if extra is not None:
        denom = denom + jnp.exp(extra - m)
    return out / denom


def _na_kernel(q_ref, k_ref, v_ref, kc_ref, vc_ref, bias_ref, qn_ref, kn_ref, o_ref, *, n_steps):
    i = pl.program_id(2)
    band = NA_BAND_ROWS * GRID_W
    start = jnp.clip(NA_Q_ROWS * i - NA_ROWS // 2, 0, NA_Q_ROWS * n_steps - NA_BAND_ROWS) * GRID_W
    start = pl.multiple_of(start, NA_ROWS // 2 * GRID_W)
    q = (_head_rms(q_ref[0], qn_ref[...]) * ATTN_SCALE).astype(BF16)
    kb = _head_rms(k_ref[0, pl.ds(start, band), :], kn_ref[...]).astype(BF16)
    vb = v_ref[0, pl.ds(start, band), :].astype(BF16)
    kc = _head_rms(kc_ref[0], kn_ref[...]).astype(BF16)
    vc = vc_ref[0].astype(BF16)
    outs = []
    for h in range(2):
        sl = slice(h * HEAD_DIM, (h + 1) * HEAD_DIM)
        s_loc = _dot_nt(q[:, sl], kb[:, sl]) + bias_ref[0, h]
        s_ctx = _dot_nt(q[:, sl], kc[:, sl])
        outs.append(_softmax_pv([s_loc, s_ctx], [vb[:, sl], vc[:, sl]]))
    o_ref[0] = jnp.concatenate(outs, axis=-1)


def _na_bias(rpb, rows):
    h = rpb.shape[0]
    n_steps = rows // NA_Q_ROWS
    col = np.arange(GRID_W)
    c0 = np.clip(col - NA_COLS // 2, 0, GRID_W - NA_COLS)
    dc = col[None, :] - col[:, None] + (NA_COLS - 1)
    ok_c = (col[None, :] >= c0[:, None]) & (col[None, :] < c0[:, None] + NA_COLS)
    pick = ((np.arange(2 * NA_COLS - 1)[:, None, None] == dc[None]) & ok_c[None]).astype(np.float32)
    blocks = jnp.einsum('hrd,dqk->hrqk', rpb, pick, precision=lax.Precision.HIGHEST)
    blocks = jnp.where(ok_c, blocks, NEG_INF)
    masked = jnp.full((h, GRID_W, GRID_W), NEG_INF, F32)
    classes = []
    for step in (0, 1, n_steps - 1):
        band0 = min(max(NA_Q_ROWS * step - NA_ROWS // 2, 0), rows - NA_BAND_ROWS)
        q_rows = []
        for rq in range(NA_Q_ROWS):
            r = NA_Q_ROWS * step + rq
            r0 = min(max(r - NA_ROWS // 2, 0), rows - NA_ROWS)
            row = []
            for rk in range(band0, band0 + NA_BAND_ROWS):
                row.append(blocks[:, rk - r + NA_ROWS - 1] if r0 <= rk < r0 + NA_ROWS else masked)
            q_rows.append(jnp.concatenate(row, axis=-1))
        classes.append(jnp.concatenate(q_rows, axis=-2))
    return jnp.stack(classes)


def _na_attention(p_lat, p_ctx, bias, qn, kn, n_pairs):
    b, s, _ = p_lat.shape
    l = p_ctx.shape[1]
    rows = s // GRID_W
    n_steps = rows // NA_Q_ROWS
    qt = NA_Q_ROWS * GRID_W

    def cls(i):
        return jnp.where(i == 0, 0, jnp.where(i == n_steps - 1, 2, 1))

    return pl.pallas_call(
        functools.partial(_na_kernel, n_steps=n_steps),
        grid=(n_pairs, b, n_steps),
        in_specs=[pl.BlockSpec((1, qt, LANES), lambda hp, bb, i: (bb, i, hp)),
                  pl.BlockSpec((1, s, LANES), lambda hp, bb, i: (bb, 0, n_pairs + hp)),
                  pl.BlockSpec((1, s, LANES), lambda hp, bb, i: (bb, 0, 2 * n_pairs + hp)),
                  pl.BlockSpec((1, l, LANES), lambda hp, bb, i: (bb, 0, n_pairs + hp)),
                  pl.BlockSpec((1, l, LANES), lambda hp, bb, i: (bb, 0, 2 * n_pairs + hp)),
                  pl.BlockSpec((1, 2, qt, NA_BAND_ROWS * GRID_W), lambda hp, bb, i: (cls(i), hp, 0, 0)),
                  pl.BlockSpec((1, LANES), lambda hp, bb, i: (0, 0)),
                  pl.BlockSpec((1, LANES), lambda hp, bb, i: (0, 0))],
        out_specs=pl.BlockSpec((1, qt, LANES), lambda hp, bb, i: (bb, i, hp)),
        out_shape=jax.ShapeDtypeStruct((b, s, n_pairs * LANES), F32),
        compiler_params=_params("arbitrary", "arbitrary", "arbitrary"),
        name="neighbourhood_attention",
    )(p_lat, p_lat, p_lat, p_ctx, p_ctx, bias, qn, kn)


def _ctx_attn_kernel(q_ref, k_ref, v_ref, qn_ref, kn_ref, o_ref):
    q = (_head_rms(q_ref[0], qn_ref[...]) * ATTN_SCALE).astype(BF16)
    k = _head_rms(k_ref[0], kn_ref[...]).astype(BF16)
    v = v_ref[0].astype(BF16)
    outs = []
    for h in range(2):
        sl = slice(h * HEAD_DIM, (h + 1) * HEAD_DIM)
        outs.append(_softmax_pv([_dot_nt(q[:, sl], k[:, sl])], [v[:, sl]]))
    o_ref[0] = jnp.concatenate(outs, axis=-1)


def _ctx_attention(p_ctx, qn, kn, n_pairs):
    b, l, _ = p_ctx.shape
    return pl.pallas_call(
        _ctx_attn_kernel,
        grid=(n_pairs, b),
        in_specs=[pl.BlockSpec((1, l, LANES), lambda hp, bb: (bb, 0, hp)),
                  pl.BlockSpec((1, l, LANES), lambda hp, bb: (bb, 0, n_pairs + hp)),
                  pl.BlockSpec((1, l, LANES), lambda hp, bb: (bb, 0, 2 * n_pairs + hp)),
                  pl.BlockSpec((1, LANES), lambda hp, bb: (0, 0)),
                  pl.BlockSpec((1, LANES), lambda hp, bb: (0, 0))],
        out_specs=pl.BlockSpec((1, l, LANES), lambda hp, bb: (bb, 0, hp)),
        out_shape=jax.ShapeDtypeStruct((b, l, n_pairs * LANES), F32),
        compiler_params=_params("arbitrary", "arbitrary"),
        name="context_attention",
    )(p_ctx, p_ctx, p_ctx, qn, kn)


def _ret_kernel(q_ref, k_ref, v_ref, g_ref, cos_ref, sin_ref, lg_ref, s0_ref, y_ref, st_ref, sf_scr,
                *, n_chunks, use_rope):
    c = RET_CHUNK
    hd = HEAD_DIM
    lg = -jnp.exp(lg_ref[0])
    lgf, lgb = lg[0:1, :], lg[1:2, :]
    ii = lax.broadcasted_iota(jnp.int32, (c, LANES), 0).astype(F32)
    dq_f = jnp.exp(lgf * (ii + 1.0))
    dk_f = jnp.exp(lgf * (c - 1.0 - ii))
    dq_b = jnp.exp(lgb * (c - ii))
    dk_b = jnp.exp(lgb * ii)
    dc_f = jnp.exp(lgf * float(c))
    dc_b = jnp.exp(lgb * float(c))
    diff = (lax.broadcasted_iota(jnp.int32, (c, c), 0) - lax.broadcasted_iota(jnp.int32, (c, c), 1)).astype(F32)
    intra = []
    chunk_f = []
    chunk_b = []
    for h in range(2):
        lf = lgf[:, h * hd:h * hd + 1]
        lb = lgb[:, h * hd:h * hd + 1]
        intra.append(jnp.where(diff >= 0, jnp.exp(lf * jnp.maximum(diff, 0.0)),
                               jnp.exp(lb * jnp.maximum(-diff, 0.0))))
        chunk_f.append(dc_f[:, h * hd:h * hd + 1])
        chunk_b.append(dc_b[:, h * hd:h * hd + 1])

    def load(n):
        r = pl.multiple_of(n * c, c)
        q = q_ref[0, pl.ds(r, c), :]
        k = k_ref[0, pl.ds(r, c), :]
        v = v_ref[0, pl.ds(r, c), :]
        if use_rope:
            cs = cos_ref[pl.ds(r, c), :]
            sn = sin_ref[pl.ds(r, c), :]
            q = _rope(q, cs, sn)
            k = _rope(k, cs, sn)
        return r, q * ATTN_SCALE, k, v.astype(BF16)

    def fwd(n, carry):
        _, _, k, v = load(n)
        kd = (k * dk_f).astype(BF16)
        new = []
        for h in range(2):
            sl = slice(h * hd, (h + 1) * hd)
            sf_scr[n, h] = carry[h]
            new.append(carry[h] * chunk_f[h] + _dot_tn(kd[:, sl], v[:, sl]))
        return tuple(new)

    sf = lax.fori_loop(0, n_chunks, fwd, (s0_ref[0, 0, 0], s0_ref[0, 0, 1]))
    st_ref[0, 0, 0] = sf[0]
    st_ref[0, 0, 1] = sf[1]

    def bwd(jj, carry):
        n = n_chunks - 1 - jj
        r, q, k, v = load(n)
        qb = q.astype(BF16)
        kb = k.astype(BF16)
        qf = (q * dq_f).astype(BF16)
        qr = (q * dq_b).astype(BF16)
        kd = (k * dk_b).astype(BF16)
        outs = []
        new = []
        for h in range(2):
            sl = slice(h * hd, (h + 1) * hd)
            a = (_dot_nt(qb[:, sl], kb[:, sl]) * intra[h]).astype(BF16)
            o = (_dot(a, v[:, sl]) + _dot(qf[:, sl], sf_scr[n, h].astype(BF16))
                 + _dot(qr[:, sl], carry[h].astype(BF16)))
            oc = o - jnp.mean(o, axis=-1, keepdims=True)
            outs.append(oc * lax.rsqrt(jnp.mean(oc * oc, axis=-1, keepdims=True) + GN_EPS))
            new.append(carry[h] * chunk_b[h] + _dot_tn(kd[:, sl], v[:, sl]))
        g = g_ref[0, pl.ds(r, c), :]
        y_ref[0, pl.ds(r, c), :] = jnp.concatenate(outs, axis=-1) * (g * jax.nn.sigmoid(g))
        return tuple(new)

    sb = lax.fori_loop(0, n_chunks, bwd, (s0_ref[0, 0, 2], s0_ref[0, 0, 3]))
    st_ref[0, 0, 2] = sb[0]
    st_ref[0, 0, 3] = sb[1]


def _retention(p, cos2, sin2, lg, s0, col0, n_pairs, use_rope):
    b, t, _ = p.shape
    n_chunks = t // RET_CHUNK
    tab = lambda hp, bb: (0, 0)
    return pl.pallas_call(
        functools.partial(_ret_kernel, n_chunks=n_chunks, use_rope=use_rope),
        grid=(n_pairs, b),
        in_specs=[pl.BlockSpec((1, t, LANES), lambda hp, bb: (bb, 0, col0 + hp)),
                  pl.BlockSpec((1, t, LANES), lambda hp, bb: (bb, 0, col0 + n_pairs + hp)),
                  pl.BlockSpec((1, t, LANES), lambda hp, bb: (bb, 0, col0 + 2 * n_pairs + hp)),
                  pl.BlockSpec((1, t, LANES), lambda hp, bb: (bb, 0, col0 + 3 * n_pairs + hp)),
                  pl.BlockSpec(cos2.shape, tab),
                  pl.BlockSpec(sin2.shape, tab),
                  pl.BlockSpec((1, 2, LANES), lambda hp, bb: (hp, 0, 0)),
                  pl.BlockSpec((1, 1, 4, HEAD_DIM, HEAD_DIM), lambda hp, bb: (bb, hp, 0, 0, 0))],
        out_specs=[pl.BlockSpec((1, t, LANES), lambda hp, bb: (bb, 0, hp)),
                   pl.BlockSpec((1, 1, 4, HEAD_DIM, HEAD_DIM), lambda hp, bb: (bb, hp, 0, 0, 0))],
        out_shape=[jax.ShapeDtypeStruct((b, t, n_pairs * LANES), F32),
                   jax.ShapeDtypeStruct((b, n_pairs, 4, HEAD_DIM, HEAD_DIM), F32)],
        scratch_shapes=[pltpu.VMEM((n_chunks, 2, HEAD_DIM, HEAD_DIM), F32)],
        compiler_params=_params("arbitrary", "arbitrary"),
        name="retention",
    )(p, p, p, p, cos2, sin2, lg, s0)


def _swa_kernel(q_ref, k_ref, v_ref, kc_ref, vc_ref, cos_ref, sin_ref, qn_ref, kn_ref, sink_ref, o_ref,
                *, seq):
    qt = SWA_Q_TILE
    wk = qt + 2 * SWA_WINDOW
    hd = HEAD_DIM
    n = pl.program_id(2)
    q0 = pl.multiple_of(n * qt, qt)
    ws = pl.multiple_of(jnp.clip(n * qt - SWA_WINDOW, 0, seq - wk), SWA_WINDOW)
    kw = _rope(_head_rms(k_ref[0, pl.ds(ws, wk), :], kn_ref[...]),
               cos_ref[pl.ds(ws, wk), :], sin_ref[pl.ds(ws, wk), :]).astype(BF16)
    vw = v_ref[0, pl.ds(ws, wk), :].astype(BF16)
    kc = _head_rms(kc_ref[0], kn_ref[...]).astype(BF16)
    vc = vc_ref[0].astype(BF16)
    cos_q = cos_ref[pl.ds(q0, qt), :]
    sin_q = sin_ref[pl.ds(q0, qt), :]
    qs = []
    for s in range(4):
        slab = q_ref[0, :, s * LANES:(s + 1) * LANES]
        qs.append((_rope(_head_rms(slab, qn_ref[...]), cos_q, sin_q) * ATTN_SCALE).astype(BF16))
    qpos = q0 + (lax.broadcasted_iota(jnp.int32, (4 * qt, wk), 0) & (qt - 1))
    kpos = ws + lax.broadcasted_iota(jnp.int32, (4 * qt, wk), 1)
    near4 = jnp.abs(qpos - kpos) <= SWA_WINDOW
    for kh in range(2):
        sl = slice(kh * hd, (kh + 1) * hd)
        qstack = jnp.concatenate(
            [qs[kh * 2 + g // 2][:, (g % 2) * hd:(g % 2 + 1) * hd] for g in range(4)], axis=0)
        sink = jnp.concatenate(
            [jnp.broadcast_to(sink_ref[kh * 4 + g:kh * 4 + g + 1, 0:1], (qt, 1)) for g in range(4)], axis=0)
        s_loc = jnp.where(near4, _dot_nt(qstack, kw[:, sl]), NEG_INF)
        s_ctx = _dot_nt(qstack, kc[:, sl])
        o = _softmax_pv([s_loc, s_ctx], [vw[:, sl], vc[:, sl]], extra=sink)
        for pair in range(2):
            col = (kh * 2 + pair) * LANES
            o_ref[0, :, col:col + LANES] = jnp.concatenate(
                [o[(2 * pair) * qt:(2 * pair + 1) * qt], o[(2 * pair + 1) * qt:(2 * pair + 2) * qt]], axis=-1)


def _swa_attention(p_lat, p_ctx, cos2, sin2, qn, kn, sink_rows, n_q_heads, n_kv_heads):
    b, s, _ = p_lat.shape
    l = p_ctx.shape[1]
    kv_pairs = n_kv_heads // 2
    q_blocks = n_q_heads * HEAD_DIM // LANES
    q_per_pair = q_blocks // kv_pairs
    qw = q_per_pair * LANES
    tab = lambda kp, bb, n: (0, 0)
    return pl.pallas_call(
        functools.partial(_swa_kernel, seq=s),
        grid=(kv_pairs, b, s // SWA_Q_TILE),
        in_specs=[pl.BlockSpec((1, SWA_Q_TILE, qw), lambda kp, bb, n: (bb, n, kp)),
                  pl.BlockSpec((1, s, LANES), lambda kp, bb, n: (bb, 0, q_blocks + kp)),
                  pl.BlockSpec((1, s, LANES), lambda kp, bb, n: (bb, 0, q_blocks + kv_pairs + kp)),
                  pl.BlockSpec((1, l, LANES), lambda kp, bb, n: (bb, 0, q_blocks + kp)),
                  pl.BlockSpec((1, l, LANES), lambda kp, bb, n: (bb, 0, q_blocks + kv_pairs + kp)),
                  pl.BlockSpec(cos2.shape, tab),
                  pl.BlockSpec(sin2.shape, tab),
                  pl.BlockSpec((1, LANES), tab),
                  pl.BlockSpec((1, LANES), tab),
                  pl.BlockSpec((8, LANES), lambda kp, bb, n: (kp, 0))],
        out_specs=pl.BlockSpec((1, SWA_Q_TILE, qw), lambda kp, bb, n: (bb, n, kp)),
        out_shape=jax.ShapeDtypeStruct((b, s, n_q_heads * HEAD_DIM), F32),
        compiler_params=_params("arbitrary", "arbitrary", "arbitrary"),
        name="windowed_gqa",
    )(p_lat, p_lat, p_lat, p_ctx, p_ctx, cos2, sin2, qn, kn, sink_rows)


def _top_rows(x, k, scr, want_rank=False):
    work = x
    rank = jnp.full(x.shape, float(PEER_TOPK), F32) if want_rank else None
    for r in range(k):
        m = jnp.max(work, axis=0, keepdims=True)
        scr[r:r + 1, :] = m
        hit = work == m
        if want_rank and r < PEER_TOPK:
            rank = jnp.where(hit, float(r), rank)
        if r + 1 < k:
            work = jnp.where(hit, NEG_INF, work)
    return rank


def _count_at_least(sorted_scr, y):
    row = lambda i: sorted_scr[i:i + 1, :]
    c8 = row(7) >= y
    c4 = jnp.where(c8, row(11), row(3)) >= y
    c2 = jnp.where(c8, jnp.where(c4, row(13), row(9)), jnp.where(c4, row(5), row(1))) >= y
    hi = jnp.where(c4, jnp.where(c2, row(14), row(12)), jnp.where(c2, row(10), row(8)))
    lo = jnp.where(c4, jnp.where(c2, row(6), row(4)), jnp.where(c2, row(2), row(0)))
    c1 = jnp.where(c8, hi, lo) >= y
    count = (jnp.where(c8, 8.0, 0.0) + jnp.where(c4, 4.0, 0.0)) + (jnp.where(c2, 2.0, 0.0) + jnp.where(c1, 1.0, 0.0))
    return count + jnp.where(row(15) >= y, 1.0, 0.0)


def _peer_scores_kernel(h_ref, wq_ref, keys_ref, ra_ref, p1_ref, gb_ref, p2_ref,
                        q_scr, s_scr, a_scr, b_scr, c_scr, v_scr, *, n_heads):
    k = PEER_TOPK
    n_blocks = h_ref.shape[0] // LANES

    q = _dot(h_ref[...], wq_ref[...])
    for hp in range(2 * n_heads):
        q_scr[hp] = q[:, hp * PEER_KEYS:(hp + 1) * PEER_KEYS].astype(BF16)

    def lane_block(lb, h, slot):
        tops_a, tops_b, cand, tops_c = a_scr.at[slot], b_scr.at[slot], c_scr.at[slot], v_scr.at[slot]
        s1 = s_scr[0, lb]
        s2 = s_scr[1, lb]
        rank_a = _top_rows(s1, k + 1, tops_a, want_rank=True)
        _top_rows(s2, k + 1, tops_b)
        cand[0:16, :] = tops_a[0:1, :] + tops_b[0:16, :]
        for i in range(1, 8):
            cand[8 + 8 * i:16 + 8 * i, :] = tops_a[i:i + 1, :] + tops_b[0:8, :]
        cand[72:80, :] = tops_a[8:16, :] + tops_b[0:1, :]
        cand[80:81, :] = tops_a[0:1, :] + tops_b[16:17, :]
        cand[81:82, :] = tops_a[16:17, :] + tops_b[0:1, :]
        cand[82:88, :] = jnp.full((6, LANES), NEG_INF, F32)
        _top_rows(cand[...], k + 1, tops_c)
        thr = 0.5 * (tops_c[k - 1:k, :] + tops_c[k:k + 1, :])
        z = jnp.sum(jnp.exp(tops_c[0:k, :] - tops_c[0:1, :]), axis=0, keepdims=True)
        ra_ref[lb, h] = rank_a
        p1_ref[lb, h] = jnp.exp(s1 - tops_a[0:1, :]) / z
        gb_ref[lb, h] = _count_at_least(tops_a, thr - s2).astype(BF16)
        p2_ref[lb, h] = jnp.exp(s2 - tops_b[0:1, :]).astype(BF16)

    def lane_pair(i, h):
        lane_block(2 * i, h, 0)
        lane_block(2 * i + 1, h, 1)
        return h

    def head(h, carry):
        for p in range(2):
            st = _dot_nt(keys_ref[2 * h + p], q_scr[2 * h + p])
            for lb in range(n_blocks):
                s_scr[p, lb] = st[:, lb * LANES:(lb + 1) * LANES]
        lax.fori_loop(0, n_blocks // 2, lane_pair, h)
        return carry

    lax.fori_loop(0, n_heads, head, 0)


def _peer_scores(h2, wq, keys, n_heads):
    t, d = h2.shape
    tt = PEER_TOKEN_TILE
    nb = tt // LANES
    tab_shape = (t // LANES, n_heads, PEER_KEYS, LANES)
    tab_spec = pl.BlockSpec((nb, n_heads, PEER_KEYS, LANES), lambda i: (i, 0, 0, 0))
    return pl.pallas_call(
        functools.partial(_peer_scores_kernel, n_heads=n_heads),
        grid=(t // tt,),
        in_specs=[pl.BlockSpec((tt, d), lambda i: (i, 0)),
                  pl.BlockSpec(wq.shape, lambda i: (0, 0)),
                  pl.BlockSpec(keys.shape, lambda i: (0, 0, 0))],
        out_specs=[tab_spec] * 4,
        out_shape=[jax.ShapeDtypeStruct(tab_shape, dt) for dt in (F32, F32, BF16, BF16)],
        scratch_shapes=[pltpu.VMEM((2 * n_heads, tt, PEER_KEYS), BF16),
                        pltpu.VMEM((2, nb, PEER_KEYS, LANES), F32),
                        pltpu.VMEM((2, 24, LANES), F32), pltpu.VMEM((2, 24, LANES), F32),
                        pltpu.VMEM((2, 88, LANES), F32), pltpu.VMEM((2, 24, LANES), F32)],
        compiler_params=_params("arbitrary"),
        name="peer_scores",
    )(h2, wq, keys)


def _peer_mix_kernel(h_ref, u_ref, vt_ref, ra_ref, p1_ref, gb_ref, p2_ref, x_ref, g_ref, o_ref,
                     acc_scr, act_scr, a_scr, *, n_heads, n_tiles):
    s = pl.program_id(1)
    tt = h_ref.shape[0]
    rows_per_tile = PEER_EXPERT_TILE // PEER_KEYS
    tile = (PEER_KEYS, LANES)

    def readout():
        acc_scr[...] += _dot(vt_ref[0], a_scr[...])

    def gates():
        for aa in range(rows_per_tile):
            rs = slice(aa * PEER_KEYS, (aa + 1) * PEER_KEYS)
            for lb in range(tt // LANES):
                ls = slice(lb * LANES, (lb + 1) * LANES)
                w = None
                for h in range(n_heads):
                    rank = jnp.broadcast_to(ra_ref[lb, h, aa:aa + 1, :].astype(BF16), tile)
                    p1 = jnp.broadcast_to(p1_ref[lb, h, aa:aa + 1, :].astype(BF16), tile)
                    term = jnp.where(rank < gb_ref[lb, h], p2_ref[lb, h], jnp.zeros(tile, BF16)) * p1
                    w = term if w is None else w + term
                act = act_scr[rs, ls]
                gelu = 0.5 * act * (1.0 + lax.erf(act * INV_SQRT2))
                a_scr[rs, ls] = w * gelu.astype(BF16)

    def experts():
        act_scr[...] = _dot_nt(u_ref[...], h_ref[...])

    @pl.when(s == 0)
    def _():
        acc_scr[...] = jnp.zeros_like(acc_scr)
        act_scr[...] = jnp.zeros_like(act_scr)
        a_scr[...] = jnp.zeros_like(a_scr)

    @pl.when(s < n_tiles)
    def _():
        readout()
        gates()
        experts()

    @pl.when(s == n_tiles)
    def _():
        readout()
        gates()

    @pl.when(s == n_tiles + 1)
    def _():
        readout()
        o_ref[...] = x_ref[...] + g_ref[0] * acc_scr[...].T


def _peer_mix(h2, u, vt, tables, x, gate, seg, n_heads):
    t, d = h2.shape
    n_exp = u.shape[0]
    tt = PEER_TOKEN_TILE
    te = PEER_EXPERT_TILE
    n_tiles = n_exp // te
    per_seg = seg // tt
    r = gate.shape[0]
    last = n_tiles - 1
    nb = tt // LANES
    tab_spec = pl.BlockSpec((nb, n_heads, PEER_KEYS, LANES), lambda i, s: (i, 0, 0, 0))
    row_spec = pl.BlockSpec((nb, n_heads, te // PEER_KEYS, LANES),
                            lambda i, s: (i, 0, jnp.clip(s - 1, 0, last), 0))
    return pl.pallas_call(
        functools.partial(_peer_mix_kernel, n_heads=n_heads, n_tiles=n_tiles),
        grid=(t // tt, n_tiles + 2),
        in_specs=[pl.BlockSpec((tt, d), lambda i, s: (i, 0)),
                  pl.BlockSpec((te, d), lambda i, s: (jnp.minimum(s, last), 0)),
                  pl.BlockSpec((1, d, te), lambda i, s: (jnp.clip(s - 2, 0, last), 0, 0)),
                  row_spec, row_spec, tab_spec, tab_spec,
                  pl.BlockSpec((tt, d), lambda i, s: (i, 0)),
                  pl.BlockSpec((1, 1, d), lambda i, s: (i // per_seg, 0, 0))],
        out_specs=pl.BlockSpec((tt, d), lambda i, s: (i, 0)),
        out_shape=jax.ShapeDtypeStruct((t, d), F32),
        scratch_shapes=[pltpu.VMEM((d, tt), F32), pltpu.VMEM((te, tt), F32), pltpu.VMEM((te, tt), BF16)],
        compiler_params=_params("arbitrary", "arbitrary"),
        name="peer_mix",
    )(h2, u, vt, *tables, x, gate.reshape(r, 1, d))


def _peer(h2, x, gate, seg, wq, keys, u, vt):
    n_heads = keys.shape[0] // 2
    tables = _peer_scores(h2, wq, keys, n_heads)
    return _peer_mix(h2, u, vt, tables, x, gate, seg, n_heads)


def _rope_tables(n_tokens):
    t = jnp.arange(n_tokens)
    row = (t // GRID_W).astype(F32)
    col = (t % GRID_W).astype(F32)
    n_freq = HEAD_DIM // 4
    inv_freq = jnp.power(ROPE_BASE, -jnp.arange(n_freq, dtype=F32) / n_freq)
    ang = jnp.concatenate([row[:, None] * inv_freq, col[:, None] * inv_freq], axis=-1)
    cos, sin = jnp.cos(ang), jnp.sin(ang)
    cos2 = jnp.tile(jnp.concatenate([cos, cos], axis=-1), (1, 2))
    sin2 = jnp.tile(jnp.concatenate([-sin, sin], axis=-1), (1, 2))
    return cos2, sin2


def _pair_lanes(v):
    return jnp.repeat(v, HEAD_DIM, axis=-1).reshape(*v.shape[:-1], v.shape[-1] // 2, LANES)


def kernel(x, c, ctx, c_ctx, mod_w, mod_b, norm1_g, norm2_g, ab_w_in, ab_w_out, na_q_norm, na_k_norm,
           na_rpb, ret_log_decay, swa_w_in, swa_w_out, swa_q_norm, swa_k_norm, swa_sink,
           peer_w_q, peer_sub_keys, peer_u, peer_v):
    b, s, d = x.shape
    l = ctx.shape[1]
    depth = mod_w.shape[0]
    assert depth == 2 and b + 1 <= 8
    assert s % (NA_Q_ROWS * GRID_W) == 0 and s // GRID_W >= NA_BAND_ROWS
    assert s % TOKEN_TILE == 0 and (b * l) % TOKEN_TILE == 0 and l % RET_CHUNK == 0

    cond = jnp.concatenate([c, c_ctx[None, :], jnp.zeros((8 - b - 1, d), F32)], axis=0)
    mods = _adaln(cond, mod_w, mod_b).reshape(depth, 8, 6, d)
    lat = lambda layer, which: mods[layer, :b, which]
    cx = lambda layer, which: mods[layer, b:b + 1, which]

    cos2, sin2 = _rope_tables(s)
    x_lat = x.reshape(b * s, d)
    x_ctx = ctx.reshape(b * l, d)
    tile2 = lambda g: jnp.tile(g, 2).reshape(1, LANES)

    def peer_weights(layer):
        n_heads = peer_sub_keys.shape[1]
        keys = peer_sub_keys[layer].reshape(2 * n_heads, PEER_KEYS, -1).astype(BF16)
        vt = peer_v[layer].reshape(-1, PEER_EXPERT_TILE, d).transpose(0, 2, 1).astype(BF16)
        return peer_w_q[layer].astype(BF16), keys, peer_u[layer].astype(BF16), vt

    n_na = na_rpb.shape[1]
    n_ret = ret_log_decay.shape[2]
    na_pairs, ret_pairs = n_na // 2, n_ret // 2
    w_in = ab_w_in[0].astype(BF16)
    w_out = ab_w_out[0].astype(BF16)
    wa = n_na * HEAD_DIM
    p_lat = _modmm(x_lat, norm1_g[0], lat(0, 0), lat(0, 1), w_in, s).reshape(b, s, -1)
    p_ctx = _modmm(x_ctx, norm1_g[0], cx(0, 0), cx(0, 1), w_in, b * l).reshape(b, l, -1)

    qn, kn = tile2(na_q_norm[0]), tile2(na_k_norm[0])
    oa_lat = _na_attention(p_lat, p_ctx, _na_bias(na_rpb[0], s // GRID_W), qn, kn, na_pairs)
    oa_ctx = _ctx_attention(p_ctx, qn, kn, na_pairs)

    lg = _pair_lanes(ret_log_decay[0]).transpose(1, 0, 2)
    ret_col0 = 3 * wa // LANES
    zeros_state = jnp.zeros((b, ret_pairs, 4, HEAD_DIM, HEAD_DIM), F32)
    ones_tab, zeros_tab = jnp.ones((l, LANES), F32), jnp.zeros((l, LANES), F32)
    ob_ctx, st_ctx = _retention(p_ctx, ones_tab, zeros_tab, lg, zeros_state, ret_col0, ret_pairs, False)
    ob_lat, _ = _retention(p_lat, cos2, sin2, lg, st_ctx, ret_col0, ret_pairs, True)

    w_list = [w_out[:wa], w_out[wa:]]
    x_lat, h_lat = _outproj([oa_lat.reshape(b * s, -1), ob_lat.reshape(b * s, -1)], w_list, x_lat,
                            lat(0, 2), norm2_g[0], lat(0, 3), lat(0, 4), s)
    x_ctx, h_ctx = _outproj([oa_ctx.reshape(b * l, -1), ob_ctx.reshape(b * l, -1)], w_list, x_ctx,
                            cx(0, 2), norm2_g[0], cx(0, 3), cx(0, 4), b * l)
    pw = peer_weights(0)
    x_lat = _peer(h_lat, x_lat, lat(0, 5), s, *pw)
    x_ctx = _peer(h_ctx, x_ctx, cx(0, 5), b * l, *pw)

    n_q = swa_sink.shape[1]
    n_kv = (swa_w_in.shape[2] // HEAD_DIM - n_q) // 2
    w_in = swa_w_in[0].astype(BF16)
    p_lat = _modmm(x_lat, norm1_g[1], lat(1, 0), lat(1, 1), w_in, s).reshape(b, s, -1)
    p_ctx = _modmm(x_ctx, norm1_g[1], cx(1, 0), cx(1, 1), w_in, b * l).reshape(b, l, -1)
    sink_rows = jnp.broadcast_to(swa_sink[0][:, None], (n_q, LANES))
    o_lat = _swa_attention(p_lat, p_ctx, cos2, sin2, tile2(swa_q_norm[0]), tile2(swa_k_norm[0]),
                           sink_rows, n_q, n_kv)
    x_lat, h_lat = _outproj([o_lat.reshape(b * s, -1)], [swa_w_out[0].astype(BF16)], x_lat,
                            lat(1, 2), norm2_g[1], lat(1, 3), lat(1, 4), s)
    x_lat = _peer(h_lat, x_lat, lat(1, 5), s, *peer_weights(1))
    return x_lat.reshape(b, s, d)
```

```python
import functools

import numpy as np
import jax
import jax.numpy as jnp
from jax import lax
from jax.experimental import pallas as pl
from jax.experimental.pallas import tpu as pltpu

F32 = jnp.float32
BF16 = jnp.bfloat16

HEAD_DIM = 64
GRID_W = 64
NA_ROWS = 8
NA_COLS = 16
SWA_WINDOW = 128
PEER_TOPK = 16
PEER_KEYS = 128
ROPE_BASE = 10000.0
NORM_EPS = 1e-6
GN_EPS = 1e-5
NEG_INF = -1e30
ATTN_SCALE = HEAD_DIM ** -0.5
INV_SQRT2 = 0.7071067811865476

LANES = 128
VMEM_LIMIT = 56 * 1024 * 1024

TOKEN_TILE = 512
PROJ_TOKEN_TILE = 1024
PROJ_N_TILE = 1792
RET_CHUNK = 128
NA_Q_ROWS = 8
NA_BAND_ROWS = 16
SWA_Q_TILE = 256
PEER_TOKEN_TILE = 512
PEER_MIX_TOKEN_TILE = 1024
PEER_EXPERT_TILE = 1024


def _dot(a, b):
    return jnp.dot(a, b, preferred_element_type=F32)


def _dot_nt(a, b):
    return lax.dot_general(a, b, (((1,), (1,)), ((), ())), preferred_element_type=F32)


def _dot_tn(a, b):
    return lax.dot_general(a, b, (((0,), (0,)), ((), ())), preferred_element_type=F32)


def _params(*sem):
    return pltpu.CompilerParams(dimension_semantics=sem, vmem_limit_bytes=VMEM_LIMIT)


def _rms_rows(x, gain):
    ms = jnp.mean(x * x, axis=-1, keepdims=True)
    return x * lax.rsqrt(ms + NORM_EPS) * gain


def _modulate(x, gain, shift, scale):
    return _rms_rows(x, gain) * (1.0 + scale) + shift


def _head_rms(x, gain):
    lane = lax.broadcasted_iota(jnp.int32, x.shape, 1)
    lo = lane < HEAD_DIM
    ss = x * x
    s_lo = jnp.sum(jnp.where(lo, ss, 0.0), axis=-1, keepdims=True)
    s_hi = jnp.sum(jnp.where(lo, 0.0, ss), axis=-1, keepdims=True)
    ms = jnp.where(lo, s_lo, s_hi) * (1.0 / HEAD_DIM)
    return x * lax.rsqrt(ms + NORM_EPS) * gain


def _rope(x, cos2, sin2):
    lane = lax.broadcasted_iota(jnp.int32, x.shape, 1)
    first_half = (lane & (HEAD_DIM // 2)) == 0
    swapped = jnp.where(first_half, pltpu.roll(x, LANES - HEAD_DIM // 2, axis=1),
                        pltpu.roll(x, HEAD_DIM // 2, axis=1))
    return x * cos2 + swapped * sin2


def _adaln_kernel(c_ref, w_ref, b_ref, o_ref):
    c = c_ref[...]
    s = c * jax.nn.sigmoid(c)
    w = w_ref[0]
    s_hi = s.astype(BF16)
    s_lo = (s - s_hi.astype(F32)).astype(BF16)
    w_hi = w.astype(BF16)
    w_lo = (w - w_hi.astype(F32)).astype(BF16)
    acc = _dot(s_hi, w_hi) + _dot(s_lo, w_hi) + _dot(s_hi, w_lo)
    o_ref[0] = acc + b_ref[0]


def _adaln(cond, mod_w, mod_b):
    depth, d, n = mod_w.shape
    tn = n // 4
    return pl.pallas_call(
        _adaln_kernel,
        grid=(depth, n // tn),
        in_specs=[pl.BlockSpec((8, d), lambda l, j: (0, 0)),
                  pl.BlockSpec((1, d, tn), lambda l, j: (l, 0, j)),
                  pl.BlockSpec((1, 1, tn), lambda l, j: (l, 0, j))],
        out_specs=pl.BlockSpec((1, 8, tn), lambda l, j: (l, 0, j)),
        out_shape=jax.ShapeDtypeStruct((depth, 8, n), F32),
        compiler_params=_params("arbitrary", "arbitrary"),
        name="adaln",
    )(cond, mod_w, mod_b.reshape(depth, 1, n))


def _modmm_kernel(x_ref, g_ref, sh_ref, sc_ref, w_ref, o_ref, h_scr):
    @pl.when(pl.program_id(1) == 0)
    def _():
        h_scr[...] = _modulate(x_ref[...], g_ref[...], sh_ref[0], sc_ref[0]).astype(BF16)

    o_ref[...] = _dot(h_scr[...], w_ref[...])


def _modmm(x, gain, shift, scale, w, seg):
    t, d = x.shape
    n = w.shape[1]
    tm = min(PROJ_TOKEN_TILE, seg)
    tn = n if n <= PROJ_N_TILE else n // 2
    per_seg = seg // tm
    r = shift.shape[0]
    return pl.pallas_call(
        _modmm_kernel,
        grid=(t // tm, n // tn),
        in_specs=[pl.BlockSpec((tm, d), lambda i, j: (i, 0)),
                  pl.BlockSpec((1, d), lambda i, j: (0, 0)),
                  pl.BlockSpec((1, 1, d), lambda i, j: (i // per_seg, 0, 0)),
                  pl.BlockSpec((1, 1, d), lambda i, j: (i // per_seg, 0, 0)),
                  pl.BlockSpec((d, tn), lambda i, j: (0, j))],
        out_specs=pl.BlockSpec((tm, tn), lambda i, j: (i, j)),
        out_shape=jax.ShapeDtypeStruct((t, n), F32),
        scratch_shapes=[pltpu.VMEM((tm, d), BF16)],
        compiler_params=_params("arbitrary", "arbitrary"),
        name="modulate_matmul",
    )(x, gain.reshape(1, d), shift.reshape(r, 1, d), scale.reshape(r, 1, d), w)


def _outproj_kernel(*refs, n_in):
    a_refs = refs[:n_in]
    w_refs = refs[n_in:2 * n_in]
    x_ref, gate_ref, g2_ref, sh_ref, sc_ref, xo_ref, h_ref = refs[2 * n_in:]
    acc = None
    for a_ref, w_ref in zip(a_refs, w_refs):
        part = _dot(a_ref[...].astype(BF16), w_ref[...])
        acc = part if acc is None else acc + part
    xn = x_ref[...] + gate_ref[0] * acc
    xo_ref[...] = xn
    h_ref[...] = _modulate(xn, g2_ref[...], sh_ref[0], sc_ref[0]).astype(BF16)


def _outproj(a_list, w_list, x, gate, gain2, shift2, scale2, seg):
    t, d = x.shape
    tm = min(TOKEN_TILE, seg)
    per_seg = seg // tm
    r = gate.shape[0]
    n_in = len(a_list)
    row = lambda i: (i // per_seg, 0, 0)
    in_specs = ([pl.BlockSpec((tm, a.shape[1]), lambda i: (i, 0)) for a in a_list]
                + [pl.BlockSpec(w.shape, lambda i: (0, 0)) for w in w_list]
                + [pl.BlockSpec((tm, d), lambda i: (i, 0)),
                   pl.BlockSpec((1, 1, d), row),
                   pl.BlockSpec((1, d), lambda i: (0, 0)),
                   pl.BlockSpec((1, 1, d), row),
                   pl.BlockSpec((1, 1, d), row)])
    return pl.pallas_call(
        functools.partial(_outproj_kernel, n_in=n_in),
        grid=(t // tm,),
        in_specs=in_specs,
        out_specs=[pl.BlockSpec((tm, d), lambda i: (i, 0)), pl.BlockSpec((tm, d), lambda i: (i, 0))],
        out_shape=[jax.ShapeDtypeStruct((t, d), F32), jax.ShapeDtypeStruct((t, d), BF16)],
        compiler_params=_params("arbitrary"),
        name="out_proj_residual",
    )(*a_list, *w_list, x, gate.reshape(r, 1, d), gain2.reshape(1, d),
      shift2.reshape(r, 1, d), scale2.reshape(r, 1, d))


def _softmax_pv(s_list, v_list, extra=None):
    m = s_list[0].max(axis=-1, keepdims=True)
    for s in s_list[1:]:
        m = jnp.maximum(m, s.max(axis=-1, keepdims=True))
    if extra is not None:
        m = jnp.maximum(m, extra)
    denom = None
    out = None
    for s, v in zip(s_list, v_list):
        p = jnp.exp(s - m)
        ps = p.sum(axis=-1, keepdims=True)
        denom = ps if denom is None else denom + ps
        pv = _dot(p.astype(BF16), v)
        out = pv if out is None else out + pv
    if extra is not None:
        denom = denom + jnp.exp(extra - m)
    return out / denom


def _na_kernel(q_ref, k_ref, v_ref, kc_ref, vc_ref, bias_ref, qn_ref, kn_ref, o_ref,
               k_scr, v_scr, kc_scr, vc_scr, *, n_steps):
    i = pl.program_id(2)
    band = NA_BAND_ROWS * GRID_W
    start = jnp.clip(NA_Q_ROWS * i - NA_ROWS // 2, 0, NA_Q_ROWS * n_steps - NA_BAND_ROWS) * GRID_W
    start = pl.multiple_of(start, NA_ROWS // 2 * GRID_W)

    @pl.when(i == 0)
    def _():
        k_scr[...] = _head_rms(k_ref[0], kn_ref[...]).astype(BF16)
        v_scr[...] = v_ref[0].astype(BF16)
        kc_scr[...] = _head_rms(kc_ref[0], kn_ref[...]).astype(BF16)
        vc_scr[...] = vc_ref[0].astype(BF16)

    q = (_head_rms(q_ref[0], qn_ref[...]) * ATTN_SCALE).astype(BF16)
    kb = k_scr[pl.ds(start, band), :]
    vb = v_scr[pl.ds(start, band), :]
    kc = kc_scr[...]
    vc = vc_scr[...]
    outs = []
    for h in range(2):
        sl = slice(h * HEAD_DIM, (h + 1) * HEAD_DIM)
        s_loc = _dot_nt(q[:, sl], kb[:, sl]) + bias_ref[0, h]
        s_ctx = _dot_nt(q[:, sl], kc[:, sl])
        outs.append(_softmax_pv([s_loc, s_ctx], [vb[:, sl], vc[:, sl]]))
    o_ref[0] = jnp.concatenate(outs, axis=-1)


def _na_bias(rpb, rows):
    h = rpb.shape[0]
    n_steps = rows // NA_Q_ROWS
    col = np.arange(GRID_W)
    c0 = np.clip(col - NA_COLS // 2, 0, GRID_W - NA_COLS)
    dc = col[None, :] - col[:, None] + (NA_COLS - 1)
    ok_c = (col[None, :] >= c0[:, None]) & (col[None, :] < c0[:, None] + NA_COLS)
    pick = ((np.arange(2 * NA_COLS - 1)[:, None, None] == dc[None]) & ok_c[None]).astype(np.float32)
    blocks = jnp.einsum('hrd,dqk->hrqk', rpb, pick, precision=lax.Precision.HIGHEST)
    blocks = jnp.where(ok_c, blocks, NEG_INF)
    masked = jnp.full((h, GRID_W, GRID_W), NEG_INF, F32)
    classes = []
    for step in (0, 1, n_steps - 1):
        band0 = min(max(NA_Q_ROWS * step - NA_ROWS // 2, 0), rows - NA_BAND_ROWS)
        q_rows = []
        for rq in range(NA_Q_ROWS):
            r = NA_Q_ROWS * step + rq
            r0 = min(max(r - NA_ROWS // 2, 0), rows - NA_ROWS)
            row = []
            for rk in range(band0, band0 + NA_BAND_ROWS):
                row.append(blocks[:, rk - r + NA_ROWS - 1] if r0 <= rk < r0 + NA_ROWS else masked)
            q_rows.append(jnp.concatenate(row, axis=-1))
        classes.append(jnp.concatenate(q_rows, axis=-2))
    return jnp.stack(classes)


def _na_attention(p_lat, p_ctx, bias, qn, kn, n_pairs):
    b, s, _ = p_lat.shape
    l = p_ctx.shape[1]
    rows = s // GRID_W
    n_steps = rows // NA_Q_ROWS
    qt = NA_Q_ROWS * GRID_W

    def cls(i):
        return jnp.where(i == 0, 0, jnp.where(i == n_steps - 1, 2, 1))

    return pl.pallas_call(
        functools.partial(_na_kernel, n_steps=n_steps),
        grid=(n_pairs, b, n_steps),
        in_specs=[pl.BlockSpec((1, qt, LANES), lambda hp, bb, i: (bb, i, hp)),
                  pl.BlockSpec((1, s, LANES), lambda hp, bb, i: (bb, 0, n_pairs + hp)),
                  pl.BlockSpec((1, s, LANES), lambda hp, bb, i: (bb, 0, 2 * n_pairs + hp)),
                  pl.BlockSpec((1, l, LANES), lambda hp, bb, i: (bb, 0, n_pairs + hp)),
                  pl.BlockSpec((1, l, LANES), lambda hp, bb, i: (bb, 0, 2 * n_pairs + hp)),
                  pl.BlockSpec((1, 2, qt, NA_BAND_ROWS * GRID_W), lambda hp, bb, i: (cls(i), hp, 0, 0)),
                  pl.BlockSpec((1, LANES), lambda hp, bb, i: (0, 0)),
                  pl.BlockSpec((1, LANES), lambda hp, bb, i: (0, 0))],
        out_specs=pl.BlockSpec((1, qt, LANES), lambda hp, bb, i: (bb, i, hp)),
        out_shape=jax.ShapeDtypeStruct((b, s, n_pairs * LANES), F32),
        scratch_shapes=[pltpu.VMEM((s, LANES), BF16), pltpu.VMEM((s, LANES), BF16),
                        pltpu.VMEM((l, LANES), BF16), pltpu.VMEM((l, LANES), BF16)],
        compiler_params=_params("arbitrary", "arbitrary", "arbitrary"),
        name="neighbourhood_attention",
    )(p_lat, p_lat, p_lat, p_ctx, p_ctx, bias, qn, kn)


def _ctx_attn_kernel(q_ref, k_ref, v_ref, qn_ref, kn_ref, o_ref):
    q = (_head_rms(q_ref[0], qn_ref[...]) * ATTN_SCALE).astype(BF16)
    k = _head_rms(k_ref[0], kn_ref[...]).astype(BF16)
    v = v_ref[0].astype(BF16)
    outs = []
    for h in range(2):
        sl = slice(h * HEAD_DIM, (h + 1) * HEAD_DIM)
        outs.append(_softmax_pv([_dot_nt(q[:, sl], k[:, sl])], [v[:, sl]]))
    o_ref[0] = jnp.concatenate(outs, axis=-1)


def _ctx_attention(p_ctx, qn, kn, n_pairs):
    b, l, _ = p_ctx.shape
    return pl.pallas_call(
        _ctx_attn_kernel,
        grid=(n_pairs, b),
        in_specs=[pl.BlockSpec((1, l, LANES), lambda hp, bb: (bb, 0, hp)),
                  pl.BlockSpec((1, l, LANES), lambda hp, bb: (bb, 0, n_pairs + hp)),
                  pl.BlockSpec((1, l, LANES), lambda hp, bb: (bb, 0, 2 * n_pairs + hp)),
                  pl.BlockSpec((1, LANES), lambda hp, bb: (0, 0)),
                  pl.BlockSpec((1, LANES), lambda hp, bb: (0, 0))],
        out_specs=pl.BlockSpec((1, l, LANES), lambda hp, bb: (bb, 0, hp)),
        out_shape=jax.ShapeDtypeStruct((b, l, n_pairs * LANES), F32),
        compiler_params=_params("arbitrary", "arbitrary"),
        name="context_attention",
    )(p_ctx, p_ctx, p_ctx, qn, kn)


def _ret_kernel(q_ref, k_ref, v_ref, g_ref, cos_ref, sin_ref, lg_ref, s0_ref, y_ref, st_ref, sf_scr,
                *, n_chunks, use_rope):
    c = RET_CHUNK
    hd = HEAD_DIM
    lg = -jnp.exp(lg_ref[0])
    lgf, lgb = lg[0:1, :], lg[1:2, :]
    ii = lax.broadcasted_iota(jnp.int32, (c, LANES), 0).astype(F32)
    dq_f = jnp.exp(lgf * (ii + 1.0))
    dk_f = jnp.exp(lgf * (c - 1.0 - ii))
    dq_b = jnp.exp(lgb * (c - ii))
    dk_b = jnp.exp(lgb * ii)
    dc_f = jnp.exp(lgf * float(c))
    dc_b = jnp.exp(lgb * float(c))
    diff = (lax.broadcasted_iota(jnp.int32, (c, c), 0) - lax.broadcasted_iota(jnp.int32, (c, c), 1)).astype(F32)
    intra = []
    chunk_f = []
    chunk_b = []
    for h in range(2):
        lf = lgf[:, h * hd:h * hd + 1]
        lb = lgb[:, h * hd:h * hd + 1]
        intra.append(jnp.where(diff >= 0, jnp.exp(lf * jnp.maximum(diff, 0.0)),
                               jnp.exp(lb * jnp.maximum(-diff, 0.0))))
        chunk_f.append(dc_f[:, h * hd:h * hd + 1])
        chunk_b.append(dc_b[:, h * hd:h * hd + 1])

    def load(n):
        r = pl.multiple_of(n * c, c)
        q = q_ref[0, pl.ds(r, c), :]
        k = k_ref[0, pl.ds(r, c), :]
        v = v_ref[0, pl.ds(r, c), :]
        if use_rope:
            cs = cos_ref[pl.ds(r, c), :]
            sn = sin_ref[pl.ds(r, c), :]
            q = _rope(q, cs, sn)
            k = _rope(k, cs, sn)
        return r, q * ATTN_SCALE, k, v.astype(BF16)

    def fwd(n, carry):
        _, _, k, v = load(n)
        kd = (k * dk_f).astype(BF16)
        new = []
        for h in range(2):
            sl = slice(h * hd, (h + 1) * hd)
            sf_scr[n, h] = carry[h]
            new.append(carry[h] * chunk_f[h] + _dot_tn(kd[:, sl], v[:, sl]))
        return tuple(new)

    sf = lax.fori_loop(0, n_chunks, fwd, (s0_ref[0, 0, 0], s0_ref[0, 0, 1]), unroll=2)
    st_ref[0, 0, 0] = sf[0]
    st_ref[0, 0, 1] = sf[1]

    def bwd(jj, carry):
        n = n_chunks - 1 - jj
        r, q, k, v = load(n)
        qb = q.astype(BF16)
        kb = k.astype(BF16)
        qf = (q * dq_f).astype(BF16)
        qr = (q * dq_b).astype(BF16)
        kd = (k * dk_b).astype(BF16)
        outs = []
        new = []
        for h in range(2):
            sl = slice(h * hd, (h + 1) * hd)
            a = (_dot_nt(qb[:, sl], kb[:, sl]) * intra[h]).astype(BF16)
            o = (_dot(a, v[:, sl]) + _dot(qf[:, sl], sf_scr[n, h].astype(BF16))
                 + _dot(qr[:, sl], carry[h].astype(BF16)))
            oc = o - jnp.mean(o, axis=-1, keepdims=True)
            outs.append(oc * lax.rsqrt(jnp.mean(oc * oc, axis=-1, keepdims=True) + GN_EPS))
            new.append(carry[h] * chunk_b[h] + _dot_tn(kd[:, sl], v[:, sl]))
        g = g_ref[0, pl.ds(r, c), :]
        y_ref[0, pl.ds(r, c), :] = jnp.concatenate(outs, axis=-1) * (g * jax.nn.sigmoid(g))
        return tuple(new)

    sb = lax.fori_loop(0, n_chunks, bwd, (s0_ref[0, 0, 2], s0_ref[0, 0, 3]), unroll=2)
    st_ref[0, 0, 2] = sb[0]
    st_ref[0, 0, 3] = sb[1]


def _retention(p, cos2, sin2, lg, s0, col0, n_pairs, use_rope):
    b, t, _ = p.shape
    n_chunks = t // RET_CHUNK
    tab = lambda hp, bb: (0, 0)
    return pl.pallas_call(
        functools.partial(_ret_kernel, n_chunks=n_chunks, use_rope=use_rope),
        grid=(n_pairs, b),
        in_specs=[pl.BlockSpec((1, t, LANES), lambda hp, bb: (bb, 0, col0 + hp)),
                  pl.BlockSpec((1, t, LANES), lambda hp, bb: (bb, 0, col0 + n_pairs + hp)),
                  pl.BlockSpec((1, t, LANES), lambda hp, bb: (bb, 0, col0 + 2 * n_pairs + hp)),
                  pl.BlockSpec((1, t, LANES), lambda hp, bb: (bb, 0, col0 + 3 * n_pairs + hp)),
                  pl.BlockSpec(cos2.shape, tab),
                  pl.BlockSpec(sin2.shape, tab),
                  pl.BlockSpec((1, 2, LANES), lambda hp, bb: (hp, 0, 0)),
                  pl.BlockSpec((1, 1, 4, HEAD_DIM, HEAD_DIM), lambda hp, bb: (bb, hp, 0, 0, 0))],
        out_specs=[pl.BlockSpec((1, t, LANES), lambda hp, bb: (bb, 0, hp)),
                   pl.BlockSpec((1, 1, 4, HEAD_DIM, HEAD_DIM), lambda hp, bb: (bb, hp, 0, 0, 0))],
        out_shape=[jax.ShapeDtypeStruct((b, t, n_pairs * LANES), F32),
                   jax.ShapeDtypeStruct((b, n_pairs, 4, HEAD_DIM, HEAD_DIM), F32)],
        scratch_shapes=[pltpu.VMEM((n_chunks, 2, HEAD_DIM, HEAD_DIM), F32)],
        compiler_params=_params("arbitrary", "arbitrary"),
        name="retention",
    )(p, p, p, p, cos2, sin2, lg, s0)


def _swa_kernel(q_ref, k_ref, v_ref, kc_ref, vc_ref, cos_ref, sin_ref, qn_ref, kn_ref, sink_ref, o_ref,
                *, seq):
    qt = SWA_Q_TILE
    wk = qt + 2 * SWA_WINDOW
    hd = HEAD_DIM
    n = pl.program_id(2)
    q0 = pl.multiple_of(n * qt, qt)
    ws = pl.multiple_of(jnp.clip(n * qt - SWA_WINDOW, 0, seq - wk), SWA_WINDOW)
    kw = _rope(_head_rms(k_ref[0, pl.ds(ws, wk), :], kn_ref[...]),
               cos_ref[pl.ds(ws, wk), :], sin_ref[pl.ds(ws, wk), :]).astype(BF16)
    vw = v_ref[0, pl.ds(ws, wk), :].astype(BF16)
    kc = _head_rms(kc_ref[0], kn_ref[...]).astype(BF16)
    vc = vc_ref[0].astype(BF16)
    cos_q = cos_ref[pl.ds(q0, qt), :]
    sin_q = sin_ref[pl.ds(q0, qt), :]
    qs = []
    for s in range(4):
        slab = q_ref[0, :, s * LANES:(s + 1) * LANES]
        qs.append((_rope(_head_rms(slab, qn_ref[...]), cos_q, sin_q) * ATTN_SCALE).astype(BF16))
    qpos = q0 + (lax.broadcasted_iota(jnp.int32, (4 * qt, wk), 0) & (qt - 1))
    kpos = ws + lax.broadcasted_iota(jnp.int32, (4 * qt, wk), 1)
    near4 = jnp.abs(qpos - kpos) <= SWA_WINDOW
    for kh in range(2):
        sl = slice(kh * hd, (kh + 1) * hd)
        qstack = jnp.concatenate(
            [qs[kh * 2 + g // 2][:, (g % 2) * hd:(g % 2 + 1) * hd] for g in range(4)], axis=0)
        sink = jnp.concatenate(
            [jnp.broadcast_to(sink_ref[kh * 4 + g:kh * 4 + g + 1, 0:1], (qt, 1)) for g in range(4)], axis=0)
        s_loc = jnp.where(near4, _dot_nt(qstack, kw[:, sl]), NEG_INF)
        s_ctx = _dot_nt(qstack, kc[:, sl])
        o = _softmax_pv([s_loc, s_ctx], [vw[:, sl], vc[:, sl]], extra=sink)
        for pair in range(2):
            col = (kh * 2 + pair) * LANES
            o_ref[0, :, col:col + LANES] = jnp.concatenate(
                [o[(2 * pair) * qt:(2 * pair + 1) * qt], o[(2 * pair + 1) * qt:(2 * pair + 2) * qt]], axis=-1)


def _swa_attention(p_lat, p_ctx, cos2, sin2, qn, kn, sink_rows, n_q_heads, n_kv_heads):
    b, s, _ = p_lat.shape
    l = p_ctx.shape[1]
    kv_pairs = n_kv_heads // 2
    q_blocks = n_q_heads * HEAD_DIM // LANES
    q_per_pair = q_blocks // kv_pairs
    qw = q_per_pair * LANES
    tab = lambda kp, bb, n: (0, 0)
    return pl.pallas_call(
        functools.partial(_swa_kernel, seq=s),
        grid=(kv_pairs, b, s // SWA_Q_TILE),
        in_specs=[pl.BlockSpec((1, SWA_Q_TILE, qw), lambda kp, bb, n: (bb, n, kp)),
                  pl.BlockSpec((1, s, LANES), lambda kp, bb, n: (bb, 0, q_blocks + kp)),
                  pl.BlockSpec((1, s, LANES), lambda kp, bb, n: (bb, 0, q_blocks + kv_pairs + kp)),
                  pl.BlockSpec((1, l, LANES), lambda kp, bb, n: (bb, 0, q_blocks + kp)),
                  pl.BlockSpec((1, l, LANES), lambda kp, bb, n: (bb, 0, q_blocks + kv_pairs + kp)),
                  pl.BlockSpec(cos2.shape, tab),
                  pl.BlockSpec(sin2.shape, tab),
                  pl.BlockSpec((1, LANES), tab),
                  pl.BlockSpec((1, LANES), tab),
                  pl.BlockSpec((8, LANES), lambda kp, bb, n: (kp, 0))],
        out_specs=pl.BlockSpec((1, SWA_Q_TILE, qw), lambda kp, bb, n: (bb, n, kp)),
        out_shape=jax.ShapeDtypeStruct((b, s, n_q_heads * HEAD_DIM), F32),
        compiler_params=_params("arbitrary", "arbitrary", "arbitrary"),
        name="windowed_gqa",
    )(p_lat, p_lat, p_lat, p_ctx, p_ctx, cos2, sin2, qn, kn, sink_rows)


def _top_rows(x, k, scr, want_rank=False):
    work = x
    rank = jnp.full(x.shape, float(PEER_TOPK), F32) if want_rank else None
    for r in range(k):
        m = jnp.max(work, axis=0, keepdims=True)
        scr[r:r + 1, :] = m
        hit = work == m
        if want_rank and r < PEER_TOPK:
            rank = jnp.where(hit, float(r), rank)
        if r + 1 < k:
            work = jnp.where(hit, NEG_INF, work)
    return rank


def _count_at_least(sorted_scr, y):
    row = lambda i: sorted_scr[i:i + 1, :]
    c8 = row(7) >= y
    c4 = jnp.where(c8, row(11), row(3)) >= y
    c2 = jnp.where(c8, jnp.where(c4, row(13), row(9)), jnp.where(c4, row(5), row(1))) >= y
    hi = jnp.where(c4, jnp.where(c2, row(14), row(12)), jnp.where(c2, row(10), row(8)))
    lo = jnp.where(c4, jnp.where(c2, row(6), row(4)), jnp.where(c2, row(2), row(0)))
    c1 = jnp.where(c8, hi, lo) >= y
    count = (jnp.where(c8, 8.0, 0.0) + jnp.where(c4, 4.0, 0.0)) + (jnp.where(c2, 2.0, 0.0) + jnp.where(c1, 1.0, 0.0))
    return count + jnp.where(row(15) >= y, 1.0, 0.0)


def _peer_scores_kernel(h_ref, wq_ref, keys_ref, ra_ref, p1_ref, gb_ref, p2_ref,
                        q_scr, s_scr, a_scr, b_scr, c_scr, v_scr, *, n_heads):
    k = PEER_TOPK
    n_blocks = h_ref.shape[0] // LANES

    q = _dot(h_ref[...], wq_ref[...])
    for hp in range(2 * n_heads):
        q_scr[hp] = q[:, hp * PEER_KEYS:(hp + 1) * PEER_KEYS].astype(BF16)

    def lane_block(lb, h, slot):
        tops_a, tops_b, cand, tops_c = a_scr.at[slot], b_scr.at[slot], c_scr.at[slot], v_scr.at[slot]
        s1 = s_scr[0, lb]
        s2 = s_scr[1, lb]
        rank_a = _top_rows(s1, k + 1, tops_a, want_rank=True)
        _top_rows(s2, k + 1, tops_b)
        cand[0:16, :] = tops_a[0:1, :] + tops_b[0:16, :]
        for i in range(1, 8):
            cand[8 + 8 * i:16 + 8 * i, :] = tops_a[i:i + 1, :] + tops_b[0:8, :]
        cand[72:80, :] = tops_a[8:16, :] + tops_b[0:1, :]
        cand[80:81, :] = tops_a[0:1, :] + tops_b[16:17, :]
        cand[81:82, :] = tops_a[16:17, :] + tops_b[0:1, :]
        cand[82:88, :] = jnp.full((6, LANES), NEG_INF, F32)
        _top_rows(cand[...], k + 1, tops_c)
        thr = 0.5 * (tops_c[k - 1:k, :] + tops_c[k:k + 1, :])
        z = jnp.sum(jnp.exp(tops_c[0:k, :] - tops_c[0:1, :]), axis=0, keepdims=True)
        ra_ref[lb, h] = rank_a
        p1_ref[lb, h] = jnp.exp(s1 - tops_a[0:1, :]) / z
        gb_ref[lb, h] = _count_at_least(tops_a, thr - s2).astype(BF16)
        p2_ref[lb, h] = jnp.exp(s2 - tops_b[0:1, :]).astype(BF16)

    def lane_pair(i, h):
        lane_block(2 * i, h, 0)
        lane_block(2 * i + 1, h, 1)
        return h

    def head(h, carry):
        for p in range(2):
            st = _dot_nt(keys_ref[2 * h + p], q_scr[2 * h + p])
            for lb in range(n_blocks):
                s_scr[p, lb] = st[:, lb * LANES:(lb + 1) * LANES]
        lax.fori_loop(0, n_blocks // 2, lane_pair, h)
        return carry

    lax.fori_loop(0, n_heads, head, 0)


def _peer_scores(h2, wq, keys, n_heads):
    t, d = h2.shape
    tt = PEER_TOKEN_TILE
    nb = tt // LANES
    tab_shape = (t // LANES, n_heads, PEER_KEYS, LANES)
    tab_spec = pl.BlockSpec((nb, n_heads, PEER_KEYS, LANES), lambda i: (i, 0, 0, 0))
    return pl.pallas_call(
        functools.partial(_peer_scores_kernel, n_heads=n_heads),
        grid=(t // tt,),
        in_specs=[pl.BlockSpec((tt, d), lambda i: (i, 0)),
                  pl.BlockSpec(wq.shape, lambda i: (0, 0)),
                  pl.BlockSpec(keys.shape, lambda i: (0, 0, 0))],
        out_specs=[tab_spec] * 4,
        out_shape=[jax.ShapeDtypeStruct(tab_shape, dt) for dt in (F32, F32, BF16, BF16)],
        scratch_shapes=[pltpu.VMEM((2 * n_heads, tt, PEER_KEYS), BF16),
                        pltpu.VMEM((2, nb, PEER_KEYS, LANES), F32),
                        pltpu.VMEM((2, 24, LANES), F32), pltpu.VMEM((2, 24, LANES), F32),
                        pltpu.VMEM((2, 88, LANES), F32), pltpu.VMEM((2, 24, LANES), F32)],
        compiler_params=_params("arbitrary"),
        name="peer_scores",
    )(h2, wq, keys)


def _peer_mix_kernel(h_ref, u_ref, vt_ref, ra_ref, p1_ref, gb_ref, p2_ref, x_ref, g_ref, o_ref,
                     acc_scr, act_scr, a_scr, *, n_heads, n_tiles):
    s = pl.program_id(1)
    tt = h_ref.shape[0]
    rows_per_tile = PEER_EXPERT_TILE // PEER_KEYS
    tile = (PEER_KEYS, LANES)

    def readout():
        acc_scr[...] += _dot(vt_ref[0], a_scr[...])

    def gates():
        for aa in range(rows_per_tile):
            rs = slice(aa * PEER_KEYS, (aa + 1) * PEER_KEYS)
            for lb in range(tt // LANES):
                ls = slice(lb * LANES, (lb + 1) * LANES)
                w = None
                for h in range(n_heads):
                    rank = jnp.broadcast_to(ra_ref[lb, h, aa:aa + 1, :].astype(BF16), tile)
                    p1 = jnp.broadcast_to(p1_ref[lb, h, aa:aa + 1, :].astype(BF16), tile)
                    term = jnp.where(rank < gb_ref[lb, h], p2_ref[lb, h], jnp.zeros(tile, BF16)) * p1
                    w = term if w is None else w + term
                act = act_scr[rs, ls]
                gelu = 0.5 * act * (1.0 + lax.erf(act * INV_SQRT2))
                a_scr[rs, ls] = w * gelu.astype(BF16)

    def experts():
        act_scr[...] = _dot_nt(u_ref[...], h_ref[...])

    @pl.when(s == 0)
    def _():
        acc_scr[...] = jnp.zeros_like(acc_scr)
        act_scr[...] = jnp.zeros_like(act_scr)
        a_scr[...] = jnp.zeros_like(a_scr)

    @pl.when(s < n_tiles)
    def _():
        readout()
        gates()
        experts()

    @pl.when(s == n_tiles)
    def _():
        readout()
        gates()

    @pl.when(s == n_tiles + 1)
    def _():
        readout()
        o_ref[...] = x_ref[...] + g_ref[0] * acc_scr[...].T


def _peer_mix(h2, u, vt, tables, x, gate, seg, n_heads):
    t, d = h2.shape
    n_exp = u.shape[0]
    tt = min(PEER_MIX_TOKEN_TILE, seg)
    te = PEER_EXPERT_TILE
    n_tiles = n_exp // te
    per_seg = seg // tt
    r = gate.shape[0]
    last = n_tiles - 1
    nb = tt // LANES
    tab_spec = pl.BlockSpec((nb, n_heads, PEER_KEYS, LANES), lambda i, s: (i, 0, 0, 0))
    row_spec = pl.BlockSpec((nb, n_heads, te // PEER_KEYS, LANES),
                            lambda i, s: (i, 0, jnp.clip(s - 1, 0, last), 0))
    return pl.pallas_call(
        functools.partial(_peer_mix_kernel, n_heads=n_heads, n_tiles=n_tiles),
        grid=(t // tt, n_tiles + 2),
        in_specs=[pl.BlockSpec((tt, d), lambda i, s: (i, 0)),
                  pl.BlockSpec((te, d), lambda i, s: (jnp.minimum(s, last), 0)),
                  pl.BlockSpec((1, d, te), lambda i, s: (jnp.clip(s - 2, 0, last), 0, 0)),
                  row_spec, row_spec, tab_spec, tab_spec,
                  pl.BlockSpec((tt, d), lambda i, s: (i, 0)),
                  pl.BlockSpec((1, 1, d), lambda i, s: (i // per_seg, 0, 0))],
        out_specs=pl.BlockSpec((tt, d), lambda i, s: (i, 0)),
        out_shape=jax.ShapeDtypeStruct((t, d), F32),
        scratch_shapes=[pltpu.VMEM((d, tt), F32), pltpu.VMEM((te, tt), F32), pltpu.VMEM((te, tt), BF16)],
        compiler_params=_params("arbitrary", "arbitrary"),
        name="peer_mix",
    )(h2, u, vt, *tables, x, gate.reshape(r, 1, d))


def _peer(h2, x, gate, seg, wq, keys, u, vt):
    n_heads = keys.shape[0] // 2
    tables = _peer_scores(h2, wq, keys, n_heads)
    return _peer_mix(h2, u, vt, tables, x, gate, seg, n_heads)


def _rope_tables(n_tokens):
    t = jnp.arange(n_tokens)
    row = (t // GRID_W).astype(F32)
    col = (t % GRID_W).astype(F32)
    n_freq = HEAD_DIM // 4
    inv_freq = jnp.power(ROPE_BASE, -jnp.arange(n_freq, dtype=F32) / n_freq)
    ang = jnp.concatenate([row[:, None] * inv_freq, col[:, None] * inv_freq], axis=-1)
    cos, sin = jnp.cos(ang), jnp.sin(ang)
    cos2 = jnp.tile(jnp.concatenate([cos, cos], axis=-1), (1, 2))
    sin2 = jnp.tile(jnp.concatenate([-sin, sin], axis=-1), (1, 2))
    return cos2, sin2


def _pair_lanes(v):
    return jnp.repeat(v, HEAD_DIM, axis=-1).reshape(*v.shape[:-1], v.shape[-1] // 2, LANES)


def kernel(x, c, ctx, c_ctx, mod_w, mod_b, norm1_g, norm2_g, ab_w_in, ab_w_out, na_q_norm, na_k_norm,
           na_rpb, ret_log_decay, swa_w_in, swa_w_out, swa_q_norm, swa_k_norm, swa_sink,
           peer_w_q, peer_sub_keys, peer_u, peer_v):
    b, s, d = x.shape
    l = ctx.shape[1]
    depth = mod_w.shape[0]
    assert depth == 2 and b + 1 <= 8
    assert s % (NA_Q_ROWS * GRID_W) == 0 and s // GRID_W >= NA_BAND_ROWS
    assert s % TOKEN_TILE == 0 and (b * l) % TOKEN_TILE == 0 and l % RET_CHUNK == 0

    cond = jnp.concatenate([c, c_ctx[None, :], jnp.zeros((8 - b - 1, d), F32)], axis=0)
    mods = _adaln(cond, mod_w, mod_b).reshape(depth, 8, 6, d)
    lat = lambda layer, which: mods[layer, :b, which]
    cx = lambda layer, which: mods[layer, b:b + 1, which]

    cos2, sin2 = _rope_tables(s)
    x_lat = x.reshape(b * s, d)
    x_ctx = ctx.reshape(b * l, d)
    tile2 = lambda g: jnp.tile(g, 2).reshape(1, LANES)

    def peer_weights(layer):
        n_heads = peer_sub_keys.shape[1]
        keys = peer_sub_keys[layer].reshape(2 * n_heads, PEER_KEYS, -1).astype(BF16)
        vt = peer_v[layer].reshape(-1, PEER_EXPERT_TILE, d).transpose(0, 2, 1).astype(BF16)
        return peer_w_q[layer].astype(BF16), keys, peer_u[layer].astype(BF16), vt

    n_na = na_rpb.shape[1]
    n_ret = ret_log_decay.shape[2]
    na_pairs, ret_pairs = n_na // 2, n_ret // 2
    w_in = ab_w_in[0].astype(BF16)
    w_out = ab_w_out[0].astype(BF16)
    wa = n_na * HEAD_DIM
    p_lat = _modmm(x_lat, norm1_g[0], lat(0, 0), lat(0, 1), w_in, s).reshape(b, s, -1)
    p_ctx = _modmm(x_ctx, norm1_g[0], cx(0, 0), cx(0, 1), w_in, b * l).reshape(b, l, -1)

    qn, kn = tile2(na_q_norm[0]), tile2(na_k_norm[0])
    oa_lat = _na_attention(p_lat, p_ctx, _na_bias(na_rpb[0], s // GRID_W), qn, kn, na_pairs)
    oa_ctx = _ctx_attention(p_ctx, qn, kn, na_pairs)

    lg = _pair_lanes(ret_log_decay[0]).transpose(1, 0, 2)
    ret_col0 = 3 * wa // LANES
    zeros_state = jnp.zeros((b, ret_pairs, 4, HEAD_DIM, HEAD_DIM), F32)
    ones_tab, zeros_tab = jnp.ones((l, LANES), F32), jnp.zeros((l, LANES), F32)
    ob_ctx, st_ctx = _retention(p_ctx, ones_tab, zeros_tab, lg, zeros_state, ret_col0, ret_pairs, False)
    ob_lat, _ = _retention(p_lat, cos2, sin2, lg, st_ctx, ret_col0, ret_pairs, True)

    w_list = [w_out[:wa], w_out[wa:]]
    x_lat, h_lat = _outproj([oa_lat.reshape(b * s, -1), ob_lat.reshape(b * s, -1)], w_list, x_lat,
                            lat(0, 2), norm2_g[0], lat(0, 3), lat(0, 4), s)
    x_ctx, h_ctx = _outproj([oa_ctx.reshape(b * l, -1), ob_ctx.reshape(b * l, -1)], w_list, x_ctx,
                            cx(0, 2), norm2_g[0], cx(0, 3), cx(0, 4), b * l)
    pw = peer_weights(0)
    x_lat = _peer(h_lat, x_lat, lat(0, 5), s, *pw)
    x_ctx = _peer(h_ctx, x_ctx, cx(0, 5), b * l, *pw)

    n_q = swa_sink.shape[1]
    n_kv = (swa_w_in.shape[2] // HEAD_DIM - n_q) // 2
    w_in = swa_w_in[0].astype(BF16)
    p_lat = _modmm(x_lat, norm1_g[1], lat(1, 0), lat(1, 1), w_in, s).reshape(b, s, -1)
    p_ctx = _modmm(x_ctx, norm1_g[1], cx(1, 0), cx(1, 1), w_in, b * l).reshape(b, l, -1)
    sink_rows = jnp.broadcast_to(swa_sink[0][:, None], (n_q, LANES))
    o_lat = _swa_attention(p_lat, p_ctx, cos2, sin2, tile2(swa_q_norm[0]), tile2(swa_k_norm[0]),
                           sink_rows, n_q, n_kv)
    x_lat, h_lat = _outproj([o_lat.reshape(b * s, -1)], [swa_w_out[0].astype(BF16)], x_lat,
                            lat(1, 2), norm2_g[1], lat(1, 3), lat(1, 4), s)
    x_lat = _peer(h_lat, x_lat, lat(1, 5), s, *peer_weights(1))
    return x_lat.reshape(b, s, d)
```

```python
import functools

import numpy as np
import jax
import jax.numpy as jnp
from jax import lax
from jax.experimental import pallas as pl
from jax.experimental.pallas import tpu as pltpu

F32 = jnp.float32
BF16 = jnp.bfloat16

HEAD_DIM = 64
GRID_W = 64
NA_ROWS = 8
NA_COLS = 16
SWA_WINDOW = 128
PEER_TOPK = 16
PEER_KEYS = 128
ROPE_BASE = 10000.0
NORM_EPS = 1e-6
GN_EPS = 1e-5
NEG_INF = -1e30
ATTN_SCALE = HEAD_DIM ** -0.5
INV_SQRT2 = 0.7071067811865476

LANES = 128
VMEM_LIMIT = 56 * 1024 * 1024

TOKEN_TILE = 512
PROJ_TOKEN_TILE = 1024
PROJ_N_TILE = 1792
RET_CHUNK = 128
NA_Q_ROWS = 8
NA_BAND_ROWS = 16
SWA_Q_TILE = 256
PEER_TOKEN_TILE = 512
PEER_MIX_TOKEN_TILE = 1024
PEER_EXPERT_TILE = 1024


def _dot(a, b):
    return jnp.dot(a, b, preferred_element_type=F32)


def _dot_nt(a, b):
    return lax.dot_general(a, b, (((1,), (1,)), ((), ())), preferred_element_type=F32)


def _dot_tn(a, b):
    return lax.dot_general(a, b, (((0,), (0,)), ((), ())), preferred_element_type=F32)


def _params(*sem):
    return pltpu.CompilerParams(dimension_semantics=sem, vmem_limit_bytes=VMEM_LIMIT)


def _rms_rows(x, gain):
    ms = jnp.mean(x * x, axis=-1, keepdims=True)
    return x * lax.rsqrt(ms + NORM_EPS) * gain


def _modulate(x, gain, shift, scale):
    return _rms_rows(x, gain) * (1.0 + scale) + shift


def _head_rms(x, gain):
    lane = lax.broadcasted_iota(jnp.int32, x.shape, 1)
    lo = lane < HEAD_DIM
    ss = x * x
    s_lo = jnp.sum(jnp.where(lo, ss, 0.0), axis=-1, keepdims=True)
    s_hi = jnp.sum(jnp.where(lo, 0.0, ss), axis=-1, keepdims=True)
    ms = jnp.where(lo, s_lo, s_hi) * (1.0 / HEAD_DIM)
    return x * lax.rsqrt(ms + NORM_EPS) * gain


def _rope(x, cos2, sin2):
    lane = lax.broadcasted_iota(jnp.int32, x.shape, 1)
    first_half = (lane & (HEAD_DIM // 2)) == 0
    swapped = jnp.where(first_half, pltpu.roll(x, LANES - HEAD_DIM // 2, axis=1),
                        pltpu.roll(x, HEAD_DIM // 2, axis=1))
    return x * cos2 + swapped * sin2


def _adaln_kernel(c_ref, w_ref, b_ref, o_ref):
    c = c_ref[...]
    s = c * jax.nn.sigmoid(c)
    w = w_ref[0]
    s_hi = s.astype(BF16)
    s_lo = (s - s_hi.astype(F32)).astype(BF16)
    w_hi = w.astype(BF16)
    w_lo = (w - w_hi.astype(F32)).astype(BF16)
    acc = _dot(s_hi, w_hi) + _dot(s_lo, w_hi) + _dot(s_hi, w_lo)
    o_ref[0] = acc + b_ref[0]


def _adaln(cond, mod_w, mod_b):
    depth, d, n = mod_w.shape
    tn = n // 4
    return pl.pallas_call(
        _adaln_kernel,
        grid=(depth, n // tn),
        in_specs=[pl.BlockSpec((8, d), lambda l, j: (0, 0)),
                  pl.BlockSpec((1, d, tn), lambda l, j: (l, 0, j)),
                  pl.BlockSpec((1, 1, tn), lambda l, j: (l, 0, j))],
        out_specs=pl.BlockSpec((1, 8, tn), lambda l, j: (l, 0, j)),
        out_shape=jax.ShapeDtypeStruct((depth, 8, n), F32),
        compiler_params=_params("arbitrary", "arbitrary"),
        name="adaln",
    )(cond, mod_w, mod_b.reshape(depth, 1, n))


def _modmm_kernel(x_ref, g_ref, sh_ref, sc_ref, w_ref, o_ref, h_scr):
    @pl.when(pl.program_id(1) == 0)
    def _():
        h_scr[...] = _modulate(x_ref[...], g_ref[...], sh_ref[0], sc_ref[0]).astype(BF16)

    o_ref[...] = _dot(h_scr[...], w_ref[...])


def _modmm(x, gain, shift, scale, w, seg):
    t, d = x.shape
    n = w.shape[1]
    tm = min(PROJ_TOKEN_TILE, seg)
    tn = n if n <= PROJ_N_TILE else n // 2
    per_seg = seg // tm
    r = shift.shape[0]
    return pl.pallas_call(
        _modmm_kernel,
        grid=(t // tm, n // tn),
        in_specs=[pl.BlockSpec((tm, d), lambda i, j: (i, 0)),
                  pl.BlockSpec((1, d), lambda i, j: (0, 0)),
                  pl.BlockSpec((1, 1, d), lambda i, j: (i // per_seg, 0, 0)),
                  pl.BlockSpec((1, 1, d), lambda i, j: (i // per_seg, 0, 0)),
                  pl.BlockSpec((d, tn), lambda i, j: (0, j))],
        out_specs=pl.BlockSpec((tm, tn), lambda i, j: (i, j)),
        out_shape=jax.ShapeDtypeStruct((t, n), F32),
        scratch_shapes=[pltpu.VMEM((tm, d), BF16)],
        compiler_params=_params("arbitrary", "arbitrary"),
        name="modulate_matmul",
    )(x, gain.reshape(1, d), shift.reshape(r, 1, d), scale.reshape(r, 1, d), w)


def _outproj_kernel(*refs, n_in):
    a_refs = refs[:n_in]
    w_refs = refs[n_in:2 * n_in]
    x_ref, gate_ref, g2_ref, sh_ref, sc_ref, xo_ref, h_ref = refs[2 * n_in:]
    acc = None
    for a_ref, w_ref in zip(a_refs, w_refs):
        part = _dot(a_ref[...].astype(BF16), w_ref[...])
        acc = part if acc is None else acc + part
    xn = x_ref[...] + gate_ref[0] * acc
    xo_ref[...] = xn
    h_ref[...] = _modulate(xn, g2_ref[...], sh_ref[0], sc_ref[0]).astype(BF16)


def _outproj(a_list, w_list, x, gate, gain2, shift2, scale2, seg):
    t, d = x.shape
    tm = min(TOKEN_TILE, seg)
    per_seg = seg // tm
    r = gate.shape[0]
    n_in = len(a_list)
    row = lambda i: (i // per_seg, 0, 0)
    in_specs = ([pl.BlockSpec((tm, a.shape[1]), lambda i: (i, 0)) for a in a_list]
                + [pl.BlockSpec(w.shape, lambda i: (0, 0)) for w in w_list]
                + [pl.BlockSpec((tm, d), lambda i: (i, 0)),
                   pl.BlockSpec((1, 1, d), row),
                   pl.BlockSpec((1, d), lambda i: (0, 0)),
                   pl.BlockSpec((1, 1, d), row),
                   pl.BlockSpec((1, 1, d), row)])
    return pl.pallas_call(
        functools.partial(_outproj_kernel, n_in=n_in),
        grid=(t // tm,),
        in_specs=in_specs,
        out_specs=[pl.BlockSpec((tm, d), lambda i: (i, 0)), pl.BlockSpec((tm, d), lambda i: (i, 0))],
        out_shape=[jax.ShapeDtypeStruct((t, d), F32), jax.ShapeDtypeStruct((t, d), BF16)],
        compiler_params=_params("arbitrary"),
        name="out_proj_residual",
    )(*a_list, *w_list, x, gate.reshape(r, 1, d), gain2.reshape(1, d),
      shift2.reshape(r, 1, d), scale2.reshape(r, 1, d))


def _softmax_pv(s_list, v_list, extra=None):
    m = s_list[0].max(axis=-1, keepdims=True)
    for s in s_list[1:]:
        m = jnp.maximum(m, s.max(axis=-1, keepdims=True))
    if extra is not None:
        m = jnp.maximum(m, extra)
    denom = None
    out = None
    for s, v in zip(s_list, v_list):
        p = jnp.exp(s - m)
        ps = p.sum(axis=-1, keepdims=True)
        denom = ps if denom is None else denom + ps
        pv = _dot(p.astype(BF16), v)
        out = pv if out is None else out + pv
    if extra is not None:
        denom = denom + jnp.exp(extra - m)
    return out / denom


def _na_kernel(q_ref, k_ref, v_ref, kc_ref, vc_ref, bias_ref, qn_ref, kn_ref, o_ref,
               k_scr, v_scr, kc_scr, vc_scr, *, n_steps):
    i = pl.program_id(2)
    band = NA_BAND_ROWS * GRID_W
    start = jnp.clip(NA_Q_ROWS * i - NA_ROWS // 2, 0, NA_Q_ROWS * n_steps - NA_BAND_ROWS) * GRID_W
    start = pl.multiple_of(start, NA_ROWS // 2 * GRID_W)

    @pl.when(i == 0)
    def _():
        k_scr[...] = _head_rms(k_ref[0], kn_ref[...]).astype(BF16)
        v_scr[...] = v_ref[0].astype(BF16)
        kc_scr[...] = _head_rms(kc_ref[0], kn_ref[...]).astype(BF16)
        vc_scr[...] = vc_ref[0].astype(BF16)

    q = (_head_rms(q_ref[0], qn_ref[...]) * ATTN_SCALE).astype(BF16)
    kb = k_scr[pl.ds(start, band), :]
    vb = v_scr[pl.ds(start, band), :]
    kc = kc_scr[...]
    vc = vc_scr[...]
    outs = []
    for h in range(2):
        sl = slice(h * HEAD_DIM, (h + 1) * HEAD_DIM)
        s_loc = _dot_nt(q[:, sl], kb[:, sl]) + bias_ref[0, h]
        s_ctx = _dot_nt(q[:, sl], kc[:, sl])
        outs.append(_softmax_pv([s_loc, s_ctx], [vb[:, sl], vc[:, sl]]))
    o_ref[0] = jnp.concatenate(outs, axis=-1)


def _na_bias(rpb, rows):
    h = rpb.shape[0]
    n_steps = rows // NA_Q_ROWS
    col = np.arange(GRID_W)
    c0 = np.clip(col - NA_COLS // 2, 0, GRID_W - NA_COLS)
    dc = col[None, :] - col[:, None] + (NA_COLS - 1)
    ok_c = (col[None, :] >= c0[:, None]) & (col[None, :] < c0[:, None] + NA_COLS)
    pick = ((np.arange(2 * NA_COLS - 1)[:, None, None] == dc[None]) & ok_c[None]).astype(np.float32)
    blocks = jnp.einsum('hrd,dqk->hrqk', rpb, pick, precision=lax.Precision.HIGHEST)
    blocks = jnp.where(ok_c, blocks, NEG_INF)
    masked = jnp.full((h, GRID_W, GRID_W), NEG_INF, F32)
    classes = []
    for step in (0, 1, n_steps - 1):
        band0 = min(max(NA_Q_ROWS * step - NA_ROWS // 2, 0), rows - NA_BAND_ROWS)
        q_rows = []
        for rq in range(NA_Q_ROWS):
            r = NA_Q_ROWS * step + rq
            r0 = min(max(r - NA_ROWS // 2, 0), rows - NA_ROWS)
            row = []
            for rk in range(band0, band0 + NA_BAND_ROWS):
                row.append(blocks[:, rk - r + NA_ROWS - 1] if r0 <= rk < r0 + NA_ROWS else masked)
            q_rows.append(jnp.concatenate(row, axis=-1))
        classes.append(jnp.concatenate(q_rows, axis=-2))
    return jnp.stack(classes)


def _na_attention(p_lat, p_ctx, bias, qn, kn, n_pairs):
    b, s, _ = p_lat.shape
    l = p_ctx.shape[1]
    rows = s // GRID_W
    n_steps = rows // NA_Q_ROWS
    qt = NA_Q_ROWS * GRID_W

    def cls(i):
        return jnp.where(i == 0, 0, jnp.where(i == n_steps - 1, 2, 1))

    return pl.pallas_call(
        functools.partial(_na_kernel, n_steps=n_steps),
        grid=(n_pairs, b, n_steps),
        in_specs=[pl.BlockSpec((1, qt, LANES), lambda hp, bb, i: (bb, i, hp)),
                  pl.BlockSpec((1, s, LANES), lambda hp, bb, i: (bb, 0, n_pairs + hp)),
                  pl.BlockSpec((1, s, LANES), lambda hp, bb, i: (bb, 0, 2 * n_pairs + hp)),
                  pl.BlockSpec((1, l, LANES), lambda hp, bb, i: (bb, 0, n_pairs + hp)),
                  pl.BlockSpec((1, l, LANES), lambda hp, bb, i: (bb, 0, 2 * n_pairs + hp)),
                  pl.BlockSpec((1, 2, qt, NA_BAND_ROWS * GRID_W), lambda hp, bb, i: (cls(i), hp, 0, 0)),
                  pl.BlockSpec((1, LANES), lambda hp, bb, i: (0, 0)),
                  pl.BlockSpec((1, LANES), lambda hp, bb, i: (0, 0))],
        out_specs=pl.BlockSpec((1, qt, LANES), lambda hp, bb, i: (bb, i, hp)),
        out_shape=jax.ShapeDtypeStruct((b, s, n_pairs * LANES), F32),
        scratch_shapes=[pltpu.VMEM((s, LANES), BF16), pltpu.VMEM((s, LANES), BF16),
                        pltpu.VMEM((l, LANES), BF16), pltpu.VMEM((l, LANES), BF16)],
        compiler_params=_params("arbitrary", "arbitrary", "arbitrary"),
        name="neighbourhood_attention",
    )(p_lat, p_lat, p_lat, p_ctx, p_ctx, bias, qn, kn)


def _ctx_attn_kernel(q_ref, k_ref, v_ref, qn_ref, kn_ref, o_ref):
    q = (_head_rms(q_ref[0], qn_ref[...]) * ATTN_SCALE).astype(BF16)
    k = _head_rms(k_ref[0], kn_ref[...]).astype(BF16)
    v = v_ref[0].astype(BF16)
    outs = []
    for h in range(2):
        sl = slice(h * HEAD_DIM, (h + 1) * HEAD_DIM)
        outs.append(_softmax_pv([_dot_nt(q[:, sl], k[:, sl])], [v[:, sl]]))
    o_ref[0] = jnp.concatenate(outs, axis=-1)


def _ctx_attention(p_ctx, qn, kn, n_pairs):
    b, l, _ = p_ctx.shape
    return pl.pallas_call(
        _ctx_attn_kernel,
        grid=(n_pairs, b),
        in_specs=[pl.BlockSpec((1, l, LANES), lambda hp, bb: (bb, 0, hp)),
                  pl.BlockSpec((1, l, LANES), lambda hp, bb: (bb, 0, n_pairs + hp)),
                  pl.BlockSpec((1, l, LANES), lambda hp, bb: (bb, 0, 2 * n_pairs + hp)),
                  pl.BlockSpec((1, LANES), lambda hp, bb: (0, 0)),
                  pl.BlockSpec((1, LANES), lambda hp, bb: (0, 0))],
        out_specs=pl.BlockSpec((1, l, LANES), lambda hp, bb: (bb, 0, hp)),
        out_shape=jax.ShapeDtypeStruct((b, l, n_pairs * LANES), F32),
        compiler_params=_params("arbitrary", "arbitrary"),
        name="context_attention",
    )(p_ctx, p_ctx, p_ctx, qn, kn)


def _ret_kernel(q_ref, k_ref, v_ref, g_ref, cos_ref, sin_ref, lg_ref, s0_ref, y_ref, st_ref, sf_scr,
                *, n_chunks, use_rope):
    c = RET_CHUNK
    hd = HEAD_DIM
    lg = -jnp.exp(lg_ref[0])
    lgf, lgb = lg[0:1, :], lg[1:2, :]
    ii = lax.broadcasted_iota(jnp.int32, (c, LANES), 0).astype(F32)
    dq_f = jnp.exp(lgf * (ii + 1.0))
    dk_f = jnp.exp(lgf * (c - 1.0 - ii))
    dq_b = jnp.exp(lgb * (c - ii))
    dk_b = jnp.exp(lgb * ii)
    dc_f = jnp.exp(lgf * float(c))
    dc_b = jnp.exp(lgb * float(c))
    diff = (lax.broadcasted_iota(jnp.int32, (c, c), 0) - lax.broadcasted_iota(jnp.int32, (c, c), 1)).astype(F32)
    intra = []
    chunk_f = []
    chunk_b = []
    for h in range(2):
        lf = lgf[:, h * hd:h * hd + 1]
        lb = lgb[:, h * hd:h * hd + 1]
        intra.append(jnp.where(diff >= 0, jnp.exp(lf * jnp.maximum(diff, 0.0)),
                               jnp.exp(lb * jnp.maximum(-diff, 0.0))))
        chunk_f.append(dc_f[:, h * hd:h * hd + 1])
        chunk_b.append(dc_b[:, h * hd:h * hd + 1])

    def load(n):
        r = pl.multiple_of(n * c, c)
        q = q_ref[0, pl.ds(r, c), :]
        k = k_ref[0, pl.ds(r, c), :]
        v = v_ref[0, pl.ds(r, c), :]
        if use_rope:
            cs = cos_ref[pl.ds(r, c), :]
            sn = sin_ref[pl.ds(r, c), :]
            q = _rope(q, cs, sn)
            k = _rope(k, cs, sn)
        return r, q * ATTN_SCALE, k, v.astype(BF16)

    def fwd(n, carry):
        _, _, k, v = load(n)
        kd = (k * dk_f).astype(BF16)
        new = []
        for h in range(2):
            sl = slice(h * hd, (h + 1) * hd)
            sf_scr[n, h] = carry[h]
            new.append(carry[h] * chunk_f[h] + _dot_tn(kd[:, sl], v[:, sl]))
        return tuple(new)

    sf = lax.fori_loop(0, n_chunks, fwd, (s0_ref[0, 0, 0], s0_ref[0, 0, 1]), unroll=2)
    st_ref[0, 0, 0] = sf[0]
    st_ref[0, 0, 1] = sf[1]

    def bwd(jj, carry):
        n = n_chunks - 1 - jj
        r, q, k, v = load(n)
        qb = q.astype(BF16)
        kb = k.astype(BF16)
        qf = (q * dq_f).astype(BF16)
        qr = (q * dq_b).astype(BF16)
        kd = (k * dk_b).astype(BF16)
        outs = []
        new = []
        for h in range(2):
            sl = slice(h * hd, (h + 1) * hd)
            a = (_dot_nt(qb[:, sl], kb[:, sl]) * intra[h]).astype(BF16)
            o = (_dot(a, v[:, sl]) + _dot(qf[:, sl], sf_scr[n, h].astype(BF16))
                 + _dot(qr[:, sl], carry[h].astype(BF16)))
            oc = o - jnp.mean(o, axis=-1, keepdims=True)
            outs.append(oc * lax.rsqrt(jnp.mean(oc * oc, axis=-1, keepdims=True) + GN_EPS))
            new.append(carry[h] * chunk_b[h] + _dot_tn(kd[:, sl], v[:, sl]))
        g = g_ref[0, pl.ds(r, c), :]
        y_ref[0, pl.ds(r, c), :] = jnp.concatenate(outs, axis=-1) * (g * jax.nn.sigmoid(g))
        return tuple(new)

    sb = lax.fori_loop(0, n_chunks, bwd, (s0_ref[0, 0, 2], s0_ref[0, 0, 3]), unroll=2)
    st_ref[0, 0, 2] = sb[0]
    st_ref[0, 0, 3] = sb[1]


def _retention(p, cos2, sin2, lg, s0, col0, n_pairs, use_rope):
    b, t, _ = p.shape
    n_chunks = t // RET_CHUNK
    tab = lambda hp, bb: (0, 0)
    return pl.pallas_call(
        functools.partial(_ret_kernel, n_chunks=n_chunks, use_rope=use_rope),
        grid=(n_pairs, b),
        in_specs=[pl.BlockSpec((1, t, LANES), lambda hp, bb: (bb, 0, col0 + hp)),
                  pl.BlockSpec((1, t, LANES), lambda hp, bb: (bb, 0, col0 + n_pairs + hp)),
                  pl.BlockSpec((1, t, LANES), lambda hp, bb: (bb, 0, col0 + 2 * n_pairs + hp)),
                  pl.BlockSpec((1, t, LANES), lambda hp, bb: (bb, 0, col0 + 3 * n_pairs + hp)),
                  pl.BlockSpec(cos2.shape, tab),
                  pl.BlockSpec(sin2.shape, tab),
                  pl.BlockSpec((1, 2, LANES), lambda hp, bb: (hp, 0, 0)),
                  pl.BlockSpec((1, 1, 4, HEAD_DIM, HEAD_DIM), lambda hp, bb: (bb, hp, 0, 0, 0))],
        out_specs=[pl.BlockSpec((1, t, LANES), lambda hp, bb: (bb, 0, hp)),
                   pl.BlockSpec((1, 1, 4, HEAD_DIM, HEAD_DIM), lambda hp, bb: (bb, hp, 0, 0, 0))],
        out_shape=[jax.ShapeDtypeStruct((b, t, n_pairs * LANES), F32),
                   jax.ShapeDtypeStruct((b, n_pairs, 4, HEAD_DIM, HEAD_DIM), F32)],
        scratch_shapes=[pltpu.VMEM((n_chunks, 2, HEAD_DIM, HEAD_DIM), F32)],
        compiler_params=_params("arbitrary", "arbitrary"),
        name="retention",
    )(p, p, p, p, cos2, sin2, lg, s0)


def _swa_kernel(q_ref, k_ref, v_ref, kc_ref, vc_ref, cos_ref, sin_ref, qn_ref, kn_ref, sink_ref, o_ref,
                *, seq):
    qt = SWA_Q_TILE
    wk = qt + 2 * SWA_WINDOW
    hd = HEAD_DIM
    n = pl.program_id(2)
    q0 = pl.multiple_of(n * qt, qt)
    ws = pl.multiple_of(jnp.clip(n * qt - SWA_WINDOW, 0, seq - wk), SWA_WINDOW)
    kw = _rope(_head_rms(k_ref[0, pl.ds(ws, wk), :], kn_ref[...]),
               cos_ref[pl.ds(ws, wk), :], sin_ref[pl.ds(ws, wk), :]).astype(BF16)
    vw = v_ref[0, pl.ds(ws, wk), :].astype(BF16)
    kc = _head_rms(kc_ref[0], kn_ref[...]).astype(BF16)
    vc = vc_ref[0].astype(BF16)
    cos_q = cos_ref[pl.ds(q0, qt), :]
    sin_q = sin_ref[pl.ds(q0, qt), :]
    qs = []
    for s in range(4):
        slab = q_ref[0, :, s * LANES:(s + 1) * LANES]
        qs.append((_rope(_head_rms(slab, qn_ref[...]), cos_q, sin_q) * ATTN_SCALE).astype(BF16))
    qpos = q0 + (lax.broadcasted_iota(jnp.int32, (4 * qt, wk), 0) & (qt - 1))
    kpos = ws + lax.broadcasted_iota(jnp.int32, (4 * qt, wk), 1)
    near4 = jnp.abs(qpos - kpos) <= SWA_WINDOW
    for kh in range(2):
        sl = slice(kh * hd, (kh + 1) * hd)
        qstack = jnp.concatenate(
            [qs[kh * 2 + g // 2][:, (g % 2) * hd:(g % 2 + 1) * hd] for g in range(4)], axis=0)
        sink = jnp.concatenate(
            [jnp.broadcast_to(sink_ref[kh * 4 + g:kh * 4 + g + 1, 0:1], (qt, 1)) for g in range(4)], axis=0)
        s_loc = jnp.where(near4, _dot_nt(qstack, kw[:, sl]), NEG_INF)
        s_ctx = _dot_nt(qstack, kc[:, sl])
        o = _softmax_pv([s_loc, s_ctx], [vw[:, sl], vc[:, sl]], extra=sink)
        for pair in range(2):
            col = (kh * 2 + pair) * LANES
            o_ref[0, :, col:col + LANES] = jnp.concatenate(
                [o[(2 * pair) * qt:(2 * pair + 1) * qt], o[(2 * pair + 1) * qt:(2 * pair + 2) * qt]], axis=-1)


def _swa_attention(p_lat, p_ctx, cos2, sin2, qn, kn, sink_rows, n_q_heads, n_kv_heads):
    b, s, _ = p_lat.shape
    l = p_ctx.shape[1]
    kv_pairs = n_kv_heads // 2
    q_blocks = n_q_heads * HEAD_DIM // LANES
    q_per_pair = q_blocks // kv_pairs
    qw = q_per_pair * LANES
    tab = lambda kp, bb, n: (0, 0)
    return pl.pallas_call(
        functools.partial(_swa_kernel, seq=s),
        grid=(kv_pairs, b, s // SWA_Q_TILE),
        in_specs=[pl.BlockSpec((1, SWA_Q_TILE, qw), lambda kp, bb, n: (bb, n, kp)),
                  pl.BlockSpec((1, s, LANES), lambda kp, bb, n: (bb, 0, q_blocks + kp)),
                  pl.BlockSpec((1, s, LANES), lambda kp, bb, n: (bb, 0, q_blocks + kv_pairs + kp)),
                  pl.BlockSpec((1, l, LANES), lambda kp, bb, n: (bb, 0, q_blocks + kp)),
                  pl.BlockSpec((1, l, LANES), lambda kp, bb, n: (bb, 0, q_blocks + kv_pairs + kp)),
                  pl.BlockSpec(cos2.shape, tab),
                  pl.BlockSpec(sin2.shape, tab),
                  pl.BlockSpec((1, LANES), tab),
                  pl.BlockSpec((1, LANES), tab),
                  pl.BlockSpec((8, LANES), lambda kp, bb, n: (kp, 0))],
        out_specs=pl.BlockSpec((1, SWA_Q_TILE, qw), lambda kp, bb, n: (bb, n, kp)),
        out_shape=jax.ShapeDtypeStruct((b, s, n_q_heads * HEAD_DIM), F32),
        compiler_params=_params("arbitrary", "arbitrary", "arbitrary"),
        name="windowed_gqa",
    )(p_lat, p_lat, p_lat, p_ctx, p_ctx, cos2, sin2, qn, kn, sink_rows)


def _top_rows(x, k, scr, want_rank=False):
    work = x
    rank = jnp.full(x.shape, float(PEER_TOPK), F32) if want_rank else None
    for r in range(k):
        m = jnp.max(work, axis=0, keepdims=True)
        scr[r:r + 1, :] = m
        hit = work == m
        if want_rank and r < PEER_TOPK:
            rank = jnp.where(hit, float(r), rank)
        if r + 1 < k:
            work = jnp.where(hit, NEG_INF, work)
    return rank


def _bf16_pair(x):
    bits = lax.bitcast_convert_type(x.astype(BF16).astype(F32), jnp.uint32)
    return bits | (bits >> 16)


def _count_at_least(sorted_scr, y):
    row = lambda i: sorted_scr[i:i + 1, :]
    c8 = row(7) >= y
    c4 = jnp.where(c8, row(11), row(3)) >= y
    c2 = jnp.where(c8, jnp.where(c4, row(13), row(9)), jnp.where(c4, row(5), row(1))) >= y
    hi = jnp.where(c4, jnp.where(c2, row(14), row(12)), jnp.where(c2, row(10), row(8)))
    lo = jnp.where(c4, jnp.where(c2, row(6), row(4)), jnp.where(c2, row(2), row(0)))
    c1 = jnp.where(c8, hi, lo) >= y
    count = (jnp.where(c8, 8.0, 0.0) + jnp.where(c4, 4.0, 0.0)) + (jnp.where(c2, 2.0, 0.0) + jnp.where(c1, 1.0, 0.0))
    return count + jnp.where(row(15) >= y, 1.0, 0.0)


def _peer_scores_kernel(h_ref, wq_ref, keys_ref, ra_ref, p1_ref, gb_ref, p2_ref,
                        q_scr, s_scr, a_scr, b_scr, c_scr, v_scr, *, n_heads):
    k = PEER_TOPK
    n_blocks = h_ref.shape[0] // LANES

    q = _dot(h_ref[...], wq_ref[...])
    for hp in range(2 * n_heads):
        q_scr[hp] = q[:, hp * PEER_KEYS:(hp + 1) * PEER_KEYS].astype(BF16)

    def lane_block(lb, h, slot):
        tops_a, tops_b, cand, tops_c = a_scr.at[slot], b_scr.at[slot], c_scr.at[slot], v_scr.at[slot]
        s1 = s_scr[0, lb]
        s2 = s_scr[1, lb]
        rank_a = _top_rows(s1, k + 1, tops_a, want_rank=True)
        _top_rows(s2, k + 1, tops_b)
        cand[0:16, :] = tops_a[0:1, :] + tops_b[0:16, :]
        for i in range(1, 8):
            cand[8 + 8 * i:16 + 8 * i, :] = tops_a[i:i + 1, :] + tops_b[0:8, :]
        cand[72:80, :] = tops_a[8:16, :] + tops_b[0:1, :]
        cand[80:81, :] = tops_a[0:1, :] + tops_b[16:17, :]
        cand[81:82, :] = tops_a[16:17, :] + tops_b[0:1, :]
        cand[82:88, :] = jnp.full((6, LANES), NEG_INF, F32)
        _top_rows(cand[...], k + 1, tops_c)
        thr = 0.5 * (tops_c[k - 1:k, :] + tops_c[k:k + 1, :])
        z = jnp.sum(jnp.exp(tops_c[0:k, :] - tops_c[0:1, :]), axis=0, keepdims=True)
        ra_ref[lb, h] = _bf16_pair(rank_a)
        p1_ref[lb, h] = _bf16_pair(jnp.exp(s1 - tops_a[0:1, :]) * (0.5 / z))
        gb_ref[lb, h] = _count_at_least(tops_a, thr - s2).astype(BF16)
        p2_ref[lb, h] = jnp.exp(s2 - tops_b[0:1, :]).astype(BF16)

    def lane_pair(i, h):
        lane_block(2 * i, h, 0)
        lane_block(2 * i + 1, h, 1)
        return h

    def head(h, carry):
        for p in range(2):
            st = _dot_nt(keys_ref[2 * h + p], q_scr[2 * h + p])
            for lb in range(n_blocks):
                s_scr[p, lb] = st[:, lb * LANES:(lb + 1) * LANES]
        lax.fori_loop(0, n_blocks // 2, lane_pair, h)
        return carry

    lax.fori_loop(0, n_heads, head, 0)


def _peer_scores(h2, wq, keys, n_heads):
    t, d = h2.shape
    tt = PEER_TOKEN_TILE
    nb = tt // LANES
    tab_shape = (t // LANES, n_heads, PEER_KEYS, LANES)
    tab_spec = pl.BlockSpec((nb, n_heads, PEER_KEYS, LANES), lambda i: (i, 0, 0, 0))
    return pl.pallas_call(
        functools.partial(_peer_scores_kernel, n_heads=n_heads),
        grid=(t // tt,),
        in_specs=[pl.BlockSpec((tt, d), lambda i: (i, 0)),
                  pl.BlockSpec(wq.shape, lambda i: (0, 0)),
                  pl.BlockSpec(keys.shape, lambda i: (0, 0, 0))],
        out_specs=[tab_spec] * 4,
        out_shape=[jax.ShapeDtypeStruct(tab_shape, dt) for dt in (jnp.uint32, jnp.uint32, BF16, BF16)],
        scratch_shapes=[pltpu.VMEM((2 * n_heads, tt, PEER_KEYS), BF16),
                        pltpu.VMEM((2, nb, PEER_KEYS, LANES), F32),
                        pltpu.VMEM((2, 24, LANES), F32), pltpu.VMEM((2, 24, LANES), F32),
                        pltpu.VMEM((2, 88, LANES), F32), pltpu.VMEM((2, 24, LANES), F32)],
        compiler_params=_params("arbitrary"),
        name="peer_scores",
    )(h2, wq, keys)


def _peer_mix_kernel(h_ref, u_ref, vt_ref, ra_ref, p1_ref, gb_ref, p2_ref, x_ref, g_ref, o_ref,
                     acc_scr, act_scr, a_scr, *, n_heads, n_tiles):
    s = pl.program_id(1)
    tt = h_ref.shape[0]
    rows_per_tile = PEER_EXPERT_TILE // PEER_KEYS
    tile = (PEER_KEYS, LANES)

    def readout():
        acc_scr[...] += _dot(vt_ref[0], a_scr[...])

    def row_tile(ref, lb, h, aa):
        words = jnp.broadcast_to(ref[lb, h, aa:aa + 1, :], (8, LANES))
        packed = pltpu.bitcast(words, BF16)
        return jnp.broadcast_to(packed[None], (PEER_KEYS // 16, 16, LANES)).reshape(tile)

    def gates():
        for aa in range(rows_per_tile):
            rs = slice(aa * PEER_KEYS, (aa + 1) * PEER_KEYS)
            for lb in range(tt // LANES):
                ls = slice(lb * LANES, (lb + 1) * LANES)
                w = None
                for h in range(n_heads):
                    rank = row_tile(ra_ref, lb, h, aa)
                    p1 = row_tile(p1_ref, lb, h, aa)
                    term = jnp.where(rank < gb_ref[lb, h], p2_ref[lb, h], jnp.zeros(tile, BF16)) * p1
                    w = term if w is None else w + term
                act = act_scr[rs, ls]
                gelu2 = act * (1.0 + lax.erf(act * INV_SQRT2))
                a_scr[rs, ls] = w * gelu2.astype(BF16)

    def experts():
        act_scr[...] = _dot_nt(u_ref[...], h_ref[...])

    @pl.when(s == 0)
    def _():
        acc_scr[...] = jnp.zeros_like(acc_scr)

    i = pl.program_id(0)

    @pl.when(s < n_tiles)
    def _():
        experts()

    @pl.when(i >= 0)
    def _():
        gates()

    @pl.when(s + i >= 0)
    def _():
        readout()

    @pl.when(s == n_tiles - 1)
    def _():
        o_ref[...] = x_ref[...] + g_ref[0] * acc_scr[...].T


def _peer_mix(h2, u, vt, tables, x, gate, seg, n_heads):
    t, d = h2.shape
    n_exp = u.shape[0]
    tt = min(PEER_MIX_TOKEN_TILE, seg)
    te = PEER_EXPERT_TILE
    n_tiles = n_exp // te
    per_seg = seg // tt
    r = gate.shape[0]
    nb = tt // LANES
    tab_spec = pl.BlockSpec((nb, n_heads, PEER_KEYS, LANES), lambda i, s: (i, 0, 0, 0))
    row_spec = pl.BlockSpec((nb, n_heads, te // PEER_KEYS, LANES), lambda i, s: (i, 0, s, 0))
    return pl.pallas_call(
        functools.partial(_peer_mix_kernel, n_heads=n_heads, n_tiles=n_tiles),
        grid=(t // tt, n_tiles),
        in_specs=[pl.BlockSpec((tt, d), lambda i, s: (i, 0)),
                  pl.BlockSpec((te, d), lambda i, s: (s, 0)),
                  pl.BlockSpec((1, d, te), lambda i, s: (s, 0, 0)),
                  row_spec, row_spec, tab_spec, tab_spec,
                  pl.BlockSpec((tt, d), lambda i, s: (i, 0)),
                  pl.BlockSpec((1, 1, d), lambda i, s: (i // per_seg, 0, 0))],
        out_specs=pl.BlockSpec((tt, d), lambda i, s: (i, 0)),
        out_shape=jax.ShapeDtypeStruct((t, d), F32),
        scratch_shapes=[pltpu.VMEM((d, tt), F32), pltpu.VMEM((te, tt), F32), pltpu.VMEM((te, tt), BF16)],
        compiler_params=_params("arbitrary", "arbitrary"),
        name="peer_mix",
    )(h2, u, vt, *tables, x, gate.reshape(r, 1, d))


def _peer(h2, x, gate, seg, wq, keys, u, vt):
    n_heads = keys.shape[0] // 2
    tables = _peer_scores(h2, wq, keys, n_heads)
    return _peer_mix(h2, u, vt, tables, x, gate, seg, n_heads)


def _rope_tables(n_tokens):
    t = jnp.arange(n_tokens)
    row = (t // GRID_W).astype(F32)
    col = (t % GRID_W).astype(F32)
    n_freq = HEAD_DIM // 4
    inv_freq = jnp.power(ROPE_BASE, -jnp.arange(n_freq, dtype=F32) / n_freq)
    ang = jnp.concatenate([row[:, None] * inv_freq, col[:, None] * inv_freq], axis=-1)
    cos, sin = jnp.cos(ang), jnp.sin(ang)
    cos2 = jnp.tile(jnp.concatenate([cos, cos], axis=-1), (1, 2))
    sin2 = jnp.tile(jnp.concatenate([-sin, sin], axis=-1), (1, 2))
    return cos2, sin2


def _pair_lanes(v):
    return jnp.repeat(v, HEAD_DIM, axis=-1).reshape(*v.shape[:-1], v.shape[-1] // 2, LANES)


def kernel(x, c, ctx, c_ctx, mod_w, mod_b, norm1_g, norm2_g, ab_w_in, ab_w_out, na_q_norm, na_k_norm,
           na_rpb, ret_log_decay, swa_w_in, swa_w_out, swa_q_norm, swa_k_norm, swa_sink,
           peer_w_q, peer_sub_keys, peer_u, peer_v):
    b, s, d = x.shape
    l = ctx.shape[1]
    depth = mod_w.shape[0]
    assert depth == 2 and b + 1 <= 8
    assert s % (NA_Q_ROWS * GRID_W) == 0 and s // GRID_W >= NA_BAND_ROWS
    assert s % TOKEN_TILE == 0 and (b * l) % TOKEN_TILE == 0 and l % RET_CHUNK == 0

    cond = jnp.concatenate([c, c_ctx[None, :], jnp.zeros((8 - b - 1, d), F32)], axis=0)
    mods = _adaln(cond, mod_w, mod_b).reshape(depth, 8, 6, d)
    lat = lambda layer, which: mods[layer, :b, which]
    cx = lambda layer, which: mods[layer, b:b + 1, which]

    cos2, sin2 = _rope_tables(s)
    x_lat = x.reshape(b * s, d)
    x_ctx = ctx.reshape(b * l, d)
    tile2 = lambda g: jnp.tile(g, 2).reshape(1, LANES)

    def peer_weights(layer):
        n_heads = peer_sub_keys.shape[1]
        keys = peer_sub_keys[layer].reshape(2 * n_heads, PEER_KEYS, -1).astype(BF16)
        vt = peer_v[layer].reshape(-1, PEER_EXPERT_TILE, d).transpose(0, 2, 1).astype(BF16)
        return peer_w_q[layer].astype(BF16), keys, peer_u[layer].astype(BF16), vt

    n_na = na_rpb.shape[1]
    n_ret = ret_log_decay.shape[2]
    na_pairs, ret_pairs = n_na // 2, n_ret // 2
    w_in = ab_w_in[0].astype(BF16)
    w_out = ab_w_out[0].astype(BF16)
    wa = n_na * HEAD_DIM
    p_lat = _modmm(x_lat, norm1_g[0], lat(0, 0), lat(0, 1), w_in, s).reshape(b, s, -1)
    p_ctx = _modmm(x_ctx, norm1_g[0], cx(0, 0), cx(0, 1), w_in, b * l).reshape(b, l, -1)

    qn, kn = tile2(na_q_norm[0]), tile2(na_k_norm[0])
    oa_lat = _na_attention(p_lat, p_ctx, _na_bias(na_rpb[0], s // GRID_W), qn, kn, na_pairs)
    oa_ctx = _ctx_attention(p_ctx, qn, kn, na_pairs)

    lg = _pair_lanes(ret_log_decay[0]).transpose(1, 0, 2)
    ret_col0 = 3 * wa // LANES
    zeros_state = jnp.zeros((b, ret_pairs, 4, HEAD_DIM, HEAD_DIM), F32)
    ones_tab, zeros_tab = jnp.ones((l, LANES), F32), jnp.zeros((l, LANES), F32)
    ob_ctx, st_ctx = _retention(p_ctx, ones_tab, zeros_tab, lg, zeros_state, ret_col0, ret_pairs, False)
    ob_lat, _ = _retention(p_lat, cos2, sin2, lg, st_ctx, ret_col0, ret_pairs, True)

    w_list = [w_out[:wa], w_out[wa:]]
    x_lat, h_lat = _outproj([oa_lat.reshape(b * s, -1), ob_lat.reshape(b * s, -1)], w_list, x_lat,
                            lat(0, 2), norm2_g[0], lat(0, 3), lat(0, 4), s)
    x_ctx, h_ctx = _outproj([oa_ctx.reshape(b * l, -1), ob_ctx.reshape(b * l, -1)], w_list, x_ctx,
                            cx(0, 2), norm2_g[0], cx(0, 3), cx(0, 4), b * l)
    pw = peer_weights(0)
    x_lat = _peer(h_lat, x_lat, lat(0, 5), s, *pw)
    x_ctx = _peer(h_ctx, x_ctx, cx(0, 5), b * l, *pw)

    n_q = swa_sink.shape[1]
    n_kv = (swa_w_in.shape[2] // HEAD_DIM - n_q) // 2
    w_in = swa_w_in[0].astype(BF16)
    p_lat = _modmm(x_lat, norm1_g[1], lat(1, 0), lat(1, 1), w_in, s).reshape(b, s, -1)
    p_ctx = _modmm(x_ctx, norm1_g[1], cx(1, 0), cx(1, 1), w_in, b * l).reshape(b, l, -1)
    sink_rows = jnp.broadcast_to(swa_sink[0][:, None], (n_q, LANES))
    o_lat = _swa_attention(p_lat, p_ctx, cos2, sin2, tile2(swa_q_norm[0]), tile2(swa_k_norm[0]),
                           sink_rows, n_q, n_kv)
    x_lat, h_lat = _outproj([o_lat.reshape(b * s, -1)], [swa_w_out[0].astype(BF16)], x_lat,
                            lat(1, 2), norm2_g[1], lat(1, 3), lat(1, 4), s)
    x_lat = _peer(h_lat, x_lat, lat(1, 5), s, *peer_weights(1))
    return x_lat.reshape(b, s, d)
```

```python
import functools

import numpy as np
import jax
import jax.numpy as jnp
from jax import lax
from jax.experimental import pallas as pl
from jax.experimental.pallas import tpu as pltpu

F32 = jnp.float32
BF16 = jnp.bfloat16

HEAD_DIM = 64
GRID_W = 64
NA_ROWS = 8
NA_COLS = 16
SWA_WINDOW = 128
PEER_TOPK = 16
PEER_KEYS = 128
ROPE_BASE = 10000.0
NORM_EPS = 1e-6
GN_EPS = 1e-5
NEG_INF = -1e30
ATTN_SCALE = HEAD_DIM ** -0.5
INV_SQRT2 = 0.7071067811865476

LANES = 128
VMEM_LIMIT = 56 * 1024 * 1024

TOKEN_TILE = 512
PROJ_TOKEN_TILE = 1024
PROJ_N_TILE = 1792
RET_CHUNK = 128
NA_Q_ROWS = 8
NA_BAND_ROWS = 16
SWA_Q_TILE = 256
PEER_TOKEN_TILE = 512
PEER_MIX_TOKEN_TILE = 1024
PEER_EXPERT_TILE = 1024


def _dot(a, b):
    return jnp.dot(a, b, preferred_element_type=F32)


def _dot_nt(a, b):
    return lax.dot_general(a, b, (((1,), (1,)), ((), ())), preferred_element_type=F32)


def _dot_tn(a, b):
    return lax.dot_general(a, b, (((0,), (0,)), ((), ())), preferred_element_type=F32)


def _params(*sem):
    return pltpu.CompilerParams(dimension_semantics=sem, vmem_limit_bytes=VMEM_LIMIT)


def _rms_rows(x, gain):
    ms = jnp.mean(x * x, axis=-1, keepdims=True)
    return x * lax.rsqrt(ms + NORM_EPS) * gain


def _modulate(x, gain, shift, scale):
    return _rms_rows(x, gain) * (1.0 + scale) + shift


def _head_rms(x, gain):
    lane = lax.broadcasted_iota(jnp.int32, x.shape, 1)
    lo = lane < HEAD_DIM
    ss = x * x
    s_lo = jnp.sum(jnp.where(lo, ss, 0.0), axis=-1, keepdims=True)
    s_hi = jnp.sum(jnp.where(lo, 0.0, ss), axis=-1, keepdims=True)
    ms = jnp.where(lo, s_lo, s_hi) * (1.0 / HEAD_DIM)
    return x * lax.rsqrt(ms + NORM_EPS) * gain


def _rope(x, cos2, sin2):
    lane = lax.broadcasted_iota(jnp.int32, x.shape, 1)
    first_half = (lane & (HEAD_DIM // 2)) == 0
    swapped = jnp.where(first_half, pltpu.roll(x, LANES - HEAD_DIM // 2, axis=1),
                        pltpu.roll(x, HEAD_DIM // 2, axis=1))
    return x * cos2 + swapped * sin2


def _adaln_kernel(c_ref, w_ref, b_ref, o_ref):
    c = c_ref[...]
    s = c * jax.nn.sigmoid(c)
    w = w_ref[0]
    s_hi = s.astype(BF16)
    s_lo = (s - s_hi.astype(F32)).astype(BF16)
    w_hi = w.astype(BF16)
    w_lo = (w - w_hi.astype(F32)).astype(BF16)
    acc = _dot(s_hi, w_hi) + _dot(s_lo, w_hi) + _dot(s_hi, w_lo)
    o_ref[0] = acc + b_ref[0]


def _adaln(cond, mod_w, mod_b):
    depth, d, n = mod_w.shape
    tn = n // 4
    return pl.pallas_call(
        _adaln_kernel,
        grid=(depth, n // tn),
        in_specs=[pl.BlockSpec((8, d), lambda l, j: (0, 0)),
                  pl.BlockSpec((1, d, tn), lambda l, j: (l, 0, j)),
                  pl.BlockSpec((1, 1, tn), lambda l, j: (l, 0, j))],
        out_specs=pl.BlockSpec((1, 8, tn), lambda l, j: (l, 0, j)),
        out_shape=jax.ShapeDtypeStruct((depth, 8, n), F32),
        compiler_params=_params("arbitrary", "arbitrary"),
        name="adaln",
    )(cond, mod_w, mod_b.reshape(depth, 1, n))


def _modmm_kernel(x_ref, g_ref, sh_ref, sc_ref, w_ref, o_ref, h_scr):
    @pl.when(pl.program_id(1) == 0)
    def _():
        h_scr[...] = _modulate(x_ref[...], g_ref[...], sh_ref[0], sc_ref[0]).astype(BF16)

    o_ref[...] = _dot(h_scr[...], w_ref[...])


def _modmm(x, gain, shift, scale, w, seg):
    t, d = x.shape
    n = w.shape[1]
    tm = min(PROJ_TOKEN_TILE, seg)
    tn = n if n <= PROJ_N_TILE else n // 2
    per_seg = seg // tm
    r = shift.shape[0]
    return pl.pallas_call(
        _modmm_kernel,
        grid=(t // tm, n // tn),
        in_specs=[pl.BlockSpec((tm, d), lambda i, j: (i, 0)),
                  pl.BlockSpec((1, d), lambda i, j: (0, 0)),
                  pl.BlockSpec((1, 1, d), lambda i, j: (i // per_seg, 0, 0)),
                  pl.BlockSpec((1, 1, d), lambda i, j: (i // per_seg, 0, 0)),
                  pl.BlockSpec((d, tn), lambda i, j: (0, j))],
        out_specs=pl.BlockSpec((tm, tn), lambda i, j: (i, j)),
        out_shape=jax.ShapeDtypeStruct((t, n), F32),
        scratch_shapes=[pltpu.VMEM((tm, d), BF16)],
        compiler_params=_params("arbitrary", "arbitrary"),
        name="modulate_matmul",
    )(x, gain.reshape(1, d), shift.reshape(r, 1, d), scale.reshape(r, 1, d), w)


def _outproj_kernel(*refs, n_in):
    a_refs = refs[:n_in]
    w_refs = refs[n_in:2 * n_in]
    x_ref, gate_ref, g2_ref, sh_ref, sc_ref, xo_ref, h_ref = refs[2 * n_in:]
    acc = None
    for a_ref, w_ref in zip(a_refs, w_refs):
        part = _dot(a_ref[...].astype(BF16), w_ref[...])
        acc = part if acc is None else acc + part
    xn = x_ref[...] + gate_ref[0] * acc
    xo_ref[...] = xn
    h_ref[...] = _modulate(xn, g2_ref[...], sh_ref[0], sc_ref[0]).astype(BF16)


def _outproj(a_list, w_list, x, gate, gain2, shift2, scale2, seg):
    t, d = x.shape
    tm = min(TOKEN_TILE, seg)
    per_seg = seg // tm
    r = gate.shape[0]
    n_in = len(a_list)
    row = lambda i: (i // per_seg, 0, 0)
    in_specs = ([pl.BlockSpec((tm, a.shape[1]), lambda i: (i, 0)) for a in a_list]
                + [pl.BlockSpec(w.shape, lambda i: (0, 0)) for w in w_list]
                + [pl.BlockSpec((tm, d), lambda i: (i, 0)),
                   pl.BlockSpec((1, 1, d), row),
                   pl.BlockSpec((1, d), lambda i: (0, 0)),
                   pl.BlockSpec((1, 1, d), row),
                   pl.BlockSpec((1, 1, d), row)])
    return pl.pallas_call(
        functools.partial(_outproj_kernel, n_in=n_in),
        grid=(t // tm,),
        in_specs=in_specs,
        out_specs=[pl.BlockSpec((tm, d), lambda i: (i, 0)), pl.BlockSpec((tm, d), lambda i: (i, 0))],
        out_shape=[jax.ShapeDtypeStruct((t, d), F32), jax.ShapeDtypeStruct((t, d), BF16)],
        compiler_params=_params("arbitrary"),
        name="out_proj_residual",
    )(*a_list, *w_list, x, gate.reshape(r, 1, d), gain2.reshape(1, d),
      shift2.reshape(r, 1, d), scale2.reshape(r, 1, d))


def _softmax_pv(s_list, v_list, extra=None, fold_lanes=False):
    def lane_chunks(x):
        return [x[:, c:c + LANES] for c in range(0, x.shape[1], LANES)] if fold_lanes else [x]

    def row_reduce(blocks, op, reduce):
        acc = None
        for blk in blocks:
            chunks = lane_chunks(blk)
            part = chunks[0]
            for c in chunks[1:]:
                part = op(part, c)
            if not fold_lanes:
                part = reduce(part, axis=-1, keepdims=True)
            acc = part if acc is None else op(acc, part)
        return reduce(acc, axis=-1, keepdims=True) if fold_lanes else acc

    m = row_reduce(s_list, jnp.maximum, jnp.max)
    if extra is not None:
        m = jnp.maximum(m, extra)
    ps = []
    out = None
    for s, v in zip(s_list, v_list):
        p = jnp.exp(s - m)
        ps.append(p)
        pv = _dot(p.astype(BF16), v)
        out = pv if out is None else out + pv
    denom = row_reduce(ps, jnp.add, jnp.sum)
    if extra is not None:
        denom = denom + jnp.exp(extra - m)
    return out / denom


def _na_kernel(q_ref, k_ref, v_ref, kc_ref, vc_ref, bias_ref, qn_ref, kn_ref, o_ref,
               k_scr, v_scr, kc_scr, vc_scr, *, n_steps):
    i = pl.program_id(2)
    band = NA_BAND_ROWS * GRID_W
    start = jnp.clip(NA_Q_ROWS * i - NA_ROWS // 2, 0, NA_Q_ROWS * n_steps - NA_BAND_ROWS) * GRID_W
    start = pl.multiple_of(start, NA_ROWS // 2 * GRID_W)

    @pl.when(i == 0)
    def _():
        k_scr[...] = _head_rms(k_ref[0], kn_ref[...]).astype(BF16)
        v_scr[...] = v_ref[0].astype(BF16)
        kc_scr[...] = _head_rms(kc_ref[0], kn_ref[...]).astype(BF16)
        vc_scr[...] = vc_ref[0].astype(BF16)

    q = (_head_rms(q_ref[0], qn_ref[...]) * ATTN_SCALE).astype(BF16)
    kb = k_scr[pl.ds(start, band), :]
    vb = v_scr[pl.ds(start, band), :]
    kc = kc_scr[...]
    vc = vc_scr[...]
    outs = []
    for h in range(2):
        sl = slice(h * HEAD_DIM, (h + 1) * HEAD_DIM)
        s_loc = _dot_nt(q[:, sl], kb[:, sl]) + bias_ref[0, h]
        s_ctx = _dot_nt(q[:, sl], kc[:, sl])
        outs.append(_softmax_pv([s_loc, s_ctx], [vb[:, sl], vc[:, sl]]))
    o_ref[0] = jnp.concatenate(outs, axis=-1)


def _na_bias(rpb, rows):
    h = rpb.shape[0]
    n_steps = rows // NA_Q_ROWS
    col = np.arange(GRID_W)
    c0 = np.clip(col - NA_COLS // 2, 0, GRID_W - NA_COLS)
    dc = col[None, :] - col[:, None] + (NA_COLS - 1)
    ok_c = (col[None, :] >= c0[:, None]) & (col[None, :] < c0[:, None] + NA_COLS)
    pick = ((np.arange(2 * NA_COLS - 1)[:, None, None] == dc[None]) & ok_c[None]).astype(np.float32)
    blocks = jnp.einsum('hrd,dqk->hrqk', rpb, pick, precision=lax.Precision.HIGHEST)
    blocks = jnp.where(ok_c, blocks, NEG_INF)
    masked = jnp.full((h, GRID_W, GRID_W), NEG_INF, F32)
    classes = []
    for step in (0, 1, n_steps - 1):
        band0 = min(max(NA_Q_ROWS * step - NA_ROWS // 2, 0), rows - NA_BAND_ROWS)
        q_rows = []
        for rq in range(NA_Q_ROWS):
            r = NA_Q_ROWS * step + rq
            r0 = min(max(r - NA_ROWS // 2, 0), rows - NA_ROWS)
            row = []
            for rk in range(band0, band0 + NA_BAND_ROWS):
                row.append(blocks[:, rk - r + NA_ROWS - 1] if r0 <= rk < r0 + NA_ROWS else masked)
            q_rows.append(jnp.concatenate(row, axis=-1))
        classes.append(jnp.concatenate(q_rows, axis=-2))
    return jnp.stack(classes)


def _na_attention(p_lat, p_ctx, bias, qn, kn, n_pairs):
    b, s, _ = p_lat.shape
    l = p_ctx.shape[1]
    rows = s // GRID_W
    n_steps = rows // NA_Q_ROWS
    qt = NA_Q_ROWS * GRID_W

    def cls(i):
        return jnp.where(i == 0, 0, jnp.where(i == n_steps - 1, 2, 1))

    return pl.pallas_call(
        functools.partial(_na_kernel, n_steps=n_steps),
        grid=(n_pairs, b, n_steps),
        in_specs=[pl.BlockSpec((1, qt, LANES), lambda hp, bb, i: (bb, i, hp)),
                  pl.BlockSpec((1, s, LANES), lambda hp, bb, i: (bb, 0, n_pairs + hp)),
                  pl.BlockSpec((1, s, LANES), lambda hp, bb, i: (bb, 0, 2 * n_pairs + hp)),
                  pl.BlockSpec((1, l, LANES), lambda hp, bb, i: (bb, 0, n_pairs + hp)),
                  pl.BlockSpec((1, l, LANES), lambda hp, bb, i: (bb, 0, 2 * n_pairs + hp)),
                  pl.BlockSpec((1, 2, qt, NA_BAND_ROWS * GRID_W), lambda hp, bb, i: (cls(i), hp, 0, 0)),
                  pl.BlockSpec((1, LANES), lambda hp, bb, i: (0, 0)),
                  pl.BlockSpec((1, LANES), lambda hp, bb, i: (0, 0))],
        out_specs=pl.BlockSpec((1, qt, LANES), lambda hp, bb, i: (bb, i, hp)),
        out_shape=jax.ShapeDtypeStruct((b, s, n_pairs * LANES), F32),
        scratch_shapes=[pltpu.VMEM((s, LANES), BF16), pltpu.VMEM((s, LANES), BF16),
                        pltpu.VMEM((l, LANES), BF16), pltpu.VMEM((l, LANES), BF16)],
        compiler_params=_params("arbitrary", "arbitrary", "arbitrary"),
        name="neighbourhood_attention",
    )(p_lat, p_lat, p_lat, p_ctx, p_ctx, bias, qn, kn)


def _ctx_attn_kernel(q_ref, k_ref, v_ref, qn_ref, kn_ref, o_ref):
    q = (_head_rms(q_ref[0], qn_ref[...]) * ATTN_SCALE).astype(BF16)
    k = _head_rms(k_ref[0], kn_ref[...]).astype(BF16)
    v = v_ref[0].astype(BF16)
    outs = []
    for h in range(2):
        sl = slice(h * HEAD_DIM, (h + 1) * HEAD_DIM)
        outs.append(_softmax_pv([_dot_nt(q[:, sl], k[:, sl])], [v[:, sl]]))
    o_ref[0] = jnp.concatenate(outs, axis=-1)


def _ctx_attention(p_ctx, qn, kn, n_pairs):
    b, l, _ = p_ctx.shape
    return pl.pallas_call(
        _ctx_attn_kernel,
        grid=(n_pairs, b),
        in_specs=[pl.BlockSpec((1, l, LANES), lambda hp, bb: (bb, 0, hp)),
                  pl.BlockSpec((1, l, LANES), lambda hp, bb: (bb, 0, n_pairs + hp)),
                  pl.BlockSpec((1, l, LANES), lambda hp, bb: (bb, 0, 2 * n_pairs + hp)),
                  pl.BlockSpec((1, LANES), lambda hp, bb: (0, 0)),
                  pl.BlockSpec((1, LANES), lambda hp, bb: (0, 0))],
        out_specs=pl.BlockSpec((1, l, LANES), lambda hp, bb: (bb, 0, hp)),
        out_shape=jax.ShapeDtypeStruct((b, l, n_pairs * LANES), F32),
        compiler_params=_params("arbitrary", "arbitrary"),
        name="context_attention",
    )(p_ctx, p_ctx, p_ctx, qn, kn)


def _ret_kernel(q_ref, k_ref, v_ref, g_ref, cos_ref, sin_ref, lg_ref, s0_ref, y_ref, st_ref, sf_scr,
                *, n_chunks, use_rope):
    c = RET_CHUNK
    hd = HEAD_DIM
    lg = -jnp.exp(lg_ref[0])
    lgf, lgb = lg[0:1, :], lg[1:2, :]
    ii = lax.broadcasted_iota(jnp.int32, (c, LANES), 0).astype(F32)
    dq_f = jnp.exp(lgf * (ii + 1.0))
    dk_f = jnp.exp(lgf * (c - 1.0 - ii))
    dq_b = jnp.exp(lgb * (c - ii))
    dk_b = jnp.exp(lgb * ii)
    dc_f = jnp.exp(lgf * float(c))
    dc_b = jnp.exp(lgb * float(c))
    diff = (lax.broadcasted_iota(jnp.int32, (c, c), 0) - lax.broadcasted_iota(jnp.int32, (c, c), 1)).astype(F32)
    intra = []
    chunk_f = []
    chunk_b = []
    for h in range(2):
        lf = lgf[:, h * hd:h * hd + 1]
        lb = lgb[:, h * hd:h * hd + 1]
        intra.append(jnp.where(diff >= 0, jnp.exp(lf * jnp.maximum(diff, 0.0)),
                               jnp.exp(lb * jnp.maximum(-diff, 0.0))))
        chunk_f.append(dc_f[:, h * hd:h * hd + 1])
        chunk_b.append(dc_b[:, h * hd:h * hd + 1])

    def load(n):
        r = pl.multiple_of(n * c, c)
        q = q_ref[0, pl.ds(r, c), :]
        k = k_ref[0, pl.ds(r, c), :]
        v = v_ref[0, pl.ds(r, c), :]
        if use_rope:
            cs = cos_ref[pl.ds(r, c), :]
            sn = sin_ref[pl.ds(r, c), :]
            q = _rope(q, cs, sn)
            k = _rope(k, cs, sn)
        return r, q * ATTN_SCALE, k, v.astype(BF16)

    def fwd(n, carry):
        _, _, k, v = load(n)
        kd = (k * dk_f).astype(BF16)
        new = []
        for h in range(2):
            sl = slice(h * hd, (h + 1) * hd)
            sf_scr[n, h] = carry[h]
            new.append(carry[h] * chunk_f[h] + _dot_tn(kd[:, sl], v[:, sl]))
        return tuple(new)

    sf = lax.fori_loop(0, n_chunks, fwd, (s0_ref[0, 0, 0], s0_ref[0, 0, 1]), unroll=2)
    st_ref[0, 0, 0] = sf[0]
    st_ref[0, 0, 1] = sf[1]

    def bwd(jj, carry):
        n = n_chunks - 1 - jj
        r, q, k, v = load(n)
        qb = q.astype(BF16)
        kb = k.astype(BF16)
        qf = (q * dq_f).astype(BF16)
        qr = (q * dq_b).astype(BF16)
        kd = (k * dk_b).astype(BF16)
        outs = []
        new = []
        for h in range(2):
            sl = slice(h * hd, (h + 1) * hd)
            a = (_dot_nt(qb[:, sl], kb[:, sl]) * intra[h]).astype(BF16)
            o = (_dot(a, v[:, sl]) + _dot(qf[:, sl], sf_scr[n, h].astype(BF16))
                 + _dot(qr[:, sl], carry[h].astype(BF16)))
            oc = o - jnp.mean(o, axis=-1, keepdims=True)
            outs.append(oc * lax.rsqrt(jnp.mean(oc * oc, axis=-1, keepdims=True) + GN_EPS))
            new.append(carry[h] * chunk_b[h] + _dot_tn(kd[:, sl], v[:, sl]))
        g = g_ref[0, pl.ds(r, c), :]
        y_ref[0, pl.ds(r, c), :] = jnp.concatenate(outs, axis=-1) * (g * jax.nn.sigmoid(g))
        return tuple(new)

    sb = lax.fori_loop(0, n_chunks, bwd, (s0_ref[0, 0, 2], s0_ref[0, 0, 3]), unroll=2)
    st_ref[0, 0, 2] = sb[0]
    st_ref[0, 0, 3] = sb[1]


def _retention(p, cos2, sin2, lg, s0, col0, n_pairs, use_rope):
    b, t, _ = p.shape
    n_chunks = t // RET_CHUNK
    tab = lambda hp, bb: (0, 0)
    return pl.pallas_call(
        functools.partial(_ret_kernel, n_chunks=n_chunks, use_rope=use_rope),
        grid=(n_pairs, b),
        in_specs=[pl.BlockSpec((1, t, LANES), lambda hp, bb: (bb, 0, col0 + hp)),
                  pl.BlockSpec((1, t, LANES), lambda hp, bb: (bb, 0, col0 + n_pairs + hp)),
                  pl.BlockSpec((1, t, LANES), lambda hp, bb: (bb, 0, col0 + 2 * n_pairs + hp)),
                  pl.BlockSpec((1, t, LANES), lambda hp, bb: (bb, 0, col0 + 3 * n_pairs + hp)),
                  pl.BlockSpec(cos2.shape, tab),
                  pl.BlockSpec(sin2.shape, tab),
                  pl.BlockSpec((1, 2, LANES), lambda hp, bb: (hp, 0, 0)),
                  pl.BlockSpec((1, 1, 4, HEAD_DIM, HEAD_DIM), lambda hp, bb: (bb, hp, 0, 0, 0))],
        out_specs=[pl.BlockSpec((1, t, LANES), lambda hp, bb: (bb, 0, hp)),
                   pl.BlockSpec((1, 1, 4, HEAD_DIM, HEAD_DIM), lambda hp, bb: (bb, hp, 0, 0, 0))],
        out_shape=[jax.ShapeDtypeStruct((b, t, n_pairs * LANES), F32),
                   jax.ShapeDtypeStruct((b, n_pairs, 4, HEAD_DIM, HEAD_DIM), F32)],
        scratch_shapes=[pltpu.VMEM((n_chunks, 2, HEAD_DIM, HEAD_DIM), F32)],
        compiler_params=_params("arbitrary", "arbitrary"),
        name="retention",
    )(p, p, p, p, cos2, sin2, lg, s0)


def _swa_kernel(q_ref, k_ref, v_ref, kc_ref, vc_ref, cos_ref, sin_ref, qn_ref, kn_ref, sink_ref, far_ref,
                o_ref, *, seq):
    qt = SWA_Q_TILE
    wk = qt + 2 * SWA_WINDOW
    hd = HEAD_DIM
    n = pl.program_id(2)
    q0 = pl.multiple_of(n * qt, qt)
    ws = pl.multiple_of(jnp.clip(n * qt - SWA_WINDOW, 0, seq - wk), SWA_WINDOW)
    kw = _rope(_head_rms(k_ref[0, pl.ds(ws, wk), :], kn_ref[...]),
               cos_ref[pl.ds(ws, wk), :], sin_ref[pl.ds(ws, wk), :]).astype(BF16)
    vw = v_ref[0, pl.ds(ws, wk), :].astype(BF16)
    kc = _head_rms(kc_ref[0], kn_ref[...]).astype(BF16)
    vc = vc_ref[0].astype(BF16)
    cos_q = cos_ref[pl.ds(q0, qt), :]
    sin_q = sin_ref[pl.ds(q0, qt), :]
    qs = []
    for s in range(4):
        slab = q_ref[0, :, s * LANES:(s + 1) * LANES]
        qs.append((_rope(_head_rms(slab, qn_ref[...]), cos_q, sin_q) * ATTN_SCALE).astype(BF16))
    far = far_ref[0][None]
    for kh in range(2):
        sl = slice(kh * hd, (kh + 1) * hd)
        qstack = jnp.concatenate(
            [qs[kh * 2 + g // 2][:, (g % 2) * hd:(g % 2 + 1) * hd] for g in range(4)], axis=0)
        sink = jnp.concatenate(
            [jnp.broadcast_to(sink_ref[kh * 4 + g:kh * 4 + g + 1, 0:1], (qt, 1)) for g in range(4)], axis=0)
        s_loc = (_dot_nt(qstack, kw[:, sl]).reshape(4, qt, wk) + far).reshape(4 * qt, wk)
        s_ctx = _dot_nt(qstack, kc[:, sl])
        o = _softmax_pv([s_loc, s_ctx], [vw[:, sl], vc[:, sl]], extra=sink, fold_lanes=True)
        for pair in range(2):
            col = (kh * 2 + pair) * LANES
            o_ref[0, :, col:col + LANES] = jnp.concatenate(
                [o[(2 * pair) * qt:(2 * pair + 1) * qt], o[(2 * pair + 1) * qt:(2 * pair + 2) * qt]], axis=-1)


def _swa_attention(p_lat, p_ctx, cos2, sin2, qn, kn, sink_rows, n_q_heads, n_kv_heads):
    b, s, _ = p_lat.shape
    l = p_ctx.shape[1]
    kv_pairs = n_kv_heads // 2
    q_blocks = n_q_heads * HEAD_DIM // LANES
    q_per_pair = q_blocks // kv_pairs
    qw = q_per_pair * LANES
    tab = lambda kp, bb, n: (0, 0)
    n_steps = s // SWA_Q_TILE
    wk = SWA_Q_TILE + 2 * SWA_WINDOW
    far = []
    for step in (0, 1, n_steps - 1):
        q0 = step * SWA_Q_TILE
        ws = min(max(q0 - SWA_WINDOW, 0), s - wk)
        dist = np.abs((q0 + np.arange(SWA_Q_TILE))[:, None] - (ws + np.arange(wk))[None, :])
        far.append(np.where(dist <= SWA_WINDOW, 0.0, NEG_INF))
    far = jnp.asarray(np.stack(far), F32)

    def cls(n):
        return jnp.where(n == 0, 0, jnp.where(n == n_steps - 1, 2, 1))

    return pl.pallas_call(
        functools.partial(_swa_kernel, seq=s),
        grid=(kv_pairs, b, n_steps),
        in_specs=[pl.BlockSpec((1, SWA_Q_TILE, qw), lambda kp, bb, n: (bb, n, kp)),
                  pl.BlockSpec((1, s, LANES), lambda kp, bb, n: (bb, 0, q_blocks + kp)),
                  pl.BlockSpec((1, s, LANES), lambda kp, bb, n: (bb, 0, q_blocks + kv_pairs + kp)),
                  pl.BlockSpec((1, l, LANES), lambda kp, bb, n: (bb, 0, q_blocks + kp)),
                  pl.BlockSpec((1, l, LANES), lambda kp, bb, n: (bb, 0, q_blocks + kv_pairs + kp)),
                  pl.BlockSpec(cos2.shape, tab),
                  pl.BlockSpec(sin2.shape, tab),
                  pl.BlockSpec((1, LANES), tab),
                  pl.BlockSpec((1, LANES), tab),
                  pl.BlockSpec((8, LANES), lambda kp, bb, n: (kp, 0)),
                  pl.BlockSpec((1, SWA_Q_TILE, wk), lambda kp, bb, n: (cls(n), 0, 0))],
        out_specs=pl.BlockSpec((1, SWA_Q_TILE, qw), lambda kp, bb, n: (bb, n, kp)),
        out_shape=jax.ShapeDtypeStruct((b, s, n_q_heads * HEAD_DIM), F32),
        compiler_params=_params("arbitrary", "arbitrary", "arbitrary"),
        name="windowed_gqa",
    )(p_lat, p_lat, p_lat, p_ctx, p_ctx, cos2, sin2, qn, kn, sink_rows, far)


def _top_rows(x, k, scr, want_rank=False):
    work = x
    rank = jnp.full(x.shape, float(PEER_TOPK), F32) if want_rank else None
    for r in range(k):
        m = jnp.max(work, axis=0, keepdims=True)
        scr[r:r + 1, :] = m
        hit = work == m
        if want_rank and r < PEER_TOPK:
            rank = jnp.where(hit, float(r), rank)
        if r + 1 < k:
            work = jnp.where(hit, NEG_INF, work)
    return rank


def _bf16_pair(x):
    bits = lax.bitcast_convert_type(x.astype(BF16).astype(F32), jnp.uint32)
    return bits | (bits >> 16)


def _count_at_least(sorted_scr, y):
    row = lambda i: sorted_scr[i:i + 1, :]
    c8 = row(7) >= y
    c4 = jnp.where(c8, row(11), row(3)) >= y
    c2 = jnp.where(c8, jnp.where(c4, row(13), row(9)), jnp.where(c4, row(5), row(1))) >= y
    hi = jnp.where(c4, jnp.where(c2, row(14), row(12)), jnp.where(c2, row(10), row(8)))
    lo = jnp.where(c4, jnp.where(c2, row(6), row(4)), jnp.where(c2, row(2), row(0)))
    c1 = jnp.where(c8, hi, lo) >= y
    count = (jnp.where(c8, 8.0, 0.0) + jnp.where(c4, 4.0, 0.0)) + (jnp.where(c2, 2.0, 0.0) + jnp.where(c1, 1.0, 0.0))
    return count + jnp.where(row(15) >= y, 1.0, 0.0)


def _peer_scores_kernel(h_ref, wq_ref, keys_ref, ra_ref, p1_ref, gb_ref, p2_ref,
                        q_scr, s_scr, a_scr, b_scr, c_scr, v_scr, *, n_heads):
    k = PEER_TOPK
    n_blocks = h_ref.shape[0] // LANES

    q = _dot(h_ref[...], wq_ref[...])
    for hp in range(2 * n_heads):
        q_scr[hp] = q[:, hp * PEER_KEYS:(hp + 1) * PEER_KEYS].astype(BF16)

    def lane_block(lb, h, slot):
        tops_a, tops_b, cand, tops_c = a_scr.at[slot], b_scr.at[slot], c_scr.at[slot], v_scr.at[slot]
        s1 = s_scr[0, lb]
        s2 = s_scr[1, lb]
        rank_a = _top_rows(s1, k + 1, tops_a, want_rank=True)
        _top_rows(s2, k + 1, tops_b)
        cand[0:16, :] = tops_a[0:1, :] + tops_b[0:16, :]
        for i in range(1, 8):
            cand[8 + 8 * i:16 + 8 * i, :] = tops_a[i:i + 1, :] + tops_b[0:8, :]
        cand[72:80, :] = tops_a[8:16, :] + tops_b[0:1, :]
        cand[80:81, :] = tops_a[0:1, :] + tops_b[16:17, :]
        cand[81:82, :] = tops_a[16:17, :] + tops_b[0:1, :]
        cand[82:88, :] = jnp.full((6, LANES), NEG_INF, F32)
        _top_rows(cand[...], k + 1, tops_c)
        thr = 0.5 * (tops_c[k - 1:k, :] + tops_c[k:k + 1, :])
        z = jnp.sum(jnp.exp(tops_c[0:k, :] - tops_c[0:1, :]), axis=0, keepdims=True)
        ra_ref[lb, h] = _bf16_pair(rank_a)
        p1_ref[lb, h] = _bf16_pair(jnp.exp(s1 - tops_a[0:1, :]) * (0.5 / z))
        gb_ref[lb, h] = _count_at_least(tops_a, thr - s2).astype(BF16)
        p2_ref[lb, h] = jnp.exp(s2 - tops_b[0:1, :]).astype(BF16)

    def lane_pair(i, h):
        lane_block(2 * i, h, 0)
        lane_block(2 * i + 1, h, 1)
        return h

    def head(h, carry):
        for p in range(2):
            st = _dot_nt(keys_ref[2 * h + p], q_scr[2 * h + p])
            for lb in range(n_blocks):
                s_scr[p, lb] = st[:, lb * LANES:(lb + 1) * LANES]
        lax.fori_loop(0, n_blocks // 2, lane_pair, h)
        return carry

    lax.fori_loop(0, n_heads, head, 0)


def _peer_scores(h2, wq, keys, n_heads):
    t, d = h2.shape
    tt = PEER_TOKEN_TILE
    nb = tt // LANES
    tab_shape = (t // LANES, n_heads, PEER_KEYS, LANES)
    tab_spec = pl.BlockSpec((nb, n_heads, PEER_KEYS, LANES), lambda i: (i, 0, 0, 0))
    return pl.pallas_call(
        functools.partial(_peer_scores_kernel, n_heads=n_heads),
        grid=(t // tt,),
        in_specs=[pl.BlockSpec((tt, d), lambda i: (i, 0)),
                  pl.BlockSpec(wq.shape, lambda i: (0, 0)),
                  pl.BlockSpec(keys.shape, lambda i: (0, 0, 0))],
        out_specs=[tab_spec] * 4,
        out_shape=[jax.ShapeDtypeStruct(tab_shape, dt) for dt in (jnp.uint32, jnp.uint32, BF16, BF16)],
        scratch_shapes=[pltpu.VMEM((2 * n_heads, tt, PEER_KEYS), BF16),
                        pltpu.VMEM((2, nb, PEER_KEYS, LANES), F32),
                        pltpu.VMEM((2, 24, LANES), F32), pltpu.VMEM((2, 24, LANES), F32),
                        pltpu.VMEM((2, 88, LANES), F32), pltpu.VMEM((2, 24, LANES), F32)],
        compiler_params=_params("arbitrary"),
        name="peer_scores",
    )(h2, wq, keys)


def _peer_mix_kernel(h_ref, u_ref, vt_ref, ra_ref, p1_ref, gb_ref, p2_ref, x_ref, g_ref, o_ref,
                     acc_scr, act_scr, a_scr, *, n_heads, n_tiles):
    s = pl.program_id(1)
    tt = h_ref.shape[0]
    rows_per_tile = PEER_EXPERT_TILE // PEER_KEYS
    tile = (PEER_KEYS, LANES)

    def readout():
        acc_scr[...] += _dot(vt_ref[0], a_scr[...])

    def row_tile(ref, lb, h, aa):
        words = jnp.broadcast_to(ref[lb, h, aa:aa + 1, :], (8, LANES))
        packed = pltpu.bitcast(words, BF16)
        return jnp.broadcast_to(packed[None], (PEER_KEYS // 16, 16, LANES)).reshape(tile)

    def gates():
        for aa in range(rows_per_tile):
            rs = slice(aa * PEER_KEYS, (aa + 1) * PEER_KEYS)
            for lb in range(tt // LANES):
                ls = slice(lb * LANES, (lb + 1) * LANES)
                w = None
                for h in range(n_heads):
                    rank = row_tile(ra_ref, lb, h, aa)
                    p1 = row_tile(p1_ref, lb, h, aa)
                    term = jnp.where(rank < gb_ref[lb, h], p2_ref[lb, h], jnp.zeros(tile, BF16)) * p1
                    w = term if w is None else w + term
                act = act_scr[rs, ls]
                gelu2 = act * (1.0 + lax.erf(act * INV_SQRT2))
                a_scr[rs, ls] = w * gelu2.astype(BF16)

    def experts():
        act_scr[...] = _dot_nt(u_ref[...], h_ref[...])

    @pl.when(s == 0)
    def _():
        acc_scr[...] = jnp.zeros_like(acc_scr)

    i = pl.program_id(0)

    @pl.when(s < n_tiles)
    def _():
        experts()

    @pl.when(i >= 0)
    def _():
        gates()

    @pl.when(s + i >= 0)
    def _():
        readout()

    @pl.when(s == n_tiles - 1)
    def _():
        o_ref[...] = x_ref[...] + g_ref[0] * acc_scr[...].T


def _peer_mix(h2, u, vt, tables, x, gate, seg, n_heads):
    t, d = h2.shape
    n_exp = u.shape[0]
    tt = min(PEER_MIX_TOKEN_TILE, seg)
    te = PEER_EXPERT_TILE
    n_tiles = n_exp // te
    per_seg = seg // tt
    r = gate.shape[0]
    nb = tt // LANES
    tab_spec = pl.BlockSpec((nb, n_heads, PEER_KEYS, LANES), lambda i, s: (i, 0, 0, 0))
    row_spec = pl.BlockSpec((nb, n_heads, te // PEER_KEYS, LANES), lambda i, s: (i, 0, s, 0))
    return pl.pallas_call(
        functools.partial(_peer_mix_kernel, n_heads=n_heads, n_tiles=n_tiles),
        grid=(t // tt, n_tiles),
        in_specs=[pl.BlockSpec((tt, d), lambda i, s: (i, 0)),
                  pl.BlockSpec((te, d), lambda i, s: (s, 0)),
                  pl.BlockSpec((1, d, te), lambda i, s: (s, 0, 0)),
                  row_spec, row_spec, tab_spec, tab_spec,
                  pl.BlockSpec((tt, d), lambda i, s: (i, 0)),
                  pl.BlockSpec((1, 1, d), lambda i, s: (i // per_seg, 0, 0))],
        out_specs=pl.BlockSpec((tt, d), lambda i, s: (i, 0)),
        out_shape=jax.ShapeDtypeStruct((t, d), F32),
        scratch_shapes=[pltpu.VMEM((d, tt), F32), pltpu.VMEM((te, tt), F32), pltpu.VMEM((te, tt), BF16)],
        compiler_params=_params("arbitrary", "arbitrary"),
        name="peer_mix",
    )(h2, u, vt, *tables, x, gate.reshape(r, 1, d))


def _peer(h2, x, gate, seg, wq, keys, u, vt):
    n_heads = keys.shape[0] // 2
    tables = _peer_scores(h2, wq, keys, n_heads)
    return _peer_mix(h2, u, vt, tables, x, gate, seg, n_heads)


def _rope_tables(n_tokens):
    t = jnp.arange(n_tokens)
    row = (t // GRID_W).astype(F32)
    col = (t % GRID_W).astype(F32)
    n_freq = HEAD_DIM // 4
    inv_freq = jnp.power(ROPE_BASE, -jnp.arange(n_freq, dtype=F32) / n_freq)
    ang = jnp.concatenate([row[:, None] * inv_freq, col[:, None] * inv_freq], axis=-1)
    cos, sin = jnp.cos(ang), jnp.sin(ang)
    cos2 = jnp.tile(jnp.concatenate([cos, cos], axis=-1), (1, 2))
    sin2 = jnp.tile(jnp.concatenate([-sin, sin], axis=-1), (1, 2))
    return cos2, sin2


def _pair_lanes(v):
    return jnp.repeat(v, HEAD_DIM, axis=-1).reshape(*v.shape[:-1], v.shape[-1] // 2, LANES)


def kernel(x, c, ctx, c_ctx, mod_w, mod_b, norm1_g, norm2_g, ab_w_in, ab_w_out, na_q_norm, na_k_norm,
           na_rpb, ret_log_decay, swa_w_in, swa_w_out, swa_q_norm, swa_k_norm, swa_sink,
           peer_w_q, peer_sub_keys, peer_u, peer_v):
    b, s, d = x.shape
    l = ctx.shape[1]
    depth = mod_w.shape[0]
    assert depth == 2 and b + 1 <= 8
    assert s % (NA_Q_ROWS * GRID_W) == 0 and s // GRID_W >= NA_BAND_ROWS
    assert s % TOKEN_TILE == 0 and (b * l) % TOKEN_TILE == 0 and l % RET_CHUNK == 0

    cond = jnp.concatenate([c, c_ctx[None, :], jnp.zeros((8 - b - 1, d), F32)], axis=0)
    mods = _adaln(cond, mod_w, mod_b).reshape(depth, 8, 6, d)
    lat = lambda layer, which: mods[layer, :b, which]
    cx = lambda layer, which: mods[layer, b:b + 1, which]

    cos2, sin2 = _rope_tables(s)
    x_lat = x.reshape(b * s, d)
    x_ctx = ctx.reshape(b * l, d)
    tile2 = lambda g: jnp.tile(g, 2).reshape(1, LANES)

    def peer_weights(layer):
        n_heads = peer_sub_keys.shape[1]
        keys = peer_sub_keys[layer].reshape(2 * n_heads, PEER_KEYS, -1).astype(BF16)
        vt = peer_v[layer].reshape(-1, PEER_EXPERT_TILE, d).transpose(0, 2, 1).astype(BF16)
        return peer_w_q[layer].astype(BF16), keys, peer_u[layer].astype(BF16), vt

    n_na = na_rpb.shape[1]
    n_ret = ret_log_decay.shape[2]
    na_pairs, ret_pairs = n_na // 2, n_ret // 2
    w_in = ab_w_in[0].astype(BF16)
    w_out = ab_w_out[0].astype(BF16)
    wa = n_na * HEAD_DIM
    p_lat = _modmm(x_lat, norm1_g[0], lat(0, 0), lat(0, 1), w_in, s).reshape(b, s, -1)
    p_ctx = _modmm(x_ctx, norm1_g[0], cx(0, 0), cx(0, 1), w_in, b * l).reshape(b, l, -1)

    qn, kn = tile2(na_q_norm[0]), tile2(na_k_norm[0])
    oa_lat = _na_attention(p_lat, p_ctx, _na_bias(na_rpb[0], s // GRID_W), qn, kn, na_pairs)
    oa_ctx = _ctx_attention(p_ctx, qn, kn, na_pairs)

    lg = _pair_lanes(ret_log_decay[0]).transpose(1, 0, 2)
    ret_col0 = 3 * wa // LANES
    zeros_state = jnp.zeros((b, ret_pairs, 4, HEAD_DIM, HEAD_DIM), F32)
    ones_tab, zeros_tab = jnp.ones((l, LANES), F32), jnp.zeros((l, LANES), F32)
    ob_ctx, st_ctx = _retention(p_ctx, ones_tab, zeros_tab, lg, zeros_state, ret_col0, ret_pairs, False)
    ob_lat, _ = _retention(p_lat, cos2, sin2, lg, st_ctx, ret_col0, ret_pairs, True)

    w_list = [w_out[:wa], w_out[wa:]]
    x_lat, h_lat = _outproj([oa_lat.reshape(b * s, -1), ob_lat.reshape(b * s, -1)], w_list, x_lat,
                            lat(0, 2), norm2_g[0], lat(0, 3), lat(0, 4), s)
    x_ctx, h_ctx = _outproj([oa_ctx.reshape(b * l, -1), ob_ctx.reshape(b * l, -1)], w_list, x_ctx,
                            cx(0, 2), norm2_g[0], cx(0, 3), cx(0, 4), b * l)
    pw = peer_weights(0)
    x_lat = _peer(h_lat, x_lat, lat(0, 5), s, *pw)
    x_ctx = _peer(h_ctx, x_ctx, cx(0, 5), b * l, *pw)

    n_q = swa_sink.shape[1]
    n_kv = (swa_w_in.shape[2] // HEAD_DIM - n_q) // 2
    w_in = swa_w_in[0].astype(BF16)
    p_lat = _modmm(x_lat, norm1_g[1], lat(1, 0), lat(1, 1), w_in, s).reshape(b, s, -1)
    p_ctx = _modmm(x_ctx, norm1_g[1], cx(1, 0), cx(1, 1), w_in, b * l).reshape(b, l, -1)
    sink_rows = jnp.broadcast_to(swa_sink[0][:, None], (n_q, LANES))
    o_lat = _swa_attention(p_lat, p_ctx, cos2, sin2, tile2(swa_q_norm[0]), tile2(swa_k_norm[0]),
                           sink_rows, n_q, n_kv)
    x_lat, h_lat = _outproj([o_lat.reshape(b * s, -1)], [swa_w_out[0].astype(BF16)], x_lat,
                            lat(1, 2), norm2_g[1], lat(1, 3), lat(1, 4), s)
    x_lat = _peer(h_lat, x_lat, lat(1, 5), s, *peer_weights(1))
    return x_lat.reshape(b, s, d)
```

```python
import functools

import numpy as np
import jax
import jax.numpy as jnp
from jax import lax
from jax.experimental import pallas as pl
from jax.experimental.pallas import tpu as pltpu

F32 = jnp.float32
BF16 = jnp.bfloat16

HEAD_DIM = 64
GRID_W = 64
NA_ROWS = 8
NA_COLS = 16
SWA_WINDOW = 128
PEER_TOPK = 16
PEER_KEYS = 128
ROPE_BASE = 10000.0
NORM_EPS = 1e-6
GN_EPS = 1e-5
NEG_INF = -1e30
ATTN_SCALE = HEAD_DIM ** -0.5
INV_SQRT2 = 0.7071067811865476

LANES = 128
VMEM_LIMIT = 56 * 1024 * 1024

TOKEN_TILE = 512
PROJ_TOKEN_TILE = 1024
PROJ_N_TILE = 1792
RET_CHUNK = 128
NA_Q_ROWS = 8
NA_BAND_ROWS = 16
SWA_Q_TILE = 256
PEER_TOKEN_TILE = 512
PEER_MIX_TOKEN_TILE = 1024
PEER_EXPERT_TILE = 1024


def _dot(a, b):
    return jnp.dot(a, b, preferred_element_type=F32)


def _dot_nt(a, b):
    return lax.dot_general(a, b, (((1,), (1,)), ((), ())), preferred_element_type=F32)


def _dot_tn(a, b):
    return lax.dot_general(a, b, (((0,), (0,)), ((), ())), preferred_element_type=F32)


def _params(*sem):
    return pltpu.CompilerParams(dimension_semantics=sem, vmem_limit_bytes=VMEM_LIMIT)


def _rms_rows(x, gain):
    ms = jnp.mean(x * x, axis=-1, keepdims=True)
    return x * lax.rsqrt(ms + NORM_EPS) * gain


def _modulate(x, gain, shift, scale):
    return _rms_rows(x, gain) * (1.0 + scale) + shift


def _head_rms(x, gain):
    lane = lax.broadcasted_iota(jnp.int32, x.shape, 1)
    lo = lane < HEAD_DIM
    ss = x * x
    s_lo = jnp.sum(jnp.where(lo, ss, 0.0), axis=-1, keepdims=True)
    s_hi = jnp.sum(jnp.where(lo, 0.0, ss), axis=-1, keepdims=True)
    ms = jnp.where(lo, s_lo, s_hi) * (1.0 / HEAD_DIM)
    return x * lax.rsqrt(ms + NORM_EPS) * gain


def _rope(x, cos2, sin2):
    lane = lax.broadcasted_iota(jnp.int32, x.shape, 1)
    first_half = (lane & (HEAD_DIM // 2)) == 0
    swapped = jnp.where(first_half, pltpu.roll(x, LANES - HEAD_DIM // 2, axis=1),
                        pltpu.roll(x, HEAD_DIM // 2, axis=1))
    return x * cos2 + swapped * sin2


def _adaln_kernel(c_ref, w_ref, b_ref, o_ref):
    c = c_ref[...]
    s = c * jax.nn.sigmoid(c)
    w = w_ref[0]
    s_hi = s.astype(BF16)
    s_lo = (s - s_hi.astype(F32)).astype(BF16)
    w_hi = w.astype(BF16)
    w_lo = (w - w_hi.astype(F32)).astype(BF16)
    acc = _dot(s_hi, w_hi) + _dot(s_lo, w_hi) + _dot(s_hi, w_lo)
    o_ref[0] = acc + b_ref[0]


def _adaln(cond, mod_w, mod_b):
    depth, d, n = mod_w.shape
    tn = n // 4
    return pl.pallas_call(
        _adaln_kernel,
        grid=(depth, n // tn),
        in_specs=[pl.BlockSpec((8, d), lambda l, j: (0, 0)),
                  pl.BlockSpec((1, d, tn), lambda l, j: (l, 0, j)),
                  pl.BlockSpec((1, 1, tn), lambda l, j: (l, 0, j))],
        out_specs=pl.BlockSpec((1, 8, tn), lambda l, j: (l, 0, j)),
        out_shape=jax.ShapeDtypeStruct((depth, 8, n), F32),
        compiler_params=_params("arbitrary", "arbitrary"),
        name="adaln",
    )(cond, mod_w, mod_b.reshape(depth, 1, n))


def _modmm_kernel(x_ref, g_ref, sh_ref, sc_ref, w_ref, o_ref, h_scr):
    @pl.when(pl.program_id(1) == 0)
    def _():
        h_scr[...] = _modulate(x_ref[...], g_ref[...], sh_ref[0], sc_ref[0]).astype(BF16)

    o_ref[...] = _dot(h_scr[...], w_ref[...])


def _modmm(x, gain, shift, scale, w, seg):
    t, d = x.shape
    n = w.shape[1]
    tm = min(PROJ_TOKEN_TILE, seg)
    tn = n if n <= PROJ_N_TILE else n // 2
    per_seg = seg // tm
    r = shift.shape[0]
    return pl.pallas_call(
        _modmm_kernel,
        grid=(t // tm, n // tn),
        in_specs=[pl.BlockSpec((tm, d), lambda i, j: (i, 0)),
                  pl.BlockSpec((1, d), lambda i, j: (0, 0)),
                  pl.BlockSpec((1, 1, d), lambda i, j: (i // per_seg, 0, 0)),
                  pl.BlockSpec((1, 1, d), lambda i, j: (i // per_seg, 0, 0)),
                  pl.BlockSpec((d, tn), lambda i, j: (0, j))],
        out_specs=pl.BlockSpec((tm, tn), lambda i, j: (i, j)),
        out_shape=jax.ShapeDtypeStruct((t, n), F32),
        scratch_shapes=[pltpu.VMEM((tm, d), BF16)],
        compiler_params=_params("arbitrary", "arbitrary"),
        name="modulate_matmul",
    )(x, gain.reshape(1, d), shift.reshape(r, 1, d), scale.reshape(r, 1, d), w)


def _outproj_kernel(*refs, n_in):
    a_refs = refs[:n_in]
    w_refs = refs[n_in:2 * n_in]
    x_ref, gate_ref, g2_ref, sh_ref, sc_ref, xo_ref, h_ref = refs[2 * n_in:]
    acc = None
    for a_ref, w_ref in zip(a_refs, w_refs):
        part = _dot(a_ref[...].astype(BF16), w_ref[...])
        acc = part if acc is None else acc + part
    xn = x_ref[...] + gate_ref[0] * acc
    xo_ref[...] = xn
    h_ref[...] = _modulate(xn, g2_ref[...], sh_ref[0], sc_ref[0]).astype(BF16)


def _outproj(a_list, w_list, x, gate, gain2, shift2, scale2, seg):
    t, d = x.shape
    tm = min(TOKEN_TILE, seg)
    per_seg = seg // tm
    r = gate.shape[0]
    n_in = len(a_list)
    row = lambda i: (i // per_seg, 0, 0)
    in_specs = ([pl.BlockSpec((tm, a.shape[1]), lambda i: (i, 0)) for a in a_list]
                + [pl.BlockSpec(w.shape, lambda i: (0, 0)) for w in w_list]
                + [pl.BlockSpec((tm, d), lambda i: (i, 0)),
                   pl.BlockSpec((1, 1, d), row),
                   pl.BlockSpec((1, d), lambda i: (0, 0)),
                   pl.BlockSpec((1, 1, d), row),
                   pl.BlockSpec((1, 1, d), row)])
    return pl.pallas_call(
        functools.partial(_outproj_kernel, n_in=n_in),
        grid=(t // tm,),
        in_specs=in_specs,
        out_specs=[pl.BlockSpec((tm, d), lambda i: (i, 0)), pl.BlockSpec((tm, d), lambda i: (i, 0))],
        out_shape=[jax.ShapeDtypeStruct((t, d), F32), jax.ShapeDtypeStruct((t, d), BF16)],
        compiler_params=_params("arbitrary"),
        name="out_proj_residual",
    )(*a_list, *w_list, x, gate.reshape(r, 1, d), gain2.reshape(1, d),
      shift2.reshape(r, 1, d), scale2.reshape(r, 1, d))


def _softmax_pv(s_list, v_list, extra=None, fold_lanes=False):
    def lane_chunks(x):
        return [x[:, c:c + LANES] for c in range(0, x.shape[1], LANES)] if fold_lanes else [x]

    def row_reduce(blocks, op, reduce):
        acc = None
        for blk in blocks:
            chunks = lane_chunks(blk)
            part = chunks[0]
            for c in chunks[1:]:
                part = op(part, c)
            if not fold_lanes:
                part = reduce(part, axis=-1, keepdims=True)
            acc = part if acc is None else op(acc, part)
        return reduce(acc, axis=-1, keepdims=True) if fold_lanes else acc

    m = row_reduce(s_list, jnp.maximum, jnp.max)
    if extra is not None:
        m = jnp.maximum(m, extra)
    ps = []
    out = None
    for s, v in zip(s_list, v_list):
        p = jnp.exp(s - m)
        ps.append(p)
        pv = _dot(p.astype(BF16), v)
        out = pv if out is None else out + pv
    denom = row_reduce(ps, jnp.add, jnp.sum)
    if extra is not None:
        denom = denom + jnp.exp(extra - m)
    return out / denom


def _na_kernel(q_ref, k_ref, v_ref, kc_ref, vc_ref, bias_ref, qn_ref, kn_ref, o_ref,
               k_scr, v_scr, kc_scr, vc_scr, *, n_steps):
    i = pl.program_id(2)
    band = NA_BAND_ROWS * GRID_W
    start = jnp.clip(NA_Q_ROWS * i - NA_ROWS // 2, 0, NA_Q_ROWS * n_steps - NA_BAND_ROWS) * GRID_W
    start = pl.multiple_of(start, NA_ROWS // 2 * GRID_W)

    @pl.when(i == 0)
    def _():
        k_scr[...] = _head_rms(k_ref[0], kn_ref[...]).astype(BF16)
        v_scr[...] = v_ref[0].astype(BF16)
        kc_scr[...] = _head_rms(kc_ref[0], kn_ref[...]).astype(BF16)
        vc_scr[...] = vc_ref[0].astype(BF16)

    q = (_head_rms(q_ref[0], qn_ref[...]) * ATTN_SCALE).astype(BF16)
    kb = k_scr[pl.ds(start, band), :]
    vb = v_scr[pl.ds(start, band), :]
    kc = kc_scr[...]
    vc = vc_scr[...]
    outs = []
    for h in range(2):
        sl = slice(h * HEAD_DIM, (h + 1) * HEAD_DIM)
        s_loc = _dot_nt(q[:, sl], kb[:, sl]) + bias_ref[0, h]
        s_ctx = _dot_nt(q[:, sl], kc[:, sl])
        outs.append(_softmax_pv([s_loc, s_ctx], [vb[:, sl], vc[:, sl]]))
    o_ref[0] = jnp.concatenate(outs, axis=-1)


def _na_bias(rpb, rows):
    h = rpb.shape[0]
    n_steps = rows // NA_Q_ROWS
    col = np.arange(GRID_W)
    c0 = np.clip(col - NA_COLS // 2, 0, GRID_W - NA_COLS)
    dc = col[None, :] - col[:, None] + (NA_COLS - 1)
    ok_c = (col[None, :] >= c0[:, None]) & (col[None, :] < c0[:, None] + NA_COLS)
    pick = ((np.arange(2 * NA_COLS - 1)[:, None, None] == dc[None]) & ok_c[None]).astype(np.float32)
    blocks = jnp.einsum('hrd,dqk->hrqk', rpb, pick, precision=lax.Precision.HIGHEST)
    blocks = jnp.where(ok_c, blocks, NEG_INF)
    masked = jnp.full((h, GRID_W, GRID_W), NEG_INF, F32)
    classes = []
    for step in (0, 1, n_steps - 1):
        band0 = min(max(NA_Q_ROWS * step - NA_ROWS // 2, 0), rows - NA_BAND_ROWS)
        q_rows = []
        for rq in range(NA_Q_ROWS):
            r = NA_Q_ROWS * step + rq
            r0 = min(max(r - NA_ROWS // 2, 0), rows - NA_ROWS)
            row = []
            for rk in range(band0, band0 + NA_BAND_ROWS):
                row.append(blocks[:, rk - r + NA_ROWS - 1] if r0 <= rk < r0 + NA_ROWS else masked)
            q_rows.append(jnp.concatenate(row, axis=-1))
        classes.append(jnp.concatenate(q_rows, axis=-2))
    return jnp.stack(classes)


def _na_attention(p_lat, p_ctx, bias, qn, kn, n_pairs):
    b, s, _ = p_lat.shape
    l = p_ctx.shape[1]
    rows = s // GRID_W
    n_steps = rows // NA_Q_ROWS
    qt = NA_Q_ROWS * GRID_W

    def cls(i):
        return jnp.where(i == 0, 0, jnp.where(i == n_steps - 1, 2, 1))

    return pl.pallas_call(
        functools.partial(_na_kernel, n_steps=n_steps),
        grid=(n_pairs, b, n_steps),
        in_specs=[pl.BlockSpec((1, qt, LANES), lambda hp, bb, i: (bb, i, hp)),
                  pl.BlockSpec((1, s, LANES), lambda hp, bb, i: (bb, 0, n_pairs + hp)),
                  pl.BlockSpec((1, s, LANES), lambda hp, bb, i: (bb, 0, 2 * n_pairs + hp)),
                  pl.BlockSpec((1, l, LANES), lambda hp, bb, i: (bb, 0, n_pairs + hp)),
                  pl.BlockSpec((1, l, LANES), lambda hp, bb, i: (bb, 0, 2 * n_pairs + hp)),
                  pl.BlockSpec((1, 2, qt, NA_BAND_ROWS * GRID_W), lambda hp, bb, i: (cls(i), hp, 0, 0)),
                  pl.BlockSpec((1, LANES), lambda hp, bb, i: (0, 0)),
                  pl.BlockSpec((1, LANES), lambda hp, bb, i: (0, 0))],
        out_specs=pl.BlockSpec((1, qt, LANES), lambda hp, bb, i: (bb, i, hp)),
        out_shape=jax.ShapeDtypeStruct((b, s, n_pairs * LANES), F32),
        scratch_shapes=[pltpu.VMEM((s, LANES), BF16), pltpu.VMEM((s, LANES), BF16),
                        pltpu.VMEM((l, LANES), BF16), pltpu.VMEM((l, LANES), BF16)],
        compiler_params=_params("arbitrary", "arbitrary", "arbitrary"),
        name="neighbourhood_attention",
    )(p_lat, p_lat, p_lat, p_ctx, p_ctx, bias, qn, kn)


def _ctx_attn_kernel(q_ref, k_ref, v_ref, qn_ref, kn_ref, o_ref):
    q = (_head_rms(q_ref[0], qn_ref[...]) * ATTN_SCALE).astype(BF16)
    k = _head_rms(k_ref[0], kn_ref[...]).astype(BF16)
    v = v_ref[0].astype(BF16)
    outs = []
    for h in range(2):
        sl = slice(h * HEAD_DIM, (h + 1) * HEAD_DIM)
        outs.append(_softmax_pv([_dot_nt(q[:, sl], k[:, sl])], [v[:, sl]]))
    o_ref[0] = jnp.concatenate(outs, axis=-1)


def _ctx_attention(p_ctx, qn, kn, n_pairs):
    b, l, _ = p_ctx.shape
    return pl.pallas_call(
        _ctx_attn_kernel,
        grid=(n_pairs, b),
        in_specs=[pl.BlockSpec((1, l, LANES), lambda hp, bb: (bb, 0, hp)),
                  pl.BlockSpec((1, l, LANES), lambda hp, bb: (bb, 0, n_pairs + hp)),
                  pl.BlockSpec((1, l, LANES), lambda hp, bb: (bb, 0, 2 * n_pairs + hp)),
                  pl.BlockSpec((1, LANES), lambda hp, bb: (0, 0)),
                  pl.BlockSpec((1, LANES), lambda hp, bb: (0, 0))],
        out_specs=pl.BlockSpec((1, l, LANES), lambda hp, bb: (bb, 0, hp)),
        out_shape=jax.ShapeDtypeStruct((b, l, n_pairs * LANES), F32),
        compiler_params=_params("arbitrary", "arbitrary"),
        name="context_attention",
    )(p_ctx, p_ctx, p_ctx, qn, kn)


def _ret_kernel(q_ref, k_ref, v_ref, g_ref, cos_ref, sin_ref, lg_ref, s0_ref, y_ref, st_ref, sf_scr,
                *, n_chunks, use_rope):
    c = RET_CHUNK
    hd = HEAD_DIM
    lg = -jnp.exp(lg_ref[0])
    lgf, lgb = lg[0:1, :], lg[1:2, :]
    ii = lax.broadcasted_iota(jnp.int32, (c, LANES), 0).astype(F32)
    dq_f = jnp.exp(lgf * (ii + 1.0))
    dk_f = jnp.exp(lgf * (c - 1.0 - ii))
    dq_b = jnp.exp(lgb * (c - ii))
    dk_b = jnp.exp(lgb * ii)
    dc_f = jnp.exp(lgf * float(c))
    dc_b = jnp.exp(lgb * float(c))
    diff = (lax.broadcasted_iota(jnp.int32, (c, c), 0) - lax.broadcasted_iota(jnp.int32, (c, c), 1)).astype(F32)
    intra = []
    chunk_f = []
    chunk_b = []
    for h in range(2):
        lf = lgf[:, h * hd:h * hd + 1]
        lb = lgb[:, h * hd:h * hd + 1]
        intra.append(jnp.where(diff >= 0, jnp.exp(lf * jnp.maximum(diff, 0.0)),
                               jnp.exp(lb * jnp.maximum(-diff, 0.0))))
        chunk_f.append(dc_f[:, h * hd:h * hd + 1])
        chunk_b.append(dc_b[:, h * hd:h * hd + 1])

    def load(n):
        r = pl.multiple_of(n * c, c)
        q = q_ref[0, pl.ds(r, c), :]
        k = k_ref[0, pl.ds(r, c), :]
        v = v_ref[0, pl.ds(r, c), :]
        if use_rope:
            cs = cos_ref[pl.ds(r, c), :]
            sn = sin_ref[pl.ds(r, c), :]
            q = _rope(q, cs, sn)
            k = _rope(k, cs, sn)
        return r, q * ATTN_SCALE, k, v.astype(BF16)

    def fwd(n, carry):
        _, _, k, v = load(n)
        kd = (k * dk_f).astype(BF16)
        new = []
        for h in range(2):
            sl = slice(h * hd, (h + 1) * hd)
            sf_scr[n, h] = carry[h]
            new.append(carry[h] * chunk_f[h] + _dot_tn(kd[:, sl], v[:, sl]))
        return tuple(new)

    sf = lax.fori_loop(0, n_chunks, fwd, (s0_ref[0, 0, 0], s0_ref[0, 0, 1]), unroll=2)
    st_ref[0, 0, 0] = sf[0]
    st_ref[0, 0, 1] = sf[1]

    def bwd(jj, carry):
        n = n_chunks - 1 - jj
        r, q, k, v = load(n)
        qb = q.astype(BF16)
        kb = k.astype(BF16)
        qf = (q * dq_f).astype(BF16)
        qr = (q * dq_b).astype(BF16)
        kd = (k * dk_b).astype(BF16)
        outs = []
        new = []
        for h in range(2):
            sl = slice(h * hd, (h + 1) * hd)
            a = (_dot_nt(qb[:, sl], kb[:, sl]) * intra[h]).astype(BF16)
            o = (_dot(a, v[:, sl]) + _dot(qf[:, sl], sf_scr[n, h].astype(BF16))
                 + _dot(qr[:, sl], carry[h].astype(BF16)))
            oc = o - jnp.mean(o, axis=-1, keepdims=True)
            outs.append(oc * lax.rsqrt(jnp.mean(oc * oc, axis=-1, keepdims=True) + GN_EPS))
            new.append(carry[h] * chunk_b[h] + _dot_tn(kd[:, sl], v[:, sl]))
        g = g_ref[0, pl.ds(r, c), :]
        y_ref[0, pl.ds(r, c), :] = jnp.concatenate(outs, axis=-1) * (g * jax.nn.sigmoid(g))
        return tuple(new)

    sb = lax.fori_loop(0, n_chunks, bwd, (s0_ref[0, 0, 2], s0_ref[0, 0, 3]), unroll=2)
    st_ref[0, 0, 2] = sb[0]
    st_ref[0, 0, 3] = sb[1]


def _retention(p, cos2, sin2, lg, s0, col0, n_pairs, use_rope):
    b, t, _ = p.shape
    n_chunks = t // RET_CHUNK
    tab = lambda hp, bb: (0, 0)
    return pl.pallas_call(
        functools.partial(_ret_kernel, n_chunks=n_chunks, use_rope=use_rope),
        grid=(n_pairs, b),
        in_specs=[pl.BlockSpec((1, t, LANES), lambda hp, bb: (bb, 0, col0 + hp)),
                  pl.BlockSpec((1, t, LANES), lambda hp, bb: (bb, 0, col0 + n_pairs + hp)),
                  pl.BlockSpec((1, t, LANES), lambda hp, bb: (bb, 0, col0 + 2 * n_pairs + hp)),
                  pl.BlockSpec((1, t, LANES), lambda hp, bb: (bb, 0, col0 + 3 * n_pairs + hp)),
                  pl.BlockSpec(cos2.shape, tab),
                  pl.BlockSpec(sin2.shape, tab),
                  pl.BlockSpec((1, 2, LANES), lambda hp, bb: (hp, 0, 0)),
                  pl.BlockSpec((1, 1, 4, HEAD_DIM, HEAD_DIM), lambda hp, bb: (bb, hp, 0, 0, 0))],
        out_specs=[pl.BlockSpec((1, t, LANES), lambda hp, bb: (bb, 0, hp)),
                   pl.BlockSpec((1, 1, 4, HEAD_DIM, HEAD_DIM), lambda hp, bb: (bb, hp, 0, 0, 0))],
        out_shape=[jax.ShapeDtypeStruct((b, t, n_pairs * LANES), F32),
                   jax.ShapeDtypeStruct((b, n_pairs, 4, HEAD_DIM, HEAD_DIM), F32)],
        scratch_shapes=[pltpu.VMEM((n_chunks, 2, HEAD_DIM, HEAD_DIM), F32)],
        compiler_params=_params("arbitrary", "arbitrary"),
        name="retention",
    )(p, p, p, p, cos2, sin2, lg, s0)


def _swa_kernel(q_ref, k_ref, v_ref, kc_ref, vc_ref, cos_ref, sin_ref, qn_ref, kn_ref, sink_ref, far_ref,
                o_ref, *, seq):
    qt = SWA_Q_TILE
    wk = qt + 2 * SWA_WINDOW
    hd = HEAD_DIM
    n = pl.program_id(2)
    q0 = pl.multiple_of(n * qt, qt)
    ws = pl.multiple_of(jnp.clip(n * qt - SWA_WINDOW, 0, seq - wk), SWA_WINDOW)
    kw = _rope(_head_rms(k_ref[0, pl.ds(ws, wk), :], kn_ref[...]),
               cos_ref[pl.ds(ws, wk), :], sin_ref[pl.ds(ws, wk), :]).astype(BF16)
    vw = v_ref[0, pl.ds(ws, wk), :].astype(BF16)
    kc = _head_rms(kc_ref[0], kn_ref[...]).astype(BF16)
    vc = vc_ref[0].astype(BF16)
    cos_q = cos_ref[pl.ds(q0, qt), :]
    sin_q = sin_ref[pl.ds(q0, qt), :]
    qs = []
    for s in range(4):
        slab = q_ref[0, :, s * LANES:(s + 1) * LANES]
        qs.append((_rope(_head_rms(slab, qn_ref[...]), cos_q, sin_q) * ATTN_SCALE).astype(BF16))
    far = far_ref[0][None]
    for kh in range(2):
        sl = slice(kh * hd, (kh + 1) * hd)
        qstack = jnp.concatenate(
            [qs[kh * 2 + g // 2][:, (g % 2) * hd:(g % 2 + 1) * hd] for g in range(4)], axis=0)
        sink = jnp.concatenate(
            [jnp.broadcast_to(sink_ref[kh * 4 + g:kh * 4 + g + 1, 0:1], (qt, 1)) for g in range(4)], axis=0)
        s_loc = (_dot_nt(qstack, kw[:, sl]).reshape(4, qt, wk) + far).reshape(4 * qt, wk)
        s_ctx = _dot_nt(qstack, kc[:, sl])
        o = _softmax_pv([s_loc, s_ctx], [vw[:, sl], vc[:, sl]], extra=sink, fold_lanes=True)
        for pair in range(2):
            col = (kh * 2 + pair) * LANES
            o_ref[0, :, col:col + LANES] = jnp.concatenate(
                [o[(2 * pair) * qt:(2 * pair + 1) * qt], o[(2 * pair + 1) * qt:(2 * pair + 2) * qt]], axis=-1)


def _swa_attention(p_lat, p_ctx, cos2, sin2, qn, kn, sink_rows, n_q_heads, n_kv_heads):
    b, s, _ = p_lat.shape
    l = p_ctx.shape[1]
    kv_pairs = n_kv_heads // 2
    q_blocks = n_q_heads * HEAD_DIM // LANES
    q_per_pair = q_blocks // kv_pairs
    qw = q_per_pair * LANES
    tab = lambda kp, bb, n: (0, 0)
    n_steps = s // SWA_Q_TILE
    wk = SWA_Q_TILE + 2 * SWA_WINDOW
    far = []
    for step in (0, 1, n_steps - 1):
        q0 = step * SWA_Q_TILE
        ws = min(max(q0 - SWA_WINDOW, 0), s - wk)
        dist = np.abs((q0 + np.arange(SWA_Q_TILE))[:, None] - (ws + np.arange(wk))[None, :])
        far.append(np.where(dist <= SWA_WINDOW, 0.0, NEG_INF))
    far = jnp.asarray(np.stack(far), F32)

    def cls(n):
        return jnp.where(n == 0, 0, jnp.where(n == n_steps - 1, 2, 1))

    return pl.pallas_call(
        functools.partial(_swa_kernel, seq=s),
        grid=(kv_pairs, b, n_steps),
        in_specs=[pl.BlockSpec((1, SWA_Q_TILE, qw), lambda kp, bb, n: (bb, n, kp)),
                  pl.BlockSpec((1, s, LANES), lambda kp, bb, n: (bb, 0, q_blocks + kp)),
                  pl.BlockSpec((1, s, LANES), lambda kp, bb, n: (bb, 0, q_blocks + kv_pairs + kp)),
                  pl.BlockSpec((1, l, LANES), lambda kp, bb, n: (bb, 0, q_blocks + kp)),
                  pl.BlockSpec((1, l, LANES), lambda kp, bb, n: (bb, 0, q_blocks + kv_pairs + kp)),
                  pl.BlockSpec(cos2.shape, tab),
                  pl.BlockSpec(sin2.shape, tab),
                  pl.BlockSpec((1, LANES), tab),
                  pl.BlockSpec((1, LANES), tab),
                  pl.BlockSpec((8, LANES), lambda kp, bb, n: (kp, 0)),
                  pl.BlockSpec((1, SWA_Q_TILE, wk), lambda kp, bb, n: (cls(n), 0, 0))],
        out_specs=pl.BlockSpec((1, SWA_Q_TILE, qw), lambda kp, bb, n: (bb, n, kp)),
        out_shape=jax.ShapeDtypeStruct((b, s, n_q_heads * HEAD_DIM), F32),
        compiler_params=_params("arbitrary", "arbitrary", "arbitrary"),
        name="windowed_gqa",
    )(p_lat, p_lat, p_lat, p_ctx, p_ctx, cos2, sin2, qn, kn, sink_rows, far)


def _oddeven_merge_sort(n):
    pairs = []

    def merge(lo, size, r):
        step = r * 2
        if step < size:
            merge(lo, size, step)
            merge(lo + r, size, step)
            pairs.extend((i, i + r) for i in range(lo + r, lo + size - r, step))
        else:
            pairs.append((lo, lo + r))

    def sort(lo, size):
        if size > 1:
            sort(lo, size // 2)
            sort(lo + size // 2, size // 2)
            merge(lo, size, 1)

    sort(0, n)
    return pairs


def _top_rows(x, k, scr):
    tiles = [x[8 * i:8 * i + 8, :] for i in range(x.shape[0] // 8)]
    for i, j in _oddeven_merge_sort(len(tiles)):
        tiles[i], tiles[j] = jnp.maximum(tiles[i], tiles[j]), jnp.minimum(tiles[i], tiles[j])
    for r in range(k):
        m = jnp.max(tiles[0], axis=0, keepdims=True)
        scr[r:r + 1, :] = m
        need = k - r - 1
        if need > 0:
            hit = tiles[0] == m
            for d in range(min(need, len(tiles) - 1)):
                tiles[d] = jnp.where(hit, tiles[d + 1], tiles[d])
            if need >= len(tiles):
                tiles[-1] = jnp.where(hit, NEG_INF, tiles[-1])


def _bf16_pair(x):
    bits = lax.bitcast_convert_type(x.astype(BF16).astype(F32), jnp.uint32)
    return bits | (bits >> 16)


def _count_above(sorted_scr, y, strict):
    row = lambda i: sorted_scr[i:i + 1, :]
    above = (lambda r: r > y) if strict else (lambda r: r >= y)
    c8 = above(row(7))
    c4 = above(jnp.where(c8, row(11), row(3)))
    c2 = above(jnp.where(c8, jnp.where(c4, row(13), row(9)), jnp.where(c4, row(5), row(1))))
    hi = jnp.where(c4, jnp.where(c2, row(14), row(12)), jnp.where(c2, row(10), row(8)))
    lo = jnp.where(c4, jnp.where(c2, row(6), row(4)), jnp.where(c2, row(2), row(0)))
    c1 = above(jnp.where(c8, hi, lo))
    count = (jnp.where(c8, 8.0, 0.0) + jnp.where(c4, 4.0, 0.0)) + (jnp.where(c2, 2.0, 0.0) + jnp.where(c1, 1.0, 0.0))
    return count + jnp.where(above(row(15)), 1.0, 0.0)


def _peer_scores_kernel(h_ref, wq_ref, keys_ref, ra_ref, p1_ref, gb_ref, p2_ref,
                        q_scr, s_scr, a_scr, b_scr, c_scr, v_scr, *, n_heads):
    k = PEER_TOPK
    n_blocks = h_ref.shape[0] // LANES

    q = _dot(h_ref[...], wq_ref[...])
    for hp in range(2 * n_heads):
        q_scr[hp] = q[:, hp * PEER_KEYS:(hp + 1) * PEER_KEYS].astype(BF16)

    def lane_block(lb, h, slot):
        tops_a, tops_b, cand, tops_c = a_scr.at[slot], b_scr.at[slot], c_scr.at[slot], v_scr.at[slot]
        s1 = s_scr[0, lb]
        s2 = s_scr[1, lb]
        _top_rows(s1, k + 1, tops_a)
        _top_rows(s2, k + 1, tops_b)
        cand[0:16, :] = tops_a[0:1, :] + tops_b[0:16, :]
        for i in range(1, 8):
            cand[8 + 8 * i:16 + 8 * i, :] = tops_a[i:i + 1, :] + tops_b[0:8, :]
        cand[72:80, :] = tops_a[8:16, :] + tops_b[0:1, :]
        cand[80:81, :] = tops_a[0:1, :] + tops_b[16:17, :]
        cand[81:82, :] = tops_a[16:17, :] + tops_b[0:1, :]
        cand[82:PEER_KEYS, :] = jnp.full((PEER_KEYS - 82, LANES), NEG_INF, F32)
        _top_rows(cand[...], k + 1, tops_c)
        thr = 0.5 * (tops_c[k - 1:k, :] + tops_c[k:k + 1, :])
        z = jnp.sum(jnp.exp(tops_c[0:k, :] - tops_c[0:1, :]), axis=0, keepdims=True)
        ra_ref[lb, h] = _bf16_pair(_count_above(tops_a, s1, strict=True))
        p1_ref[lb, h] = _bf16_pair(jnp.exp(s1 - tops_a[0:1, :]) * (0.5 / z))
        gb_ref[lb, h] = _count_above(tops_a, thr - s2, strict=False).astype(BF16)
        p2_ref[lb, h] = jnp.exp(s2 - tops_b[0:1, :]).astype(BF16)

    def lane_pair(i, h):
        lane_block(2 * i, h, 0)
        lane_block(2 * i + 1, h, 1)
        return h

    def head(h, carry):
        for p in range(2):
            st = _dot_nt(keys_ref[2 * h + p], q_scr[2 * h + p])
            for lb in range(n_blocks):
                s_scr[p, lb] = st[:, lb * LANES:(lb + 1) * LANES]
        lax.fori_loop(0, n_blocks // 2, lane_pair, h)
        return carry

    lax.fori_loop(0, n_heads, head, 0)


def _peer_scores(h2, wq, keys, n_heads):
    t, d = h2.shape
    tt = PEER_TOKEN_TILE
    nb = tt // LANES
    tab_shape = (t // LANES, n_heads, PEER_KEYS, LANES)
    tab_spec = pl.BlockSpec((nb, n_heads, PEER_KEYS, LANES), lambda i: (i, 0, 0, 0))
    return pl.pallas_call(
        functools.partial(_peer_scores_kernel, n_heads=n_heads),
        grid=(t // tt,),
        in_specs=[pl.BlockSpec((tt, d), lambda i: (i, 0)),
                  pl.BlockSpec(wq.shape, lambda i: (0, 0)),
                  pl.BlockSpec(keys.shape, lambda i: (0, 0, 0))],
        out_specs=[tab_spec] * 4,
        out_shape=[jax.ShapeDtypeStruct(tab_shape, dt) for dt in (jnp.uint32, jnp.uint32, BF16, BF16)],
        scratch_shapes=[pltpu.VMEM((2 * n_heads, tt, PEER_KEYS), BF16),
                        pltpu.VMEM((2, nb, PEER_KEYS, LANES), F32),
                        pltpu.VMEM((2, 24, LANES), F32), pltpu.VMEM((2, 24, LANES), F32),
                        pltpu.VMEM((2, PEER_KEYS, LANES), F32), pltpu.VMEM((2, 24, LANES), F32)],
        compiler_params=_params("arbitrary"),
        name="peer_scores",
    )(h2, wq, keys)


def _peer_mix_kernel(h_ref, u_ref, vt_ref, ra_ref, p1_ref, gb_ref, p2_ref, x_ref, g_ref, o_ref,
                     acc_scr, act_scr, a_scr, *, n_heads, n_tiles):
    s = pl.program_id(1)
    tt = h_ref.shape[0]
    rows_per_tile = PEER_EXPERT_TILE // PEER_KEYS
    tile = (PEER_KEYS, LANES)

    def readout():
        acc_scr[...] += _dot(vt_ref[0], a_scr[...])

    def row_tile(ref, lb, h, aa):
        words = jnp.broadcast_to(ref[lb, h, aa:aa + 1, :], (8, LANES))
        packed = pltpu.bitcast(words, BF16)
        return jnp.broadcast_to(packed[None], (PEER_KEYS // 16, 16, LANES)).reshape(tile)

    def gates():
        for aa in range(rows_per_tile):
            rs = slice(aa * PEER_KEYS, (aa + 1) * PEER_KEYS)
            for lb in range(tt // LANES):
                ls = slice(lb * LANES, (lb + 1) * LANES)
                w = None
                for h in range(n_heads):
                    rank = row_tile(ra_ref, lb, h, aa)
                    p1 = row_tile(p1_ref, lb, h, aa)
                    term = jnp.where(rank < gb_ref[lb, h], p2_ref[lb, h], jnp.zeros(tile, BF16)) * p1
                    w = term if w is None else w + term
                act = act_scr[rs, ls]
                gelu2 = act * (1.0 + lax.erf(act * INV_SQRT2))
                a_scr[rs, ls] = w * gelu2.astype(BF16)

    def experts():
        act_scr[...] = _dot_nt(u_ref[...], h_ref[...])

    @pl.when(s == 0)
    def _():
        acc_scr[...] = jnp.zeros_like(acc_scr)

    i = pl.program_id(0)

    @pl.when(s < n_tiles)
    def _():
        experts()

    @pl.when(i >= 0)
    def _():
        gates()

    @pl.when(s + i >= 0)
    def _():
        readout()

    @pl.when(s == n_tiles - 1)
    def _():
        o_ref[...] = x_ref[...] + g_ref[0] * acc_scr[...].T


def _peer_mix(h2, u, vt, tables, x, gate, seg, n_heads):
    t, d = h2.shape
    n_exp = u.shape[0]
    tt = min(PEER_MIX_TOKEN_TILE, seg)
    te = PEER_EXPERT_TILE
    n_tiles = n_exp // te
    per_seg = seg // tt
    r = gate.shape[0]
    nb = tt // LANES
    tab_spec = pl.BlockSpec((nb, n_heads, PEER_KEYS, LANES), lambda i, s: (i, 0, 0, 0))
    row_spec = pl.BlockSpec((nb, n_heads, te // PEER_KEYS, LANES), lambda i, s: (i, 0, s, 0))
    return pl.pallas_call(
        functools.partial(_peer_mix_kernel, n_heads=n_heads, n_tiles=n_tiles),
        grid=(t // tt, n_tiles),
        in_specs=[pl.BlockSpec((tt, d), lambda i, s: (i, 0)),
                  pl.BlockSpec((te, d), lambda i, s: (s, 0)),
                  pl.BlockSpec((1, d, te), lambda i, s: (s, 0, 0)),
                  row_spec, row_spec, tab_spec, tab_spec,
                  pl.BlockSpec((tt, d), lambda i, s: (i, 0)),
                  pl.BlockSpec((1, 1, d), lambda i, s: (i // per_seg, 0, 0))],
        out_specs=pl.BlockSpec((tt, d), lambda i, s: (i, 0)),
        out_shape=jax.ShapeDtypeStruct((t, d), F32),
        scratch_shapes=[pltpu.VMEM((d, tt), F32), pltpu.VMEM((te, tt), F32), pltpu.VMEM((te, tt), BF16)],
        compiler_params=_params("arbitrary", "arbitrary"),
        name="peer_mix",
    )(h2, u, vt, *tables, x, gate.reshape(r, 1, d))


def _peer(h2, x, gate, seg, wq, keys, u, vt):
    n_heads = keys.shape[0] // 2
    tables = _peer_scores(h2, wq, keys, n_heads)
    return _peer_mix(h2, u, vt, tables, x, gate, seg, n_heads)


def _rope_tables(n_tokens):
    t = jnp.arange(n_tokens)
    row = (t // GRID_W).astype(F32)
    col = (t % GRID_W).astype(F32)
    n_freq = HEAD_DIM // 4
    inv_freq = jnp.power(ROPE_BASE, -jnp.arange(n_freq, dtype=F32) / n_freq)
    ang = jnp.concatenate([row[:, None] * inv_freq, col[:, None] * inv_freq], axis=-1)
    cos, sin = jnp.cos(ang), jnp.sin(ang)
    cos2 = jnp.tile(jnp.concatenate([cos, cos], axis=-1), (1, 2))
    sin2 = jnp.tile(jnp.concatenate([-sin, sin], axis=-1), (1, 2))
    return cos2, sin2


def _pair_lanes(v):
    return jnp.repeat(v, HEAD_DIM, axis=-1).reshape(*v.shape[:-1], v.shape[-1] // 2, LANES)


def kernel(x, c, ctx, c_ctx, mod_w, mod_b, norm1_g, norm2_g, ab_w_in, ab_w_out, na_q_norm, na_k_norm,
           na_rpb, ret_log_decay, swa_w_in, swa_w_out, swa_q_norm, swa_k_norm, swa_sink,
           peer_w_q, peer_sub_keys, peer_u, peer_v):
    b, s, d = x.shape
    l = ctx.shape[1]
    depth = mod_w.shape[0]
    assert depth == 2 and b + 1 <= 8
    assert s % (NA_Q_ROWS * GRID_W) == 0 and s // GRID_W >= NA_BAND_ROWS
    assert s % TOKEN_TILE == 0 and (b * l) % TOKEN_TILE == 0 and l % RET_CHUNK == 0

    cond = jnp.concatenate([c, c_ctx[None, :], jnp.zeros((8 - b - 1, d), F32)], axis=0)
    mods = _adaln(cond, mod_w, mod_b).reshape(depth, 8, 6, d)
    lat = lambda layer, which: mods[layer, :b, which]
    cx = lambda layer, which: mods[layer, b:b + 1, which]

    cos2, sin2 = _rope_tables(s)
    x_lat = x.reshape(b * s, d)
    x_ctx = ctx.reshape(b * l, d)
    tile2 = lambda g: jnp.tile(g, 2).reshape(1, LANES)

    def peer_weights(layer):
        n_heads = peer_sub_keys.shape[1]
        keys = peer_sub_keys[layer].reshape(2 * n_heads, PEER_KEYS, -1).astype(BF16)
        vt = peer_v[layer].reshape(-1, PEER_EXPERT_TILE, d).transpose(0, 2, 1).astype(BF16)
        return peer_w_q[layer].astype(BF16), keys, peer_u[layer].astype(BF16), vt

    n_na = na_rpb.shape[1]
    n_ret = ret_log_decay.shape[2]
    na_pairs, ret_pairs = n_na // 2, n_ret // 2
    w_in = ab_w_in[0].astype(BF16)
    w_out = ab_w_out[0].astype(BF16)
    wa = n_na * HEAD_DIM
    p_lat = _modmm(x_lat, norm1_g[0], lat(0, 0), lat(0, 1), w_in, s).reshape(b, s, -1)
    p_ctx = _modmm(x_ctx, norm1_g[0], cx(0, 0), cx(0, 1), w_in, b * l).reshape(b, l, -1)

    qn, kn = tile2(na_q_norm[0]), tile2(na_k_norm[0])
    oa_lat = _na_attention(p_lat, p_ctx, _na_bias(na_rpb[0], s // GRID_W), qn, kn, na_pairs)
    oa_ctx = _ctx_attention(p_ctx, qn, kn, na_pairs)

    lg = _pair_lanes(ret_log_decay[0]).transpose(1, 0, 2)
    ret_col0 = 3 * wa // LANES
    zeros_state = jnp.zeros((b, ret_pairs, 4, HEAD_DIM, HEAD_DIM), F32)
    ones_tab, zeros_tab = jnp.ones((l, LANES), F32), jnp.zeros((l, LANES), F32)
    ob_ctx, st_ctx = _retention(p_ctx, ones_tab, zeros_tab, lg, zeros_state, ret_col0, ret_pairs, False)
    ob_lat, _ = _retention(p_lat, cos2, sin2, lg, st_ctx, ret_col0, ret_pairs, True)

    w_list = [w_out[:wa], w_out[wa:]]
    x_lat, h_lat = _outproj([oa_lat.reshape(b * s, -1), ob_lat.reshape(b * s, -1)], w_list, x_lat,
                            lat(0, 2), norm2_g[0], lat(0, 3), lat(0, 4), s)
    x_ctx, h_ctx = _outproj([oa_ctx.reshape(b * l, -1), ob_ctx.reshape(b * l, -1)], w_list, x_ctx,
                            cx(0, 2), norm2_g[0], cx(0, 3), cx(0, 4), b * l)
    pw = peer_weights(0)
    x_lat = _peer(h_lat, x_lat, lat(0, 5), s, *pw)
    x_ctx = _peer(h_ctx, x_ctx, cx(0, 5), b * l, *pw)

    n_q = swa_sink.shape[1]
    n_kv = (swa_w_in.shape[2] // HEAD_DIM - n_q) // 2
    w_in = swa_w_in[0].astype(BF16)
    p_lat = _modmm(x_lat, norm1_g[1], lat(1, 0), lat(1, 1), w_in, s).reshape(b, s, -1)
    p_ctx = _modmm(x_ctx, norm1_g[1], cx(1, 0), cx(1, 1), w_in, b * l).reshape(b, l, -1)
    sink_rows = jnp.broadcast_to(swa_sink[0][:, None], (n_q, LANES))
    o_lat = _swa_attention(p_lat, p_ctx, cos2, sin2, tile2(swa_q_norm[0]), tile2(swa_k_norm[0]),
                           sink_rows, n_q, n_kv)
    x_lat, h_lat = _outproj([o_lat.reshape(b * s, -1)], [swa_w_out[0].astype(BF16)], x_lat,
                            lat(1, 2), norm2_g[1], lat(1, 3), lat(1, 4), s)
    x_lat = _peer(h_lat, x_lat, lat(1, 5), s, *peer_weights(1))
    return x_lat.reshape(b, s, d)
```

```python
import functools

import numpy as np
import jax
import jax.numpy as jnp
from jax import lax
from jax.experimental import pallas as pl
from jax.experimental.pallas import tpu as pltpu

F32 = jnp.float32
BF16 = jnp.bfloat16

HEAD_DIM = 64
GRID_W = 64
NA_ROWS = 8
NA_COLS = 16
SWA_WINDOW = 128
PEER_TOPK = 16
PEER_KEYS = 128
ROPE_BASE = 10000.0
NORM_EPS = 1e-6
GN_EPS = 1e-5
NEG_INF = -1e30
ATTN_SCALE = HEAD_DIM ** -0.5
INV_SQRT2 = 0.7071067811865476

LANES = 128
VMEM_LIMIT = 56 * 1024 * 1024

TOKEN_TILE = 512
PROJ_TOKEN_TILE = 1024
PROJ_N_TILE = 1792
RET_CHUNK = 128
NA_Q_ROWS = 8
NA_BAND_ROWS = 16
SWA_Q_TILE = 256
PEER_TOKEN_TILE = 512
PEER_MIX_TOKEN_TILE = 1024
PEER_EXPERT_TILE = 1024


def _dot(a, b):
    return jnp.dot(a, b, preferred_element_type=F32)


def _dot_nt(a, b):
    return lax.dot_general(a, b, (((1,), (1,)), ((), ())), preferred_element_type=F32)


def _dot_tn(a, b):
    return lax.dot_general(a, b, (((0,), (0,)), ((), ())), preferred_element_type=F32)


def _params(*sem):
    return pltpu.CompilerParams(dimension_semantics=sem, vmem_limit_bytes=VMEM_LIMIT)


def _rms_rows(x, gain):
    ms = jnp.mean(x * x, axis=-1, keepdims=True)
    return x * lax.rsqrt(ms + NORM_EPS) * gain


def _modulate(x, gain, shift, scale):
    return _rms_rows(x, gain) * (1.0 + scale) + shift


def _head_rms(x, gain):
    lane = lax.broadcasted_iota(jnp.int32, x.shape, 1)
    lo = lane < HEAD_DIM
    ss = x * x
    s_lo = jnp.sum(jnp.where(lo, ss, 0.0), axis=-1, keepdims=True)
    s_hi = jnp.sum(jnp.where(lo, 0.0, ss), axis=-1, keepdims=True)
    ms = jnp.where(lo, s_lo, s_hi) * (1.0 / HEAD_DIM)
    return x * lax.rsqrt(ms + NORM_EPS) * gain


def _rope(x, cos2, sin2):
    lane = lax.broadcasted_iota(jnp.int32, x.shape, 1)
    first_half = (lane & (HEAD_DIM // 2)) == 0
    swapped = jnp.where(first_half, pltpu.roll(x, LANES - HEAD_DIM // 2, axis=1),
                        pltpu.roll(x, HEAD_DIM // 2, axis=1))
    return x * cos2 + swapped * sin2


def _adaln_kernel(c_ref, w_ref, b_ref, o_ref):
    c = c_ref[...]
    s = c * jax.nn.sigmoid(c)
    w = w_ref[0]
    s_hi = s.astype(BF16)
    s_lo = (s - s_hi.astype(F32)).astype(BF16)
    w_hi = w.astype(BF16)
    w_lo = (w - w_hi.astype(F32)).astype(BF16)
    acc = _dot(s_hi, w_hi) + _dot(s_lo, w_hi) + _dot(s_hi, w_lo)
    o_ref[0] = acc + b_ref[0]


def _adaln(cond, mod_w, mod_b):
    depth, d, n = mod_w.shape
    tn = n // 4
    return pl.pallas_call(
        _adaln_kernel,
        grid=(depth, n // tn),
        in_specs=[pl.BlockSpec((8, d), lambda l, j: (0, 0)),
                  pl.BlockSpec((1, d, tn), lambda l, j: (l, 0, j)),
                  pl.BlockSpec((1, 1, tn), lambda l, j: (l, 0, j))],
        out_specs=pl.BlockSpec((1, 8, tn), lambda l, j: (l, 0, j)),
        out_shape=jax.ShapeDtypeStruct((depth, 8, n), F32),
        compiler_params=_params("arbitrary", "arbitrary"),
        name="adaln",
    )(cond, mod_w, mod_b.reshape(depth, 1, n))


def _modmm_kernel(x_ref, g_ref, sh_ref, sc_ref, w_ref, o_ref, h_scr):
    @pl.when(pl.program_id(1) == 0)
    def _():
        h_scr[...] = _modulate(x_ref[...], g_ref[...], sh_ref[0], sc_ref[0]).astype(BF16)

    o_ref[...] = _dot(h_scr[...], w_ref[...])


def _modmm(x, gain, shift, scale, w, seg):
    t, d = x.shape
    n = w.shape[1]
    tm = min(PROJ_TOKEN_TILE, seg)
    tn = n if n <= PROJ_N_TILE else n // 2
    per_seg = seg // tm
    r = shift.shape[0]
    return pl.pallas_call(
        _modmm_kernel,
        grid=(t // tm, n // tn),
        in_specs=[pl.BlockSpec((tm, d), lambda i, j: (i, 0)),
                  pl.BlockSpec((1, d), lambda i, j: (0, 0)),
                  pl.BlockSpec((1, 1, d), lambda i, j: (i // per_seg, 0, 0)),
                  pl.BlockSpec((1, 1, d), lambda i, j: (i // per_seg, 0, 0)),
                  pl.BlockSpec((d, tn), lambda i, j: (0, j))],
        out_specs=pl.BlockSpec((tm, tn), lambda i, j: (i, j)),
        out_shape=jax.ShapeDtypeStruct((t, n), F32),
        scratch_shapes=[pltpu.VMEM((tm, d), BF16)],
        compiler_params=_params("arbitrary", "arbitrary"),
        name="modulate_matmul",
    )(x, gain.reshape(1, d), shift.reshape(r, 1, d), scale.reshape(r, 1, d), w)


def _outproj_kernel(*refs, n_in):
    a_refs = refs[:n_in]
    w_refs = refs[n_in:2 * n_in]
    x_ref, gate_ref, g2_ref, sh_ref, sc_ref, xo_ref, h_ref = refs[2 * n_in:]
    acc = None
    for a_ref, w_ref in zip(a_refs, w_refs):
        part = _dot(a_ref[...].astype(BF16), w_ref[...])
        acc = part if acc is None else acc + part
    xn = x_ref[...] + gate_ref[0] * acc
    xo_ref[...] = xn
    h_ref[...] = _modulate(xn, g2_ref[...], sh_ref[0], sc_ref[0]).astype(BF16)


def _outproj(a_list, w_list, x, gate, gain2, shift2, scale2, seg):
    t, d = x.shape
    tm = min(TOKEN_TILE, seg)
    per_seg = seg // tm
    r = gate.shape[0]
    n_in = len(a_list)
    row = lambda i: (i // per_seg, 0, 0)
    in_specs = ([pl.BlockSpec((tm, a.shape[1]), lambda i: (i, 0)) for a in a_list]
                + [pl.BlockSpec(w.shape, lambda i: (0, 0)) for w in w_list]
                + [pl.BlockSpec((tm, d), lambda i: (i, 0)),
                   pl.BlockSpec((1, 1, d), row),
                   pl.BlockSpec((1, d), lambda i: (0, 0)),
                   pl.BlockSpec((1, 1, d), row),
                   pl.BlockSpec((1, 1, d), row)])
    return pl.pallas_call(
        functools.partial(_outproj_kernel, n_in=n_in),
        grid=(t // tm,),
        in_specs=in_specs,
        out_specs=[pl.BlockSpec((tm, d), lambda i: (i, 0)), pl.BlockSpec((tm, d), lambda i: (i, 0))],
        out_shape=[jax.ShapeDtypeStruct((t, d), F32), jax.ShapeDtypeStruct((t, d), BF16)],
        compiler_params=_params("arbitrary"),
        name="out_proj_residual",
    )(*a_list, *w_list, x, gate.reshape(r, 1, d), gain2.reshape(1, d),
      shift2.reshape(r, 1, d), scale2.reshape(r, 1, d))


def _softmax_pv(s_list, v_list, extra=None, fold_lanes=False):
    def lane_chunks(x):
        return [x[:, c:c + LANES] for c in range(0, x.shape[1], LANES)] if fold_lanes else [x]

    def row_reduce(blocks, op, reduce):
        acc = None
        for blk in blocks:
            chunks = lane_chunks(blk)
            part = chunks[0]
            for c in chunks[1:]:
                part = op(part, c)
            if not fold_lanes:
                part = reduce(part, axis=-1, keepdims=True)
            acc = part if acc is None else op(acc, part)
        return reduce(acc, axis=-1, keepdims=True) if fold_lanes else acc

    m = row_reduce(s_list, jnp.maximum, jnp.max)
    if extra is not None:
        m = jnp.maximum(m, extra)
    ps = []
    out = None
    for s, v in zip(s_list, v_list):
        p = jnp.exp(s - m)
        ps.append(p)
        pv = _dot(p.astype(BF16), v)
        out = pv if out is None else out + pv
    denom = row_reduce(ps, jnp.add, jnp.sum)
    if extra is not None:
        denom = denom + jnp.exp(extra - m)
    return out / denom


def _na_kernel(q_ref, k_ref, v_ref, kc_ref, vc_ref, bias_ref, qn_ref, kn_ref, o_ref,
               k_scr, v_scr, kc_scr, vc_scr, *, n_steps):
    i = pl.program_id(2)
    band = NA_BAND_ROWS * GRID_W
    start = jnp.clip(NA_Q_ROWS * i - NA_ROWS // 2, 0, NA_Q_ROWS * n_steps - NA_BAND_ROWS) * GRID_W
    start = pl.multiple_of(start, NA_ROWS // 2 * GRID_W)

    @pl.when(i == 0)
    def _():
        k_scr[...] = _head_rms(k_ref[0], kn_ref[...]).astype(BF16)
        v_scr[...] = v_ref[0].astype(BF16)
        kc_scr[...] = _head_rms(kc_ref[0], kn_ref[...]).astype(BF16)
        vc_scr[...] = vc_ref[0].astype(BF16)

    q = (_head_rms(q_ref[0], qn_ref[...]) * ATTN_SCALE).astype(BF16)
    kb = k_scr[pl.ds(start, band), :]
    vb = v_scr[pl.ds(start, band), :]
    kc = kc_scr[...]
    vc = vc_scr[...]
    outs = []
    for h in range(2):
        sl = slice(h * HEAD_DIM, (h + 1) * HEAD_DIM)
        s_loc = _dot_nt(q[:, sl], kb[:, sl]) + bias_ref[0, h]
        s_ctx = _dot_nt(q[:, sl], kc[:, sl])
        outs.append(_softmax_pv([s_loc, s_ctx], [vb[:, sl], vc[:, sl]]))
    o_ref[0] = jnp.concatenate(outs, axis=-1)


def _na_bias(rpb, rows):
    h = rpb.shape[0]
    n_steps = rows // NA_Q_ROWS
    col = np.arange(GRID_W)
    c0 = np.clip(col - NA_COLS // 2, 0, GRID_W - NA_COLS)
    dc = col[None, :] - col[:, None] + (NA_COLS - 1)
    ok_c = (col[None, :] >= c0[:, None]) & (col[None, :] < c0[:, None] + NA_COLS)
    pick = ((np.arange(2 * NA_COLS - 1)[:, None, None] == dc[None]) & ok_c[None]).astype(np.float32)
    blocks = jnp.einsum('hrd,dqk->hrqk', rpb, pick, precision=lax.Precision.HIGHEST)
    blocks = jnp.where(ok_c, blocks, NEG_INF)
    masked = jnp.full((h, GRID_W, GRID_W), NEG_INF, F32)
    classes = []
    for step in (0, 1, n_steps - 1):
        band0 = min(max(NA_Q_ROWS * step - NA_ROWS // 2, 0), rows - NA_BAND_ROWS)
        q_rows = []
        for rq in range(NA_Q_ROWS):
            r = NA_Q_ROWS * step + rq
            r0 = min(max(r - NA_ROWS // 2, 0), rows - NA_ROWS)
            row = []
            for rk in range(band0, band0 + NA_BAND_ROWS):
                row.append(blocks[:, rk - r + NA_ROWS - 1] if r0 <= rk < r0 + NA_ROWS else masked)
            q_rows.append(jnp.concatenate(row, axis=-1))
        classes.append(jnp.concatenate(q_rows, axis=-2))
    return jnp.stack(classes)


def _na_attention(p_lat, p_ctx, bias, qn, kn, n_pairs):
    b, s, _ = p_lat.shape
    l = p_ctx.shape[1]
    rows = s // GRID_W
    n_steps = rows // NA_Q_ROWS
    qt = NA_Q_ROWS * GRID_W

    def cls(i):
        return jnp.where(i == 0, 0, jnp.where(i == n_steps - 1, 2, 1))

    return pl.pallas_call(
        functools.partial(_na_kernel, n_steps=n_steps),
        grid=(n_pairs, b, n_steps),
        in_specs=[pl.BlockSpec((1, qt, LANES), lambda hp, bb, i: (bb, i, hp)),
                  pl.BlockSpec((1, s, LANES), lambda hp, bb, i: (bb, 0, n_pairs + hp)),
                  pl.BlockSpec((1, s, LANES), lambda hp, bb, i: (bb, 0, 2 * n_pairs + hp)),
                  pl.BlockSpec((1, l, LANES), lambda hp, bb, i: (bb, 0, n_pairs + hp)),
                  pl.BlockSpec((1, l, LANES), lambda hp, bb, i: (bb, 0, 2 * n_pairs + hp)),
                  pl.BlockSpec((1, 2, qt, NA_BAND_ROWS * GRID_W), lambda hp, bb, i: (cls(i), hp, 0, 0)),
                  pl.BlockSpec((1, LANES), lambda hp, bb, i: (0, 0)),
                  pl.BlockSpec((1, LANES), lambda hp, bb, i: (0, 0))],
        out_specs=pl.BlockSpec((1, qt, LANES), lambda hp, bb, i: (bb, i, hp)),
        out_shape=jax.ShapeDtypeStruct((b, s, n_pairs * LANES), F32),
        scratch_shapes=[pltpu.VMEM((s, LANES), BF16), pltpu.VMEM((s, LANES), BF16),
                        pltpu.VMEM((l, LANES), BF16), pltpu.VMEM((l, LANES), BF16)],
        compiler_params=_params("arbitrary", "arbitrary", "arbitrary"),
        name="neighbourhood_attention",
    )(p_lat, p_lat, p_lat, p_ctx, p_ctx, bias, qn, kn)


def _ctx_attn_kernel(q_ref, k_ref, v_ref, qn_ref, kn_ref, o_ref):
    q = (_head_rms(q_ref[0], qn_ref[...]) * ATTN_SCALE).astype(BF16)
    k = _head_rms(k_ref[0], kn_ref[...]).astype(BF16)
    v = v_ref[0].astype(BF16)
    outs = []
    for h in range(2):
        sl = slice(h * HEAD_DIM, (h + 1) * HEAD_DIM)
        outs.append(_softmax_pv([_dot_nt(q[:, sl], k[:, sl])], [v[:, sl]]))
    o_ref[0] = jnp.concatenate(outs, axis=-1)


def _ctx_attention(p_ctx, qn, kn, n_pairs):
    b, l, _ = p_ctx.shape
    return pl.pallas_call(
        _ctx_attn_kernel,
        grid=(n_pairs, b),
        in_specs=[pl.BlockSpec((1, l, LANES), lambda hp, bb: (bb, 0, hp)),
                  pl.BlockSpec((1, l, LANES), lambda hp, bb: (bb, 0, n_pairs + hp)),
                  pl.BlockSpec((1, l, LANES), lambda hp, bb: (bb, 0, 2 * n_pairs + hp)),
                  pl.BlockSpec((1, LANES), lambda hp, bb: (0, 0)),
                  pl.BlockSpec((1, LANES), lambda hp, bb: (0, 0))],
        out_specs=pl.BlockSpec((1, l, LANES), lambda hp, bb: (bb, 0, hp)),
        out_shape=jax.ShapeDtypeStruct((b, l, n_pairs * LANES), F32),
        compiler_params=_params("arbitrary", "arbitrary"),
        name="context_attention",
    )(p_ctx, p_ctx, p_ctx, qn, kn)


def _ret_kernel(q_ref, k_ref, v_ref, g_ref, cos_ref, sin_ref, lg_ref, s0_ref, y_ref, st_ref, sf_scr,
                *, n_chunks, use_rope):
    c = RET_CHUNK
    hd = HEAD_DIM
    lg = -jnp.exp(lg_ref[0])
    lgf, lgb = lg[0:1, :], lg[1:2, :]
    ii = lax.broadcasted_iota(jnp.int32, (c, LANES), 0).astype(F32)
    dq_f = jnp.exp(lgf * (ii + 1.0))
    dk_f = jnp.exp(lgf * (c - 1.0 - ii))
    dq_b = jnp.exp(lgb * (c - ii))
    dk_b = jnp.exp(lgb * ii)
    dc_f = jnp.exp(lgf * float(c))
    dc_b = jnp.exp(lgb * float(c))
    diff = (lax.broadcasted_iota(jnp.int32, (c, c), 0) - lax.broadcasted_iota(jnp.int32, (c, c), 1)).astype(F32)
    intra = []
    chunk_f = []
    chunk_b = []
    for h in range(2):
        lf = lgf[:, h * hd:h * hd + 1]
        lb = lgb[:, h * hd:h * hd + 1]
        intra.append(jnp.where(diff >= 0, jnp.exp(lf * jnp.maximum(diff, 0.0)),
                               jnp.exp(lb * jnp.maximum(-diff, 0.0))))
        chunk_f.append(dc_f[:, h * hd:h * hd + 1])
        chunk_b.append(dc_b[:, h * hd:h * hd + 1])

    def load(n):
        r = pl.multiple_of(n * c, c)
        q = q_ref[0, pl.ds(r, c), :]
        k = k_ref[0, pl.ds(r, c), :]
        v = v_ref[0, pl.ds(r, c), :]
        if use_rope:
            cs = cos_ref[pl.ds(r, c), :]
            sn = sin_ref[pl.ds(r, c), :]
            q = _rope(q, cs, sn)
            k = _rope(k, cs, sn)
        return r, q * ATTN_SCALE, k, v.astype(BF16)

    def fwd(n, carry):
        _, _, k, v = load(n)
        kd = (k * dk_f).astype(BF16)
        new = []
        for h in range(2):
            sl = slice(h * hd, (h + 1) * hd)
            sf_scr[n, h] = carry[h]
            new.append(carry[h] * chunk_f[h] + _dot_tn(kd[:, sl], v[:, sl]))
        return tuple(new)

    sf = lax.fori_loop(0, n_chunks, fwd, (s0_ref[0, 0, 0], s0_ref[0, 0, 1]), unroll=2)
    st_ref[0, 0, 0] = sf[0]
    st_ref[0, 0, 1] = sf[1]

    def bwd(jj, carry):
        n = n_chunks - 1 - jj
        r, q, k, v = load(n)
        qb = q.astype(BF16)
        kb = k.astype(BF16)
        qf = (q * dq_f).astype(BF16)
        qr = (q * dq_b).astype(BF16)
        kd = (k * dk_b).astype(BF16)
        outs = []
        new = []
        for h in range(2):
            sl = slice(h * hd, (h + 1) * hd)
            a = (_dot_nt(qb[:, sl], kb[:, sl]) * intra[h]).astype(BF16)
            o = (_dot(a, v[:, sl]) + _dot(qf[:, sl], sf_scr[n, h].astype(BF16))
                 + _dot(qr[:, sl], carry[h].astype(BF16)))
            oc = o - jnp.mean(o, axis=-1, keepdims=True)
            outs.append(oc * lax.rsqrt(jnp.mean(oc * oc, axis=-1, keepdims=True) + GN_EPS))
            new.append(carry[h] * chunk_b[h] + _dot_tn(kd[:, sl], v[:, sl]))
        g = g_ref[0, pl.ds(r, c), :]
        y_ref[0, pl.ds(r, c), :] = jnp.concatenate(outs, axis=-1) * (g * jax.nn.sigmoid(g))
        return tuple(new)

    sb = lax.fori_loop(0, n_chunks, bwd, (s0_ref[0, 0, 2], s0_ref[0, 0, 3]), unroll=2)
    st_ref[0, 0, 2] = sb[0]
    st_ref[0, 0, 3] = sb[1]


def _retention(p, cos2, sin2, lg, s0, col0, n_pairs, use_rope):
    b, t, _ = p.shape
    n_chunks = t // RET_CHUNK
    tab = lambda hp, bb: (0, 0)
    return pl.pallas_call(
        functools.partial(_ret_kernel, n_chunks=n_chunks, use_rope=use_rope),
        grid=(n_pairs, b),
        in_specs=[pl.BlockSpec((1, t, LANES), lambda hp, bb: (bb, 0, col0 + hp)),
                  pl.BlockSpec((1, t, LANES), lambda hp, bb: (bb, 0, col0 + n_pairs + hp)),
                  pl.BlockSpec((1, t, LANES), lambda hp, bb: (bb, 0, col0 + 2 * n_pairs + hp)),
                  pl.BlockSpec((1, t, LANES), lambda hp, bb: (bb, 0, col0 + 3 * n_pairs + hp)),
                  pl.BlockSpec(cos2.shape, tab),
                  pl.BlockSpec(sin2.shape, tab),
                  pl.BlockSpec((1, 2, LANES), lambda hp, bb: (hp, 0, 0)),
                  pl.BlockSpec((1, 1, 4, HEAD_DIM, HEAD_DIM), lambda hp, bb: (bb, hp, 0, 0, 0))],
        out_specs=[pl.BlockSpec((1, t, LANES), lambda hp, bb: (bb, 0, hp)),
                   pl.BlockSpec((1, 1, 4, HEAD_DIM, HEAD_DIM), lambda hp, bb: (bb, hp, 0, 0, 0))],
        out_shape=[jax.ShapeDtypeStruct((b, t, n_pairs * LANES), F32),
                   jax.ShapeDtypeStruct((b, n_pairs, 4, HEAD_DIM, HEAD_DIM), F32)],
        scratch_shapes=[pltpu.VMEM((n_chunks, 2, HEAD_DIM, HEAD_DIM), F32)],
        compiler_params=_params("arbitrary", "arbitrary"),
        name="retention",
    )(p, p, p, p, cos2, sin2, lg, s0)


def _swa_kernel(q_ref, k_ref, v_ref, kc_ref, vc_ref, cos_ref, sin_ref, qn_ref, kn_ref, sink_ref, far_ref,
                o_ref, *, seq):
    qt = SWA_Q_TILE
    wk = qt + 2 * SWA_WINDOW
    hd = HEAD_DIM
    n = pl.program_id(2)
    q0 = pl.multiple_of(n * qt, qt)
    ws = pl.multiple_of(jnp.clip(n * qt - SWA_WINDOW, 0, seq - wk), SWA_WINDOW)
    kw = _rope(_head_rms(k_ref[0, pl.ds(ws, wk), :], kn_ref[...]),
               cos_ref[pl.ds(ws, wk), :], sin_ref[pl.ds(ws, wk), :]).astype(BF16)
    vw = v_ref[0, pl.ds(ws, wk), :].astype(BF16)
    kc = _head_rms(kc_ref[0], kn_ref[...]).astype(BF16)
    vc = vc_ref[0].astype(BF16)
    cos_q = cos_ref[pl.ds(q0, qt), :]
    sin_q = sin_ref[pl.ds(q0, qt), :]
    qs = []
    for s in range(4):
        slab = q_ref[0, :, s * LANES:(s + 1) * LANES]
        qs.append((_rope(_head_rms(slab, qn_ref[...]), cos_q, sin_q) * ATTN_SCALE).astype(BF16))
    far = far_ref[0][None]
    for kh in range(2):
        sl = slice(kh * hd, (kh + 1) * hd)
        qstack = jnp.concatenate(
            [qs[kh * 2 + g // 2][:, (g % 2) * hd:(g % 2 + 1) * hd] for g in range(4)], axis=0)
        sink = jnp.concatenate(
            [jnp.broadcast_to(sink_ref[kh * 4 + g:kh * 4 + g + 1, 0:1], (qt, 1)) for g in range(4)], axis=0)
        s_loc = (_dot_nt(qstack, kw[:, sl]).reshape(4, qt, wk) + far).reshape(4 * qt, wk)
        s_ctx = _dot_nt(qstack, kc[:, sl])
        o = _softmax_pv([s_loc, s_ctx], [vw[:, sl], vc[:, sl]], extra=sink, fold_lanes=True)
        for pair in range(2):
            col = (kh * 2 + pair) * LANES
            o_ref[0, :, col:col + LANES] = jnp.concatenate(
                [o[(2 * pair) * qt:(2 * pair + 1) * qt], o[(2 * pair + 1) * qt:(2 * pair + 2) * qt]], axis=-1)


def _swa_attention(p_lat, p_ctx, cos2, sin2, qn, kn, sink_rows, n_q_heads, n_kv_heads):
    b, s, _ = p_lat.shape
    l = p_ctx.shape[1]
    kv_pairs = n_kv_heads // 2
    q_blocks = n_q_heads * HEAD_DIM // LANES
    q_per_pair = q_blocks // kv_pairs
    qw = q_per_pair * LANES
    tab = lambda kp, bb, n: (0, 0)
    n_steps = s // SWA_Q_TILE
    wk = SWA_Q_TILE + 2 * SWA_WINDOW
    far = []
    for step in (0, 1, n_steps - 1):
        q0 = step * SWA_Q_TILE
        ws = min(max(q0 - SWA_WINDOW, 0), s - wk)
        dist = np.abs((q0 + np.arange(SWA_Q_TILE))[:, None] - (ws + np.arange(wk))[None, :])
        far.append(np.where(dist <= SWA_WINDOW, 0.0, NEG_INF))
    far = jnp.asarray(np.stack(far), F32)

    def cls(n):
        return jnp.where(n == 0, 0, jnp.where(n == n_steps - 1, 2, 1))

    return pl.pallas_call(
        functools.partial(_swa_kernel, seq=s),
        grid=(kv_pairs, b, n_steps),
        in_specs=[pl.BlockSpec((1, SWA_Q_TILE, qw), lambda kp, bb, n: (bb, n, kp)),
                  pl.BlockSpec((1, s, LANES), lambda kp, bb, n: (bb, 0, q_blocks + kp)),
                  pl.BlockSpec((1, s, LANES), lambda kp, bb, n: (bb, 0, q_blocks + kv_pairs + kp)),
                  pl.BlockSpec((1, l, LANES), lambda kp, bb, n: (bb, 0, q_blocks + kp)),
                  pl.BlockSpec((1, l, LANES), lambda kp, bb, n: (bb, 0, q_blocks + kv_pairs + kp)),
                  pl.BlockSpec(cos2.shape, tab),
                  pl.BlockSpec(sin2.shape, tab),
                  pl.BlockSpec((1, LANES), tab),
                  pl.BlockSpec((1, LANES), tab),
                  pl.BlockSpec((8, LANES), lambda kp, bb, n: (kp, 0)),
                  pl.BlockSpec((1, SWA_Q_TILE, wk), lambda kp, bb, n: (cls(n), 0, 0))],
        out_specs=pl.BlockSpec((1, SWA_Q_TILE, qw), lambda kp, bb, n: (bb, n, kp)),
        out_shape=jax.ShapeDtypeStruct((b, s, n_q_heads * HEAD_DIM), F32),
        compiler_params=_params("arbitrary", "arbitrary", "arbitrary"),
        name="windowed_gqa",
    )(p_lat, p_lat, p_lat, p_ctx, p_ctx, cos2, sin2, qn, kn, sink_rows, far)


def _oddeven_merge_sort(n):
    pairs = []

    def merge(lo, size, r):
        step = r * 2
        if step < size:
            merge(lo, size, step)
            merge(lo + r, size, step)
            pairs.extend((i, i + r) for i in range(lo + r, lo + size - r, step))
        else:
            pairs.append((lo, lo + r))

    def sort(lo, size):
        if size > 1:
            sort(lo, size // 2)
            sort(lo + size // 2, size // 2)
            merge(lo, size, 1)

    sort(0, n)
    return pairs


def _top_rows(x, k, scr):
    tiles = [x[8 * i:8 * i + 8, :] for i in range(x.shape[0] // 8)]
    for i, j in _oddeven_merge_sort(len(tiles)):
        tiles[i], tiles[j] = jnp.maximum(tiles[i], tiles[j]), jnp.minimum(tiles[i], tiles[j])
    for r in range(k):
        m = jnp.max(tiles[0], axis=0, keepdims=True)
        scr[r:r + 1, :] = m
        need = k - r - 1
        if need > 0:
            hit = tiles[0] == m
            for d in range(min(need, len(tiles) - 1)):
                tiles[d] = jnp.where(hit, tiles[d + 1], tiles[d])
            if need >= len(tiles):
                tiles[-1] = jnp.where(hit, NEG_INF, tiles[-1])


def _bf16_pair(x):
    bits = lax.bitcast_convert_type(x.astype(BF16).astype(F32), jnp.uint32)
    return bits | (bits >> 16)


def _count_above(sorted_scr, y, strict):
    row = lambda i: sorted_scr[i:i + 1, :]
    above = (lambda r: r > y) if strict else (lambda r: r >= y)
    c8 = above(row(7))
    c4 = above(jnp.where(c8, row(11), row(3)))
    c2 = above(jnp.where(c8, jnp.where(c4, row(13), row(9)), jnp.where(c4, row(5), row(1))))
    hi = jnp.where(c4, jnp.where(c2, row(14), row(12)), jnp.where(c2, row(10), row(8)))
    lo = jnp.where(c4, jnp.where(c2, row(6), row(4)), jnp.where(c2, row(2), row(0)))
    c1 = above(jnp.where(c8, hi, lo))
    count = (jnp.where(c8, 8.0, 0.0) + jnp.where(c4, 4.0, 0.0)) + (jnp.where(c2, 2.0, 0.0) + jnp.where(c1, 1.0, 0.0))
    return count + jnp.where(above(row(15)), 1.0, 0.0)


def _peer_scores_kernel(h_ref, wq_ref, keys_ref, ra_ref, p1_ref, gb_ref, p2_ref,
                        q_scr, s_scr, a_scr, b_scr, c_scr, v_scr, *, n_heads):
    k = PEER_TOPK
    n_blocks = h_ref.shape[0] // LANES

    q = _dot(h_ref[...], wq_ref[...])
    for hp in range(2 * n_heads):
        q_scr[hp] = q[:, hp * PEER_KEYS:(hp + 1) * PEER_KEYS].astype(BF16)

    def lane_block(lb, h, slot):
        tops_a, tops_b, cand, tops_c = a_scr.at[slot], b_scr.at[slot], c_scr.at[slot], v_scr.at[slot]
        s1 = s_scr[0, lb]
        s2 = s_scr[1, lb]
        _top_rows(s1, k + 1, tops_a)
        _top_rows(s2, k + 1, tops_b)
        cand[0:16, :] = tops_a[0:1, :] + tops_b[0:16, :]
        for i in range(1, 8):
            cand[8 + 8 * i:16 + 8 * i, :] = tops_a[i:i + 1, :] + tops_b[0:8, :]
        cand[72:80, :] = tops_a[8:16, :] + tops_b[0:1, :]
        cand[80:81, :] = tops_a[0:1, :] + tops_b[16:17, :]
        cand[81:82, :] = tops_a[16:17, :] + tops_b[0:1, :]
        cand[82:PEER_KEYS, :] = jnp.full((PEER_KEYS - 82, LANES), NEG_INF, F32)
        _top_rows(cand[...], k + 1, tops_c)
        thr = 0.5 * (tops_c[k - 1:k, :] + tops_c[k:k + 1, :])
        z = jnp.sum(jnp.exp(tops_c[0:k, :] - tops_c[0:1, :]), axis=0, keepdims=True)
        ra_ref[lb, h] = _bf16_pair(_count_above(tops_a, s1, strict=True))
        p1_ref[lb, h] = _bf16_pair(jnp.exp(s1 - tops_a[0:1, :]) * (0.5 / z))
        gb_ref[lb, h] = _count_above(tops_a, thr - s2, strict=False).astype(BF16)
        p2_ref[lb, h] = jnp.exp(s2 - tops_b[0:1, :]).astype(BF16)

    def lane_pair(i, h):
        lane_block(2 * i, h, 0)
        lane_block(2 * i + 1, h, 1)
        return h

    def head(h, carry):
        for p in range(2):
            st = _dot_nt(keys_ref[2 * h + p], q_scr[2 * h + p])
            for lb in range(n_blocks):
                s_scr[p, lb] = st[:, lb * LANES:(lb + 1) * LANES]
        lax.fori_loop(0, n_blocks // 2, lane_pair, h)
        return carry

    lax.fori_loop(0, n_heads, head, 0)


def _peer_scores(h2, wq, keys, n_heads):
    t, d = h2.shape
    tt = PEER_TOKEN_TILE
    nb = tt // LANES
    tab_shape = (t // LANES, n_heads, PEER_KEYS, LANES)
    tab_spec = pl.BlockSpec((nb, n_heads, PEER_KEYS, LANES), lambda i: (i, 0, 0, 0))
    return pl.pallas_call(
        functools.partial(_peer_scores_kernel, n_heads=n_heads),
        grid=(t // tt,),
        in_specs=[pl.BlockSpec((tt, d), lambda i: (i, 0)),
                  pl.BlockSpec(wq.shape, lambda i: (0, 0)),
                  pl.BlockSpec(keys.shape, lambda i: (0, 0, 0))],
        out_specs=[tab_spec] * 4,
        out_shape=[jax.ShapeDtypeStruct(tab_shape, dt) for dt in (jnp.uint32, jnp.uint32, BF16, BF16)],
        scratch_shapes=[pltpu.VMEM((2 * n_heads, tt, PEER_KEYS), BF16),
                        pltpu.VMEM((2, nb, PEER_KEYS, LANES), F32),
                        pltpu.VMEM((2, 24, LANES), F32), pltpu.VMEM((2, 24, LANES), F32),
                        pltpu.VMEM((2, PEER_KEYS, LANES), F32), pltpu.VMEM((2, 24, LANES), F32)],
        compiler_params=_params("arbitrary"),
        name="peer_scores",
    )(h2, wq, keys)


def _peer_mix_kernel(h_ref, u_ref, vt_ref, ra_ref, p1_ref, gb_ref, p2_ref, x_ref, g_ref, o_ref,
                     acc_scr, act_scr, a_scr, *, n_heads, n_tiles):
    s = pl.program_id(1)
    tt = h_ref.shape[0]
    rows_per_tile = PEER_EXPERT_TILE // PEER_KEYS
    tile = (PEER_KEYS, LANES)

    def readout():
        acc_scr[...] += _dot(vt_ref[0], a_scr[...])

    def row_tile(ref, lb, h, aa):
        words = jnp.broadcast_to(ref[lb, h, aa:aa + 1, :], (8, LANES))
        packed = pltpu.bitcast(words, BF16)
        return jnp.broadcast_to(packed[None], (PEER_KEYS // 16, 16, LANES)).reshape(tile)

    def gates():
        for aa in range(rows_per_tile):
            rs = slice(aa * PEER_KEYS, (aa + 1) * PEER_KEYS)
            for lb in range(tt // LANES):
                ls = slice(lb * LANES, (lb + 1) * LANES)
                w = None
                for h in range(n_heads):
                    rank = row_tile(ra_ref, lb, h, aa)
                    p1 = row_tile(p1_ref, lb, h, aa)
                    term = jnp.where(rank < gb_ref[lb, h], p2_ref[lb, h], jnp.zeros(tile, BF16)) * p1
                    w = term if w is None else w + term
                act = act_scr[rs, ls]
                gelu2 = act * (1.0 + lax.erf(act * INV_SQRT2))
                a_scr[rs, ls] = w * gelu2.astype(BF16)

    def experts():
        act_scr[...] = _dot_nt(u_ref[0].astype(BF16), h_ref[...])

    @pl.when(s == 0)
    def _():
        acc_scr[...] = jnp.zeros_like(acc_scr)

    i = pl.program_id(0)

    @pl.when(s < n_tiles)
    def _():
        experts()

    @pl.when(i >= 0)
    def _():
        gates()

    @pl.when(s + i >= 0)
    def _():
        readout()

    @pl.when(s == n_tiles - 1)
    def _():
        o_ref[...] = x_ref[...] + g_ref[0] * acc_scr[...].T


def _peer_mix(h2, u_all, layer, vt, tables, x, gate, seg, n_heads):
    t, d = h2.shape
    n_exp = u_all.shape[1]
    tt = min(PEER_MIX_TOKEN_TILE, seg)
    te = PEER_EXPERT_TILE
    n_tiles = n_exp // te
    per_seg = seg // tt
    r = gate.shape[0]
    nb = tt // LANES
    tab_spec = pl.BlockSpec((nb, n_heads, PEER_KEYS, LANES), lambda i, s: (i, 0, 0, 0))
    row_spec = pl.BlockSpec((nb, n_heads, te // PEER_KEYS, LANES), lambda i, s: (i, 0, s, 0))
    return pl.pallas_call(
        functools.partial(_peer_mix_kernel, n_heads=n_heads, n_tiles=n_tiles),
        grid=(t // tt, n_tiles),
        in_specs=[pl.BlockSpec((tt, d), lambda i, s: (i, 0)),
                  pl.BlockSpec((1, te, d), lambda i, s: (layer, s, 0)),
                  pl.BlockSpec((1, d, te), lambda i, s: (s, 0, 0)),
                  row_spec, row_spec, tab_spec, tab_spec,
                  pl.BlockSpec((tt, d), lambda i, s: (i, 0)),
                  pl.BlockSpec((1, 1, d), lambda i, s: (i // per_seg, 0, 0))],
        out_specs=pl.BlockSpec((tt, d), lambda i, s: (i, 0)),
        out_shape=jax.ShapeDtypeStruct((t, d), F32),
        scratch_shapes=[pltpu.VMEM((d, tt), F32), pltpu.VMEM((te, tt), F32), pltpu.VMEM((te, tt), BF16)],
        compiler_params=_params("arbitrary", "arbitrary"),
        name="peer_mix",
    )(h2, u_all, vt, *tables, x, gate.reshape(r, 1, d))


def _peer(h2, x, gate, seg, wq, keys, u_all, layer, vt):
    n_heads = keys.shape[0] // 2
    tables = _peer_scores(h2, wq, keys, n_heads)
    return _peer_mix(h2, u_all, layer, vt, tables, x, gate, seg, n_heads)


def _rope_tables(n_tokens):
    t = jnp.arange(n_tokens)
    row = (t // GRID_W).astype(F32)
    col = (t % GRID_W).astype(F32)
    n_freq = HEAD_DIM // 4
    inv_freq = jnp.power(ROPE_BASE, -jnp.arange(n_freq, dtype=F32) / n_freq)
    ang = jnp.concatenate([row[:, None] * inv_freq, col[:, None] * inv_freq], axis=-1)
    cos, sin = jnp.cos(ang), jnp.sin(ang)
    cos2 = jnp.tile(jnp.concatenate([cos, cos], axis=-1), (1, 2))
    sin2 = jnp.tile(jnp.concatenate([-sin, sin], axis=-1), (1, 2))
    return cos2, sin2


def _pair_lanes(v):
    return jnp.repeat(v, HEAD_DIM, axis=-1).reshape(*v.shape[:-1], v.shape[-1] // 2, LANES)


def kernel(x, c, ctx, c_ctx, mod_w, mod_b, norm1_g, norm2_g, ab_w_in, ab_w_out, na_q_norm, na_k_norm,
           na_rpb, ret_log_decay, swa_w_in, swa_w_out, swa_q_norm, swa_k_norm, swa_sink,
           peer_w_q, peer_sub_keys, peer_u, peer_v):
    b, s, d = x.shape
    l = ctx.shape[1]
    depth = mod_w.shape[0]
    assert depth == 2 and b + 1 <= 8
    assert s % (NA_Q_ROWS * GRID_W) == 0 and s // GRID_W >= NA_BAND_ROWS
    assert s % TOKEN_TILE == 0 and (b * l) % TOKEN_TILE == 0 and l % RET_CHUNK == 0

    cond = jnp.concatenate([c, c_ctx[None, :], jnp.zeros((8 - b - 1, d), F32)], axis=0)
    mods = _adaln(cond, mod_w, mod_b).reshape(depth, 8, 6, d)
    lat = lambda layer, which: mods[layer, :b, which]
    cx = lambda layer, which: mods[layer, b:b + 1, which]

    cos2, sin2 = _rope_tables(s)
    x_lat = x.reshape(b * s, d)
    x_ctx = ctx.reshape(b * l, d)
    tile2 = lambda g: jnp.tile(g, 2).reshape(1, LANES)

    def peer_weights(layer):
        n_heads = peer_sub_keys.shape[1]
        keys = peer_sub_keys[layer].reshape(2 * n_heads, PEER_KEYS, -1).astype(BF16)
        vt = peer_v[layer].reshape(-1, PEER_EXPERT_TILE, d).transpose(0, 2, 1).astype(BF16)
        return peer_w_q[layer].astype(BF16), keys, peer_u, layer, vt

    n_na = na_rpb.shape[1]
    n_ret = ret_log_decay.shape[2]
    na_pairs, ret_pairs = n_na // 2, n_ret // 2
    w_in = ab_w_in[0].astype(BF16)
    w_out = ab_w_out[0].astype(BF16)
    wa = n_na * HEAD_DIM
    p_lat = _modmm(x_lat, norm1_g[0], lat(0, 0), lat(0, 1), w_in, s).reshape(b, s, -1)
    p_ctx = _modmm(x_ctx, norm1_g[0], cx(0, 0), cx(0, 1), w_in, b * l).reshape(b, l, -1)

    qn, kn = tile2(na_q_norm[0]), tile2(na_k_norm[0])
    oa_lat = _na_attention(p_lat, p_ctx, _na_bias(na_rpb[0], s // GRID_W), qn, kn, na_pairs)
    oa_ctx = _ctx_attention(p_ctx, qn, kn, na_pairs)

    lg = _pair_lanes(ret_log_decay[0]).transpose(1, 0, 2)
    ret_col0 = 3 * wa // LANES
    zeros_state = jnp.zeros((b, ret_pairs, 4, HEAD_DIM, HEAD_DIM), F32)
    ones_tab, zeros_tab = jnp.ones((l, LANES), F32), jnp.zeros((l, LANES), F32)
    ob_ctx, st_ctx = _retention(p_ctx, ones_tab, zeros_tab, lg, zeros_state, ret_col0, ret_pairs, False)
    ob_lat, _ = _retention(p_lat, cos2, sin2, lg, st_ctx, ret_col0, ret_pairs, True)

    w_list = [w_out[:wa], w_out[wa:]]
    x_lat, h_lat = _outproj([oa_lat.reshape(b * s, -1), ob_lat.reshape(b * s, -1)], w_list, x_lat,
                            lat(0, 2), norm2_g[0], lat(0, 3), lat(0, 4), s)
    x_ctx, h_ctx = _outproj([oa_ctx.reshape(b * l, -1), ob_ctx.reshape(b * l, -1)], w_list, x_ctx,
                            cx(0, 2), norm2_g[0], cx(0, 3), cx(0, 4), b * l)
    pw = peer_weights(0)
    x_lat = _peer(h_lat, x_lat, lat(0, 5), s, *pw)
    x_ctx = _peer(h_ctx, x_ctx, cx(0, 5), b * l, *pw)

    n_q = swa_sink.shape[1]
    n_kv = (swa_w_in.shape[2] // HEAD_DIM - n_q) // 2
    w_in = swa_w_in[0].astype(BF16)
    p_lat = _modmm(x_lat, norm1_g[1], lat(1, 0), lat(1, 1), w_in, s).reshape(b, s, -1)
    p_ctx = _modmm(x_ctx, norm1_g[1], cx(1, 0), cx(1, 1), w_in, b * l).reshape(b, l, -1)
    sink_rows = jnp.broadcast_to(swa_sink[0][:, None], (n_q, LANES))
    o_lat = _swa_attention(p_lat, p_ctx, cos2, sin2, tile2(swa_q_norm[0]), tile2(swa_k_norm[0]),
                           sink_rows, n_q, n_kv)
    x_lat, h_lat = _outproj([o_lat.reshape(b * s, -1)], [swa_w_out[0].astype(BF16)], x_lat,
                            lat(1, 2), norm2_g[1], lat(1, 3), lat(1, 4), s)
    x_lat = _peer(h_lat, x_lat, lat(1, 5), s, *peer_weights(1))
    return x_lat.reshape(b, s, d)
```

```python
import functools

import numpy as np
import jax
import jax.numpy as jnp
from jax import lax
from jax.experimental import pallas as pl
from jax.experimental.pallas import tpu as pltpu

F32 = jnp.float32
BF16 = jnp.bfloat16

HEAD_DIM = 64
GRID_W = 64
NA_ROWS = 8
NA_COLS = 16
SWA_WINDOW = 128
PEER_TOPK = 16
PEER_KEYS = 128
ROPE_BASE = 10000.0
NORM_EPS = 1e-6
GN_EPS = 1e-5
NEG_INF = -1e30
ATTN_SCALE = HEAD_DIM ** -0.5
INV_SQRT2 = 0.7071067811865476

LANES = 128
VMEM_LIMIT = 56 * 1024 * 1024

TOKEN_TILE = 512
PROJ_TOKEN_TILE = 1024
PROJ_N_TILE = 1792
RET_CHUNK = 256
NA_Q_ROWS = 8
NA_BAND_ROWS = 16
SWA_Q_TILE = 256
PEER_TOKEN_TILE = 512
PEER_MIX_TOKEN_TILE = 1024
PEER_EXPERT_TILE = 1024


def _dot(a, b):
    return jnp.dot(a, b, preferred_element_type=F32)


def _dot_nt(a, b):
    return lax.dot_general(a, b, (((1,), (1,)), ((), ())), preferred_element_type=F32)


def _dot_tn(a, b):
    return lax.dot_general(a, b, (((0,), (0,)), ((), ())), preferred_element_type=F32)


def _params(*sem):
    return pltpu.CompilerParams(dimension_semantics=sem, vmem_limit_bytes=VMEM_LIMIT)


def _rms_rows(x, gain):
    ms = jnp.mean(x * x, axis=-1, keepdims=True)
    return x * lax.rsqrt(ms + NORM_EPS) * gain


def _modulate(x, gain, shift, scale):
    return _rms_rows(x, gain) * (1.0 + scale) + shift


def _head_rms(x, gain):
    lane = lax.broadcasted_iota(jnp.int32, x.shape, 1)
    lo = lane < HEAD_DIM
    ss = x * x
    s_lo = jnp.sum(jnp.where(lo, ss, 0.0), axis=-1, keepdims=True)
    s_hi = jnp.sum(jnp.where(lo, 0.0, ss), axis=-1, keepdims=True)
    ms = jnp.where(lo, s_lo, s_hi) * (1.0 / HEAD_DIM)
    return x * lax.rsqrt(ms + NORM_EPS) * gain


def _rope(x, cos2, sin2):
    lane = lax.broadcasted_iota(jnp.int32, x.shape, 1)
    first_half = (lane & (HEAD_DIM // 2)) == 0
    swapped = jnp.where(first_half, pltpu.roll(x, LANES - HEAD_DIM // 2, axis=1),
                        pltpu.roll(x, HEAD_DIM // 2, axis=1))
    return x * cos2 + swapped * sin2


def _adaln_kernel(c_ref, w_ref, b_ref, o_ref):
    c = c_ref[...]
    s = c * jax.nn.sigmoid(c)
    w = w_ref[0]
    s_hi = s.astype(BF16)
    s_lo = (s - s_hi.astype(F32)).astype(BF16)
    w_hi = w.astype(BF16)
    w_lo = (w - w_hi.astype(F32)).astype(BF16)
    acc = _dot(s_hi, w_hi) + _dot(s_lo, w_hi) + _dot(s_hi, w_lo)
    o_ref[0] = acc + b_ref[0]


def _adaln(cond, mod_w, mod_b):
    depth, d, n = mod_w.shape
    tn = n // 4
    return pl.pallas_call(
        _adaln_kernel,
        grid=(depth, n // tn),
        in_specs=[pl.BlockSpec((8, d), lambda l, j: (0, 0)),
                  pl.BlockSpec((1, d, tn), lambda l, j: (l, 0, j)),
                  pl.BlockSpec((1, 1, tn), lambda l, j: (l, 0, j))],
        out_specs=pl.BlockSpec((1, 8, tn), lambda l, j: (l, 0, j)),
        out_shape=jax.ShapeDtypeStruct((depth, 8, n), F32),
        compiler_params=_params("arbitrary", "arbitrary"),
        name="adaln",
    )(cond, mod_w, mod_b.reshape(depth, 1, n))


def _modmm_kernel(x_ref, g_ref, sh_ref, sc_ref, w_ref, o_ref, h_scr):
    @pl.when(pl.program_id(1) == 0)
    def _():
        h_scr[...] = _modulate(x_ref[...], g_ref[...], sh_ref[0], sc_ref[0]).astype(BF16)

    o_ref[...] = _dot(h_scr[...], w_ref[...])


def _modmm(x, gain, shift, scale, w, seg):
    t, d = x.shape
    n = w.shape[1]
    tm = min(PROJ_TOKEN_TILE, seg)
    tn = n if n <= PROJ_N_TILE else n // 2
    per_seg = seg // tm
    r = shift.shape[0]
    return pl.pallas_call(
        _modmm_kernel,
        grid=(t // tm, n // tn),
        in_specs=[pl.BlockSpec((tm, d), lambda i, j: (i, 0)),
                  pl.BlockSpec((1, d), lambda i, j: (0, 0)),
                  pl.BlockSpec((1, 1, d), lambda i, j: (i // per_seg, 0, 0)),
                  pl.BlockSpec((1, 1, d), lambda i, j: (i // per_seg, 0, 0)),
                  pl.BlockSpec((d, tn), lambda i, j: (0, j))],
        out_specs=pl.BlockSpec((tm, tn), lambda i, j: (i, j)),
        out_shape=jax.ShapeDtypeStruct((t, n), F32),
        scratch_shapes=[pltpu.VMEM((tm, d), BF16)],
        compiler_params=_params("arbitrary", "arbitrary"),
        name="modulate_matmul",
    )(x, gain.reshape(1, d), shift.reshape(r, 1, d), scale.reshape(r, 1, d), w)


def _outproj_kernel(*refs, n_in):
    a_refs = refs[:n_in]
    w_refs = refs[n_in:2 * n_in]
    x_ref, gate_ref, g2_ref, sh_ref, sc_ref, xo_ref, h_ref = refs[2 * n_in:]
    acc = None
    for a_ref, w_ref in zip(a_refs, w_refs):
        part = _dot(a_ref[...].astype(BF16), w_ref[...])
        acc = part if acc is None else acc + part
    xn = x_ref[...] + gate_ref[0] * acc
    xo_ref[...] = xn
    h_ref[...] = _modulate(xn, g2_ref[...], sh_ref[0], sc_ref[0]).astype(BF16)


def _outproj(a_list, w_list, x, gate, gain2, shift2, scale2, seg):
    t, d = x.shape
    tm = min(TOKEN_TILE, seg)
    per_seg = seg // tm
    r = gate.shape[0]
    n_in = len(a_list)
    row = lambda i: (i // per_seg, 0, 0)
    in_specs = ([pl.BlockSpec((tm, a.shape[1]), lambda i: (i, 0)) for a in a_list]
                + [pl.BlockSpec(w.shape, lambda i: (0, 0)) for w in w_list]
                + [pl.BlockSpec((tm, d), lambda i: (i, 0)),
                   pl.BlockSpec((1, 1, d), row),
                   pl.BlockSpec((1, d), lambda i: (0, 0)),
                   pl.BlockSpec((1, 1, d), row),
                   pl.BlockSpec((1, 1, d), row)])
    return pl.pallas_call(
        functools.partial(_outproj_kernel, n_in=n_in),
        grid=(t // tm,),
        in_specs=in_specs,
        out_specs=[pl.BlockSpec((tm, d), lambda i: (i, 0)), pl.BlockSpec((tm, d), lambda i: (i, 0))],
        out_shape=[jax.ShapeDtypeStruct((t, d), F32), jax.ShapeDtypeStruct((t, d), BF16)],
        compiler_params=_params("arbitrary"),
        name="out_proj_residual",
    )(*a_list, *w_list, x, gate.reshape(r, 1, d), gain2.reshape(1, d),
      shift2.reshape(r, 1, d), scale2.reshape(r, 1, d))


def _softmax_pv(s_list, v_list, extra=None, fold_lanes=False):
    def lane_chunks(x):
        return [x[:, c:c + LANES] for c in range(0, x.shape[1], LANES)] if fold_lanes else [x]

    def row_reduce(blocks, op, reduce):
        acc = None
        for blk in blocks:
            chunks = lane_chunks(blk)
            part = chunks[0]
            for c in chunks[1:]:
                part = op(part, c)
            if not fold_lanes:
                part = reduce(part, axis=-1, keepdims=True)
            acc = part if acc is None else op(acc, part)
        return reduce(acc, axis=-1, keepdims=True) if fold_lanes else acc

    m = row_reduce(s_list, jnp.maximum, jnp.max)
    if extra is not None:
        m = jnp.maximum(m, extra)
    ps = []
    out = None
    for s, v in zip(s_list, v_list):
        p = jnp.exp(s - m)
        ps.append(p)
        pv = _dot(p.astype(BF16), v)
        out = pv if out is None else out + pv
    denom = row_reduce(ps, jnp.add, jnp.sum)
    if extra is not None:
        denom = denom + jnp.exp(extra - m)
    return out / denom


def _na_kernel(q_ref, k_ref, v_ref, kc_ref, vc_ref, bias_ref, qn_ref, kn_ref, o_ref,
               k_scr, v_scr, kc_scr, vc_scr, *, n_steps):
    i = pl.program_id(2)
    band = NA_BAND_ROWS * GRID_W
    start = jnp.clip(NA_Q_ROWS * i - NA_ROWS // 2, 0, NA_Q_ROWS * n_steps - NA_BAND_ROWS) * GRID_W
    start = pl.multiple_of(start, NA_ROWS // 2 * GRID_W)

    @pl.when(i == 0)
    def _():
        k_scr[...] = _head_rms(k_ref[0], kn_ref[...]).astype(BF16)
        v_scr[...] = v_ref[0].astype(BF16)
        kc_scr[...] = _head_rms(kc_ref[0], kn_ref[...]).astype(BF16)
        vc_scr[...] = vc_ref[0].astype(BF16)

    q = (_head_rms(q_ref[0], qn_ref[...]) * ATTN_SCALE).astype(BF16)
    kb = k_scr[pl.ds(start, band), :]
    vb = v_scr[pl.ds(start, band), :]
    kc = kc_scr[...]
    vc = vc_scr[...]
    outs = []
    for h in range(2):
        sl = slice(h * HEAD_DIM, (h + 1) * HEAD_DIM)
        s_loc = _dot_nt(q[:, sl], kb[:, sl]) + bias_ref[0, h]
        s_ctx = _dot_nt(q[:, sl], kc[:, sl])
        outs.append(_softmax_pv([s_loc, s_ctx], [vb[:, sl], vc[:, sl]]))
    o_ref[0] = jnp.concatenate(outs, axis=-1)


def _na_bias(rpb, rows):
    h = rpb.shape[0]
    col = np.arange(GRID_W)
    c0 = np.clip(col - NA_COLS // 2, 0, GRID_W - NA_COLS)
    dc = col[None, :] - col[:, None] + (NA_COLS - 1)
    ok_c = (col[None, :] >= c0[:, None]) & (col[None, :] < c0[:, None] + NA_COLS)
    pick = ((np.arange(2 * NA_COLS - 1)[:, None, None] == dc[None]) & ok_c[None]).astype(np.float32)
    blocks = jnp.einsum('hrd,dqk->hrqk', rpb, pick, precision=lax.Precision.HIGHEST)
    blocks = jnp.where(ok_c, blocks, NEG_INF)
    qt, band = NA_Q_ROWS * GRID_W, NA_BAND_ROWS * GRID_W
    return pl.pallas_call(
        functools.partial(_na_bias_kernel, rows=rows),
        grid=(3, h),
        in_specs=[pl.BlockSpec((1,) + blocks.shape[1:], lambda c, hh: (hh, 0, 0, 0))],
        out_specs=pl.BlockSpec((1, 1, qt, band), lambda c, hh: (c, hh, 0, 0)),
        out_shape=jax.ShapeDtypeStruct((3, h, qt, band), F32),
        compiler_params=_params("arbitrary", "arbitrary"),
        name="na_bias_tables",
    )(blocks)


def _na_bias_kernel(blk_ref, o_ref, *, rows):
    n_steps = rows // NA_Q_ROWS
    for c, step in enumerate((0, 1, n_steps - 1)):
        @pl.when(pl.program_id(0) == c)
        def _(step=step):
            o_ref[0, 0] = jnp.full(o_ref.shape[2:], NEG_INF, F32)
            band0 = min(max(NA_Q_ROWS * step - NA_ROWS // 2, 0), rows - NA_BAND_ROWS)
            for rq in range(NA_Q_ROWS):
                r = NA_Q_ROWS * step + rq
                r0 = min(max(r - NA_ROWS // 2, 0), rows - NA_ROWS)
                for rk in range(r0, r0 + NA_ROWS):
                    col0 = (rk - band0) * GRID_W
                    o_ref[0, 0, rq * GRID_W:(rq + 1) * GRID_W, col0:col0 + GRID_W] = (
                        blk_ref[0, rk - r + NA_ROWS - 1])


def _na_attention(p_lat, p_ctx, bias, qn, kn, n_pairs):
    b, s, _ = p_lat.shape
    l = p_ctx.shape[1]
    rows = s // GRID_W
    n_steps = rows // NA_Q_ROWS
    qt = NA_Q_ROWS * GRID_W

    def cls(i):
        return jnp.where(i == 0, 0, jnp.where(i == n_steps - 1, 2, 1))

    return pl.pallas_call(
        functools.partial(_na_kernel, n_steps=n_steps),
        grid=(n_pairs, b, n_steps),
        in_specs=[pl.BlockSpec((1, qt, LANES), lambda hp, bb, i: (bb, i, hp)),
                  pl.BlockSpec((1, s, LANES), lambda hp, bb, i: (bb, 0, n_pairs + hp)),
                  pl.BlockSpec((1, s, LANES), lambda hp, bb, i: (bb, 0, 2 * n_pairs + hp)),
                  pl.BlockSpec((1, l, LANES), lambda hp, bb, i: (bb, 0, n_pairs + hp)),
                  pl.BlockSpec((1, l, LANES), lambda hp, bb, i: (bb, 0, 2 * n_pairs + hp)),
                  pl.BlockSpec((1, 2, qt, NA_BAND_ROWS * GRID_W), lambda hp, bb, i: (cls(i), hp, 0, 0)),
                  pl.BlockSpec((1, LANES), lambda hp, bb, i: (0, 0)),
                  pl.BlockSpec((1, LANES), lambda hp, bb, i: (0, 0))],
        out_specs=pl.BlockSpec((1, qt, LANES), lambda hp, bb, i: (bb, i, hp)),
        out_shape=jax.ShapeDtypeStruct((b, s, n_pairs * LANES), F32),
        scratch_shapes=[pltpu.VMEM((s, LANES), BF16), pltpu.VMEM((s, LANES), BF16),
                        pltpu.VMEM((l, LANES), BF16), pltpu.VMEM((l, LANES), BF16)],
        compiler_params=_params("arbitrary", "arbitrary", "arbitrary"),
        name="neighbourhood_attention",
    )(p_lat, p_lat, p_lat, p_ctx, p_ctx, bias, qn, kn)


def _ctx_attn_kernel(q_ref, k_ref, v_ref, qn_ref, kn_ref, o_ref):
    q = (_head_rms(q_ref[0], qn_ref[...]) * ATTN_SCALE).astype(BF16)
    k = _head_rms(k_ref[0], kn_ref[...]).astype(BF16)
    v = v_ref[0].astype(BF16)
    outs = []
    for h in range(2):
        sl = slice(h * HEAD_DIM, (h + 1) * HEAD_DIM)
        outs.append(_softmax_pv([_dot_nt(q[:, sl], k[:, sl])], [v[:, sl]]))
    o_ref[0] = jnp.concatenate(outs, axis=-1)


def _ctx_attention(p_ctx, qn, kn, n_pairs):
    b, l, _ = p_ctx.shape
    return pl.pallas_call(
        _ctx_attn_kernel,
        grid=(n_pairs, b),
        in_specs=[pl.BlockSpec((1, l, LANES), lambda hp, bb: (bb, 0, hp)),
                  pl.BlockSpec((1, l, LANES), lambda hp, bb: (bb, 0, n_pairs + hp)),
                  pl.BlockSpec((1, l, LANES), lambda hp, bb: (bb, 0, 2 * n_pairs + hp)),
                  pl.BlockSpec((1, LANES), lambda hp, bb: (0, 0)),
                  pl.BlockSpec((1, LANES), lambda hp, bb: (0, 0))],
        out_specs=pl.BlockSpec((1, l, LANES), lambda hp, bb: (bb, 0, hp)),
        out_shape=jax.ShapeDtypeStruct((b, l, n_pairs * LANES), F32),
        compiler_params=_params("arbitrary", "arbitrary"),
        name="context_attention",
    )(p_ctx, p_ctx, p_ctx, qn, kn)


def _ret_kernel(q_ref, k_ref, v_ref, g_ref, cos_ref, sin_ref, lg_ref, s0_ref, y_ref, st_ref, sf_scr,
                *, n_chunks, use_rope):
    c = RET_CHUNK
    hd = HEAD_DIM
    lg = -jnp.exp(lg_ref[0])
    lgf, lgb = lg[0:1, :], lg[1:2, :]
    ii = lax.broadcasted_iota(jnp.int32, (c, LANES), 0).astype(F32)
    dq_f = jnp.exp(lgf * (ii + 1.0))
    dk_f = jnp.exp(lgf * (c - 1.0 - ii))
    dq_b = jnp.exp(lgb * (c - ii))
    dk_b = jnp.exp(lgb * ii)
    dc_f = jnp.exp(lgf * float(c))
    dc_b = jnp.exp(lgb * float(c))
    diff = (lax.broadcasted_iota(jnp.int32, (c, c), 0) - lax.broadcasted_iota(jnp.int32, (c, c), 1)).astype(F32)
    intra = []
    chunk_f = []
    chunk_b = []
    for h in range(2):
        lf = lgf[:, h * hd:h * hd + 1]
        lb = lgb[:, h * hd:h * hd + 1]
        intra.append(jnp.where(diff >= 0, jnp.exp(lf * jnp.maximum(diff, 0.0)),
                               jnp.exp(lb * jnp.maximum(-diff, 0.0))))
        chunk_f.append(dc_f[:, h * hd:h * hd + 1])
        chunk_b.append(dc_b[:, h * hd:h * hd + 1])

    def load(n):
        r = pl.multiple_of(n * c, c)
        q = q_ref[0, pl.ds(r, c), :]
        k = k_ref[0, pl.ds(r, c), :]
        v = v_ref[0, pl.ds(r, c), :]
        if use_rope:
            cs = cos_ref[pl.ds(r, c), :]
            sn = sin_ref[pl.ds(r, c), :]
            q = _rope(q, cs, sn)
            k = _rope(k, cs, sn)
        return r, q * ATTN_SCALE, k, v.astype(BF16)

    def fwd(n, carry):
        _, _, k, v = load(n)
        kd = (k * dk_f).astype(BF16)
        new = []
        for h in range(2):
            sl = slice(h * hd, (h + 1) * hd)
            sf_scr[n, h] = carry[h]
            new.append(carry[h] * chunk_f[h] + _dot_tn(kd[:, sl], v[:, sl]))
        return tuple(new)

    sf = lax.fori_loop(0, n_chunks, fwd, (s0_ref[0, 0, 0], s0_ref[0, 0, 1]), unroll=2)
    st_ref[0, 0, 0] = sf[0]
    st_ref[0, 0, 1] = sf[1]

    def bwd(jj, carry):
        n = n_chunks - 1 - jj
        r, q, k, v = load(n)
        qb = q.astype(BF16)
        kb = k.astype(BF16)
        qf = (q * dq_f).astype(BF16)
        qr = (q * dq_b).astype(BF16)
        kd = (k * dk_b).astype(BF16)
        outs = []
        new = []
        for h in range(2):
            sl = slice(h * hd, (h + 1) * hd)
            a = (_dot_nt(qb[:, sl], kb[:, sl]) * intra[h]).astype(BF16)
            o = (_dot(a, v[:, sl]) + _dot(qf[:, sl], sf_scr[n, h].astype(BF16))
                 + _dot(qr[:, sl], carry[h].astype(BF16)))
            oc = o - jnp.mean(o, axis=-1, keepdims=True)
            outs.append(oc * lax.rsqrt(jnp.mean(oc * oc, axis=-1, keepdims=True) + GN_EPS))
            new.append(carry[h] * chunk_b[h] + _dot_tn(kd[:, sl], v[:, sl]))
        g = g_ref[0, pl.ds(r, c), :]
        y_ref[0, pl.ds(r, c), :] = jnp.concatenate(outs, axis=-1) * (g * jax.nn.sigmoid(g))
        return tuple(new)

    sb = lax.fori_loop(0, n_chunks, bwd, (s0_ref[0, 0, 2], s0_ref[0, 0, 3]), unroll=2)
    st_ref[0, 0, 2] = sb[0]
    st_ref[0, 0, 3] = sb[1]


def _retention(p, cos2, sin2, lg, s0, col0, n_pairs, use_rope):
    b, t, _ = p.shape
    n_chunks = t // RET_CHUNK
    tab = lambda hp, bb: (0, 0)
    return pl.pallas_call(
        functools.partial(_ret_kernel, n_chunks=n_chunks, use_rope=use_rope),
        grid=(n_pairs, b),
        in_specs=[pl.BlockSpec((1, t, LANES), lambda hp, bb: (bb, 0, col0 + hp)),
                  pl.BlockSpec((1, t, LANES), lambda hp, bb: (bb, 0, col0 + n_pairs + hp)),
                  pl.BlockSpec((1, t, LANES), lambda hp, bb: (bb, 0, col0 + 2 * n_pairs + hp)),
                  pl.BlockSpec((1, t, LANES), lambda hp, bb: (bb, 0, col0 + 3 * n_pairs + hp)),
                  pl.BlockSpec(cos2.shape, tab),
                  pl.BlockSpec(sin2.shape, tab),
                  pl.BlockSpec((1, 2, LANES), lambda hp, bb: (hp, 0, 0)),
                  pl.BlockSpec((1, 1, 4, HEAD_DIM, HEAD_DIM), lambda hp, bb: (bb, hp, 0, 0, 0))],
        out_specs=[pl.BlockSpec((1, t, LANES), lambda hp, bb: (bb, 0, hp)),
                   pl.BlockSpec((1, 1, 4, HEAD_DIM, HEAD_DIM), lambda hp, bb: (bb, hp, 0, 0, 0))],
        out_shape=[jax.ShapeDtypeStruct((b, t, n_pairs * LANES), F32),
                   jax.ShapeDtypeStruct((b, n_pairs, 4, HEAD_DIM, HEAD_DIM), F32)],
        scratch_shapes=[pltpu.VMEM((n_chunks, 2, HEAD_DIM, HEAD_DIM), F32)],
        compiler_params=_params("arbitrary", "arbitrary"),
        name="retention",
    )(p, p, p, p, cos2, sin2, lg, s0)


def _swa_kernel(q_ref, k_ref, v_ref, kc_ref, vc_ref, cos_ref, sin_ref, qn_ref, kn_ref, sink_ref, far_ref,
                o_ref, *, seq):
    qt = SWA_Q_TILE
    wk = qt + 2 * SWA_WINDOW
    hd = HEAD_DIM
    n = pl.program_id(2)
    q0 = pl.multiple_of(n * qt, qt)
    ws = pl.multiple_of(jnp.clip(n * qt - SWA_WINDOW, 0, seq - wk), SWA_WINDOW)
    kw = _rope(_head_rms(k_ref[0, pl.ds(ws, wk), :], kn_ref[...]),
               cos_ref[pl.ds(ws, wk), :], sin_ref[pl.ds(ws, wk), :]).astype(BF16)
    vw = v_ref[0, pl.ds(ws, wk), :].astype(BF16)
    kc = _head_rms(kc_ref[0], kn_ref[...]).astype(BF16)
    vc = vc_ref[0].astype(BF16)
    cos_q = cos_ref[pl.ds(q0, qt), :]
    sin_q = sin_ref[pl.ds(q0, qt), :]
    qs = []
    for s in range(4):
        slab = q_ref[0, :, s * LANES:(s + 1) * LANES]
        qs.append((_rope(_head_rms(slab, qn_ref[...]), cos_q, sin_q) * ATTN_SCALE).astype(BF16))
    far = far_ref[0][None]
    for kh in range(2):
        sl = slice(kh * hd, (kh + 1) * hd)
        qstack = jnp.concatenate(
            [qs[kh * 2 + g // 2][:, (g % 2) * hd:(g % 2 + 1) * hd] for g in range(4)], axis=0)
        sink = jnp.concatenate(
            [jnp.broadcast_to(sink_ref[kh * 4 + g:kh * 4 + g + 1, 0:1], (qt, 1)) for g in range(4)], axis=0)
        s_loc = (_dot_nt(qstack, kw[:, sl]).reshape(4, qt, wk) + far).reshape(4 * qt, wk)
        s_ctx = _dot_nt(qstack, kc[:, sl])
        o = _softmax_pv([s_loc, s_ctx], [vw[:, sl], vc[:, sl]], extra=sink, fold_lanes=True)
        for pair in range(2):
            col = (kh * 2 + pair) * LANES
            o_ref[0, :, col:col + LANES] = jnp.concatenate(
                [o[(2 * pair) * qt:(2 * pair + 1) * qt], o[(2 * pair + 1) * qt:(2 * pair + 2) * qt]], axis=-1)


def _swa_attention(p_lat, p_ctx, cos2, sin2, qn, kn, sink_rows, n_q_heads, n_kv_heads):
    b, s, _ = p_lat.shape
    l = p_ctx.shape[1]
    kv_pairs = n_kv_heads // 2
    q_blocks = n_q_heads * HEAD_DIM // LANES
    q_per_pair = q_blocks // kv_pairs
    qw = q_per_pair * LANES
    tab = lambda kp, bb, n: (0, 0)
    n_steps = s // SWA_Q_TILE
    wk = SWA_Q_TILE + 2 * SWA_WINDOW
    far = []
    for step in (0, 1, n_steps - 1):
        q0 = step * SWA_Q_TILE
        ws = min(max(q0 - SWA_WINDOW, 0), s - wk)
        dist = np.abs((q0 + np.arange(SWA_Q_TILE))[:, None] - (ws + np.arange(wk))[None, :])
        far.append(np.where(dist <= SWA_WINDOW, 0.0, NEG_INF))
    far = jnp.asarray(np.stack(far), F32)

    def cls(n):
        return jnp.where(n == 0, 0, jnp.where(n == n_steps - 1, 2, 1))

    return pl.pallas_call(
        functools.partial(_swa_kernel, seq=s),
        grid=(kv_pairs, b, n_steps),
        in_specs=[pl.BlockSpec((1, SWA_Q_TILE, qw), lambda kp, bb, n: (bb, n, kp)),
                  pl.BlockSpec((1, s, LANES), lambda kp, bb, n: (bb, 0, q_blocks + kp)),
                  pl.BlockSpec((1, s, LANES), lambda kp, bb, n: (bb, 0, q_blocks + kv_pairs + kp)),
                  pl.BlockSpec((1, l, LANES), lambda kp, bb, n: (bb, 0, q_blocks + kp)),
                  pl.BlockSpec((1, l, LANES), lambda kp, bb, n: (bb, 0, q_blocks + kv_pairs + kp)),
                  pl.BlockSpec(cos2.shape, tab),
                  pl.BlockSpec(sin2.shape, tab),
                  pl.BlockSpec((1, LANES), tab),
                  pl.BlockSpec((1, LANES), tab),
                  pl.BlockSpec((8, LANES), lambda kp, bb, n: (kp, 0)),
                  pl.BlockSpec((1, SWA_Q_TILE, wk), lambda kp, bb, n: (cls(n), 0, 0))],
        out_specs=pl.BlockSpec((1, SWA_Q_TILE, qw), lambda kp, bb, n: (bb, n, kp)),
        out_shape=jax.ShapeDtypeStruct((b, s, n_q_heads * HEAD_DIM), F32),
        compiler_params=_params("arbitrary", "arbitrary", "arbitrary"),
        name="windowed_gqa",
    )(p_lat, p_lat, p_lat, p_ctx, p_ctx, cos2, sin2, qn, kn, sink_rows, far)


def _oddeven_merge_sort(n):
    pairs = []

    def merge(lo, size, r):
        step = r * 2
        if step < size:
            merge(lo, size, step)
            merge(lo + r, size, step)
            pairs.extend((i, i + r) for i in range(lo + r, lo + size - r, step))
        else:
            pairs.append((lo, lo + r))

    def sort(lo, size):
        if size > 1:
            sort(lo, size // 2)
            sort(lo + size // 2, size // 2)
            merge(lo, size, 1)

    sort(0, n)
    return pairs


def _top_rows(x, k, scr):
    tiles = [x[8 * i:8 * i + 8, :] for i in range(x.shape[0] // 8)]
    for i, j in _oddeven_merge_sort(len(tiles)):
        tiles[i], tiles[j] = jnp.maximum(tiles[i], tiles[j]), jnp.minimum(tiles[i], tiles[j])
    for r in range(k):
        m = jnp.max(tiles[0], axis=0, keepdims=True)
        scr[r:r + 1, :] = m
        need = k - r - 1
        if need > 0:
            hit = tiles[0] == m
            for d in range(min(need, len(tiles) - 1)):
                tiles[d] = jnp.where(hit, tiles[d + 1], tiles[d])
            if need >= len(tiles):
                tiles[-1] = jnp.where(hit, NEG_INF, tiles[-1])


def _bf16_pair(x):
    bits = lax.bitcast_convert_type(x.astype(BF16).astype(F32), jnp.uint32)
    return bits | (bits >> 16)


def _count_above(sorted_scr, y, strict):
    row = lambda i: sorted_scr[i:i + 1, :]
    above = (lambda r: r > y) if strict else (lambda r: r >= y)
    c8 = above(row(7))
    c4 = above(jnp.where(c8, row(11), row(3)))
    c2 = above(jnp.where(c8, jnp.where(c4, row(13), row(9)), jnp.where(c4, row(5), row(1))))
    hi = jnp.where(c4, jnp.where(c2, row(14), row(12)), jnp.where(c2, row(10), row(8)))
    lo = jnp.where(c4, jnp.where(c2, row(6), row(4)), jnp.where(c2, row(2), row(0)))
    c1 = above(jnp.where(c8, hi, lo))
    count = (jnp.where(c8, 8.0, 0.0) + jnp.where(c4, 4.0, 0.0)) + (jnp.where(c2, 2.0, 0.0) + jnp.where(c1, 1.0, 0.0))
    return count + jnp.where(above(row(15)), 1.0, 0.0)


def _peer_scores_kernel(h_ref, wq_ref, keys_ref, ra_ref, p1_ref, gb_ref, p2_ref,
                        q_scr, s_scr, a_scr, b_scr, c_scr, v_scr, *, n_heads):
    k = PEER_TOPK
    n_blocks = h_ref.shape[0] // LANES

    q = _dot(h_ref[...], wq_ref[...])
    for hp in range(2 * n_heads):
        q_scr[hp] = q[:, hp * PEER_KEYS:(hp + 1) * PEER_KEYS].astype(BF16)

    def lane_block(lb, h, slot):
        tops_a, tops_b, cand, tops_c = a_scr.at[slot], b_scr.at[slot], c_scr.at[slot], v_scr.at[slot]
        s1 = s_scr[0, lb]
        s2 = s_scr[1, lb]
        _top_rows(s1, k + 1, tops_a)
        _top_rows(s2, k + 1, tops_b)
        cand[0:16, :] = tops_a[0:1, :] + tops_b[0:16, :]
        for i in range(1, 8):
            cand[8 + 8 * i:16 + 8 * i, :] = tops_a[i:i + 1, :] + tops_b[0:8, :]
        cand[72:80, :] = tops_a[8:16, :] + tops_b[0:1, :]
        cand[80:81, :] = tops_a[0:1, :] + tops_b[16:17, :]
        cand[81:82, :] = tops_a[16:17, :] + tops_b[0:1, :]
        cand[82:PEER_KEYS, :] = jnp.full((PEER_KEYS - 82, LANES), NEG_INF, F32)
        _top_rows(cand[...], k + 1, tops_c)
        thr = 0.5 * (tops_c[k - 1:k, :] + tops_c[k:k + 1, :])
        z = jnp.sum(jnp.exp(tops_c[0:k, :] - tops_c[0:1, :]), axis=0, keepdims=True)
        ra_ref[lb, h] = _bf16_pair(_count_above(tops_a, s1, strict=True))
        p1_ref[lb, h] = _bf16_pair(jnp.exp(s1 - tops_a[0:1, :]) * (0.5 / z))
        gb_ref[lb, h] = _count_above(tops_a, thr - s2, strict=False).astype(BF16)
        p2_ref[lb, h] = jnp.exp(s2 - tops_b[0:1, :]).astype(BF16)

    def lane_pair(i, h):
        lane_block(2 * i, h, 0)
        lane_block(2 * i + 1, h, 1)
        return h

    def head(h, carry):
        for p in range(2):
            st = _dot_nt(keys_ref[2 * h + p], q_scr[2 * h + p])
            for lb in range(n_blocks):
                s_scr[p, lb] = st[:, lb * LANES:(lb + 1) * LANES]
        lax.fori_loop(0, n_blocks // 2, lane_pair, h)
        return carry

    lax.fori_loop(0, n_heads, head, 0)


def _peer_scores(h2, wq, keys, n_heads):
    t, d = h2.shape
    tt = PEER_TOKEN_TILE
    nb = tt // LANES
    tab_shape = (t // LANES, n_heads, PEER_KEYS, LANES)
    tab_spec = pl.BlockSpec((nb, n_heads, PEER_KEYS, LANES), lambda i: (i, 0, 0, 0))
    return pl.pallas_call(
        functools.partial(_peer_scores_kernel, n_heads=n_heads),
        grid=(t // tt,),
        in_specs=[pl.BlockSpec((tt, d), lambda i: (i, 0)),
                  pl.BlockSpec(wq.shape, lambda i: (0, 0)),
                  pl.BlockSpec(keys.shape, lambda i: (0, 0, 0))],
        out_specs=[tab_spec] * 4,
        out_shape=[jax.ShapeDtypeStruct(tab_shape, dt) for dt in (jnp.uint32, jnp.uint32, BF16, BF16)],
        scratch_shapes=[pltpu.VMEM((2 * n_heads, tt, PEER_KEYS), BF16),
                        pltpu.VMEM((2, nb, PEER_KEYS, LANES), F32),
                        pltpu.VMEM((2, 24, LANES), F32), pltpu.VMEM((2, 24, LANES), F32),
                        pltpu.VMEM((2, PEER_KEYS, LANES), F32), pltpu.VMEM((2, 24, LANES), F32)],
        compiler_params=_params("arbitrary"),
        name="peer_scores",
    )(h2, wq, keys)


def _peer_mix_kernel(h_ref, u_ref, vt_ref, ra_ref, p1_ref, gb_ref, p2_ref, x_ref, g_ref, o_ref,
                     acc_scr, act_scr, a_scr, *, n_heads, n_tiles):
    s = pl.program_id(1)
    tt = h_ref.shape[0]
    rows_per_tile = PEER_EXPERT_TILE // PEER_KEYS
    tile = (PEER_KEYS, LANES)

    def readout():
        acc_scr[...] += _dot(vt_ref[0], a_scr[...])

    def row_tile(ref, lb, h, aa):
        words = jnp.broadcast_to(ref[lb, h, aa:aa + 1, :], (8, LANES))
        packed = pltpu.bitcast(words, BF16)
        return jnp.broadcast_to(packed[None], (PEER_KEYS // 16, 16, LANES)).reshape(tile)

    def gates():
        for aa in range(rows_per_tile):
            rs = slice(aa * PEER_KEYS, (aa + 1) * PEER_KEYS)
            for lb in range(tt // LANES):
                ls = slice(lb * LANES, (lb + 1) * LANES)
                w = None
                for h in range(n_heads):
                    rank = row_tile(ra_ref, lb, h, aa)
                    p1 = row_tile(p1_ref, lb, h, aa)
                    term = jnp.where(rank < gb_ref[lb, h], p2_ref[lb, h], jnp.zeros(tile, BF16)) * p1
                    w = term if w is None else w + term
                act = act_scr[rs, ls]
                gelu2 = act * (1.0 + lax.erf(act * INV_SQRT2))
                a_scr[rs, ls] = w * gelu2.astype(BF16)

    def experts():
        act_scr[...] = _dot_nt(u_ref[0].astype(BF16), h_ref[...])

    @pl.when(s == 0)
    def _():
        acc_scr[...] = jnp.zeros_like(acc_scr)

    i = pl.program_id(0)

    @pl.when(s < n_tiles)
    def _():
        experts()

    @pl.when(i >= 0)
    def _():
        gates()

    @pl.when(s + i >= 0)
    def _():
        readout()

    @pl.when(s == n_tiles - 1)
    def _():
        o_ref[...] = x_ref[...] + g_ref[0] * acc_scr[...].T


def _peer_mix(h2, u_all, layer, vt, tables, x, gate, seg, n_heads):
    t, d = h2.shape
    n_exp = u_all.shape[1]
    tt = min(PEER_MIX_TOKEN_TILE, seg)
    te = PEER_EXPERT_TILE
    n_tiles = n_exp // te
    per_seg = seg // tt
    r = gate.shape[0]
    nb = tt // LANES
    tab_spec = pl.BlockSpec((nb, n_heads, PEER_KEYS, LANES), lambda i, s: (i, 0, 0, 0))
    row_spec = pl.BlockSpec((nb, n_heads, te // PEER_KEYS, LANES), lambda i, s: (i, 0, s, 0))
    return pl.pallas_call(
        functools.partial(_peer_mix_kernel, n_heads=n_heads, n_tiles=n_tiles),
        grid=(t // tt, n_tiles),
        in_specs=[pl.BlockSpec((tt, d), lambda i, s: (i, 0)),
                  pl.BlockSpec((1, te, d), lambda i, s: (layer, s, 0)),
                  pl.BlockSpec((1, d, te), lambda i, s: (s, 0, 0)),
                  row_spec, row_spec, tab_spec, tab_spec,
                  pl.BlockSpec((tt, d), lambda i, s: (i, 0)),
                  pl.BlockSpec((1, 1, d), lambda i, s: (i // per_seg, 0, 0))],
        out_specs=pl.BlockSpec((tt, d), lambda i, s: (i, 0)),
        out_shape=jax.ShapeDtypeStruct((t, d), F32),
        scratch_shapes=[pltpu.VMEM((d, tt), F32), pltpu.VMEM((te, tt), F32), pltpu.VMEM((te, tt), BF16)],
        compiler_params=_params("arbitrary", "arbitrary"),
        name="peer_mix",
    )(h2, u_all, vt, *tables, x, gate.reshape(r, 1, d))


def _peer(h2, x, gate, seg, wq, keys, u_all, layer, vt):
    n_heads = keys.shape[0] // 2
    tables = _peer_scores(h2, wq, keys, n_heads)
    return _peer_mix(h2, u_all, layer, vt, tables, x, gate, seg, n_heads)


def _rope_tables(n_tokens):
    t = jnp.arange(n_tokens)
    row = (t // GRID_W).astype(F32)
    col = (t % GRID_W).astype(F32)
    n_freq = HEAD_DIM // 4
    inv_freq = jnp.power(ROPE_BASE, -jnp.arange(n_freq, dtype=F32) / n_freq)
    ang = jnp.concatenate([row[:, None] * inv_freq, col[:, None] * inv_freq], axis=-1)
    cos, sin = jnp.cos(ang), jnp.sin(ang)
    cos2 = jnp.tile(jnp.concatenate([cos, cos], axis=-1), (1, 2))
    sin2 = jnp.tile(jnp.concatenate([-sin, sin], axis=-1), (1, 2))
    return cos2, sin2


def _pair_lanes(v):
    return jnp.repeat(v, HEAD_DIM, axis=-1).reshape(*v.shape[:-1], v.shape[-1] // 2, LANES)


def kernel(x, c, ctx, c_ctx, mod_w, mod_b, norm1_g, norm2_g, ab_w_in, ab_w_out, na_q_norm, na_k_norm,
           na_rpb, ret_log_decay, swa_w_in, swa_w_out, swa_q_norm, swa_k_norm, swa_sink,
           peer_w_q, peer_sub_keys, peer_u, peer_v):
    b, s, d = x.shape
    l = ctx.shape[1]
    depth = mod_w.shape[0]
    assert depth == 2 and b + 1 <= 8
    assert s % (NA_Q_ROWS * GRID_W) == 0 and s // GRID_W >= NA_BAND_ROWS
    assert s % TOKEN_TILE == 0 and (b * l) % TOKEN_TILE == 0 and l % RET_CHUNK == 0

    cond = jnp.concatenate([c, c_ctx[None, :], jnp.zeros((8 - b - 1, d), F32)], axis=0)
    mods = _adaln(cond, mod_w, mod_b).reshape(depth, 8, 6, d)
    lat = lambda layer, which: mods[layer, :b, which]
    cx = lambda layer, which: mods[layer, b:b + 1, which]

    cos2, sin2 = _rope_tables(s)
    x_lat = x.reshape(b * s, d)
    x_ctx = ctx.reshape(b * l, d)
    tile2 = lambda g: jnp.tile(g, 2).reshape(1, LANES)

    def peer_weights(layer):
        n_heads = peer_sub_keys.shape[1]
        keys = peer_sub_keys[layer].reshape(2 * n_heads, PEER_KEYS, -1).astype(BF16)
        vt = peer_v[layer].reshape(-1, PEER_EXPERT_TILE, d).transpose(0, 2, 1).astype(BF16)
        return peer_w_q[layer].astype(BF16), keys, peer_u, layer, vt

    n_na = na_rpb.shape[1]
    n_ret = ret_log_decay.shape[2]
    na_pairs, ret_pairs = n_na // 2, n_ret // 2
    w_in = ab_w_in[0].astype(BF16)
    w_out = ab_w_out[0].astype(BF16)
    wa = n_na * HEAD_DIM
    p_lat = _modmm(x_lat, norm1_g[0], lat(0, 0), lat(0, 1), w_in, s).reshape(b, s, -1)
    p_ctx = _modmm(x_ctx, norm1_g[0], cx(0, 0), cx(0, 1), w_in, b * l).reshape(b, l, -1)

    qn, kn = tile2(na_q_norm[0]), tile2(na_k_norm[0])
    oa_lat = _na_attention(p_lat, p_ctx, _na_bias(na_rpb[0], s // GRID_W), qn, kn, na_pairs)
    oa_ctx = _ctx_attention(p_ctx, qn, kn, na_pairs)

    lg = _pair_lanes(ret_log_decay[0]).transpose(1, 0, 2)
    ret_col0 = 3 * wa // LANES
    zeros_state = jnp.zeros((b, ret_pairs, 4, HEAD_DIM, HEAD_DIM), F32)
    ones_tab, zeros_tab = jnp.ones((l, LANES), F32), jnp.zeros((l, LANES), F32)
    ob_ctx, st_ctx = _retention(p_ctx, ones_tab, zeros_tab, lg, zeros_state, ret_col0, ret_pairs, False)
    ob_lat, _ = _retention(p_lat, cos2, sin2, lg, st_ctx, ret_col0, ret_pairs, True)

    w_list = [w_out[:wa], w_out[wa:]]
    x_lat, h_lat = _outproj([oa_lat.reshape(b * s, -1), ob_lat.reshape(b * s, -1)], w_list, x_lat,
                            lat(0, 2), norm2_g[0], lat(0, 3), lat(0, 4), s)
    x_ctx, h_ctx = _outproj([oa_ctx.reshape(b * l, -1), ob_ctx.reshape(b * l, -1)], w_list, x_ctx,
                            cx(0, 2), norm2_g[0], cx(0, 3), cx(0, 4), b * l)
    pw = peer_weights(0)
    x_lat = _peer(h_lat, x_lat, lat(0, 5), s, *pw)
    x_ctx = _peer(h_ctx, x_ctx, cx(0, 5), b * l, *pw)

    n_q = swa_sink.shape[1]
    n_kv = (swa_w_in.shape[2] // HEAD_DIM - n_q) // 2
    w_in = swa_w_in[0].astype(BF16)
    p_lat = _modmm(x_lat, norm1_g[1], lat(1, 0), lat(1, 1), w_in, s).reshape(b, s, -1)
    p_ctx = _modmm(x_ctx, norm1_g[1], cx(1, 0), cx(1, 1), w_in, b * l).reshape(b, l, -1)
    sink_rows = jnp.broadcast_to(swa_sink[0][:, None], (n_q, LANES))
    o_lat = _swa_attention(p_lat, p_ctx, cos2, sin2, tile2(swa_q_norm[0]), tile2(swa_k_norm[0]),
                           sink_rows, n_q, n_kv)
    x_lat, h_lat = _outproj([o_lat.reshape(b * s, -1)], [swa_w_out[0].astype(BF16)], x_lat,
                            lat(1, 2), norm2_g[1], lat(1, 3), lat(1, 4), s)
    x_lat = _peer(h_lat, x_lat, lat(1, 5), s, *peer_weights(1))
    return x_lat.reshape(b, s, d)
```

```python
import functools

import numpy as np
import jax
import jax.numpy as jnp
from jax import lax
from jax.experimental import pallas as pl
from jax.experimental.pallas import tpu as pltpu

F32 = jnp.float32
BF16 = jnp.bfloat16

HEAD_DIM = 64
GRID_W = 64
NA_ROWS = 8
NA_COLS = 16
SWA_WINDOW = 128
PEER_TOPK = 16
PEER_KEYS = 128
ROPE_BASE = 10000.0
NORM_EPS = 1e-6
GN_EPS = 1e-5
NEG_INF = -1e30
ATTN_SCALE = HEAD_DIM ** -0.5
INV_SQRT2 = 0.7071067811865476

LANES = 128
VMEM_LIMIT = 56 * 1024 * 1024

TOKEN_TILE = 512
PROJ_TOKEN_TILE = 1024
PROJ_N_TILE = 1792
RET_CHUNK = 256
NA_Q_ROWS = 8
NA_BAND_ROWS = 16
SWA_Q_TILE = 256
PEER_TOKEN_TILE = 512
PEER_MIX_TOKEN_TILE = 1024
PEER_EXPERT_TILE = 1024


def _dot(a, b):
    return jnp.dot(a, b, preferred_element_type=F32)


def _dot_nt(a, b):
    return lax.dot_general(a, b, (((1,), (1,)), ((), ())), preferred_element_type=F32)


def _dot_tn(a, b):
    return lax.dot_general(a, b, (((0,), (0,)), ((), ())), preferred_element_type=F32)


def _params(*sem):
    return pltpu.CompilerParams(dimension_semantics=sem, vmem_limit_bytes=VMEM_LIMIT)


def _rms_rows(x, gain):
    ms = jnp.mean(x * x, axis=-1, keepdims=True)
    return x * lax.rsqrt(ms + NORM_EPS) * gain


def _modulate(x, gain, shift, scale):
    return _rms_rows(x, gain) * (1.0 + scale) + shift


def _head_rms(x, gain):
    lane = lax.broadcasted_iota(jnp.int32, x.shape, 1)
    lo = lane < HEAD_DIM
    ss = x * x
    s_lo = jnp.sum(jnp.where(lo, ss, 0.0), axis=-1, keepdims=True)
    s_hi = jnp.sum(jnp.where(lo, 0.0, ss), axis=-1, keepdims=True)
    ms = jnp.where(lo, s_lo, s_hi) * (1.0 / HEAD_DIM)
    return x * lax.rsqrt(ms + NORM_EPS) * gain


def _rope(x, cos2, sin2):
    lane = lax.broadcasted_iota(jnp.int32, x.shape, 1)
    first_half = (lane & (HEAD_DIM // 2)) == 0
    swapped = jnp.where(first_half, pltpu.roll(x, LANES - HEAD_DIM // 2, axis=1),
                        pltpu.roll(x, HEAD_DIM // 2, axis=1))
    return x * cos2 + swapped * sin2


def _adaln_kernel(c_ref, w_ref, b_ref, o_ref):
    c = c_ref[...]
    s = c * jax.nn.sigmoid(c)
    w = w_ref[0]
    s_hi = s.astype(BF16)
    s_lo = (s - s_hi.astype(F32)).astype(BF16)
    w_hi = w.astype(BF16)
    w_lo = (w - w_hi.astype(F32)).astype(BF16)
    acc = _dot(s_hi, w_hi) + _dot(s_lo, w_hi) + _dot(s_hi, w_lo)
    o_ref[0] = acc + b_ref[0]


def _adaln(cond, mod_w, mod_b):
    depth, d, n = mod_w.shape
    tn = n // 4
    return pl.pallas_call(
        _adaln_kernel,
        grid=(depth, n // tn),
        in_specs=[pl.BlockSpec((8, d), lambda l, j: (0, 0)),
                  pl.BlockSpec((1, d, tn), lambda l, j: (l, 0, j)),
                  pl.BlockSpec((1, 1, tn), lambda l, j: (l, 0, j))],
        out_specs=pl.BlockSpec((1, 8, tn), lambda l, j: (l, 0, j)),
        out_shape=jax.ShapeDtypeStruct((depth, 8, n), F32),
        compiler_params=_params("arbitrary", "arbitrary"),
        name="adaln",
    )(cond, mod_w, mod_b.reshape(depth, 1, n))


def _modmm_kernel(x_ref, g_ref, sh_ref, sc_ref, w_ref, o_ref, h_scr):
    @pl.when(pl.program_id(1) == 0)
    def _():
        h_scr[...] = _modulate(x_ref[...], g_ref[...], sh_ref[0], sc_ref[0]).astype(BF16)

    o_ref[...] = _dot(h_scr[...], w_ref[...])


def _modmm(x, gain, shift, scale, w, seg):
    t, d = x.shape
    n = w.shape[1]
    tm = min(PROJ_TOKEN_TILE, seg)
    tn = n if n <= PROJ_N_TILE else n // 2
    per_seg = seg // tm
    r = shift.shape[0]
    return pl.pallas_call(
        _modmm_kernel,
        grid=(t // tm, n // tn),
        in_specs=[pl.BlockSpec((tm, d), lambda i, j: (i, 0)),
                  pl.BlockSpec((1, d), lambda i, j: (0, 0)),
                  pl.BlockSpec((1, 1, d), lambda i, j: (i // per_seg, 0, 0)),
                  pl.BlockSpec((1, 1, d), lambda i, j: (i // per_seg, 0, 0)),
                  pl.BlockSpec((d, tn), lambda i, j: (0, j))],
        out_specs=pl.BlockSpec((tm, tn), lambda i, j: (i, j)),
        out_shape=jax.ShapeDtypeStruct((t, n), F32),
        scratch_shapes=[pltpu.VMEM((tm, d), BF16)],
        compiler_params=_params("arbitrary", "arbitrary"),
        name="modulate_matmul",
    )(x, gain.reshape(1, d), shift.reshape(r, 1, d), scale.reshape(r, 1, d), w)


def _outproj_kernel(*refs, n_in):
    a_refs = refs[:n_in]
    w_refs = refs[n_in:2 * n_in]
    x_ref, gate_ref, g2_ref, sh_ref, sc_ref, xo_ref, h_ref = refs[2 * n_in:]
    acc = None
    for a_ref, w_ref in zip(a_refs, w_refs):
        part = _dot(a_ref[...].astype(BF16), w_ref[...])
        acc = part if acc is None else acc + part
    xn = x_ref[...] + gate_ref[0] * acc
    xo_ref[...] = xn
    h_ref[...] = _modulate(xn, g2_ref[...], sh_ref[0], sc_ref[0]).astype(BF16)


def _outproj(a_list, w_list, x, gate, gain2, shift2, scale2, seg):
    t, d = x.shape
    tm = min(TOKEN_TILE, seg)
    per_seg = seg // tm
    r = gate.shape[0]
    n_in = len(a_list)
    row = lambda i: (i // per_seg, 0, 0)
    in_specs = ([pl.BlockSpec((tm, a.shape[1]), lambda i: (i, 0)) for a in a_list]
                + [pl.BlockSpec(w.shape, lambda i: (0, 0)) for w in w_list]
                + [pl.BlockSpec((tm, d), lambda i: (i, 0)),
                   pl.BlockSpec((1, 1, d), row),
                   pl.BlockSpec((1, d), lambda i: (0, 0)),
                   pl.BlockSpec((1, 1, d), row),
                   pl.BlockSpec((1, 1, d), row)])
    return pl.pallas_call(
        functools.partial(_outproj_kernel, n_in=n_in),
        grid=(t // tm,),
        in_specs=in_specs,
        out_specs=[pl.BlockSpec((tm, d), lambda i: (i, 0)), pl.BlockSpec((tm, d), lambda i: (i, 0))],
        out_shape=[jax.ShapeDtypeStruct((t, d), F32), jax.ShapeDtypeStruct((t, d), BF16)],
        compiler_params=_params("arbitrary"),
        name="out_proj_residual",
    )(*a_list, *w_list, x, gate.reshape(r, 1, d), gain2.reshape(1, d),
      shift2.reshape(r, 1, d), scale2.reshape(r, 1, d))


def _softmax_pv(s_list, v_list, extra=None, fold_lanes=False):
    def lane_chunks(x):
        return [x[:, c:c + LANES] for c in range(0, x.shape[1], LANES)] if fold_lanes else [x]

    def row_reduce(blocks, op, reduce):
        acc = None
        for blk in blocks:
            chunks = lane_chunks(blk)
            part = chunks[0]
            for c in chunks[1:]:
                part = op(part, c)
            if not fold_lanes:
                part = reduce(part, axis=-1, keepdims=True)
            acc = part if acc is None else op(acc, part)
        return reduce(acc, axis=-1, keepdims=True) if fold_lanes else acc

    m = row_reduce(s_list, jnp.maximum, jnp.max)
    if extra is not None:
        m = jnp.maximum(m, extra)
    ps = []
    out = None
    for s, v in zip(s_list, v_list):
        p = jnp.exp(s - m)
        ps.append(p)
        pv = _dot(p.astype(BF16), v)
        out = pv if out is None else out + pv
    denom = row_reduce(ps, jnp.add, jnp.sum)
    if extra is not None:
        denom = denom + jnp.exp(extra - m)
    return out / denom


def _na_kernel(q_ref, k_ref, v_ref, kc_ref, vc_ref, bias_ref, qn_ref, kn_ref, o_ref,
               k_scr, v_scr, kc_scr, vc_scr, *, n_steps):
    i = pl.program_id(2)
    band = NA_BAND_ROWS * GRID_W
    start = jnp.clip(NA_Q_ROWS * i - NA_ROWS // 2, 0, NA_Q_ROWS * n_steps - NA_BAND_ROWS) * GRID_W
    start = pl.multiple_of(start, NA_ROWS // 2 * GRID_W)

    @pl.when(i == 0)
    def _():
        k_scr[...] = _head_rms(k_ref[0], kn_ref[...]).astype(BF16)
        v_scr[...] = v_ref[0].astype(BF16)
        kc_scr[...] = _head_rms(kc_ref[0], kn_ref[...]).astype(BF16)
        vc_scr[...] = vc_ref[0].astype(BF16)

    q = (_head_rms(q_ref[0], qn_ref[...]) * ATTN_SCALE).astype(BF16)
    kb = k_scr[pl.ds(start, band), :]
    vb = v_scr[pl.ds(start, band), :]
    kc = kc_scr[...]
    vc = vc_scr[...]
    outs = []
    for h in range(2):
        sl = slice(h * HEAD_DIM, (h + 1) * HEAD_DIM)
        s_loc = _dot_nt(q[:, sl], kb[:, sl]) + bias_ref[0, h]
        s_ctx = _dot_nt(q[:, sl], kc[:, sl])
        outs.append(_softmax_pv([s_loc, s_ctx], [vb[:, sl], vc[:, sl]]))
    o_ref[0] = jnp.concatenate(outs, axis=-1)


def _na_bias(rpb, rows):
    h = rpb.shape[0]
    col = np.arange(GRID_W)
    c0 = np.clip(col - NA_COLS // 2, 0, GRID_W - NA_COLS)
    dc = col[None, :] - col[:, None] + (NA_COLS - 1)
    ok_c = (col[None, :] >= c0[:, None]) & (col[None, :] < c0[:, None] + NA_COLS)
    pick = ((np.arange(2 * NA_COLS - 1)[:, None, None] == dc[None]) & ok_c[None]).astype(np.float32)
    blocks = jnp.einsum('hrd,dqk->hrqk', rpb, pick, precision=lax.Precision.HIGHEST)
    blocks = jnp.where(ok_c, blocks, NEG_INF)
    qt, band = NA_Q_ROWS * GRID_W, NA_BAND_ROWS * GRID_W
    return pl.pallas_call(
        functools.partial(_na_bias_kernel, rows=rows),
        grid=(3, h),
        in_specs=[pl.BlockSpec((1,) + blocks.shape[1:], lambda c, hh: (hh, 0, 0, 0))],
        out_specs=pl.BlockSpec((1, 1, qt, band), lambda c, hh: (c, hh, 0, 0)),
        out_shape=jax.ShapeDtypeStruct((3, h, qt, band), F32),
        compiler_params=_params("arbitrary", "arbitrary"),
        name="na_bias_tables",
    )(blocks)


def _na_bias_kernel(blk_ref, o_ref, *, rows):
    n_steps = rows // NA_Q_ROWS
    for c, step in enumerate((0, 1, n_steps - 1)):
        @pl.when(pl.program_id(0) == c)
        def _(step=step):
            o_ref[0, 0] = jnp.full(o_ref.shape[2:], NEG_INF, F32)
            band0 = min(max(NA_Q_ROWS * step - NA_ROWS // 2, 0), rows - NA_BAND_ROWS)
            for rq in range(NA_Q_ROWS):
                r = NA_Q_ROWS * step + rq
                r0 = min(max(r - NA_ROWS // 2, 0), rows - NA_ROWS)
                for rk in range(r0, r0 + NA_ROWS):
                    col0 = (rk - band0) * GRID_W
                    o_ref[0, 0, rq * GRID_W:(rq + 1) * GRID_W, col0:col0 + GRID_W] = (
                        blk_ref[0, rk - r + NA_ROWS - 1])


def _na_attention(p_lat, p_ctx, bias, qn, kn, n_pairs):
    b, s, _ = p_lat.shape
    l = p_ctx.shape[1]
    rows = s // GRID_W
    n_steps = rows // NA_Q_ROWS
    qt = NA_Q_ROWS * GRID_W

    def cls(i):
        return jnp.where(i == 0, 0, jnp.where(i == n_steps - 1, 2, 1))

    return pl.pallas_call(
        functools.partial(_na_kernel, n_steps=n_steps),
        grid=(n_pairs, b, n_steps),
        in_specs=[pl.BlockSpec((1, qt, LANES), lambda hp, bb, i: (bb, i, hp)),
                  pl.BlockSpec((1, s, LANES), lambda hp, bb, i: (bb, 0, n_pairs + hp)),
                  pl.BlockSpec((1, s, LANES), lambda hp, bb, i: (bb, 0, 2 * n_pairs + hp)),
                  pl.BlockSpec((1, l, LANES), lambda hp, bb, i: (bb, 0, n_pairs + hp)),
                  pl.BlockSpec((1, l, LANES), lambda hp, bb, i: (bb, 0, 2 * n_pairs + hp)),
                  pl.BlockSpec((1, 2, qt, NA_BAND_ROWS * GRID_W), lambda hp, bb, i: (cls(i), hp, 0, 0)),
                  pl.BlockSpec((1, LANES), lambda hp, bb, i: (0, 0)),
                  pl.BlockSpec((1, LANES), lambda hp, bb, i: (0, 0))],
        out_specs=pl.BlockSpec((1, qt, LANES), lambda hp, bb, i: (bb, i, hp)),
        out_shape=jax.ShapeDtypeStruct((b, s, n_pairs * LANES), F32),
        scratch_shapes=[pltpu.VMEM((s, LANES), BF16), pltpu.VMEM((s, LANES), BF16),
                        pltpu.VMEM((l, LANES), BF16), pltpu.VMEM((l, LANES), BF16)],
        compiler_params=_params("arbitrary", "arbitrary", "arbitrary"),
        name="neighbourhood_attention",
    )(p_lat, p_lat, p_lat, p_ctx, p_ctx, bias, qn, kn)


def _ctx_attn_kernel(q_ref, k_ref, v_ref, qn_ref, kn_ref, o_ref):
    q = (_head_rms(q_ref[0], qn_ref[...]) * ATTN_SCALE).astype(BF16)
    k = _head_rms(k_ref[0], kn_ref[...]).astype(BF16)
    v = v_ref[0].astype(BF16)
    outs = []
    for h in range(2):
        sl = slice(h * HEAD_DIM, (h + 1) * HEAD_DIM)
        outs.append(_softmax_pv([_dot_nt(q[:, sl], k[:, sl])], [v[:, sl]]))
    o_ref[0] = jnp.concatenate(outs, axis=-1)


def _ctx_attention(p_ctx, qn, kn, n_pairs):
    b, l, _ = p_ctx.shape
    return pl.pallas_call(
        _ctx_attn_kernel,
        grid=(n_pairs, b),
        in_specs=[pl.BlockSpec((1, l, LANES), lambda hp, bb: (bb, 0, hp)),
                  pl.BlockSpec((1, l, LANES), lambda hp, bb: (bb, 0, n_pairs + hp)),
                  pl.BlockSpec((1, l, LANES), lambda hp, bb: (bb, 0, 2 * n_pairs + hp)),
                  pl.BlockSpec((1, LANES), lambda hp, bb: (0, 0)),
                  pl.BlockSpec((1, LANES), lambda hp, bb: (0, 0))],
        out_specs=pl.BlockSpec((1, l, LANES), lambda hp, bb: (bb, 0, hp)),
        out_shape=jax.ShapeDtypeStruct((b, l, n_pairs * LANES), F32),
        compiler_params=_params("arbitrary", "arbitrary"),
        name="context_attention",
    )(p_ctx, p_ctx, p_ctx, qn, kn)


def _ret_kernel(q_ref, k_ref, v_ref, g_ref, cos_ref, sin_ref, lg_ref, s0_ref, y_ref, st_ref, sf_scr,
                *, n_chunks, use_rope):
    c = RET_CHUNK
    hd = HEAD_DIM
    lg = -jnp.exp(lg_ref[0])
    lgf, lgb = lg[0:1, :], lg[1:2, :]
    ii = lax.broadcasted_iota(jnp.int32, (c, LANES), 0).astype(F32)
    dq_f = jnp.exp(lgf * (ii + 1.0))
    dk_f = jnp.exp(lgf * (c - 1.0 - ii))
    dq_b = jnp.exp(lgb * (c - ii))
    dk_b = jnp.exp(lgb * ii)
    dc_f = jnp.exp(lgf * float(c))
    dc_b = jnp.exp(lgb * float(c))
    diff = (lax.broadcasted_iota(jnp.int32, (c, c), 0) - lax.broadcasted_iota(jnp.int32, (c, c), 1)).astype(F32)
    intra = []
    chunk_f = []
    chunk_b = []
    for h in range(2):
        lf = lgf[:, h * hd:h * hd + 1]
        lb = lgb[:, h * hd:h * hd + 1]
        intra.append(jnp.where(diff >= 0, jnp.exp(lf * jnp.maximum(diff, 0.0)),
                               jnp.exp(lb * jnp.maximum(-diff, 0.0))))
        chunk_f.append(dc_f[:, h * hd:h * hd + 1])
        chunk_b.append(dc_b[:, h * hd:h * hd + 1])

    def load(n):
        r = pl.multiple_of(n * c, c)
        q = q_ref[0, pl.ds(r, c), :]
        k = k_ref[0, pl.ds(r, c), :]
        v = v_ref[0, pl.ds(r, c), :]
        if use_rope:
            cs = cos_ref[pl.ds(r, c), :]
            sn = sin_ref[pl.ds(r, c), :]
            q = _rope(q, cs, sn)
            k = _rope(k, cs, sn)
        return r, q * ATTN_SCALE, k, v.astype(BF16)

    def fwd(n, carry):
        _, _, k, v = load(n)
        kd = (k * dk_f).astype(BF16)
        new = []
        for h in range(2):
            sl = slice(h * hd, (h + 1) * hd)
            sf_scr[n, h] = carry[h]
            new.append(carry[h] * chunk_f[h] + _dot_tn(kd[:, sl], v[:, sl]))
        return tuple(new)

    sf = lax.fori_loop(0, n_chunks, fwd, (s0_ref[0, 0, 0], s0_ref[0, 0, 1]), unroll=2)
    st_ref[0, 0, 0] = sf[0]
    st_ref[0, 0, 1] = sf[1]

    def bwd(jj, carry):
        n = n_chunks - 1 - jj
        r, q, k, v = load(n)
        qb = q.astype(BF16)
        kb = k.astype(BF16)
        qf = (q * dq_f).astype(BF16)
        qr = (q * dq_b).astype(BF16)
        kd = (k * dk_b).astype(BF16)
        outs = []
        new = []
        for h in range(2):
            sl = slice(h * hd, (h + 1) * hd)
            a = (_dot_nt(qb[:, sl], kb[:, sl]) * intra[h]).astype(BF16)
            o = (_dot(a, v[:, sl]) + _dot(qf[:, sl], sf_scr[n, h].astype(BF16))
                 + _dot(qr[:, sl], carry[h].astype(BF16)))
            oc = o - jnp.mean(o, axis=-1, keepdims=True)
            outs.append(oc * lax.rsqrt(jnp.mean(oc * oc, axis=-1, keepdims=True) + GN_EPS))
            new.append(carry[h] * chunk_b[h] + _dot_tn(kd[:, sl], v[:, sl]))
        g = g_ref[0, pl.ds(r, c), :]
        y_ref[0, pl.ds(r, c), :] = jnp.concatenate(outs, axis=-1) * (g * jax.nn.sigmoid(g))
        return tuple(new)

    sb = lax.fori_loop(0, n_chunks, bwd, (s0_ref[0, 0, 2], s0_ref[0, 0, 3]), unroll=2)
    st_ref[0, 0, 2] = sb[0]
    st_ref[0, 0, 3] = sb[1]


def _retention(p, cos2, sin2, lg, s0, col0, n_pairs, use_rope):
    b, t, _ = p.shape
    n_chunks = t // RET_CHUNK
    tab = lambda hp, bb: (0, 0)
    return pl.pallas_call(
        functools.partial(_ret_kernel, n_chunks=n_chunks, use_rope=use_rope),
        grid=(n_pairs, b),
        in_specs=[pl.BlockSpec((1, t, LANES), lambda hp, bb: (bb, 0, col0 + hp)),
                  pl.BlockSpec((1, t, LANES), lambda hp, bb: (bb, 0, col0 + n_pairs + hp)),
                  pl.BlockSpec((1, t, LANES), lambda hp, bb: (bb, 0, col0 + 2 * n_pairs + hp)),
                  pl.BlockSpec((1, t, LANES), lambda hp, bb: (bb, 0, col0 + 3 * n_pairs + hp)),
                  pl.BlockSpec(cos2.shape, tab),
                  pl.BlockSpec(sin2.shape, tab),
                  pl.BlockSpec((1, 2, LANES), lambda hp, bb: (hp, 0, 0)),
                  pl.BlockSpec((1, 1, 4, HEAD_DIM, HEAD_DIM), lambda hp, bb: (bb, hp, 0, 0, 0))],
        out_specs=[pl.BlockSpec((1, t, LANES), lambda hp, bb: (bb, 0, hp)),
                   pl.BlockSpec((1, 1, 4, HEAD_DIM, HEAD_DIM), lambda hp, bb: (bb, hp, 0, 0, 0))],
        out_shape=[jax.ShapeDtypeStruct((b, t, n_pairs * LANES), F32),
                   jax.ShapeDtypeStruct((b, n_pairs, 4, HEAD_DIM, HEAD_DIM), F32)],
        scratch_shapes=[pltpu.VMEM((n_chunks, 2, HEAD_DIM, HEAD_DIM), F32)],
        compiler_params=_params("arbitrary", "arbitrary"),
        name="retention",
    )(p, p, p, p, cos2, sin2, lg, s0)


def _swa_kernel(q_ref, k_ref, v_ref, kc_ref, vc_ref, cos_ref, sin_ref, qn_ref, kn_ref, sink_ref, far_ref,
                o_ref, k_scr, kc_scr, *, seq):
    qt = SWA_Q_TILE
    wk = qt + 2 * SWA_WINDOW
    hd = HEAD_DIM
    n = pl.program_id(2)
    q0 = pl.multiple_of(n * qt, qt)
    ws = pl.multiple_of(jnp.clip(n * qt - SWA_WINDOW, 0, seq - wk), SWA_WINDOW)

    @pl.when(n == 0)
    def _():
        k_scr[...] = _rope(_head_rms(k_ref[0], kn_ref[...]), cos_ref[...], sin_ref[...])
        kc_scr[...] = _head_rms(kc_ref[0], kn_ref[...])

    kw = k_scr[pl.ds(ws, wk), :].astype(BF16)
    vw = v_ref[0, pl.ds(ws, wk), :].astype(BF16)
    kc = kc_scr[...].astype(BF16)
    vc = vc_ref[0].astype(BF16)
    cos_q = cos_ref[pl.ds(q0, qt), :]
    sin_q = sin_ref[pl.ds(q0, qt), :]
    qs = []
    for s in range(4):
        slab = q_ref[0, :, s * LANES:(s + 1) * LANES]
        qs.append((_rope(_head_rms(slab, qn_ref[...]), cos_q, sin_q) * ATTN_SCALE).astype(BF16))
    far = far_ref[0][None]
    for kh in range(2):
        sl = slice(kh * hd, (kh + 1) * hd)
        qstack = jnp.concatenate(
            [qs[kh * 2 + g // 2][:, (g % 2) * hd:(g % 2 + 1) * hd] for g in range(4)], axis=0)
        sink = jnp.concatenate(
            [jnp.broadcast_to(sink_ref[kh * 4 + g:kh * 4 + g + 1, 0:1], (qt, 1)) for g in range(4)], axis=0)
        s_loc = (_dot_nt(qstack, kw[:, sl]).reshape(4, qt, wk) + far).reshape(4 * qt, wk)
        s_ctx = _dot_nt(qstack, kc[:, sl])
        o = _softmax_pv([s_loc, s_ctx], [vw[:, sl], vc[:, sl]], extra=sink, fold_lanes=True)
        for pair in range(2):
            col = (kh * 2 + pair) * LANES
            o_ref[0, :, col:col + LANES] = jnp.concatenate(
                [o[(2 * pair) * qt:(2 * pair + 1) * qt], o[(2 * pair + 1) * qt:(2 * pair + 2) * qt]], axis=-1)


def _swa_attention(p_lat, p_ctx, cos2, sin2, qn, kn, sink_rows, n_q_heads, n_kv_heads):
    b, s, _ = p_lat.shape
    l = p_ctx.shape[1]
    kv_pairs = n_kv_heads // 2
    q_blocks = n_q_heads * HEAD_DIM // LANES
    q_per_pair = q_blocks // kv_pairs
    qw = q_per_pair * LANES
    tab = lambda kp, bb, n: (0, 0)
    n_steps = s // SWA_Q_TILE
    wk = SWA_Q_TILE + 2 * SWA_WINDOW
    far = []
    for step in (0, 1, n_steps - 1):
        q0 = step * SWA_Q_TILE
        ws = min(max(q0 - SWA_WINDOW, 0), s - wk)
        dist = np.abs((q0 + np.arange(SWA_Q_TILE))[:, None] - (ws + np.arange(wk))[None, :])
        far.append(np.where(dist <= SWA_WINDOW, 0.0, NEG_INF))
    far = jnp.asarray(np.stack(far), F32)

    def cls(n):
        return jnp.where(n == 0, 0, jnp.where(n == n_steps - 1, 2, 1))

    return pl.pallas_call(
        functools.partial(_swa_kernel, seq=s),
        grid=(kv_pairs, b, n_steps),
        in_specs=[pl.BlockSpec((1, SWA_Q_TILE, qw), lambda kp, bb, n: (bb, n, kp)),
                  pl.BlockSpec((1, s, LANES), lambda kp, bb, n: (bb, 0, q_blocks + kp)),
                  pl.BlockSpec((1, s, LANES), lambda kp, bb, n: (bb, 0, q_blocks + kv_pairs + kp)),
                  pl.BlockSpec((1, l, LANES), lambda kp, bb, n: (bb, 0, q_blocks + kp)),
                  pl.BlockSpec((1, l, LANES), lambda kp, bb, n: (bb, 0, q_blocks + kv_pairs + kp)),
                  pl.BlockSpec(cos2.shape, tab),
                  pl.BlockSpec(sin2.shape, tab),
                  pl.BlockSpec((1, LANES), tab),
                  pl.BlockSpec((1, LANES), tab),
                  pl.BlockSpec((8, LANES), lambda kp, bb, n: (kp, 0)),
                  pl.BlockSpec((1, SWA_Q_TILE, wk), lambda kp, bb, n: (cls(n), 0, 0))],
        out_specs=pl.BlockSpec((1, SWA_Q_TILE, qw), lambda kp, bb, n: (bb, n, kp)),
        out_shape=jax.ShapeDtypeStruct((b, s, n_q_heads * HEAD_DIM), F32),
        scratch_shapes=[pltpu.VMEM((s, LANES), F32), pltpu.VMEM((l, LANES), F32)],
        compiler_params=_params("arbitrary", "arbitrary", "arbitrary"),
        name="windowed_gqa",
    )(p_lat, p_lat, p_lat, p_ctx, p_ctx, cos2, sin2, qn, kn, sink_rows, far)


def _oddeven_merge_sort(n):
    pairs = []

    def merge(lo, size, r):
        step = r * 2
        if step < size:
            merge(lo, size, step)
            merge(lo + r, size, step)
            pairs.extend((i, i + r) for i in range(lo + r, lo + size - r, step))
        else:
            pairs.append((lo, lo + r))

    def sort(lo, size):
        if size > 1:
            sort(lo, size // 2)
            sort(lo + size // 2, size // 2)
            merge(lo, size, 1)

    sort(0, n)
    return pairs


def _top_rows(x, k, scr):
    tiles = [x[8 * i:8 * i + 8, :] for i in range(x.shape[0] // 8)]
    for i, j in _oddeven_merge_sort(len(tiles)):
        tiles[i], tiles[j] = jnp.maximum(tiles[i], tiles[j]), jnp.minimum(tiles[i], tiles[j])
    for r in range(k):
        m = jnp.max(tiles[0], axis=0, keepdims=True)
        scr[r:r + 1, :] = m
        need = k - r - 1
        if need > 0:
            hit = tiles[0] == m
            for d in range(min(need, len(tiles) - 1)):
                tiles[d] = jnp.where(hit, tiles[d + 1], tiles[d])
            if need >= len(tiles):
                tiles[-1] = jnp.where(hit, NEG_INF, tiles[-1])


def _bf16_pair(x):
    bits = lax.bitcast_convert_type(x.astype(BF16).astype(F32), jnp.uint32)
    return bits | (bits >> 16)


def _count_above(sorted_scr, y, strict):
    row = lambda i: sorted_scr[i:i + 1, :]
    above = (lambda r: r > y) if strict else (lambda r: r >= y)
    c8 = above(row(7))
    c4 = above(jnp.where(c8, row(11), row(3)))
    c2 = above(jnp.where(c8, jnp.where(c4, row(13), row(9)), jnp.where(c4, row(5), row(1))))
    hi = jnp.where(c4, jnp.where(c2, row(14), row(12)), jnp.where(c2, row(10), row(8)))
    lo = jnp.where(c4, jnp.where(c2, row(6), row(4)), jnp.where(c2, row(2), row(0)))
    c1 = above(jnp.where(c8, hi, lo))
    count = (jnp.where(c8, 8.0, 0.0) + jnp.where(c4, 4.0, 0.0)) + (jnp.where(c2, 2.0, 0.0) + jnp.where(c1, 1.0, 0.0))
    return count + jnp.where(above(row(15)), 1.0, 0.0)


def _peer_scores_kernel(h_ref, wq_ref, keys_ref, ra_ref, p1_ref, gb_ref, p2_ref,
                        q_scr, s_scr, a_scr, b_scr, c_scr, v_scr, *, n_heads):
    k = PEER_TOPK
    n_blocks = h_ref.shape[0] // LANES

    q = _dot(h_ref[...], wq_ref[...])
    for hp in range(2 * n_heads):
        q_scr[hp] = q[:, hp * PEER_KEYS:(hp + 1) * PEER_KEYS].astype(BF16)

    def lane_block(lb, h, slot):
        tops_a, tops_b, cand, tops_c = a_scr.at[slot], b_scr.at[slot], c_scr.at[slot], v_scr.at[slot]
        s1 = s_scr[0, lb]
        s2 = s_scr[1, lb]
        _top_rows(s1, k + 1, tops_a)
        _top_rows(s2, k + 1, tops_b)
        cand[0:16, :] = tops_a[0:1, :] + tops_b[0:16, :]
        for i in range(1, 8):
            cand[8 + 8 * i:16 + 8 * i, :] = tops_a[i:i + 1, :] + tops_b[0:8, :]
        cand[72:80, :] = tops_a[8:16, :] + tops_b[0:1, :]
        cand[80:81, :] = tops_a[0:1, :] + tops_b[16:17, :]
        cand[81:82, :] = tops_a[16:17, :] + tops_b[0:1, :]
        cand[82:PEER_KEYS, :] = jnp.full((PEER_KEYS - 82, LANES), NEG_INF, F32)
        _top_rows(cand[...], k + 1, tops_c)
        thr = 0.5 * (tops_c[k - 1:k, :] + tops_c[k:k + 1, :])
        z = jnp.sum(jnp.exp(tops_c[0:k, :] - tops_c[0:1, :]), axis=0, keepdims=True)
        ra_ref[lb, h] = _bf16_pair(_count_above(tops_a, s1, strict=True))
        p1_ref[lb, h] = _bf16_pair(jnp.exp(s1 - tops_a[0:1, :]) * (0.5 / z))
        gb_ref[lb, h] = _count_above(tops_a, thr - s2, strict=False).astype(BF16)
        p2_ref[lb, h] = jnp.exp(s2 - tops_b[0:1, :]).astype(BF16)

    def lane_pair(i, h):
        lane_block(2 * i, h, 0)
        lane_block(2 * i + 1, h, 1)
        return h

    def head(h, carry):
        for p in range(2):
            st = _dot_nt(keys_ref[2 * h + p], q_scr[2 * h + p])
            for lb in range(n_blocks):
                s_scr[p, lb] = st[:, lb * LANES:(lb + 1) * LANES]
        lax.fori_loop(0, n_blocks // 2, lane_pair, h)
        return carry

    lax.fori_loop(0, n_heads, head, 0)


def _peer_scores(h2, wq, keys, n_heads):
    t, d = h2.shape
    tt = PEER_TOKEN_TILE
    nb = tt // LANES
    tab_shape = (t // LANES, n_heads, PEER_KEYS, LANES)
    tab_spec = pl.BlockSpec((nb, n_heads, PEER_KEYS, LANES), lambda i: (i, 0, 0, 0))
    return pl.pallas_call(
        functools.partial(_peer_scores_kernel, n_heads=n_heads),
        grid=(t // tt,),
        in_specs=[pl.BlockSpec((tt, d), lambda i: (i, 0)),
                  pl.BlockSpec(wq.shape, lambda i: (0, 0)),
                  pl.BlockSpec(keys.shape, lambda i: (0, 0, 0))],
        out_specs=[tab_spec] * 4,
        out_shape=[jax.ShapeDtypeStruct(tab_shape, dt) for dt in (jnp.uint32, jnp.uint32, BF16, BF16)],
        scratch_shapes=[pltpu.VMEM((2 * n_heads, tt, PEER_KEYS), BF16),
                        pltpu.VMEM((2, nb, PEER_KEYS, LANES), F32),
                        pltpu.VMEM((2, 24, LANES), F32), pltpu.VMEM((2, 24, LANES), F32),
                        pltpu.VMEM((2, PEER_KEYS, LANES), F32), pltpu.VMEM((2, 24, LANES), F32)],
        compiler_params=_params("arbitrary"),
        name="peer_scores",
    )(h2, wq, keys)


def _peer_mix_kernel(h_ref, u_ref, vt_ref, ra_ref, p1_ref, gb_ref, p2_ref, x_ref, g_ref, o_ref,
                     acc_scr, act_scr, a_scr, *, n_heads, n_tiles):
    s = pl.program_id(1)
    tt = h_ref.shape[0]
    rows_per_tile = PEER_EXPERT_TILE // PEER_KEYS
    tile = (PEER_KEYS, LANES)

    def readout():
        acc_scr[...] += _dot(vt_ref[0], a_scr[...])

    def row_tile(ref, lb, h, aa):
        words = jnp.broadcast_to(ref[lb, h, aa:aa + 1, :], (8, LANES))
        packed = pltpu.bitcast(words, BF16)
        return jnp.broadcast_to(packed[None], (PEER_KEYS // 16, 16, LANES)).reshape(tile)

    def gates():
        for aa in range(rows_per_tile):
            rs = slice(aa * PEER_KEYS, (aa + 1) * PEER_KEYS)
            for lb in range(tt // LANES):
                ls = slice(lb * LANES, (lb + 1) * LANES)
                w = None
                for h in range(n_heads):
                    rank = row_tile(ra_ref, lb, h, aa)
                    p1 = row_tile(p1_ref, lb, h, aa)
                    term = jnp.where(rank < gb_ref[lb, h], p2_ref[lb, h], jnp.zeros(tile, BF16)) * p1
                    w = term if w is None else w + term
                act = act_scr[rs, ls]
                gelu2 = act * (1.0 + lax.erf(act * INV_SQRT2))
                a_scr[rs, ls] = w * gelu2.astype(BF16)

    def experts():
        act_scr[...] = _dot_nt(u_ref[0].astype(BF16), h_ref[...])

    @pl.when(s == 0)
    def _():
        acc_scr[...] = jnp.zeros_like(acc_scr)

    i = pl.program_id(0)

    @pl.when(s < n_tiles)
    def _():
        experts()

    @pl.when(i >= 0)
    def _():
        gates()

    @pl.when(s + i >= 0)
    def _():
        readout()

    @pl.when(s == n_tiles - 1)
    def _():
        o_ref[...] = x_ref[...] + g_ref[0] * acc_scr[...].T


def _peer_mix(h2, u_all, layer, vt, tables, x, gate, seg, n_heads):
    t, d = h2.shape
    n_exp = u_all.shape[1]
    tt = min(PEER_MIX_TOKEN_TILE, seg)
    te = PEER_EXPERT_TILE
    n_tiles = n_exp // te
    per_seg = seg // tt
    r = gate.shape[0]
    nb = tt // LANES
    tab_spec = pl.BlockSpec((nb, n_heads, PEER_KEYS, LANES), lambda i, s: (i, 0, 0, 0))
    row_spec = pl.BlockSpec((nb, n_heads, te // PEER_KEYS, LANES), lambda i, s: (i, 0, s, 0))
    return pl.pallas_call(
        functools.partial(_peer_mix_kernel, n_heads=n_heads, n_tiles=n_tiles),
        grid=(t // tt, n_tiles),
        in_specs=[pl.BlockSpec((tt, d), lambda i, s: (i, 0)),
                  pl.BlockSpec((1, te, d), lambda i, s: (layer, s, 0)),
                  pl.BlockSpec((1, d, te), lambda i, s: (s, 0, 0)),
                  row_spec, row_spec, tab_spec, tab_spec,
                  pl.BlockSpec((tt, d), lambda i, s: (i, 0)),
                  pl.BlockSpec((1, 1, d), lambda i, s: (i // per_seg, 0, 0))],
        out_specs=pl.BlockSpec((tt, d), lambda i, s: (i, 0)),
        out_shape=jax.ShapeDtypeStruct((t, d), F32),
        scratch_shapes=[pltpu.VMEM((d, tt), F32), pltpu.VMEM((te, tt), F32), pltpu.VMEM((te, tt), BF16)],
        compiler_params=_params("arbitrary", "arbitrary"),
        name="peer_mix",
    )(h2, u_all, vt, *tables, x, gate.reshape(r, 1, d))


def _peer(h2, x, gate, seg, wq, keys, u_all, layer, vt):
    n_heads = keys.shape[0] // 2
    tables = _peer_scores(h2, wq, keys, n_heads)
    return _peer_mix(h2, u_all, layer, vt, tables, x, gate, seg, n_heads)


def _rope_tables(n_tokens):
    t = jnp.arange(n_tokens)
    row = (t // GRID_W).astype(F32)
    col = (t % GRID_W).astype(F32)
    n_freq = HEAD_DIM // 4
    inv_freq = jnp.power(ROPE_BASE, -jnp.arange(n_freq, dtype=F32) / n_freq)
    ang = jnp.concatenate([row[:, None] * inv_freq, col[:, None] * inv_freq], axis=-1)
    cos, sin = jnp.cos(ang), jnp.sin(ang)
    cos2 = jnp.tile(jnp.concatenate([cos, cos], axis=-1), (1, 2))
    sin2 = jnp.tile(jnp.concatenate([-sin, sin], axis=-1), (1, 2))
    return cos2, sin2


def _pair_lanes(v):
    return jnp.repeat(v, HEAD_DIM, axis=-1).reshape(*v.shape[:-1], v.shape[-1] // 2, LANES)


def kernel(x, c, ctx, c_ctx, mod_w, mod_b, norm1_g, norm2_g, ab_w_in, ab_w_out, na_q_norm, na_k_norm,
           na_rpb, ret_log_decay, swa_w_in, swa_w_out, swa_q_norm, swa_k_norm, swa_sink,
           peer_w_q, peer_sub_keys, peer_u, peer_v):
    b, s, d = x.shape
    l = ctx.shape[1]
    depth = mod_w.shape[0]
    assert depth == 2 and b + 1 <= 8
    assert s % (NA_Q_ROWS * GRID_W) == 0 and s // GRID_W >= NA_BAND_ROWS
    assert s % TOKEN_TILE == 0 and (b * l) % TOKEN_TILE == 0 and l % RET_CHUNK == 0

    cond = jnp.concatenate([c, c_ctx[None, :], jnp.zeros((8 - b - 1, d), F32)], axis=0)
    mods = _adaln(cond, mod_w, mod_b).reshape(depth, 8, 6, d)
    lat = lambda layer, which: mods[layer, :b, which]
    cx = lambda layer, which: mods[layer, b:b + 1, which]

    cos2, sin2 = _rope_tables(s)
    x_lat = x.reshape(b * s, d)
    x_ctx = ctx.reshape(b * l, d)
    tile2 = lambda g: jnp.tile(g, 2).reshape(1, LANES)

    def peer_weights(layer):
        n_heads = peer_sub_keys.shape[1]
        keys = peer_sub_keys[layer].reshape(2 * n_heads, PEER_KEYS, -1).astype(BF16)
        vt = peer_v[layer].reshape(-1, PEER_EXPERT_TILE, d).transpose(0, 2, 1).astype(BF16)
        return peer_w_q[layer].astype(BF16), keys, peer_u, layer, vt

    n_na = na_rpb.shape[1]
    n_ret = ret_log_decay.shape[2]
    na_pairs, ret_pairs = n_na // 2, n_ret // 2
    w_in = ab_w_in[0].astype(BF16)
    w_out = ab_w_out[0].astype(BF16)
    wa = n_na * HEAD_DIM
    p_lat = _modmm(x_lat, norm1_g[0], lat(0, 0), lat(0, 1), w_in, s).reshape(b, s, -1)
    p_ctx = _modmm(x_ctx, norm1_g[0], cx(0, 0), cx(0, 1), w_in, b * l).reshape(b, l, -1)

    qn, kn = tile2(na_q_norm[0]), tile2(na_k_norm[0])
    oa_lat = _na_attention(p_lat, p_ctx, _na_bias(na_rpb[0], s // GRID_W), qn, kn, na_pairs)
    oa_ctx = _ctx_attention(p_ctx, qn, kn, na_pairs)

    lg = _pair_lanes(ret_log_decay[0]).transpose(1, 0, 2)
    ret_col0 = 3 * wa // LANES
    zeros_state = jnp.zeros((b, ret_pairs, 4, HEAD_DIM, HEAD_DIM), F32)
    ones_tab, zeros_tab = jnp.ones((l, LANES), F32), jnp.zeros((l, LANES), F32)
    ob_ctx, st_ctx = _retention(p_ctx, ones_tab, zeros_tab, lg, zeros_state, ret_col0, ret_pairs, False)
    ob_lat, _ = _retention(p_lat, cos2, sin2, lg, st_ctx, ret_col0, ret_pairs, True)

    w_list = [w_out[:wa], w_out[wa:]]
    x_lat, h_lat = _outproj([oa_lat.reshape(b * s, -1), ob_lat.reshape(b * s, -1)], w_list, x_lat,
                            lat(0, 2), norm2_g[0], lat(0, 3), lat(0, 4), s)
    x_ctx, h_ctx = _outproj([oa_ctx.reshape(b * l, -1), ob_ctx.reshape(b * l, -1)], w_list, x_ctx,
                            cx(0, 2), norm2_g[0], cx(0, 3), cx(0, 4), b * l)
    pw = peer_weights(0)
    x_lat = _peer(h_lat, x_lat, lat(0, 5), s, *pw)
    x_ctx = _peer(h_ctx, x_ctx, cx(0, 5), b * l, *pw)

    n_q = swa_sink.shape[1]
    n_kv = (swa_w_in.shape[2] // HEAD_DIM - n_q) // 2
    w_in = swa_w_in[0].astype(BF16)
    p_lat = _modmm(x_lat, norm1_g[1], lat(1, 0), lat(1, 1), w_in, s).reshape(b, s, -1)
    p_ctx = _modmm(x_ctx, norm1_g[1], cx(1, 0), cx(1, 1), w_in, b * l).reshape(b, l, -1)
    sink_rows = jnp.broadcast_to(swa_sink[0][:, None], (n_q, LANES))
    o_lat = _swa_attention(p_lat, p_ctx, cos2, sin2, tile2(swa_q_norm[0]), tile2(swa_k_norm[0]),
                           sink_rows, n_q, n_kv)
    x_lat, h_lat = _outproj([o_lat.reshape(b * s, -1)], [swa_w_out[0].astype(BF16)], x_lat,
                            lat(1, 2), norm2_g[1], lat(1, 3), lat(1, 4), s)
    x_lat = _peer(h_lat, x_lat, lat(1, 5), s, *peer_weights(1))
    return x_lat.reshape(b, s, d)
```

```python
import functools

import numpy as np
import jax
import jax.numpy as jnp
from jax import lax
from jax.experimental import pallas as pl
from jax.experimental.pallas import tpu as pltpu

F32 = jnp.float32
BF16 = jnp.bfloat16

HEAD_DIM = 64
GRID_W = 64
NA_ROWS = 8
NA_COLS = 16
SWA_WINDOW = 128
PEER_TOPK = 16
PEER_KEYS = 128
ROPE_BASE = 10000.0
NORM_EPS = 1e-6
GN_EPS = 1e-5
NEG_INF = -1e30
ATTN_SCALE = HEAD_DIM ** -0.5
INV_SQRT2 = 0.7071067811865476

LANES = 128
VMEM_LIMIT = 56 * 1024 * 1024

TOKEN_TILE = 512
PROJ_TOKEN_TILE = 1024
PROJ_N_TILE = 1792
RET_CHUNK = 256
NA_Q_ROWS = 8
NA_BAND_ROWS = 16
SWA_Q_TILE = 256
PEER_TOKEN_TILE = 512
PEER_MIX_TOKEN_TILE = 1024
PEER_EXPERT_TILE = 1024


def _dot(a, b):
    return jnp.dot(a, b, preferred_element_type=F32)


def _dot_nt(a, b):
    return lax.dot_general(a, b, (((1,), (1,)), ((), ())), preferred_element_type=F32)


def _dot_tn(a, b):
    return lax.dot_general(a, b, (((0,), (0,)), ((), ())), preferred_element_type=F32)


def _params(*sem):
    return pltpu.CompilerParams(dimension_semantics=sem, vmem_limit_bytes=VMEM_LIMIT)


def _rms_rows(x, gain):
    ms = jnp.mean(x * x, axis=-1, keepdims=True)
    return x * lax.rsqrt(ms + NORM_EPS) * gain


def _modulate(x, gain, shift, scale):
    return _rms_rows(x, gain) * (1.0 + scale) + shift


def _head_rms(x, gain):
    lane = lax.broadcasted_iota(jnp.int32, x.shape, 1)
    lo = lane < HEAD_DIM
    ss = x * x
    s_lo = jnp.sum(jnp.where(lo, ss, 0.0), axis=-1, keepdims=True)
    s_hi = jnp.sum(jnp.where(lo, 0.0, ss), axis=-1, keepdims=True)
    ms = jnp.where(lo, s_lo, s_hi) * (1.0 / HEAD_DIM)
    return x * lax.rsqrt(ms + NORM_EPS) * gain


def _rope(x, cos2, sin2):
    lane = lax.broadcasted_iota(jnp.int32, x.shape, 1)
    first_half = (lane & (HEAD_DIM // 2)) == 0
    swapped = jnp.where(first_half, pltpu.roll(x, LANES - HEAD_DIM // 2, axis=1),
                        pltpu.roll(x, HEAD_DIM // 2, axis=1))
    return x * cos2 + swapped * sin2


def _adaln_kernel(c_ref, w_ref, b_ref, o_ref):
    c = c_ref[...]
    s = c * jax.nn.sigmoid(c)
    w = w_ref[0]
    s_hi = s.astype(BF16)
    s_lo = (s - s_hi.astype(F32)).astype(BF16)
    w_hi = w.astype(BF16)
    w_lo = (w - w_hi.astype(F32)).astype(BF16)
    acc = _dot(s_hi, w_hi) + _dot(s_lo, w_hi) + _dot(s_hi, w_lo)
    o_ref[0] = acc + b_ref[0]


def _adaln(cond, mod_w, mod_b):
    depth, d, n = mod_w.shape
    tn = n // 4
    return pl.pallas_call(
        _adaln_kernel,
        grid=(depth, n // tn),
        in_specs=[pl.BlockSpec((8, d), lambda l, j: (0, 0)),
                  pl.BlockSpec((1, d, tn), lambda l, j: (l, 0, j)),
                  pl.BlockSpec((1, 1, tn), lambda l, j: (l, 0, j))],
        out_specs=pl.BlockSpec((1, 8, tn), lambda l, j: (l, 0, j)),
        out_shape=jax.ShapeDtypeStruct((depth, 8, n), F32),
        compiler_params=_params("arbitrary", "arbitrary"),
        name="adaln",
    )(cond, mod_w, mod_b.reshape(depth, 1, n))


def _modmm_kernel(x_ref, g_ref, sh_ref, sc_ref, w_ref, o_ref, h_scr):
    @pl.when(pl.program_id(1) == 0)
    def _():
        h_scr[...] = _modulate(x_ref[...], g_ref[...], sh_ref[0], sc_ref[0]).astype(BF16)

    o_ref[...] = _dot(h_scr[...], w_ref[...])


def _modmm(x, gain, shift, scale, w, seg):
    t, d = x.shape
    n = w.shape[1]
    tm = min(PROJ_TOKEN_TILE, seg)
    tn = n if n <= PROJ_N_TILE else n // 2
    per_seg = seg // tm
    r = shift.shape[0]
    return pl.pallas_call(
        _modmm_kernel,
        grid=(t // tm, n // tn),
        in_specs=[pl.BlockSpec((tm, d), lambda i, j: (i, 0)),
                  pl.BlockSpec((1, d), lambda i, j: (0, 0)),
                  pl.BlockSpec((1, 1, d), lambda i, j: (i // per_seg, 0, 0)),
                  pl.BlockSpec((1, 1, d), lambda i, j: (i // per_seg, 0, 0)),
                  pl.BlockSpec((d, tn), lambda i, j: (0, j))],
        out_specs=pl.BlockSpec((tm, tn), lambda i, j: (i, j)),
        out_shape=jax.ShapeDtypeStruct((t, n), F32),
        scratch_shapes=[pltpu.VMEM((tm, d), BF16)],
        compiler_params=_params("arbitrary", "arbitrary"),
        name="modulate_matmul",
    )(x, gain.reshape(1, d), shift.reshape(r, 1, d), scale.reshape(r, 1, d), w)


def _outproj_kernel(*refs, n_in):
    a_refs = refs[:n_in]
    w_refs = refs[n_in:2 * n_in]
    x_ref, gate_ref, g2_ref, sh_ref, sc_ref, xo_ref, h_ref = refs[2 * n_in:]
    acc = None
    for a_ref, w_ref in zip(a_refs, w_refs):
        part = _dot(a_ref[...].astype(BF16), w_ref[...])
        acc = part if acc is None else acc + part
    xn = x_ref[...] + gate_ref[0] * acc
    xo_ref[...] = xn
    h_ref[...] = _modulate(xn, g2_ref[...], sh_ref[0], sc_ref[0]).astype(BF16)


def _outproj(a_list, w_list, x, gate, gain2, shift2, scale2, seg):
    t, d = x.shape
    tm = min(TOKEN_TILE, seg)
    per_seg = seg // tm
    r = gate.shape[0]
    n_in = len(a_list)
    row = lambda i: (i // per_seg, 0, 0)
    in_specs = ([pl.BlockSpec((tm, a.shape[1]), lambda i: (i, 0)) for a in a_list]
                + [pl.BlockSpec(w.shape, lambda i: (0, 0)) for w in w_list]
                + [pl.BlockSpec((tm, d), lambda i: (i, 0)),
                   pl.BlockSpec((1, 1, d), row),
                   pl.BlockSpec((1, d), lambda i: (0, 0)),
                   pl.BlockSpec((1, 1, d), row),
                   pl.BlockSpec((1, 1, d), row)])
    return pl.pallas_call(
        functools.partial(_outproj_kernel, n_in=n_in),
        grid=(t // tm,),
        in_specs=in_specs,
        out_specs=[pl.BlockSpec((tm, d), lambda i: (i, 0)), pl.BlockSpec((tm, d), lambda i: (i, 0))],
        out_shape=[jax.ShapeDtypeStruct((t, d), F32), jax.ShapeDtypeStruct((t, d), BF16)],
        compiler_params=_params("arbitrary"),
        name="out_proj_residual",
    )(*a_list, *w_list, x, gate.reshape(r, 1, d), gain2.reshape(1, d),
      shift2.reshape(r, 1, d), scale2.reshape(r, 1, d))


def _softmax_pv(s_list, v_list, extra=None, fold_lanes=False):
    def lane_chunks(x):
        return [x[:, c:c + LANES] for c in range(0, x.shape[1], LANES)] if fold_lanes else [x]

    def row_reduce(blocks, op, reduce):
        acc = None
        for blk in blocks:
            chunks = lane_chunks(blk)
            part = chunks[0]
            for c in chunks[1:]:
                part = op(part, c)
            if not fold_lanes:
                part = reduce(part, axis=-1, keepdims=True)
            acc = part if acc is None else op(acc, part)
        return reduce(acc, axis=-1, keepdims=True) if fold_lanes else acc

    m = row_reduce(s_list, jnp.maximum, jnp.max)
    if extra is not None:
        m = jnp.maximum(m, extra)
    ps = []
    out = None
    for s, v in zip(s_list, v_list):
        p = jnp.exp(s - m)
        ps.append(p)
        pv = _dot(p.astype(BF16), v)
        out = pv if out is None else out + pv
    denom = row_reduce(ps, jnp.add, jnp.sum)
    if extra is not None:
        denom = denom + jnp.exp(extra - m)
    return out / denom


def _na_kernel(q_ref, k_ref, v_ref, kc_ref, vc_ref, bias_ref, qn_ref, kn_ref, o_ref,
               k_scr, v_scr, kc_scr, vc_scr, *, n_steps):
    i = pl.program_id(2)
    band = NA_BAND_ROWS * GRID_W
    start = jnp.clip(NA_Q_ROWS * i - NA_ROWS // 2, 0, NA_Q_ROWS * n_steps - NA_BAND_ROWS) * GRID_W
    start = pl.multiple_of(start, NA_ROWS // 2 * GRID_W)

    @pl.when(i == 0)
    def _():
        k_scr[...] = _head_rms(k_ref[0], kn_ref[...]).astype(BF16)
        v_scr[...] = v_ref[0].astype(BF16)
        kc_scr[...] = _head_rms(kc_ref[0], kn_ref[...]).astype(BF16)
        vc_scr[...] = vc_ref[0].astype(BF16)

    q = (_head_rms(q_ref[0], qn_ref[...]) * ATTN_SCALE).astype(BF16)
    kb = k_scr[pl.ds(start, band), :]
    vb = v_scr[pl.ds(start, band), :]
    kc = kc_scr[...]
    vc = vc_scr[...]
    outs = []
    for h in range(2):
        sl = slice(h * HEAD_DIM, (h + 1) * HEAD_DIM)
        s_loc = _dot_nt(q[:, sl], kb[:, sl]) + bias_ref[0, h]
        s_ctx = _dot_nt(q[:, sl], kc[:, sl])
        outs.append(_softmax_pv([s_loc, s_ctx], [vb[:, sl], vc[:, sl]]))
    o_ref[0] = jnp.concatenate(outs, axis=-1).astype(o_ref.dtype)


def _na_bias(rpb, rows):
    h = rpb.shape[0]
    col = np.arange(GRID_W)
    c0 = np.clip(col - NA_COLS // 2, 0, GRID_W - NA_COLS)
    dc = col[None, :] - col[:, None] + (NA_COLS - 1)
    ok_c = (col[None, :] >= c0[:, None]) & (col[None, :] < c0[:, None] + NA_COLS)
    pick = ((np.arange(2 * NA_COLS - 1)[:, None, None] == dc[None]) & ok_c[None]).astype(np.float32)
    blocks = jnp.einsum('hrd,dqk->hrqk', rpb, pick, precision=lax.Precision.HIGHEST)
    blocks = jnp.where(ok_c, blocks, NEG_INF)
    qt, band = NA_Q_ROWS * GRID_W, NA_BAND_ROWS * GRID_W
    return pl.pallas_call(
        functools.partial(_na_bias_kernel, rows=rows),
        grid=(3, h),
        in_specs=[pl.BlockSpec((1,) + blocks.shape[1:], lambda c, hh: (hh, 0, 0, 0))],
        out_specs=pl.BlockSpec((1, 1, qt, band), lambda c, hh: (c, hh, 0, 0)),
        out_shape=jax.ShapeDtypeStruct((3, h, qt, band), F32),
        compiler_params=_params("arbitrary", "arbitrary"),
        name="na_bias_tables",
    )(blocks)


def _na_bias_kernel(blk_ref, o_ref, *, rows):
    n_steps = rows // NA_Q_ROWS
    for c, step in enumerate((0, 1, n_steps - 1)):
        @pl.when(pl.program_id(0) == c)
        def _(step=step):
            o_ref[0, 0] = jnp.full(o_ref.shape[2:], NEG_INF, F32)
            band0 = min(max(NA_Q_ROWS * step - NA_ROWS // 2, 0), rows - NA_BAND_ROWS)
            for rq in range(NA_Q_ROWS):
                r = NA_Q_ROWS * step + rq
                r0 = min(max(r - NA_ROWS // 2, 0), rows - NA_ROWS)
                for rk in range(r0, r0 + NA_ROWS):
                    col0 = (rk - band0) * GRID_W
                    o_ref[0, 0, rq * GRID_W:(rq + 1) * GRID_W, col0:col0 + GRID_W] = (
                        blk_ref[0, rk - r + NA_ROWS - 1])


def _na_attention(p_lat, p_ctx, bias, qn, kn, n_pairs):
    b, s, _ = p_lat.shape
    l = p_ctx.shape[1]
    rows = s // GRID_W
    n_steps = rows // NA_Q_ROWS
    qt = NA_Q_ROWS * GRID_W

    def cls(i):
        return jnp.where(i == 0, 0, jnp.where(i == n_steps - 1, 2, 1))

    return pl.pallas_call(
        functools.partial(_na_kernel, n_steps=n_steps),
        grid=(n_pairs, b, n_steps),
        in_specs=[pl.BlockSpec((1, qt, LANES), lambda hp, bb, i: (bb, i, hp)),
                  pl.BlockSpec((1, s, LANES), lambda hp, bb, i: (bb, 0, n_pairs + hp)),
                  pl.BlockSpec((1, s, LANES), lambda hp, bb, i: (bb, 0, 2 * n_pairs + hp)),
                  pl.BlockSpec((1, l, LANES), lambda hp, bb, i: (bb, 0, n_pairs + hp)),
                  pl.BlockSpec((1, l, LANES), lambda hp, bb, i: (bb, 0, 2 * n_pairs + hp)),
                  pl.BlockSpec((1, 2, qt, NA_BAND_ROWS * GRID_W), lambda hp, bb, i: (cls(i), hp, 0, 0)),
                  pl.BlockSpec((1, LANES), lambda hp, bb, i: (0, 0)),
                  pl.BlockSpec((1, LANES), lambda hp, bb, i: (0, 0))],
        out_specs=pl.BlockSpec((1, qt, LANES), lambda hp, bb, i: (bb, i, hp)),
        out_shape=jax.ShapeDtypeStruct((b, s, n_pairs * LANES), BF16),
        scratch_shapes=[pltpu.VMEM((s, LANES), BF16), pltpu.VMEM((s, LANES), BF16),
                        pltpu.VMEM((l, LANES), BF16), pltpu.VMEM((l, LANES), BF16)],
        compiler_params=_params("arbitrary", "arbitrary", "arbitrary"),
        name="neighbourhood_attention",
    )(p_lat, p_lat, p_lat, p_ctx, p_ctx, bias, qn, kn)


def _ctx_attn_kernel(q_ref, k_ref, v_ref, qn_ref, kn_ref, o_ref):
    q = (_head_rms(q_ref[0], qn_ref[...]) * ATTN_SCALE).astype(BF16)
    k = _head_rms(k_ref[0], kn_ref[...]).astype(BF16)
    v = v_ref[0].astype(BF16)
    outs = []
    for h in range(2):
        sl = slice(h * HEAD_DIM, (h + 1) * HEAD_DIM)
        outs.append(_softmax_pv([_dot_nt(q[:, sl], k[:, sl])], [v[:, sl]]))
    o_ref[0] = jnp.concatenate(outs, axis=-1).astype(o_ref.dtype)


def _ctx_attention(p_ctx, qn, kn, n_pairs):
    b, l, _ = p_ctx.shape
    return pl.pallas_call(
        _ctx_attn_kernel,
        grid=(n_pairs, b),
        in_specs=[pl.BlockSpec((1, l, LANES), lambda hp, bb: (bb, 0, hp)),
                  pl.BlockSpec((1, l, LANES), lambda hp, bb: (bb, 0, n_pairs + hp)),
                  pl.BlockSpec((1, l, LANES), lambda hp, bb: (bb, 0, 2 * n_pairs + hp)),
                  pl.BlockSpec((1, LANES), lambda hp, bb: (0, 0)),
                  pl.BlockSpec((1, LANES), lambda hp, bb: (0, 0))],
        out_specs=pl.BlockSpec((1, l, LANES), lambda hp, bb: (bb, 0, hp)),
        out_shape=jax.ShapeDtypeStruct((b, l, n_pairs * LANES), BF16),
        compiler_params=_params("arbitrary", "arbitrary"),
        name="context_attention",
    )(p_ctx, p_ctx, p_ctx, qn, kn)


def _ret_kernel(q_ref, k_ref, v_ref, g_ref, cos_ref, sin_ref, lg_ref, s0_ref, y_ref, st_ref, sf_scr,
                *, n_chunks, use_rope):
    c = RET_CHUNK
    hd = HEAD_DIM
    lg = -jnp.exp(lg_ref[0])
    lgf, lgb = lg[0:1, :], lg[1:2, :]
    ii = lax.broadcasted_iota(jnp.int32, (c, LANES), 0).astype(F32)
    dq_f = jnp.exp(lgf * (ii + 1.0))
    dk_f = jnp.exp(lgf * (c - 1.0 - ii))
    dq_b = jnp.exp(lgb * (c - ii))
    dk_b = jnp.exp(lgb * ii)
    dc_f = jnp.exp(lgf * float(c))
    dc_b = jnp.exp(lgb * float(c))
    diff = (lax.broadcasted_iota(jnp.int32, (c, c), 0) - lax.broadcasted_iota(jnp.int32, (c, c), 1)).astype(F32)
    intra = []
    chunk_f = []
    chunk_b = []
    for h in range(2):
        lf = lgf[:, h * hd:h * hd + 1]
        lb = lgb[:, h * hd:h * hd + 1]
        intra.append(jnp.where(diff >= 0, jnp.exp(lf * jnp.maximum(diff, 0.0)),
                               jnp.exp(lb * jnp.maximum(-diff, 0.0))))
        chunk_f.append(dc_f[:, h * hd:h * hd + 1])
        chunk_b.append(dc_b[:, h * hd:h * hd + 1])

    def load(n):
        r = pl.multiple_of(n * c, c)
        q = q_ref[0, pl.ds(r, c), :]
        k = k_ref[0, pl.ds(r, c), :]
        v = v_ref[0, pl.ds(r, c), :]
        if use_rope:
            cs = cos_ref[pl.ds(r, c), :]
            sn = sin_ref[pl.ds(r, c), :]
            q = _rope(q, cs, sn)
            k = _rope(k, cs, sn)
        return r, q * ATTN_SCALE, k, v.astype(BF16)

    def fwd(n, carry):
        _, _, k, v = load(n)
        kd = (k * dk_f).astype(BF16)
        new = []
        for h in range(2):
            sl = slice(h * hd, (h + 1) * hd)
            sf_scr[n, h] = carry[h]
            new.append(carry[h] * chunk_f[h] + _dot_tn(kd[:, sl], v[:, sl]))
        return tuple(new)

    sf = lax.fori_loop(0, n_chunks, fwd, (s0_ref[0, 0, 0], s0_ref[0, 0, 1]), unroll=2)
    st_ref[0, 0, 0] = sf[0]
    st_ref[0, 0, 1] = sf[1]

    def bwd(jj, carry):
        n = n_chunks - 1 - jj
        r, q, k, v = load(n)
        qb = q.astype(BF16)
        kb = k.astype(BF16)
        qf = (q * dq_f).astype(BF16)
        qr = (q * dq_b).astype(BF16)
        kd = (k * dk_b).astype(BF16)
        outs = []
        new = []
        for h in range(2):
            sl = slice(h * hd, (h + 1) * hd)
            a = (_dot_nt(qb[:, sl], kb[:, sl]) * intra[h]).astype(BF16)
            o = (_dot(a, v[:, sl]) + _dot(qf[:, sl], sf_scr[n, h].astype(BF16))
                 + _dot(qr[:, sl], carry[h].astype(BF16)))
            oc = o - jnp.mean(o, axis=-1, keepdims=True)
            outs.append(oc * lax.rsqrt(jnp.mean(oc * oc, axis=-1, keepdims=True) + GN_EPS))
            new.append(carry[h] * chunk_b[h] + _dot_tn(kd[:, sl], v[:, sl]))
        g = g_ref[0, pl.ds(r, c), :]
        y_ref[0, pl.ds(r, c), :] = (jnp.concatenate(outs, axis=-1) * (g * jax.nn.sigmoid(g))).astype(y_ref.dtype)
        return tuple(new)

    sb = lax.fori_loop(0, n_chunks, bwd, (s0_ref[0, 0, 2], s0_ref[0, 0, 3]), unroll=2)
    st_ref[0, 0, 2] = sb[0]
    st_ref[0, 0, 3] = sb[1]


def _retention(p, cos2, sin2, lg, s0, col0, n_pairs, use_rope):
    b, t, _ = p.shape
    n_chunks = t // RET_CHUNK
    tab = lambda hp, bb: (0, 0)
    return pl.pallas_call(
        functools.partial(_ret_kernel, n_chunks=n_chunks, use_rope=use_rope),
        grid=(n_pairs, b),
        in_specs=[pl.BlockSpec((1, t, LANES), lambda hp, bb: (bb, 0, col0 + hp)),
                  pl.BlockSpec((1, t, LANES), lambda hp, bb: (bb, 0, col0 + n_pairs + hp)),
                  pl.BlockSpec((1, t, LANES), lambda hp, bb: (bb, 0, col0 + 2 * n_pairs + hp)),
                  pl.BlockSpec((1, t, LANES), lambda hp, bb: (bb, 0, col0 + 3 * n_pairs + hp)),
                  pl.BlockSpec(cos2.shape, tab),
                  pl.BlockSpec(sin2.shape, tab),
                  pl.BlockSpec((1, 2, LANES), lambda hp, bb: (hp, 0, 0)),
                  pl.BlockSpec((1, 1, 4, HEAD_DIM, HEAD_DIM), lambda hp, bb: (bb, hp, 0, 0, 0))],
        out_specs=[pl.BlockSpec((1, t, LANES), lambda hp, bb: (bb, 0, hp)),
                   pl.BlockSpec((1, 1, 4, HEAD_DIM, HEAD_DIM), lambda hp, bb: (bb, hp, 0, 0, 0))],
        out_shape=[jax.ShapeDtypeStruct((b, t, n_pairs * LANES), BF16),
                   jax.ShapeDtypeStruct((b, n_pairs, 4, HEAD_DIM, HEAD_DIM), F32)],
        scratch_shapes=[pltpu.VMEM((n_chunks, 2, HEAD_DIM, HEAD_DIM), F32)],
        compiler_params=_params("arbitrary", "arbitrary"),
        name="retention",
    )(p, p, p, p, cos2, sin2, lg, s0)


def _swa_kernel(q_ref, k_ref, v_ref, kc_ref, vc_ref, cos_ref, sin_ref, qn_ref, kn_ref, sink_ref, far_ref,
                o_ref, k_scr, kc_scr, *, seq):
    qt = SWA_Q_TILE
    wk = qt + 2 * SWA_WINDOW
    hd = HEAD_DIM
    n = pl.program_id(2)
    q0 = pl.multiple_of(n * qt, qt)
    ws = pl.multiple_of(jnp.clip(n * qt - SWA_WINDOW, 0, seq - wk), SWA_WINDOW)

    @pl.when(n == 0)
    def _():
        k_scr[...] = _rope(_head_rms(k_ref[0], kn_ref[...]), cos_ref[...], sin_ref[...])
        kc_scr[...] = _head_rms(kc_ref[0], kn_ref[...])

    kw = k_scr[pl.ds(ws, wk), :].astype(BF16)
    vw = v_ref[0, pl.ds(ws, wk), :].astype(BF16)
    kc = kc_scr[...].astype(BF16)
    vc = vc_ref[0].astype(BF16)
    cos_q = cos_ref[pl.ds(q0, qt), :]
    sin_q = sin_ref[pl.ds(q0, qt), :]
    qs = []
    for s in range(4):
        slab = q_ref[0, :, s * LANES:(s + 1) * LANES]
        qs.append((_rope(_head_rms(slab, qn_ref[...]), cos_q, sin_q) * ATTN_SCALE).astype(BF16))
    far = far_ref[0][None]
    for kh in range(2):
        sl = slice(kh * hd, (kh + 1) * hd)
        qstack = jnp.concatenate(
            [qs[kh * 2 + g // 2][:, (g % 2) * hd:(g % 2 + 1) * hd] for g in range(4)], axis=0)
        sink = jnp.concatenate(
            [jnp.broadcast_to(sink_ref[kh * 4 + g:kh * 4 + g + 1, 0:1], (qt, 1)) for g in range(4)], axis=0)
        s_loc = (_dot_nt(qstack, kw[:, sl]).reshape(4, qt, wk) + far).reshape(4 * qt, wk)
        s_ctx = _dot_nt(qstack, kc[:, sl])
        o = _softmax_pv([s_loc, s_ctx], [vw[:, sl], vc[:, sl]], extra=sink, fold_lanes=True)
        for pair in range(2):
            col = (kh * 2 + pair) * LANES
            o_ref[0, :, col:col + LANES] = jnp.concatenate(
                [o[(2 * pair) * qt:(2 * pair + 1) * qt], o[(2 * pair + 1) * qt:(2 * pair + 2) * qt]],
                axis=-1).astype(o_ref.dtype)


def _swa_attention(p_lat, p_ctx, cos2, sin2, qn, kn, sink_rows, n_q_heads, n_kv_heads):
    b, s, _ = p_lat.shape
    l = p_ctx.shape[1]
    kv_pairs = n_kv_heads // 2
    q_blocks = n_q_heads * HEAD_DIM // LANES
    q_per_pair = q_blocks // kv_pairs
    qw = q_per_pair * LANES
    tab = lambda kp, bb, n: (0, 0)
    n_steps = s // SWA_Q_TILE
    wk = SWA_Q_TILE + 2 * SWA_WINDOW
    far = []
    for step in (0, 1, n_steps - 1):
        q0 = step * SWA_Q_TILE
        ws = min(max(q0 - SWA_WINDOW, 0), s - wk)
        dist = np.abs((q0 + np.arange(SWA_Q_TILE))[:, None] - (ws + np.arange(wk))[None, :])
        far.append(np.where(dist <= SWA_WINDOW, 0.0, NEG_INF))
    far = jnp.asarray(np.stack(far), F32)

    def cls(n):
        return jnp.where(n == 0, 0, jnp.where(n == n_steps - 1, 2, 1))

    return pl.pallas_call(
        functools.partial(_swa_kernel, seq=s),
        grid=(kv_pairs, b, n_steps),
        in_specs=[pl.BlockSpec((1, SWA_Q_TILE, qw), lambda kp, bb, n: (bb, n, kp)),
                  pl.BlockSpec((1, s, LANES), lambda kp, bb, n: (bb, 0, q_blocks + kp)),
                  pl.BlockSpec((1, s, LANES), lambda kp, bb, n: (bb, 0, q_blocks + kv_pairs + kp)),
                  pl.BlockSpec((1, l, LANES), lambda kp, bb, n: (bb, 0, q_blocks + kp)),
                  pl.BlockSpec((1, l, LANES), lambda kp, bb, n: (bb, 0, q_blocks + kv_pairs + kp)),
                  pl.BlockSpec(cos2.shape, tab),
                  pl.BlockSpec(sin2.shape, tab),
                  pl.BlockSpec((1, LANES), tab),
                  pl.BlockSpec((1, LANES), tab),
                  pl.BlockSpec((8, LANES), lambda kp, bb, n: (kp, 0)),
                  pl.BlockSpec((1, SWA_Q_TILE, wk), lambda kp, bb, n: (cls(n), 0, 0))],
        out_specs=pl.BlockSpec((1, SWA_Q_TILE, qw), lambda kp, bb, n: (bb, n, kp)),
        out_shape=jax.ShapeDtypeStruct((b, s, n_q_heads * HEAD_DIM), BF16),
        scratch_shapes=[pltpu.VMEM((s, LANES), F32), pltpu.VMEM((l, LANES), F32)],
        compiler_params=_params("arbitrary", "arbitrary", "arbitrary"),
        name="windowed_gqa",
    )(p_lat, p_lat, p_lat, p_ctx, p_ctx, cos2, sin2, qn, kn, sink_rows, far)


def _oddeven_merge_sort(n):
    pairs = []

    def merge(lo, size, r):
        step = r * 2
        if step < size:
            merge(lo, size, step)
            merge(lo + r, size, step)
            pairs.extend((i, i + r) for i in range(lo + r, lo + size - r, step))
        else:
            pairs.append((lo, lo + r))

    def sort(lo, size):
        if size > 1:
            sort(lo, size // 2)
            sort(lo + size // 2, size // 2)
            merge(lo, size, 1)

    sort(0, n)
    return pairs


def _top_rows(x, k, scr):
    tiles = [x[8 * i:8 * i + 8, :] for i in range(x.shape[0] // 8)]
    for i, j in _oddeven_merge_sort(len(tiles)):
        tiles[i], tiles[j] = jnp.maximum(tiles[i], tiles[j]), jnp.minimum(tiles[i], tiles[j])
    for r in range(k):
        m = jnp.max(tiles[0], axis=0, keepdims=True)
        scr[r:r + 1, :] = m
        need = k - r - 1
        if need > 0:
            hit = tiles[0] == m
            for d in range(min(need, len(tiles) - 1)):
                tiles[d] = jnp.where(hit, tiles[d + 1], tiles[d])
            if need >= len(tiles):
                tiles[-1] = jnp.where(hit, NEG_INF, tiles[-1])


def _bf16_pair(x):
    bits = lax.bitcast_convert_type(x.astype(BF16).astype(F32), jnp.uint32)
    return bits | (bits >> 16)


def _count_above(sorted_scr, y, strict):
    row = lambda i: sorted_scr[i:i + 1, :]
    above = (lambda r: r > y) if strict else (lambda r: r >= y)
    c8 = above(row(7))
    c4 = above(jnp.where(c8, row(11), row(3)))
    c2 = above(jnp.where(c8, jnp.where(c4, row(13), row(9)), jnp.where(c4, row(5), row(1))))
    hi = jnp.where(c4, jnp.where(c2, row(14), row(12)), jnp.where(c2, row(10), row(8)))
    lo = jnp.where(c4, jnp.where(c2, row(6), row(4)), jnp.where(c2, row(2), row(0)))
    c1 = above(jnp.where(c8, hi, lo))
    count = (jnp.where(c8, 8.0, 0.0) + jnp.where(c4, 4.0, 0.0)) + (jnp.where(c2, 2.0, 0.0) + jnp.where(c1, 1.0, 0.0))
    return count + jnp.where(above(row(15)), 1.0, 0.0)


def _peer_scores_kernel(h_ref, wq_ref, keys_ref, ra_ref, p1_ref, gb_ref, p2_ref,
                        q_scr, s_scr, a_scr, b_scr, c_scr, v_scr, *, n_heads):
    k = PEER_TOPK
    n_blocks = h_ref.shape[0] // LANES

    q = _dot(h_ref[...], wq_ref[...])
    for hp in range(2 * n_heads):
        q_scr[hp] = q[:, hp * PEER_KEYS:(hp + 1) * PEER_KEYS].astype(BF16)

    def lane_block(lb, h, slot):
        tops_a, tops_b, cand, tops_c = a_scr.at[slot], b_scr.at[slot], c_scr.at[slot], v_scr.at[slot]
        s1 = s_scr[0, lb]
        s2 = s_scr[1, lb]
        _top_rows(s1, k + 1, tops_a)
        _top_rows(s2, k + 1, tops_b)
        cand[0:16, :] = tops_a[0:1, :] + tops_b[0:16, :]
        for i in range(1, 8):
            cand[8 + 8 * i:16 + 8 * i, :] = tops_a[i:i + 1, :] + tops_b[0:8, :]
        cand[72:80, :] = tops_a[8:16, :] + tops_b[0:1, :]
        cand[80:81, :] = tops_a[0:1, :] + tops_b[16:17, :]
        cand[81:82, :] = tops_a[16:17, :] + tops_b[0:1, :]
        cand[82:PEER_KEYS, :] = jnp.full((PEER_KEYS - 82, LANES), NEG_INF, F32)
        _top_rows(cand[...], k + 1, tops_c)
        thr = 0.5 * (tops_c[k - 1:k, :] + tops_c[k:k + 1, :])
        z = jnp.sum(jnp.exp(tops_c[0:k, :] - tops_c[0:1, :]), axis=0, keepdims=True)
        ra_ref[lb, h] = _bf16_pair(_count_above(tops_a, s1, strict=True))
        p1_ref[lb, h] = _bf16_pair(jnp.exp(s1 - tops_a[0:1, :]) * (0.5 / z))
        gb_ref[lb, h] = _count_above(tops_a, thr - s2, strict=False).astype(BF16)
        p2_ref[lb, h] = jnp.exp(s2 - tops_b[0:1, :]).astype(BF16)

    def lane_pair(i, h):
        lane_block(2 * i, h, 0)
        lane_block(2 * i + 1, h, 1)
        return h

    def head(h, carry):
        for p in range(2):
            st = _dot_nt(keys_ref[2 * h + p], q_scr[2 * h + p])
            for lb in range(n_blocks):
                s_scr[p, lb] = st[:, lb * LANES:(lb + 1) * LANES]
        lax.fori_loop(0, n_blocks // 2, lane_pair, h)
        return carry

    lax.fori_loop(0, n_heads, head, 0)


def _peer_scores(h2, wq, keys, n_heads):
    t, d = h2.shape
    tt = PEER_TOKEN_TILE
    nb = tt // LANES
    tab_shape = (t // LANES, n_heads, PEER_KEYS, LANES)
    tab_spec = pl.BlockSpec((nb, n_heads, PEER_KEYS, LANES), lambda i: (i, 0, 0, 0))
    return pl.pallas_call(
        functools.partial(_peer_scores_kernel, n_heads=n_heads),
        grid=(t // tt,),
        in_specs=[pl.BlockSpec((tt, d), lambda i: (i, 0)),
                  pl.BlockSpec(wq.shape, lambda i: (0, 0)),
                  pl.BlockSpec(keys.shape, lambda i: (0, 0, 0))],
        out_specs=[tab_spec] * 4,
        out_shape=[jax.ShapeDtypeStruct(tab_shape, dt) for dt in (jnp.uint32, jnp.uint32, BF16, BF16)],
        scratch_shapes=[pltpu.VMEM((2 * n_heads, tt, PEER_KEYS), BF16),
                        pltpu.VMEM((2, nb, PEER_KEYS, LANES), F32),
                        pltpu.VMEM((2, 24, LANES), F32), pltpu.VMEM((2, 24, LANES), F32),
                        pltpu.VMEM((2, PEER_KEYS, LANES), F32), pltpu.VMEM((2, 24, LANES), F32)],
        compiler_params=_params("arbitrary"),
        name="peer_scores",
    )(h2, wq, keys)


def _peer_mix_kernel(h_ref, u_ref, vt_ref, ra_ref, p1_ref, gb_ref, p2_ref, x_ref, g_ref, o_ref,
                     acc_scr, act_scr, a_scr, *, n_heads, n_tiles):
    s = pl.program_id(1)
    tt = h_ref.shape[0]
    rows_per_tile = PEER_EXPERT_TILE // PEER_KEYS
    tile = (PEER_KEYS, LANES)

    def readout():
        acc_scr[...] += _dot(vt_ref[0], a_scr[...])

    def row_tile(ref, lb, h, aa):
        words = jnp.broadcast_to(ref[lb, h, aa:aa + 1, :], (8, LANES))
        packed = pltpu.bitcast(words, BF16)
        return jnp.broadcast_to(packed[None], (PEER_KEYS // 16, 16, LANES)).reshape(tile)

    def gates():
        for aa in range(rows_per_tile):
            rs = slice(aa * PEER_KEYS, (aa + 1) * PEER_KEYS)
            for lb in range(tt // LANES):
                ls = slice(lb * LANES, (lb + 1) * LANES)
                w = None
                for h in range(n_heads):
                    rank = row_tile(ra_ref, lb, h, aa)
                    p1 = row_tile(p1_ref, lb, h, aa)
                    term = jnp.where(rank < gb_ref[lb, h], p2_ref[lb, h], jnp.zeros(tile, BF16)) * p1
                    w = term if w is None else w + term
                act = act_scr[rs, ls]
                gelu2 = act * (1.0 + lax.erf(act * INV_SQRT2))
                a_scr[rs, ls] = w * gelu2.astype(BF16)

    def experts():
        act_scr[...] = _dot_nt(u_ref[0].astype(BF16), h_ref[...])

    @pl.when(s == 0)
    def _():
        acc_scr[...] = jnp.zeros_like(acc_scr)

    i = pl.program_id(0)

    @pl.when(s < n_tiles)
    def _():
        experts()

    @pl.when(i >= 0)
    def _():
        gates()

    @pl.when(s + i >= 0)
    def _():
        readout()

    @pl.when(s == n_tiles - 1)
    def _():
        o_ref[...] = x_ref[...] + g_ref[0] * acc_scr[...].T


def _peer_mix(h2, u_all, layer, vt, tables, x, gate, seg, n_heads):
    t, d = h2.shape
    n_exp = u_all.shape[1]
    tt = min(PEER_MIX_TOKEN_TILE, seg)
    te = PEER_EXPERT_TILE
    n_tiles = n_exp // te
    per_seg = seg // tt
    r = gate.shape[0]
    nb = tt // LANES
    tab_spec = pl.BlockSpec((nb, n_heads, PEER_KEYS, LANES), lambda i, s: (i, 0, 0, 0))
    row_spec = pl.BlockSpec((nb, n_heads, te // PEER_KEYS, LANES), lambda i, s: (i, 0, s, 0))
    return pl.pallas_call(
        functools.partial(_peer_mix_kernel, n_heads=n_heads, n_tiles=n_tiles),
        grid=(t // tt, n_tiles),
        in_specs=[pl.BlockSpec((tt, d), lambda i, s: (i, 0)),
                  pl.BlockSpec((1, te, d), lambda i, s: (layer, s, 0)),
                  pl.BlockSpec((1, d, te), lambda i, s: (s, 0, 0)),
                  row_spec, row_spec, tab_spec, tab_spec,
                  pl.BlockSpec((tt, d), lambda i, s: (i, 0)),
                  pl.BlockSpec((1, 1, d), lambda i, s: (i // per_seg, 0, 0))],
        out_specs=pl.BlockSpec((tt, d), lambda i, s: (i, 0)),
        out_shape=jax.ShapeDtypeStruct((t, d), F32),
        scratch_shapes=[pltpu.VMEM((d, tt), F32), pltpu.VMEM((te, tt), F32), pltpu.VMEM((te, tt), BF16)],
        compiler_params=_params("arbitrary", "arbitrary"),
        name="peer_mix",
    )(h2, u_all, vt, *tables, x, gate.reshape(r, 1, d))


def _peer(h2, x, gate, seg, wq, keys, u_all, layer, vt):
    n_heads = keys.shape[0] // 2
    tables = _peer_scores(h2, wq, keys, n_heads)
    return _peer_mix(h2, u_all, layer, vt, tables, x, gate, seg, n_heads)


def _rope_tables(n_tokens):
    t = jnp.arange(n_tokens)
    row = (t // GRID_W).astype(F32)
    col = (t % GRID_W).astype(F32)
    n_freq = HEAD_DIM // 4
    inv_freq = jnp.power(ROPE_BASE, -jnp.arange(n_freq, dtype=F32) / n_freq)
    ang = jnp.concatenate([row[:, None] * inv_freq, col[:, None] * inv_freq], axis=-1)
    cos, sin = jnp.cos(ang), jnp.sin(ang)
    cos2 = jnp.tile(jnp.concatenate([cos, cos], axis=-1), (1, 2))
    sin2 = jnp.tile(jnp.concatenate([-sin, sin], axis=-1), (1, 2))
    return cos2, sin2


def _pair_lanes(v):
    return jnp.repeat(v, HEAD_DIM, axis=-1).reshape(*v.shape[:-1], v.shape[-1] // 2, LANES)


def kernel(x, c, ctx, c_ctx, mod_w, mod_b, norm1_g, norm2_g, ab_w_in, ab_w_out, na_q_norm, na_k_norm,
           na_rpb, ret_log_decay, swa_w_in, swa_w_out, swa_q_norm, swa_k_norm, swa_sink,
           peer_w_q, peer_sub_keys, peer_u, peer_v):
    b, s, d = x.shape
    l = ctx.shape[1]
    depth = mod_w.shape[0]
    assert depth == 2 and b + 1 <= 8
    assert s % (NA_Q_ROWS * GRID_W) == 0 and s // (NA_Q_ROWS * GRID_W) >= 3
    assert s % SWA_Q_TILE == 0 and s // SWA_Q_TILE >= 3 and s % RET_CHUNK == 0 and l % RET_CHUNK == 0
    for tile in (TOKEN_TILE, PROJ_TOKEN_TILE, PEER_TOKEN_TILE, PEER_MIX_TOKEN_TILE):
        assert s % tile == 0 and (b * l) % min(tile, b * l) == 0 and (b * l) % LANES == 0

    cond = jnp.concatenate([c, c_ctx[None, :], jnp.zeros((8 - b - 1, d), F32)], axis=0)
    mods = _adaln(cond, mod_w, mod_b).reshape(depth, 8, 6, d)
    lat = lambda layer, which: mods[layer, :b, which]
    cx = lambda layer, which: mods[layer, b:b + 1, which]

    cos2, sin2 = _rope_tables(s)
    x_lat = x.reshape(b * s, d)
    x_ctx = ctx.reshape(b * l, d)
    tile2 = lambda g: jnp.tile(g, 2).reshape(1, LANES)

    def peer_weights(layer):
        n_heads = peer_sub_keys.shape[1]
        keys = peer_sub_keys[layer].reshape(2 * n_heads, PEER_KEYS, -1).astype(BF16)
        vt = peer_v[layer].reshape(-1, PEER_EXPERT_TILE, d).transpose(0, 2, 1).astype(BF16)
        return peer_w_q[layer].astype(BF16), keys, peer_u, layer, vt

    n_na = na_rpb.shape[1]
    n_ret = ret_log_decay.shape[2]
    na_pairs, ret_pairs = n_na // 2, n_ret // 2
    w_in = ab_w_in[0].astype(BF16)
    w_out = ab_w_out[0].astype(BF16)
    wa = n_na * HEAD_DIM
    p_lat = _modmm(x_lat, norm1_g[0], lat(0, 0), lat(0, 1), w_in, s).reshape(b, s, -1)
    p_ctx = _modmm(x_ctx, norm1_g[0], cx(0, 0), cx(0, 1), w_in, b * l).reshape(b, l, -1)

    qn, kn = tile2(na_q_norm[0]), tile2(na_k_norm[0])
    oa_lat = _na_attention(p_lat, p_ctx, _na_bias(na_rpb[0], s // GRID_W), qn, kn, na_pairs)
    oa_ctx = _ctx_attention(p_ctx, qn, kn, na_pairs)

    lg = _pair_lanes(ret_log_decay[0]).transpose(1, 0, 2)
    ret_col0 = 3 * wa // LANES
    zeros_state = jnp.zeros((b, ret_pairs, 4, HEAD_DIM, HEAD_DIM), F32)
    ones_tab, zeros_tab = jnp.ones((l, LANES), F32), jnp.zeros((l, LANES), F32)
    ob_ctx, st_ctx = _retention(p_ctx, ones_tab, zeros_tab, lg, zeros_state, ret_col0, ret_pairs, False)
    ob_lat, _ = _retention(p_lat, cos2, sin2, lg, st_ctx, ret_col0, ret_pairs, True)

    w_list = [w_out[:wa], w_out[wa:]]
    x_lat, h_lat = _outproj([oa_lat.reshape(b * s, -1), ob_lat.reshape(b * s, -1)], w_list, x_lat,
                            lat(0, 2), norm2_g[0], lat(0, 3), lat(0, 4), s)
    x_ctx, h_ctx = _outproj([oa_ctx.reshape(b * l, -1), ob_ctx.reshape(b * l, -1)], w_list, x_ctx,
                            cx(0, 2), norm2_g[0], cx(0, 3), cx(0, 4), b * l)
    pw = peer_weights(0)
    x_lat = _peer(h_lat, x_lat, lat(0, 5), s, *pw)
    x_ctx = _peer(h_ctx, x_ctx, cx(0, 5), b * l, *pw)

    n_q = swa_sink.shape[1]
    n_kv = (swa_w_in.shape[2] // HEAD_DIM - n_q) // 2
    w_in = swa_w_in[0].astype(BF16)
    p_lat = _modmm(x_lat, norm1_g[1], lat(1, 0), lat(1, 1), w_in, s).reshape(b, s, -1)
    p_ctx = _modmm(x_ctx, norm1_g[1], cx(1, 0), cx(1, 1), w_in, b * l).reshape(b, l, -1)
    sink_rows = jnp.broadcast_to(swa_sink[0][:, None], (n_q, LANES))
    o_lat = _swa_attention(p_lat, p_ctx, cos2, sin2, tile2(swa_q_norm[0]), tile2(swa_k_norm[0]),
                           sink_rows, n_q, n_kv)
    x_lat, h_lat = _outproj([o_lat.reshape(b * s, -1)], [swa_w_out[0].astype(BF16)], x_lat,
                            lat(1, 2), norm2_g[1], lat(1, 3), lat(1, 4), s)
    x_lat = _peer(h_lat, x_lat, lat(1, 5), s, *peer_weights(1))
    return x_lat.reshape(b, s, d)
```

```python
import functools

import numpy as np
import jax
import jax.numpy as jnp
from jax import lax
from jax.experimental import pallas as pl
from jax.experimental.pallas import tpu as pltpu

F32 = jnp.float32
BF16 = jnp.bfloat16

HEAD_DIM = 64
GRID_W = 64
NA_ROWS = 8
NA_COLS = 16
SWA_WINDOW = 128
PEER_TOPK = 16
PEER_KEYS = 128
ROPE_BASE = 10000.0
NORM_EPS = 1e-6
GN_EPS = 1e-5
NEG_INF = -1e30
ATTN_SCALE = HEAD_DIM ** -0.5
INV_SQRT2 = 0.7071067811865476

LANES = 128
VMEM_LIMIT = 56 * 1024 * 1024

TOKEN_TILE = 1024
PROJ_TOKEN_TILE = 1024
PROJ_N_TILE = 1792
RET_CHUNK = 256
NA_Q_ROWS = 8
NA_BAND_ROWS = 16
SWA_Q_TILE = 256
PEER_TOKEN_TILE = 512
PEER_MIX_TOKEN_TILE = 1024
PEER_EXPERT_TILE = 1024


def _dot(a, b):
    return jnp.dot(a, b, preferred_element_type=F32)


def _dot_nt(a, b):
    return lax.dot_general(a, b, (((1,), (1,)), ((), ())), preferred_element_type=F32)


def _dot_tn(a, b):
    return lax.dot_general(a, b, (((0,), (0,)), ((), ())), preferred_element_type=F32)


def _params(*sem):
    return pltpu.CompilerParams(dimension_semantics=sem, vmem_limit_bytes=VMEM_LIMIT)


def _rms_rows(x, gain):
    ms = jnp.mean(x * x, axis=-1, keepdims=True)
    return x * lax.rsqrt(ms + NORM_EPS) * gain


def _modulate(x, gain, shift, scale):
    return _rms_rows(x, gain) * (1.0 + scale) + shift


def _head_rms(x, gain):
    lane = lax.broadcasted_iota(jnp.int32, x.shape, 1)
    lo = lane < HEAD_DIM
    ss = x * x
    s_lo = jnp.sum(jnp.where(lo, ss, 0.0), axis=-1, keepdims=True)
    s_hi = jnp.sum(jnp.where(lo, 0.0, ss), axis=-1, keepdims=True)
    ms = jnp.where(lo, s_lo, s_hi) * (1.0 / HEAD_DIM)
    return x * lax.rsqrt(ms + NORM_EPS) * gain


def _rope(x, cos2, sin2):
    lane = lax.broadcasted_iota(jnp.int32, x.shape, 1)
    first_half = (lane & (HEAD_DIM // 2)) == 0
    swapped = jnp.where(first_half, pltpu.roll(x, LANES - HEAD_DIM // 2, axis=1),
                        pltpu.roll(x, HEAD_DIM // 2, axis=1))
    return x * cos2 + swapped * sin2


def _adaln_kernel(c_ref, w_ref, b_ref, o_ref):
    c = c_ref[...]
    s = c * jax.nn.sigmoid(c)
    w = w_ref[0]
    s_hi = s.astype(BF16).astype(F32)
    w_hi = w.astype(BF16)
    w_lo = (w - w_hi.astype(F32)).astype(BF16)
    rows = s.shape[0]
    both = _dot(jnp.concatenate([s_hi, s - s_hi], axis=0).astype(BF16), w_hi)
    o_ref[0] = both[:rows] + both[rows:] + _dot(s_hi.astype(BF16), w_lo) + b_ref[0]


def _adaln(cond, mod_w, mod_b):
    depth, d, n = mod_w.shape
    tn = n // 4
    return pl.pallas_call(
        _adaln_kernel,
        grid=(depth, n // tn),
        in_specs=[pl.BlockSpec((8, d), lambda l, j: (0, 0)),
                  pl.BlockSpec((1, d, tn), lambda l, j: (l, 0, j)),
                  pl.BlockSpec((1, 1, tn), lambda l, j: (l, 0, j))],
        out_specs=pl.BlockSpec((1, 8, tn), lambda l, j: (l, 0, j)),
        out_shape=jax.ShapeDtypeStruct((depth, 8, n), F32),
        compiler_params=_params("arbitrary", "arbitrary"),
        name="adaln",
    )(cond, mod_w, mod_b.reshape(depth, 1, n))


def _modmm_kernel(x_ref, g_ref, sh_ref, sc_ref, w_ref, o_ref, h_scr):
    @pl.when(pl.program_id(1) == 0)
    def _():
        h_scr[...] = _modulate(x_ref[...], g_ref[...], sh_ref[0], sc_ref[0]).astype(BF16)

    o_ref[...] = _dot(h_scr[...], w_ref[...])


def _modmm(x, gain, shift, scale, w, seg):
    t, d = x.shape
    n = w.shape[1]
    tn = n
    tm = min(PROJ_TOKEN_TILE if n <= PROJ_N_TILE else PROJ_TOKEN_TILE // 2, seg)
    per_seg = seg // tm
    r = shift.shape[0]
    return pl.pallas_call(
        _modmm_kernel,
        grid=(t // tm, n // tn),
        in_specs=[pl.BlockSpec((tm, d), lambda i, j: (i, 0)),
                  pl.BlockSpec((1, d), lambda i, j: (0, 0)),
                  pl.BlockSpec((1, 1, d), lambda i, j: (i // per_seg, 0, 0)),
                  pl.BlockSpec((1, 1, d), lambda i, j: (i // per_seg, 0, 0)),
                  pl.BlockSpec((d, tn), lambda i, j: (0, j))],
        out_specs=pl.BlockSpec((tm, tn), lambda i, j: (i, j)),
        out_shape=jax.ShapeDtypeStruct((t, n), F32),
        scratch_shapes=[pltpu.VMEM((tm, d), BF16)],
        compiler_params=_params("arbitrary", "arbitrary"),
        name="modulate_matmul",
    )(x, gain.reshape(1, d), shift.reshape(r, 1, d), scale.reshape(r, 1, d), w)


def _outproj_kernel(*refs, n_in):
    a_refs = refs[:n_in]
    w_refs = refs[n_in:2 * n_in]
    x_ref, gate_ref, g2_ref, sh_ref, sc_ref, xo_ref, h_ref = refs[2 * n_in:]
    acc = None
    for a_ref, w_ref in zip(a_refs, w_refs):
        part = _dot(a_ref[...].astype(BF16), w_ref[...])
        acc = part if acc is None else acc + part
    xn = x_ref[...] + gate_ref[0] * acc
    xo_ref[...] = xn
    h_ref[...] = _modulate(xn, g2_ref[...], sh_ref[0], sc_ref[0]).astype(BF16)


def _outproj(a_list, w_list, x, gate, gain2, shift2, scale2, seg):
    t, d = x.shape
    tm = min(TOKEN_TILE, seg)
    per_seg = seg // tm
    r = gate.shape[0]
    n_in = len(a_list)
    row = lambda i: (i // per_seg, 0, 0)
    in_specs = ([pl.BlockSpec((tm, a.shape[1]), lambda i: (i, 0)) for a in a_list]
                + [pl.BlockSpec(w.shape, lambda i: (0, 0)) for w in w_list]
                + [pl.BlockSpec((tm, d), lambda i: (i, 0)),
                   pl.BlockSpec((1, 1, d), row),
                   pl.BlockSpec((1, d), lambda i: (0, 0)),
                   pl.BlockSpec((1, 1, d), row),
                   pl.BlockSpec((1, 1, d), row)])
    return pl.pallas_call(
        functools.partial(_outproj_kernel, n_in=n_in),
        grid=(t // tm,),
        in_specs=in_specs,
        out_specs=[pl.BlockSpec((tm, d), lambda i: (i, 0)), pl.BlockSpec((tm, d), lambda i: (i, 0))],
        out_shape=[jax.ShapeDtypeStruct((t, d), F32), jax.ShapeDtypeStruct((t, d), BF16)],
        compiler_params=_params("arbitrary"),
        name="out_proj_residual",
    )(*a_list, *w_list, x, gate.reshape(r, 1, d), gain2.reshape(1, d),
      shift2.reshape(r, 1, d), scale2.reshape(r, 1, d))


def _softmax_pv(s_list, v_list, extra=None, fold_lanes=False):
    def lane_chunks(x):
        return [x[:, c:c + LANES] for c in range(0, x.shape[1], LANES)] if fold_lanes else [x]

    def row_reduce(blocks, op, reduce):
        acc = None
        for blk in blocks:
            chunks = lane_chunks(blk)
            part = chunks[0]
            for c in chunks[1:]:
                part = op(part, c)
            if not fold_lanes:
                part = reduce(part, axis=-1, keepdims=True)
            acc = part if acc is None else op(acc, part)
        return reduce(acc, axis=-1, keepdims=True) if fold_lanes else acc

    m = row_reduce(s_list, jnp.maximum, jnp.max)
    if extra is not None:
        m = jnp.maximum(m, extra)
    ps = []
    out = None
    for s, v in zip(s_list, v_list):
        p = jnp.exp(s - m)
        ps.append(p)
        pv = _dot(p.astype(BF16), v)
        out = pv if out is None else out + pv
    denom = row_reduce(ps, jnp.add, jnp.sum)
    if extra is not None:
        denom = denom + jnp.exp(extra - m)
    return out / denom


def _na_kernel(q_ref, k_ref, v_ref, kc_ref, vc_ref, bias_ref, qn_ref, kn_ref, o_ref,
               k_scr, v_scr, kc_scr, vc_scr, *, n_steps):
    i = pl.program_id(2)
    band = NA_BAND_ROWS * GRID_W
    start = jnp.clip(NA_Q_ROWS * i - NA_ROWS // 2, 0, NA_Q_ROWS * n_steps - NA_BAND_ROWS) * GRID_W
    start = pl.multiple_of(start, NA_ROWS // 2 * GRID_W)

    @pl.when(i == 0)
    def _():
        k_scr[...] = _head_rms(k_ref[0], kn_ref[...]).astype(BF16)
        v_scr[...] = v_ref[0].astype(BF16)
        kc_scr[...] = _head_rms(kc_ref[0], kn_ref[...]).astype(BF16)
        vc_scr[...] = vc_ref[0].astype(BF16)

    q = (_head_rms(q_ref[0], qn_ref[...]) * ATTN_SCALE).astype(BF16)
    kb = k_scr[pl.ds(start, band), :]
    vb = v_scr[pl.ds(start, band), :]
    kc = kc_scr[...]
    vc = vc_scr[...]
    outs = []
    for h in range(2):
        sl = slice(h * HEAD_DIM, (h + 1) * HEAD_DIM)
        s_loc = _dot_nt(q[:, sl], kb[:, sl]) + bias_ref[0, h]
        s_ctx = _dot_nt(q[:, sl], kc[:, sl])
        outs.append(_softmax_pv([s_loc, s_ctx], [vb[:, sl], vc[:, sl]]))
    o_ref[0] = jnp.concatenate(outs, axis=-1).astype(o_ref.dtype)


def _na_bias(rpb, rows):
    h = rpb.shape[0]
    col = np.arange(GRID_W)
    c0 = np.clip(col - NA_COLS // 2, 0, GRID_W - NA_COLS)
    dc = col[None, :] - col[:, None] + (NA_COLS - 1)
    ok_c = (col[None, :] >= c0[:, None]) & (col[None, :] < c0[:, None] + NA_COLS)
    pick = ((np.arange(2 * NA_COLS - 1)[:, None, None] == dc[None]) & ok_c[None]).astype(np.float32)
    blocks = jnp.einsum('hrd,dqk->hrqk', rpb, pick, precision=lax.Precision.HIGHEST)
    blocks = jnp.where(ok_c, blocks, NEG_INF)
    qt, band = NA_Q_ROWS * GRID_W, NA_BAND_ROWS * GRID_W
    return pl.pallas_call(
        functools.partial(_na_bias_kernel, rows=rows),
        grid=(3, h),
        in_specs=[pl.BlockSpec((1,) + blocks.shape[1:], lambda c, hh: (hh, 0, 0, 0))],
        out_specs=pl.BlockSpec((1, 1, qt, band), lambda c, hh: (c, hh, 0, 0)),
        out_shape=jax.ShapeDtypeStruct((3, h, qt, band), F32),
        compiler_params=_params("arbitrary", "arbitrary"),
        name="na_bias_tables",
    )(blocks)


def _na_bias_kernel(blk_ref, o_ref, *, rows):
    n_steps = rows // NA_Q_ROWS
    for c, step in enumerate((0, 1, n_steps - 1)):
        @pl.when(pl.program_id(0) == c)
        def _(step=step):
            o_ref[0, 0] = jnp.full(o_ref.shape[2:], NEG_INF, F32)
            band0 = min(max(NA_Q_ROWS * step - NA_ROWS // 2, 0), rows - NA_BAND_ROWS)
            for rq in range(NA_Q_ROWS):
                r = NA_Q_ROWS * step + rq
                r0 = min(max(r - NA_ROWS // 2, 0), rows - NA_ROWS)
                for rk in range(r0, r0 + NA_ROWS):
                    col0 = (rk - band0) * GRID_W
                    o_ref[0, 0, rq * GRID_W:(rq + 1) * GRID_W, col0:col0 + GRID_W] = (
                        blk_ref[0, rk - r + NA_ROWS - 1])


def _na_attention(p_lat, p_ctx, bias, qn, kn, n_pairs):
    b, s, _ = p_lat.shape
    l = p_ctx.shape[1]
    rows = s // GRID_W
    n_steps = rows // NA_Q_ROWS
    qt = NA_Q_ROWS * GRID_W

    def cls(i):
        return jnp.where(i == 0, 0, jnp.where(i == n_steps - 1, 2, 1))

    return pl.pallas_call(
        functools.partial(_na_kernel, n_steps=n_steps),
        grid=(n_pairs, b, n_steps),
        in_specs=[pl.BlockSpec((1, qt, LANES), lambda hp, bb, i: (bb, i, hp)),
                  pl.BlockSpec((1, s, LANES), lambda hp, bb, i: (bb, 0, n_pairs + hp)),
                  pl.BlockSpec((1, s, LANES), lambda hp, bb, i: (bb, 0, 2 * n_pairs + hp)),
                  pl.BlockSpec((1, l, LANES), lambda hp, bb, i: (bb, 0, n_pairs + hp)),
                  pl.BlockSpec((1, l, LANES), lambda hp, bb, i: (bb, 0, 2 * n_pairs + hp)),
                  pl.BlockSpec((1, 2, qt, NA_BAND_ROWS * GRID_W), lambda hp, bb, i: (cls(i), hp, 0, 0)),
                  pl.BlockSpec((1, LANES), lambda hp, bb, i: (0, 0)),
                  pl.BlockSpec((1, LANES), lambda hp, bb, i: (0, 0))],
        out_specs=pl.BlockSpec((1, qt, LANES), lambda hp, bb, i: (bb, i, hp)),
        out_shape=jax.ShapeDtypeStruct((b, s, n_pairs * LANES), BF16),
        scratch_shapes=[pltpu.VMEM((s, LANES), BF16), pltpu.VMEM((s, LANES), BF16),
                        pltpu.VMEM((l, LANES), BF16), pltpu.VMEM((l, LANES), BF16)],
        compiler_params=_params("arbitrary", "arbitrary", "arbitrary"),
        name="neighbourhood_attention",
    )(p_lat, p_lat, p_lat, p_ctx, p_ctx, bias, qn, kn)


def _ctx_attn_kernel(q_ref, k_ref, v_ref, qn_ref, kn_ref, o_ref):
    q = (_head_rms(q_ref[0], qn_ref[...]) * ATTN_SCALE).astype(BF16)
    k = _head_rms(k_ref[0], kn_ref[...]).astype(BF16)
    v = v_ref[0].astype(BF16)
    outs = []
    for h in range(2):
        sl = slice(h * HEAD_DIM, (h + 1) * HEAD_DIM)
        outs.append(_softmax_pv([_dot_nt(q[:, sl], k[:, sl])], [v[:, sl]]))
    o_ref[0] = jnp.concatenate(outs, axis=-1).astype(o_ref.dtype)


def _ctx_attention(p_ctx, qn, kn, n_pairs):
    b, l, _ = p_ctx.shape
    return pl.pallas_call(
        _ctx_attn_kernel,
        grid=(n_pairs, b),
        in_specs=[pl.BlockSpec((1, l, LANES), lambda hp, bb: (bb, 0, hp)),
                  pl.BlockSpec((1, l, LANES), lambda hp, bb: (bb, 0, n_pairs + hp)),
                  pl.BlockSpec((1, l, LANES), lambda hp, bb: (bb, 0, 2 * n_pairs + hp)),
                  pl.BlockSpec((1, LANES), lambda hp, bb: (0, 0)),
                  pl.BlockSpec((1, LANES), lambda hp, bb: (0, 0))],
        out_specs=pl.BlockSpec((1, l, LANES), lambda hp, bb: (bb, 0, hp)),
        out_shape=jax.ShapeDtypeStruct((b, l, n_pairs * LANES), BF16),
        compiler_params=_params("arbitrary", "arbitrary"),
        name="context_attention",
    )(p_ctx, p_ctx, p_ctx, qn, kn)


def _ret_kernel(q_ref, k_ref, v_ref, g_ref, cos_ref, sin_ref, lg_ref, s0_ref, y_ref, st_ref, sf_scr,
                *, n_chunks, use_rope):
    c = RET_CHUNK
    hd = HEAD_DIM
    lg = -jnp.exp(lg_ref[0])
    lgf, lgb = lg[0:1, :], lg[1:2, :]
    ii = lax.broadcasted_iota(jnp.int32, (c, LANES), 0).astype(F32)
    dq_f = jnp.exp(lgf * (ii + 1.0))
    dk_f = jnp.exp(lgf * (c - 1.0 - ii))
    dq_b = jnp.exp(lgb * (c - ii))
    dk_b = jnp.exp(lgb * ii)
    dc_f = jnp.exp(lgf * float(c))
    dc_b = jnp.exp(lgb * float(c))
    diff = (lax.broadcasted_iota(jnp.int32, (c, c), 0) - lax.broadcasted_iota(jnp.int32, (c, c), 1)).astype(F32)
    intra = []
    chunk_f = []
    chunk_b = []
    for h in range(2):
        lf = lgf[:, h * hd:h * hd + 1]
        lb = lgb[:, h * hd:h * hd + 1]
        intra.append(jnp.where(diff >= 0, jnp.exp(lf * jnp.maximum(diff, 0.0)),
                               jnp.exp(lb * jnp.maximum(-diff, 0.0))))
        chunk_f.append(dc_f[:, h * hd:h * hd + 1])
        chunk_b.append(dc_b[:, h * hd:h * hd + 1])

    def load(n):
        r = pl.multiple_of(n * c, c)
        q = q_ref[0, pl.ds(r, c), :]
        k = k_ref[0, pl.ds(r, c), :]
        v = v_ref[0, pl.ds(r, c), :]
        if use_rope:
            cs = cos_ref[pl.ds(r, c), :]
            sn = sin_ref[pl.ds(r, c), :]
            q = _rope(q, cs, sn)
            k = _rope(k, cs, sn)
        return r, q * ATTN_SCALE, k, v.astype(BF16)

    def fwd(n, carry):
        _, _, k, v = load(n)
        kd = (k * dk_f).astype(BF16)
        new = []
        for h in range(2):
            sl = slice(h * hd, (h + 1) * hd)
            sf_scr[n, h] = carry[h]
            new.append(carry[h] * chunk_f[h] + _dot_tn(kd[:, sl], v[:, sl]))
        return tuple(new)

    sf = lax.fori_loop(0, n_chunks, fwd, (s0_ref[0, 0, 0], s0_ref[0, 0, 1]), unroll=2)
    st_ref[0, 0, 0] = sf[0]
    st_ref[0, 0, 1] = sf[1]

    def bwd(jj, carry):
        n = n_chunks - 1 - jj
        r, q, k, v = load(n)
        qb = q.astype(BF16)
        kb = k.astype(BF16)
        qf = (q * dq_f).astype(BF16)
        qr = (q * dq_b).astype(BF16)
        kd = (k * dk_b).astype(BF16)
        outs = []
        new = []
        for h in range(2):
            sl = slice(h * hd, (h + 1) * hd)
            a = (_dot_nt(qb[:, sl], kb[:, sl]) * intra[h]).astype(BF16)
            o = (_dot(a, v[:, sl]) + _dot(qf[:, sl], sf_scr[n, h].astype(BF16))
                 + _dot(qr[:, sl], carry[h].astype(BF16)))
            oc = o - jnp.mean(o, axis=-1, keepdims=True)
            outs.append(oc * lax.rsqrt(jnp.mean(oc * oc, axis=-1, keepdims=True) + GN_EPS))
            new.append(carry[h] * chunk_b[h] + _dot_tn(kd[:, sl], v[:, sl]))
        g = g_ref[0, pl.ds(r, c), :]
        y_ref[0, pl.ds(r, c), :] = (jnp.concatenate(outs, axis=-1) * (g * jax.nn.sigmoid(g))).astype(y_ref.dtype)
        return tuple(new)

    sb = lax.fori_loop(0, n_chunks, bwd, (s0_ref[0, 0, 2], s0_ref[0, 0, 3]), unroll=2)
    st_ref[0, 0, 2] = sb[0]
    st_ref[0, 0, 3] = sb[1]


def _retention(p, cos2, sin2, lg, s0, col0, n_pairs, use_rope):
    b, t, _ = p.shape
    n_chunks = t // RET_CHUNK
    tab = lambda hp, bb: (0, 0)
    return pl.pallas_call(
        functools.partial(_ret_kernel, n_chunks=n_chunks, use_rope=use_rope),
        grid=(n_pairs, b),
        in_specs=[pl.BlockSpec((1, t, LANES), lambda hp, bb: (bb, 0, col0 + hp)),
                  pl.BlockSpec((1, t, LANES), lambda hp, bb: (bb, 0, col0 + n_pairs + hp)),
                  pl.BlockSpec((1, t, LANES), lambda hp, bb: (bb, 0, col0 + 2 * n_pairs + hp)),
                  pl.BlockSpec((1, t, LANES), lambda hp, bb: (bb, 0, col0 + 3 * n_pairs + hp)),
                  pl.BlockSpec(cos2.shape, tab),
                  pl.BlockSpec(sin2.shape, tab),
                  pl.BlockSpec((1, 2, LANES), lambda hp, bb: (hp, 0, 0)),
                  pl.BlockSpec((1, 1, 4, HEAD_DIM, HEAD_DIM), lambda hp, bb: (bb, hp, 0, 0, 0))],
        out_specs=[pl.BlockSpec((1, t, LANES), lambda hp, bb: (bb, 0, hp)),
                   pl.BlockSpec((1, 1, 4, HEAD_DIM, HEAD_DIM), lambda hp, bb: (bb, hp, 0, 0, 0))],
        out_shape=[jax.ShapeDtypeStruct((b, t, n_pairs * LANES), BF16),
                   jax.ShapeDtypeStruct((b, n_pairs, 4, HEAD_DIM, HEAD_DIM), F32)],
        scratch_shapes=[pltpu.VMEM((n_chunks, 2, HEAD_DIM, HEAD_DIM), F32)],
        compiler_params=_params("arbitrary", "arbitrary"),
        name="retention",
    )(p, p, p, p, cos2, sin2, lg, s0)


def _swa_kernel(q_ref, k_ref, v_ref, kc_ref, vc_ref, cos_ref, sin_ref, qn_ref, kn_ref, sink_ref, far_ref,
                o_ref, k_scr, kc_scr, *, seq):
    qt = SWA_Q_TILE
    wk = qt + 2 * SWA_WINDOW
    hd = HEAD_DIM
    n = pl.program_id(2)
    q0 = pl.multiple_of(n * qt, qt)
    ws = pl.multiple_of(jnp.clip(n * qt - SWA_WINDOW, 0, seq - wk), SWA_WINDOW)

    @pl.when(n == 0)
    def _():
        k_scr[...] = _rope(_head_rms(k_ref[0], kn_ref[...]), cos_ref[...], sin_ref[...])
        kc_scr[...] = _head_rms(kc_ref[0], kn_ref[...])

    kw = k_scr[pl.ds(ws, wk), :].astype(BF16)
    vw = v_ref[0, pl.ds(ws, wk), :].astype(BF16)
    kc = kc_scr[...].astype(BF16)
    vc = vc_ref[0].astype(BF16)
    cos_q = cos_ref[pl.ds(q0, qt), :]
    sin_q = sin_ref[pl.ds(q0, qt), :]
    qs = []
    for s in range(4):
        slab = q_ref[0, :, s * LANES:(s + 1) * LANES]
        qs.append((_rope(_head_rms(slab, qn_ref[...]), cos_q, sin_q) * ATTN_SCALE).astype(BF16))
    far = far_ref[0][None]
    for kh in range(2):
        sl = slice(kh * hd, (kh + 1) * hd)
        qstack = jnp.concatenate(
            [qs[kh * 2 + g // 2][:, (g % 2) * hd:(g % 2 + 1) * hd] for g in range(4)], axis=0)
        sink = jnp.concatenate(
            [jnp.broadcast_to(sink_ref[kh * 4 + g:kh * 4 + g + 1, 0:1], (qt, 1)) for g in range(4)], axis=0)
        s_loc = (_dot_nt(qstack, kw[:, sl]).reshape(4, qt, wk) + far).reshape(4 * qt, wk)
        s_ctx = _dot_nt(qstack, kc[:, sl])
        o = _softmax_pv([s_loc, s_ctx], [vw[:, sl], vc[:, sl]], extra=sink, fold_lanes=True)
        for pair in range(2):
            col = (kh * 2 + pair) * LANES
            o_ref[0, :, col:col + LANES] = jnp.concatenate(
                [o[(2 * pair) * qt:(2 * pair + 1) * qt], o[(2 * pair + 1) * qt:(2 * pair + 2) * qt]],
                axis=-1).astype(o_ref.dtype)


def _swa_attention(p_lat, p_ctx, cos2, sin2, qn, kn, sink_rows, n_q_heads, n_kv_heads):
    b, s, _ = p_lat.shape
    l = p_ctx.shape[1]
    kv_pairs = n_kv_heads // 2
    q_blocks = n_q_heads * HEAD_DIM // LANES
    q_per_pair = q_blocks // kv_pairs
    qw = q_per_pair * LANES
    tab = lambda kp, bb, n: (0, 0)
    n_steps = s // SWA_Q_TILE
    wk = SWA_Q_TILE + 2 * SWA_WINDOW
    far = []
    for step in (0, 1, n_steps - 1):
        q0 = step * SWA_Q_TILE
        ws = min(max(q0 - SWA_WINDOW, 0), s - wk)
        dist = np.abs((q0 + np.arange(SWA_Q_TILE))[:, None] - (ws + np.arange(wk))[None, :])
        far.append(np.where(dist <= SWA_WINDOW, 0.0, NEG_INF))
    far = jnp.asarray(np.stack(far), F32)

    def cls(n):
        return jnp.where(n == 0, 0, jnp.where(n == n_steps - 1, 2, 1))

    return pl.pallas_call(
        functools.partial(_swa_kernel, seq=s),
        grid=(kv_pairs, b, n_steps),
        in_specs=[pl.BlockSpec((1, SWA_Q_TILE, qw), lambda kp, bb, n: (bb, n, kp)),
                  pl.BlockSpec((1, s, LANES), lambda kp, bb, n: (bb, 0, q_blocks + kp)),
                  pl.BlockSpec((1, s, LANES), lambda kp, bb, n: (bb, 0, q_blocks + kv_pairs + kp)),
                  pl.BlockSpec((1, l, LANES), lambda kp, bb, n: (bb, 0, q_blocks + kp)),
                  pl.BlockSpec((1, l, LANES), lambda kp, bb, n: (bb, 0, q_blocks + kv_pairs + kp)),
                  pl.BlockSpec(cos2.shape, tab),
                  pl.BlockSpec(sin2.shape, tab),
                  pl.BlockSpec((1, LANES), tab),
                  pl.BlockSpec((1, LANES), tab),
                  pl.BlockSpec((8, LANES), lambda kp, bb, n: (kp, 0)),
                  pl.BlockSpec((1, SWA_Q_TILE, wk), lambda kp, bb, n: (cls(n), 0, 0))],
        out_specs=pl.BlockSpec((1, SWA_Q_TILE, qw), lambda kp, bb, n: (bb, n, kp)),
        out_shape=jax.ShapeDtypeStruct((b, s, n_q_heads * HEAD_DIM), BF16),
        scratch_shapes=[pltpu.VMEM((s, LANES), F32), pltpu.VMEM((l, LANES), F32)],
        compiler_params=_params("arbitrary", "arbitrary", "arbitrary"),
        name="windowed_gqa",
    )(p_lat, p_lat, p_lat, p_ctx, p_ctx, cos2, sin2, qn, kn, sink_rows, far)


def _oddeven_merge_sort(n):
    pairs = []

    def merge(lo, size, r):
        step = r * 2
        if step < size:
            merge(lo, size, step)
            merge(lo + r, size, step)
            pairs.extend((i, i + r) for i in range(lo + r, lo + size - r, step))
        else:
            pairs.append((lo, lo + r))

    def sort(lo, size):
        if size > 1:
            sort(lo, size // 2)
            sort(lo + size // 2, size // 2)
            merge(lo, size, 1)

    sort(0, n)
    return pairs


def _top_rows(x, k, scr):
    tiles = [x[8 * i:8 * i + 8, :] for i in range(x.shape[0] // 8)]
    for i, j in _oddeven_merge_sort(len(tiles)):
        tiles[i], tiles[j] = jnp.maximum(tiles[i], tiles[j]), jnp.minimum(tiles[i], tiles[j])
    for r in range(k):
        m = jnp.max(tiles[0], axis=0, keepdims=True)
        scr[r:r + 1, :] = m
        need = k - r - 1
        if need > 0:
            hit = tiles[0] == m
            for d in range(min(need, len(tiles) - 1)):
                tiles[d] = jnp.where(hit, tiles[d + 1], tiles[d])
            if need >= len(tiles):
                tiles[-1] = jnp.where(hit, NEG_INF, tiles[-1])


def _bf16_pair(x):
    bits = lax.bitcast_convert_type(x.astype(BF16).astype(F32), jnp.uint32)
    return bits | (bits >> 16)


def _count_above(sorted_scr, y, strict):
    row = lambda i: sorted_scr[i:i + 1, :]
    above = (lambda r: r > y) if strict else (lambda r: r >= y)
    c8 = above(row(7))
    c4 = above(jnp.where(c8, row(11), row(3)))
    c2 = above(jnp.where(c8, jnp.where(c4, row(13), row(9)), jnp.where(c4, row(5), row(1))))
    hi = jnp.where(c4, jnp.where(c2, row(14), row(12)), jnp.where(c2, row(10), row(8)))
    lo = jnp.where(c4, jnp.where(c2, row(6), row(4)), jnp.where(c2, row(2), row(0)))
    c1 = above(jnp.where(c8, hi, lo))
    count = (jnp.where(c8, 8.0, 0.0) + jnp.where(c4, 4.0, 0.0)) + (jnp.where(c2, 2.0, 0.0) + jnp.where(c1, 1.0, 0.0))
    return count + jnp.where(above(row(15)), 1.0, 0.0)


def _peer_scores_kernel(h_ref, wq_ref, keys_ref, ra_ref, p1_ref, gb_ref, p2_ref,
                        q_scr, s_scr, a_scr, b_scr, c_scr, v_scr, *, n_heads):
    k = PEER_TOPK
    n_blocks = h_ref.shape[0] // LANES

    q = _dot(h_ref[...], wq_ref[...])
    for hp in range(2 * n_heads):
        q_scr[hp] = q[:, hp * PEER_KEYS:(hp + 1) * PEER_KEYS].astype(BF16)

    def lane_block(lb, h, slot):
        tops_a, tops_b, cand, tops_c = a_scr.at[slot], b_scr.at[slot], c_scr.at[slot], v_scr.at[slot]
        s1 = s_scr[0, lb]
        s2 = s_scr[1, lb]
        _top_rows(s1, k + 1, tops_a)
        _top_rows(s2, k + 1, tops_b)
        cand[0:16, :] = tops_a[0:1, :] + tops_b[0:16, :]
        for i in range(1, 8):
            cand[8 + 8 * i:16 + 8 * i, :] = tops_a[i:i + 1, :] + tops_b[0:8, :]
        cand[72:80, :] = tops_a[8:16, :] + tops_b[0:1, :]
        cand[80:81, :] = tops_a[0:1, :] + tops_b[16:17, :]
        cand[81:82, :] = tops_a[16:17, :] + tops_b[0:1, :]
        cand[82:PEER_KEYS, :] = jnp.full((PEER_KEYS - 82, LANES), NEG_INF, F32)
        _top_rows(cand[...], k + 1, tops_c)
        thr = 0.5 * (tops_c[k - 1:k, :] + tops_c[k:k + 1, :])
        z = jnp.sum(jnp.exp(tops_c[0:k, :] - tops_c[0:1, :]), axis=0, keepdims=True)
        ra_ref[lb, h] = _bf16_pair(_count_above(tops_a, s1, strict=True))
        p1_ref[lb, h] = _bf16_pair(jnp.exp(s1 - tops_a[0:1, :]) * (0.5 / z))
        gb_ref[lb, h] = _count_above(tops_a, thr - s2, strict=False).astype(BF16)
        p2_ref[lb, h] = jnp.exp(s2 - tops_b[0:1, :]).astype(BF16)

    def lane_pair(i, h):
        lane_block(2 * i, h, 0)
        lane_block(2 * i + 1, h, 1)
        return h

    def head(h, carry):
        for p in range(2):
            st = _dot_nt(keys_ref[2 * h + p], q_scr[2 * h + p])
            for lb in range(n_blocks):
                s_scr[p, lb] = st[:, lb * LANES:(lb + 1) * LANES]
        lax.fori_loop(0, n_blocks // 2, lane_pair, h)
        return carry

    lax.fori_loop(0, n_heads, head, 0)


def _peer_scores(h2, wq, keys, n_heads):
    t, d = h2.shape
    tt = PEER_TOKEN_TILE
    nb = tt // LANES
    tab_shape = (t // LANES, n_heads, PEER_KEYS, LANES)
    tab_spec = pl.BlockSpec((nb, n_heads, PEER_KEYS, LANES), lambda i: (i, 0, 0, 0))
    return pl.pallas_call(
        functools.partial(_peer_scores_kernel, n_heads=n_heads),
        grid=(t // tt,),
        in_specs=[pl.BlockSpec((tt, d), lambda i: (i, 0)),
                  pl.BlockSpec(wq.shape, lambda i: (0, 0)),
                  pl.BlockSpec(keys.shape, lambda i: (0, 0, 0))],
        out_specs=[tab_spec] * 4,
        out_shape=[jax.ShapeDtypeStruct(tab_shape, dt) for dt in (jnp.uint32, jnp.uint32, BF16, BF16)],
        scratch_shapes=[pltpu.VMEM((2 * n_heads, tt, PEER_KEYS), BF16),
                        pltpu.VMEM((2, nb, PEER_KEYS, LANES), F32),
                        pltpu.VMEM((2, 24, LANES), F32), pltpu.VMEM((2, 24, LANES), F32),
                        pltpu.VMEM((2, PEER_KEYS, LANES), F32), pltpu.VMEM((2, 24, LANES), F32)],
        compiler_params=_params("arbitrary"),
        name="peer_scores",
    )(h2, wq, keys)


def _peer_mix_kernel(h_ref, u_ref, vt_ref, ra_ref, p1_ref, gb_ref, p2_ref, x_ref, g_ref, o_ref,
                     acc_scr, act_scr, a_scr, *, n_heads, n_tiles):
    s = pl.program_id(1)
    tt = h_ref.shape[0]
    rows_per_tile = PEER_EXPERT_TILE // PEER_KEYS
    tile = (PEER_KEYS, LANES)

    def readout():
        acc_scr[...] += _dot(vt_ref[0], a_scr[...])

    def row_tile(ref, lb, h, aa):
        words = jnp.broadcast_to(ref[lb, h, aa:aa + 1, :], (8, LANES))
        packed = pltpu.bitcast(words, BF16)
        return jnp.broadcast_to(packed[None], (PEER_KEYS // 16, 16, LANES)).reshape(tile)

    def gates():
        for aa in range(rows_per_tile):
            rs = slice(aa * PEER_KEYS, (aa + 1) * PEER_KEYS)
            for lb in range(tt // LANES):
                ls = slice(lb * LANES, (lb + 1) * LANES)
                w = None
                for h in range(n_heads):
                    rank = row_tile(ra_ref, lb, h, aa)
                    p1 = row_tile(p1_ref, lb, h, aa)
                    term = jnp.where(rank < gb_ref[lb, h], p2_ref[lb, h], jnp.zeros(tile, BF16)) * p1
                    w = term if w is None else w + term
                act = act_scr[rs, ls]
                gelu2 = act * (1.0 + lax.erf(act * INV_SQRT2))
                a_scr[rs, ls] = w * gelu2.astype(BF16)

    def experts():
        act_scr[...] = _dot_nt(u_ref[0].astype(BF16), h_ref[...])

    @pl.when(s == 0)
    def _():
        acc_scr[...] = jnp.zeros_like(acc_scr)

    i = pl.program_id(0)

    @pl.when(s < n_tiles)
    def _():
        experts()

    @pl.when(i >= 0)
    def _():
        gates()

    @pl.when(s + i >= 0)
    def _():
        readout()

    @pl.when(s == n_tiles - 1)
    def _():
        o_ref[...] = x_ref[...] + g_ref[0] * acc_scr[...].T


def _peer_mix(h2, u_all, layer, vt, tables, x, gate, seg, n_heads):
    t, d = h2.shape
    n_exp = u_all.shape[1]
    tt = min(PEER_MIX_TOKEN_TILE, seg)
    te = PEER_EXPERT_TILE
    n_tiles = n_exp // te
    per_seg = seg // tt
    r = gate.shape[0]
    nb = tt // LANES
    tab_spec = pl.BlockSpec((nb, n_heads, PEER_KEYS, LANES), lambda i, s: (i, 0, 0, 0))
    row_spec = pl.BlockSpec((nb, n_heads, te // PEER_KEYS, LANES), lambda i, s: (i, 0, s, 0))
    return pl.pallas_call(
        functools.partial(_peer_mix_kernel, n_heads=n_heads, n_tiles=n_tiles),
        grid=(t // tt, n_tiles),
        in_specs=[pl.BlockSpec((tt, d), lambda i, s: (i, 0)),
                  pl.BlockSpec((1, te, d), lambda i, s: (layer, s, 0)),
                  pl.BlockSpec((1, d, te), lambda i, s: (s, 0, 0)),
                  row_spec, row_spec, tab_spec, tab_spec,
                  pl.BlockSpec((tt, d), lambda i, s: (i, 0)),
                  pl.BlockSpec((1, 1, d), lambda i, s: (i // per_seg, 0, 0))],
        out_specs=pl.BlockSpec((tt, d), lambda i, s: (i, 0)),
        out_shape=jax.ShapeDtypeStruct((t, d), F32),
        scratch_shapes=[pltpu.VMEM((d, tt), F32), pltpu.VMEM((te, tt), F32), pltpu.VMEM((te, tt), BF16)],
        compiler_params=_params("arbitrary", "arbitrary"),
        name="peer_mix",
    )(h2, u_all, vt, *tables, x, gate.reshape(r, 1, d))


def _peer(h2, x, gate, seg, wq, keys, u_all, layer, vt):
    n_heads = keys.shape[0] // 2
    tables = _peer_scores(h2, wq, keys, n_heads)
    return _peer_mix(h2, u_all, layer, vt, tables, x, gate, seg, n_heads)


def _rope_tables(n_tokens):
    t = jnp.arange(n_tokens)
    row = (t // GRID_W).astype(F32)
    col = (t % GRID_W).astype(F32)
    n_freq = HEAD_DIM // 4
    inv_freq = jnp.power(ROPE_BASE, -jnp.arange(n_freq, dtype=F32) / n_freq)
    ang = jnp.concatenate([row[:, None] * inv_freq, col[:, None] * inv_freq], axis=-1)
    cos, sin = jnp.cos(ang), jnp.sin(ang)
    cos2 = jnp.tile(jnp.concatenate([cos, cos], axis=-1), (1, 2))
    sin2 = jnp.tile(jnp.concatenate([-sin, sin], axis=-1), (1, 2))
    return cos2, sin2


def _pair_lanes(v):
    return jnp.repeat(v, HEAD_DIM, axis=-1).reshape(*v.shape[:-1], v.shape[-1] // 2, LANES)


def kernel(x, c, ctx, c_ctx, mod_w, mod_b, norm1_g, norm2_g, ab_w_in, ab_w_out, na_q_norm, na_k_norm,
           na_rpb, ret_log_decay, swa_w_in, swa_w_out, swa_q_norm, swa_k_norm, swa_sink,
           peer_w_q, peer_sub_keys, peer_u, peer_v):
    b, s, d = x.shape
    l = ctx.shape[1]
    depth = mod_w.shape[0]
    assert depth == 2 and b + 1 <= 8
    assert s % (NA_Q_ROWS * GRID_W) == 0 and s // (NA_Q_ROWS * GRID_W) >= 3
    assert s % SWA_Q_TILE == 0 and s // SWA_Q_TILE >= 3 and s % RET_CHUNK == 0 and l % RET_CHUNK == 0
    for tile in (TOKEN_TILE, PROJ_TOKEN_TILE, PEER_TOKEN_TILE, PEER_MIX_TOKEN_TILE):
        assert s % tile == 0 and (b * l) % min(tile, b * l) == 0 and (b * l) % LANES == 0

    cond = jnp.concatenate([c, c_ctx[None, :], jnp.zeros((8 - b - 1, d), F32)], axis=0)
    mods = _adaln(cond, mod_w, mod_b).reshape(depth, 8, 6, d)
    lat = lambda layer, which: mods[layer, :b, which]
    cx = lambda layer, which: mods[layer, b:b + 1, which]

    cos2, sin2 = _rope_tables(s)
    x_lat = x.reshape(b * s, d)
    x_ctx = ctx.reshape(b * l, d)
    tile2 = lambda g: jnp.tile(g, 2).reshape(1, LANES)

    def peer_weights(layer):
        n_heads = peer_sub_keys.shape[1]
        keys = peer_sub_keys[layer].reshape(2 * n_heads, PEER_KEYS, -1).astype(BF16)
        vt = peer_v[layer].reshape(-1, PEER_EXPERT_TILE, d).transpose(0, 2, 1).astype(BF16)
        return peer_w_q[layer].astype(BF16), keys, peer_u, layer, vt

    n_na = na_rpb.shape[1]
    n_ret = ret_log_decay.shape[2]
    na_pairs, ret_pairs = n_na // 2, n_ret // 2
    w_in = ab_w_in[0].astype(BF16)
    w_out = ab_w_out[0].astype(BF16)
    wa = n_na * HEAD_DIM
    p_lat = _modmm(x_lat, norm1_g[0], lat(0, 0), lat(0, 1), w_in, s).reshape(b, s, -1)
    p_ctx = _modmm(x_ctx, norm1_g[0], cx(0, 0), cx(0, 1), w_in, b * l).reshape(b, l, -1)

    qn, kn = tile2(na_q_norm[0]), tile2(na_k_norm[0])
    oa_lat = _na_attention(p_lat, p_ctx, _na_bias(na_rpb[0], s // GRID_W), qn, kn, na_pairs)
    oa_ctx = _ctx_attention(p_ctx, qn, kn, na_pairs)

    lg = _pair_lanes(ret_log_decay[0]).transpose(1, 0, 2)
    ret_col0 = 3 * wa // LANES
    zeros_state = jnp.zeros((b, ret_pairs, 4, HEAD_DIM, HEAD_DIM), F32)
    ones_tab, zeros_tab = jnp.ones((l, LANES), F32), jnp.zeros((l, LANES), F32)
    ob_ctx, st_ctx = _retention(p_ctx, ones_tab, zeros_tab, lg, zeros_state, ret_col0, ret_pairs, False)
    ob_lat, _ = _retention(p_lat, cos2, sin2, lg, st_ctx, ret_col0, ret_pairs, True)

    w_list = [w_out[:wa], w_out[wa:]]
    x_lat, h_lat = _outproj([oa_lat.reshape(b * s, -1), ob_lat.reshape(b * s, -1)], w_list, x_lat,
                            lat(0, 2), norm2_g[0], lat(0, 3), lat(0, 4), s)
    x_ctx, h_ctx = _outproj([oa_ctx.reshape(b * l, -1), ob_ctx.reshape(b * l, -1)], w_list, x_ctx,
                            cx(0, 2), norm2_g[0], cx(0, 3), cx(0, 4), b * l)
    pw = peer_weights(0)
    x_lat = _peer(h_lat, x_lat, lat(0, 5), s, *pw)
    x_ctx = _peer(h_ctx, x_ctx, cx(0, 5), b * l, *pw)

    n_q = swa_sink.shape[1]
    n_kv = (swa_w_in.shape[2] // HEAD_DIM - n_q) // 2
    w_in = swa_w_in[0].astype(BF16)
    p_lat = _modmm(x_lat, norm1_g[1], lat(1, 0), lat(1, 1), w_in, s).reshape(b, s, -1)
    p_ctx = _modmm(x_ctx, norm1_g[1], cx(1, 0), cx(1, 1), w_in, b * l).reshape(b, l, -1)
    sink_rows = jnp.broadcast_to(swa_sink[0][:, None], (n_q, LANES))
    o_lat = _swa_attention(p_lat, p_ctx, cos2, sin2, tile2(swa_q_norm[0]), tile2(swa_k_norm[0]),
                           sink_rows, n_q, n_kv)
    x_lat, h_lat = _outproj([o_lat.reshape(b * s, -1)], [swa_w_out[0].astype(BF16)], x_lat,
                            lat(1, 2), norm2_g[1], lat(1, 3), lat(1, 4), s)
    x_lat = _peer(h_lat, x_lat, lat(1, 5), s, *peer_weights(1))
    return x_lat.reshape(b, s, d)
```

```python
import functools

import numpy as np
import jax
import jax.numpy as jnp
from jax import lax
from jax.experimental import pallas as pl
from jax.experimental.pallas import tpu as pltpu

F32 = jnp.float32
BF16 = jnp.bfloat16

HEAD_DIM = 64
GRID_W = 64
NA_ROWS = 8
NA_COLS = 16
SWA_WINDOW = 128
PEER_TOPK = 16
PEER_KEYS = 128
ROPE_BASE = 10000.0
NORM_EPS = 1e-6
GN_EPS = 1e-5
NEG_INF = -1e30
ATTN_SCALE = HEAD_DIM ** -0.5
INV_SQRT2 = 0.7071067811865476

LANES = 128
VMEM_LIMIT = 56 * 1024 * 1024

TOKEN_TILE = 1024
PROJ_TOKEN_TILE = 1024
PROJ_N_TILE = 1792
RET_CHUNK = 256
NA_Q_ROWS = 8
NA_BAND_ROWS = 16
SWA_Q_TILE = 256
PEER_TOKEN_TILE = 512
PEER_MIX_TOKEN_TILE = 1024
PEER_EXPERT_TILE = 1024


def _dot(a, b):
    return jnp.dot(a, b, preferred_element_type=F32)


def _dot_nt(a, b):
    return lax.dot_general(a, b, (((1,), (1,)), ((), ())), preferred_element_type=F32)


def _dot_tn(a, b):
    return lax.dot_general(a, b, (((0,), (0,)), ((), ())), preferred_element_type=F32)


def _params(*sem):
    return pltpu.CompilerParams(dimension_semantics=sem, vmem_limit_bytes=VMEM_LIMIT)


def _rms_rows(x, gain):
    ms = jnp.mean(x * x, axis=-1, keepdims=True)
    return x * lax.rsqrt(ms + NORM_EPS) * gain


def _modulate(x, gain, shift, scale):
    return _rms_rows(x, gain) * (1.0 + scale) + shift


def _head_rms(x, gain):
    lane = lax.broadcasted_iota(jnp.int32, x.shape, 1)
    lo = lane < HEAD_DIM
    ss = x * x
    s_lo = jnp.sum(jnp.where(lo, ss, 0.0), axis=-1, keepdims=True)
    s_hi = jnp.sum(jnp.where(lo, 0.0, ss), axis=-1, keepdims=True)
    ms = jnp.where(lo, s_lo, s_hi) * (1.0 / HEAD_DIM)
    return x * lax.rsqrt(ms + NORM_EPS) * gain


def _rope(x, cos2, sin2):
    lane = lax.broadcasted_iota(jnp.int32, x.shape, 1)
    first_half = (lane & (HEAD_DIM // 2)) == 0
    swapped = jnp.where(first_half, pltpu.roll(x, LANES - HEAD_DIM // 2, axis=1),
                        pltpu.roll(x, HEAD_DIM // 2, axis=1))
    return x * cos2 + swapped * sin2


def _adaln_kernel(c_ref, w_ref, b_ref, o_ref):
    c = c_ref[...]
    s = c * jax.nn.sigmoid(c)
    w = w_ref[0]
    s_hi = s.astype(BF16).astype(F32)
    w_hi = w.astype(BF16)
    w_lo = (w - w_hi.astype(F32)).astype(BF16)
    rows = s.shape[0]
    both = _dot(jnp.concatenate([s_hi, s - s_hi], axis=0).astype(BF16), w_hi)
    o_ref[0] = both[:rows] + both[rows:] + _dot(s_hi.astype(BF16), w_lo) + b_ref[0]


def _adaln(cond, mod_w, mod_b):
    depth, d, n = mod_w.shape
    tn = n // 4
    return pl.pallas_call(
        _adaln_kernel,
        grid=(depth, n // tn),
        in_specs=[pl.BlockSpec((8, d), lambda l, j: (0, 0)),
                  pl.BlockSpec((1, d, tn), lambda l, j: (l, 0, j)),
                  pl.BlockSpec((1, 1, tn), lambda l, j: (l, 0, j))],
        out_specs=pl.BlockSpec((1, 8, tn), lambda l, j: (l, 0, j)),
        out_shape=jax.ShapeDtypeStruct((depth, 8, n), F32),
        compiler_params=_params("arbitrary", "arbitrary"),
        name="adaln",
    )(cond, mod_w, mod_b.reshape(depth, 1, n))


def _modmm_kernel(x_ref, g_ref, sh_ref, sc_ref, w_ref, o_ref, h_scr):
    @pl.when(pl.program_id(1) == 0)
    def _():
        h_scr[...] = _modulate(x_ref[...], g_ref[...], sh_ref[0], sc_ref[0]).astype(BF16)

    o_ref[...] = _dot(h_scr[...], w_ref[...])


def _modmm(x, gain, shift, scale, w, seg):
    t, d = x.shape
    n = w.shape[1]
    tn = n
    tm = min(PROJ_TOKEN_TILE if n <= PROJ_N_TILE else PROJ_TOKEN_TILE // 2, seg)
    per_seg = seg // tm
    r = shift.shape[0]
    return pl.pallas_call(
        _modmm_kernel,
        grid=(t // tm, n // tn),
        in_specs=[pl.BlockSpec((tm, d), lambda i, j: (i, 0)),
                  pl.BlockSpec((1, d), lambda i, j: (0, 0)),
                  pl.BlockSpec((1, 1, d), lambda i, j: (i // per_seg, 0, 0)),
                  pl.BlockSpec((1, 1, d), lambda i, j: (i // per_seg, 0, 0)),
                  pl.BlockSpec((d, tn), lambda i, j: (0, j))],
        out_specs=pl.BlockSpec((tm, tn), lambda i, j: (i, j)),
        out_shape=jax.ShapeDtypeStruct((t, n), F32),
        scratch_shapes=[pltpu.VMEM((tm, d), BF16)],
        compiler_params=_params("arbitrary", "arbitrary"),
        name="modulate_matmul",
    )(x, gain.reshape(1, d), shift.reshape(r, 1, d), scale.reshape(r, 1, d), w)


def _outproj_kernel(*refs, n_in):
    a_refs = refs[:n_in]
    w_refs = refs[n_in:2 * n_in]
    x_ref, gate_ref, g2_ref, sh_ref, sc_ref, xo_ref, h_ref = refs[2 * n_in:]
    acc = None
    for a_ref, w_ref in zip(a_refs, w_refs):
        part = _dot(a_ref[...].astype(BF16), w_ref[...])
        acc = part if acc is None else acc + part
    xn = x_ref[...] + gate_ref[0] * acc
    xo_ref[...] = xn
    h_ref[...] = _modulate(xn, g2_ref[...], sh_ref[0], sc_ref[0]).astype(BF16)


def _outproj(a_list, w_list, x, gate, gain2, shift2, scale2, seg):
    t, d = x.shape
    tm = min(TOKEN_TILE, seg)
    per_seg = seg // tm
    r = gate.shape[0]
    n_in = len(a_list)
    row = lambda i: (i // per_seg, 0, 0)
    in_specs = ([pl.BlockSpec((tm, a.shape[1]), lambda i: (i, 0)) for a in a_list]
                + [pl.BlockSpec(w.shape, lambda i: (0, 0)) for w in w_list]
                + [pl.BlockSpec((tm, d), lambda i: (i, 0)),
                   pl.BlockSpec((1, 1, d), row),
                   pl.BlockSpec((1, d), lambda i: (0, 0)),
                   pl.BlockSpec((1, 1, d), row),
                   pl.BlockSpec((1, 1, d), row)])
    return pl.pallas_call(
        functools.partial(_outproj_kernel, n_in=n_in),
        grid=(t // tm,),
        in_specs=in_specs,
        out_specs=[pl.BlockSpec((tm, d), lambda i: (i, 0)), pl.BlockSpec((tm, d), lambda i: (i, 0))],
        out_shape=[jax.ShapeDtypeStruct((t, d), F32), jax.ShapeDtypeStruct((t, d), BF16)],
        compiler_params=_params("arbitrary"),
        name="out_proj_residual",
    )(*a_list, *w_list, x, gate.reshape(r, 1, d), gain2.reshape(1, d),
      shift2.reshape(r, 1, d), scale2.reshape(r, 1, d))


def _softmax_pv(s_list, v_list, extra=None, fold_lanes=False):
    def lane_chunks(x):
        return [x[:, c:c + LANES] for c in range(0, x.shape[1], LANES)] if fold_lanes else [x]

    def row_reduce(blocks, op, reduce):
        acc = None
        for blk in blocks:
            chunks = lane_chunks(blk)
            part = chunks[0]
            for c in chunks[1:]:
                part = op(part, c)
            if not fold_lanes:
                part = reduce(part, axis=-1, keepdims=True)
            acc = part if acc is None else op(acc, part)
        return reduce(acc, axis=-1, keepdims=True) if fold_lanes else acc

    m = row_reduce(s_list, jnp.maximum, jnp.max)
    if extra is not None:
        m = jnp.maximum(m, extra)
    ps = []
    out = None
    for s, v in zip(s_list, v_list):
        p = jnp.exp(s - m)
        ps.append(p)
        pv = _dot(p.astype(BF16), v)
        out = pv if out is None else out + pv
    denom = row_reduce(ps, jnp.add, jnp.sum)
    if extra is not None:
        denom = denom + jnp.exp(extra - m)
    return out / denom


def _na_kernel(q_ref, k_ref, v_ref, kc_ref, vc_ref, bias_ref, qn_ref, kn_ref, o_ref,
               k_scr, v_scr, kc_scr, vc_scr, *, n_steps):
    i = pl.program_id(2)
    band = NA_BAND_ROWS * GRID_W
    start = jnp.clip(NA_Q_ROWS * i - NA_ROWS // 2, 0, NA_Q_ROWS * n_steps - NA_BAND_ROWS) * GRID_W
    start = pl.multiple_of(start, NA_ROWS // 2 * GRID_W)

    @pl.when(i == 0)
    def _():
        k_scr[...] = _head_rms(k_ref[0], kn_ref[...]).astype(BF16)
        v_scr[...] = v_ref[0].astype(BF16)
        kc_scr[...] = _head_rms(kc_ref[0], kn_ref[...]).astype(BF16)
        vc_scr[...] = vc_ref[0].astype(BF16)

    q = (_head_rms(q_ref[0], qn_ref[...]) * ATTN_SCALE).astype(BF16)
    kb = k_scr[pl.ds(start, band), :]
    vb = v_scr[pl.ds(start, band), :]
    kc = kc_scr[...]
    vc = vc_scr[...]
    outs = []
    for h in range(2):
        sl = slice(h * HEAD_DIM, (h + 1) * HEAD_DIM)
        s_loc = _dot_nt(q[:, sl], kb[:, sl]) + bias_ref[0, h]
        s_ctx = _dot_nt(q[:, sl], kc[:, sl])
        outs.append(_softmax_pv([s_loc, s_ctx], [vb[:, sl], vc[:, sl]]))
    o_ref[0] = jnp.concatenate(outs, axis=-1).astype(o_ref.dtype)


def _na_bias(rpb, rows):
    h = rpb.shape[0]
    col = np.arange(GRID_W)
    c0 = np.clip(col - NA_COLS // 2, 0, GRID_W - NA_COLS)
    dc = col[None, :] - col[:, None] + (NA_COLS - 1)
    ok_c = (col[None, :] >= c0[:, None]) & (col[None, :] < c0[:, None] + NA_COLS)
    pick = ((np.arange(2 * NA_COLS - 1)[:, None, None] == dc[None]) & ok_c[None]).astype(np.float32)
    blocks = jnp.einsum('hrd,dqk->hrqk', rpb, pick, precision=lax.Precision.HIGHEST)
    blocks = jnp.where(ok_c, blocks, NEG_INF)
    qt, band = NA_Q_ROWS * GRID_W, NA_BAND_ROWS * GRID_W
    return pl.pallas_call(
        functools.partial(_na_bias_kernel, rows=rows),
        grid=(3, h),
        in_specs=[pl.BlockSpec((1,) + blocks.shape[1:], lambda c, hh: (hh, 0, 0, 0))],
        out_specs=pl.BlockSpec((1, 1, qt, band), lambda c, hh: (c, hh, 0, 0)),
        out_shape=jax.ShapeDtypeStruct((3, h, qt, band), F32),
        compiler_params=_params("arbitrary", "arbitrary"),
        name="na_bias_tables",
    )(blocks)


def _na_bias_kernel(blk_ref, o_ref, *, rows):
    n_steps = rows // NA_Q_ROWS
    for c, step in enumerate((0, 1, n_steps - 1)):
        @pl.when(pl.program_id(0) == c)
        def _(step=step):
            o_ref[0, 0] = jnp.full(o_ref.shape[2:], NEG_INF, F32)
            band0 = min(max(NA_Q_ROWS * step - NA_ROWS // 2, 0), rows - NA_BAND_ROWS)
            for rq in range(NA_Q_ROWS):
                r = NA_Q_ROWS * step + rq
                r0 = min(max(r - NA_ROWS // 2, 0), rows - NA_ROWS)
                for rk in range(r0, r0 + NA_ROWS):
                    col0 = (rk - band0) * GRID_W
                    o_ref[0, 0, rq * GRID_W:(rq + 1) * GRID_W, col0:col0 + GRID_W] = (
                        blk_ref[0, rk - r + NA_ROWS - 1])


def _na_attention(p_lat, p_ctx, bias, qn, kn, n_pairs):
    b, s, _ = p_lat.shape
    l = p_ctx.shape[1]
    rows = s // GRID_W
    n_steps = rows // NA_Q_ROWS
    qt = NA_Q_ROWS * GRID_W

    def cls(i):
        return jnp.where(i == 0, 0, jnp.where(i == n_steps - 1, 2, 1))

    return pl.pallas_call(
        functools.partial(_na_kernel, n_steps=n_steps),
        grid=(n_pairs, b, n_steps),
        in_specs=[pl.BlockSpec((1, qt, LANES), lambda hp, bb, i: (bb, i, hp)),
                  pl.BlockSpec((1, s, LANES), lambda hp, bb, i: (bb, 0, n_pairs + hp)),
                  pl.BlockSpec((1, s, LANES), lambda hp, bb, i: (bb, 0, 2 * n_pairs + hp)),
                  pl.BlockSpec((1, l, LANES), lambda hp, bb, i: (bb, 0, n_pairs + hp)),
                  pl.BlockSpec((1, l, LANES), lambda hp, bb, i: (bb, 0, 2 * n_pairs + hp)),
                  pl.BlockSpec((1, 2, qt, NA_BAND_ROWS * GRID_W), lambda hp, bb, i: (cls(i), hp, 0, 0)),
                  pl.BlockSpec((1, LANES), lambda hp, bb, i: (0, 0)),
                  pl.BlockSpec((1, LANES), lambda hp, bb, i: (0, 0))],
        out_specs=pl.BlockSpec((1, qt, LANES), lambda hp, bb, i: (bb, i, hp)),
        out_shape=jax.ShapeDtypeStruct((b, s, n_pairs * LANES), BF16),
        scratch_shapes=[pltpu.VMEM((s, LANES), BF16), pltpu.VMEM((s, LANES), BF16),
                        pltpu.VMEM((l, LANES), BF16), pltpu.VMEM((l, LANES), BF16)],
        compiler_params=_params("arbitrary", "arbitrary", "arbitrary"),
        name="neighbourhood_attention",
    )(p_lat, p_lat, p_lat, p_ctx, p_ctx, bias, qn, kn)


def _ctx_attn_kernel(q_ref, k_ref, v_ref, qn_ref, kn_ref, o_ref):
    q = (_head_rms(q_ref[0], qn_ref[...]) * ATTN_SCALE).astype(BF16)
    k = _head_rms(k_ref[0], kn_ref[...]).astype(BF16)
    v = v_ref[0].astype(BF16)
    outs = []
    for h in range(2):
        sl = slice(h * HEAD_DIM, (h + 1) * HEAD_DIM)
        outs.append(_softmax_pv([_dot_nt(q[:, sl], k[:, sl])], [v[:, sl]]))
    o_ref[0] = jnp.concatenate(outs, axis=-1).astype(o_ref.dtype)


def _ctx_attention(p_ctx, qn, kn, n_pairs):
    b, l, _ = p_ctx.shape
    return pl.pallas_call(
        _ctx_attn_kernel,
        grid=(n_pairs, b),
        in_specs=[pl.BlockSpec((1, l, LANES), lambda hp, bb: (bb, 0, hp)),
                  pl.BlockSpec((1, l, LANES), lambda hp, bb: (bb, 0, n_pairs + hp)),
                  pl.BlockSpec((1, l, LANES), lambda hp, bb: (bb, 0, 2 * n_pairs + hp)),
                  pl.BlockSpec((1, LANES), lambda hp, bb: (0, 0)),
                  pl.BlockSpec((1, LANES), lambda hp, bb: (0, 0))],
        out_specs=pl.BlockSpec((1, l, LANES), lambda hp, bb: (bb, 0, hp)),
        out_shape=jax.ShapeDtypeStruct((b, l, n_pairs * LANES), BF16),
        compiler_params=_params("arbitrary", "arbitrary"),
        name="context_attention",
    )(p_ctx, p_ctx, p_ctx, qn, kn)


def _ret_kernel(q_ref, k_ref, v_ref, g_ref, cos_ref, sin_ref, lg_ref, s0_ref, y_ref, st_ref, sf_scr,
                *, n_chunks, use_rope):
    c = RET_CHUNK
    hd = HEAD_DIM
    lg = -jnp.exp(lg_ref[0])
    lgf, lgb = lg[0:1, :], lg[1:2, :]
    ii = lax.broadcasted_iota(jnp.int32, (c, LANES), 0).astype(F32)
    dq_f = jnp.exp(lgf * (ii + 1.0))
    dk_f = jnp.exp(lgf * (c - 1.0 - ii))
    dq_b = jnp.exp(lgb * (c - ii))
    dk_b = jnp.exp(lgb * ii)
    dc_f = jnp.exp(lgf * float(c))
    dc_b = jnp.exp(lgb * float(c))
    diff = (lax.broadcasted_iota(jnp.int32, (c, c), 0) - lax.broadcasted_iota(jnp.int32, (c, c), 1)).astype(F32)
    intra = []
    chunk_f = []
    chunk_b = []
    for h in range(2):
        lf = lgf[:, h * hd:h * hd + 1]
        lb = lgb[:, h * hd:h * hd + 1]
        intra.append(jnp.where(diff >= 0, jnp.exp(lf * jnp.maximum(diff, 0.0)),
                               jnp.exp(lb * jnp.maximum(-diff, 0.0))))
        chunk_f.append(dc_f[:, h * hd:h * hd + 1])
        chunk_b.append(dc_b[:, h * hd:h * hd + 1])

    def load(n):
        r = pl.multiple_of(n * c, c)
        q = q_ref[0, pl.ds(r, c), :]
        k = k_ref[0, pl.ds(r, c), :]
        v = v_ref[0, pl.ds(r, c), :]
        if use_rope:
            cs = cos_ref[pl.ds(r, c), :]
            sn = sin_ref[pl.ds(r, c), :]
            q = _rope(q, cs, sn)
            k = _rope(k, cs, sn)
        return r, q * ATTN_SCALE, k, v.astype(BF16)

    def fwd(n, carry):
        _, _, k, v = load(n)
        kd = (k * dk_f).astype(BF16)
        new = []
        for h in range(2):
            sl = slice(h * hd, (h + 1) * hd)
            sf_scr[n, h] = carry[h]
            new.append(carry[h] * chunk_f[h] + _dot_tn(kd[:, sl], v[:, sl]))
        return tuple(new)

    sf = lax.fori_loop(0, n_chunks, fwd, (s0_ref[0, 0, 0], s0_ref[0, 0, 1]), unroll=2)
    st_ref[0, 0, 0] = sf[0]
    st_ref[0, 0, 1] = sf[1]

    def bwd(jj, carry):
        n = n_chunks - 1 - jj
        r, q, k, v = load(n)
        qb = q.astype(BF16)
        kb = k.astype(BF16)
        qf = (q * dq_f).astype(BF16)
        qr = (q * dq_b).astype(BF16)
        kd = (k * dk_b).astype(BF16)
        outs = []
        new = []
        for h in range(2):
            sl = slice(h * hd, (h + 1) * hd)
            a = (_dot_nt(qb[:, sl], kb[:, sl]) * intra[h]).astype(BF16)
            o = (_dot(a, v[:, sl]) + _dot(qf[:, sl], sf_scr[n, h].astype(BF16))
                 + _dot(qr[:, sl], carry[h].astype(BF16)))
            oc = o - jnp.mean(o, axis=-1, keepdims=True)
            outs.append(oc * lax.rsqrt(jnp.mean(oc * oc, axis=-1, keepdims=True) + GN_EPS))
            new.append(carry[h] * chunk_b[h] + _dot_tn(kd[:, sl], v[:, sl]))
        g = g_ref[0, pl.ds(r, c), :]
        y_ref[0, pl.ds(r, c), :] = (jnp.concatenate(outs, axis=-1) * (g * jax.nn.sigmoid(g))).astype(y_ref.dtype)
        return tuple(new)

    sb = lax.fori_loop(0, n_chunks, bwd, (s0_ref[0, 0, 2], s0_ref[0, 0, 3]), unroll=2)
    st_ref[0, 0, 2] = sb[0]
    st_ref[0, 0, 3] = sb[1]


def _retention(p, cos2, sin2, lg, s0, col0, n_pairs, use_rope):
    b, t, _ = p.shape
    n_chunks = t // RET_CHUNK
    tab = lambda hp, bb: (0, 0)
    return pl.pallas_call(
        functools.partial(_ret_kernel, n_chunks=n_chunks, use_rope=use_rope),
        grid=(n_pairs, b),
        in_specs=[pl.BlockSpec((1, t, LANES), lambda hp, bb: (bb, 0, col0 + hp)),
                  pl.BlockSpec((1, t, LANES), lambda hp, bb: (bb, 0, col0 + n_pairs + hp)),
                  pl.BlockSpec((1, t, LANES), lambda hp, bb: (bb, 0, col0 + 2 * n_pairs + hp)),
                  pl.BlockSpec((1, t, LANES), lambda hp, bb: (bb, 0, col0 + 3 * n_pairs + hp)),
                  pl.BlockSpec(cos2.shape, tab),
                  pl.BlockSpec(sin2.shape, tab),
                  pl.BlockSpec((1, 2, LANES), lambda hp, bb: (hp, 0, 0)),
                  pl.BlockSpec((1, 1, 4, HEAD_DIM, HEAD_DIM), lambda hp, bb: (bb, hp, 0, 0, 0))],
        out_specs=[pl.BlockSpec((1, t, LANES), lambda hp, bb: (bb, 0, hp)),
                   pl.BlockSpec((1, 1, 4, HEAD_DIM, HEAD_DIM), lambda hp, bb: (bb, hp, 0, 0, 0))],
        out_shape=[jax.ShapeDtypeStruct((b, t, n_pairs * LANES), BF16),
                   jax.ShapeDtypeStruct((b, n_pairs, 4, HEAD_DIM, HEAD_DIM), F32)],
        scratch_shapes=[pltpu.VMEM((n_chunks, 2, HEAD_DIM, HEAD_DIM), F32)],
        compiler_params=_params("arbitrary", "arbitrary"),
        name="retention",
    )(p, p, p, p, cos2, sin2, lg, s0)


def _swa_kernel(q_ref, k_ref, v_ref, kc_ref, vc_ref, cos_ref, sin_ref, qn_ref, kn_ref, sink_ref, far_ref,
                o_ref, k_scr, kc_scr, *, seq):
    qt = SWA_Q_TILE
    sub = SWA_WINDOW
    wk = sub + 2 * SWA_WINDOW
    hd = HEAD_DIM
    n = pl.program_id(2)
    q0 = pl.multiple_of(n * qt, qt)

    @pl.when(n == 0)
    def _():
        k_scr[...] = _rope(_head_rms(k_ref[0], kn_ref[...]), cos_ref[...], sin_ref[...])
        kc_scr[...] = _head_rms(kc_ref[0], kn_ref[...])

    kc = kc_scr[...].astype(BF16)
    vc = vc_ref[0].astype(BF16)
    cos_q = cos_ref[pl.ds(q0, qt), :]
    sin_q = sin_ref[pl.ds(q0, qt), :]
    qs = []
    for s in range(4):
        slab = q_ref[0, :, s * LANES:(s + 1) * LANES]
        qs.append((_rope(_head_rms(slab, qn_ref[...]), cos_q, sin_q) * ATTN_SCALE).astype(BF16))
    for part in range(qt // sub):
        rows = slice(part * sub, (part + 1) * sub)
        ks = pl.multiple_of(jnp.clip(q0 + part * sub - SWA_WINDOW, 0, seq - wk), SWA_WINDOW)
        kw = k_scr[pl.ds(ks, wk), :].astype(BF16)
        vw = v_ref[0, pl.ds(ks, wk), :].astype(BF16)
        far = far_ref[(q0 + part * sub - ks) // SWA_WINDOW][None]
        for kh in range(2):
            sl = slice(kh * hd, (kh + 1) * hd)
            qstack = jnp.concatenate(
                [qs[kh * 2 + g // 2][rows, (g % 2) * hd:(g % 2 + 1) * hd] for g in range(4)], axis=0)
            sink = jnp.concatenate(
                [jnp.broadcast_to(sink_ref[kh * 4 + g:kh * 4 + g + 1, 0:1], (sub, 1)) for g in range(4)],
                axis=0)
            s_loc = (_dot_nt(qstack, kw[:, sl]).reshape(4, sub, wk) + far).reshape(4 * sub, wk)
            s_ctx = _dot_nt(qstack, kc[:, sl])
            o = _softmax_pv([s_loc, s_ctx], [vw[:, sl], vc[:, sl]], extra=sink, fold_lanes=True)
            for pair in range(2):
                col = (kh * 2 + pair) * LANES
                o_ref[0, rows, col:col + LANES] = jnp.concatenate(
                    [o[(2 * pair) * sub:(2 * pair + 1) * sub], o[(2 * pair + 1) * sub:(2 * pair + 2) * sub]],
                    axis=-1).astype(o_ref.dtype)


def _swa_attention(p_lat, p_ctx, cos2, sin2, qn, kn, sink_rows, n_q_heads, n_kv_heads):
    b, s, _ = p_lat.shape
    l = p_ctx.shape[1]
    kv_pairs = n_kv_heads // 2
    q_blocks = n_q_heads * HEAD_DIM // LANES
    q_per_pair = q_blocks // kv_pairs
    qw = q_per_pair * LANES
    tab = lambda kp, bb, n: (0, 0)
    n_steps = s // SWA_Q_TILE
    wk = 3 * SWA_WINDOW
    qi = np.arange(SWA_WINDOW)[None, :, None] + SWA_WINDOW * np.arange(3)[:, None, None]
    far = jnp.asarray(np.where(np.abs(qi - np.arange(wk)[None, None, :]) <= SWA_WINDOW, 0.0, NEG_INF), F32)

    return pl.pallas_call(
        functools.partial(_swa_kernel, seq=s),
        grid=(kv_pairs, b, n_steps),
        in_specs=[pl.BlockSpec((1, SWA_Q_TILE, qw), lambda kp, bb, n: (bb, n, kp)),
                  pl.BlockSpec((1, s, LANES), lambda kp, bb, n: (bb, 0, q_blocks + kp)),
                  pl.BlockSpec((1, s, LANES), lambda kp, bb, n: (bb, 0, q_blocks + kv_pairs + kp)),
                  pl.BlockSpec((1, l, LANES), lambda kp, bb, n: (bb, 0, q_blocks + kp)),
                  pl.BlockSpec((1, l, LANES), lambda kp, bb, n: (bb, 0, q_blocks + kv_pairs + kp)),
                  pl.BlockSpec(cos2.shape, tab),
                  pl.BlockSpec(sin2.shape, tab),
                  pl.BlockSpec((1, LANES), tab),
                  pl.BlockSpec((1, LANES), tab),
                  pl.BlockSpec((8, LANES), lambda kp, bb, n: (kp, 0)),
                  pl.BlockSpec(far.shape, lambda kp, bb, n: (0, 0, 0))],
        out_specs=pl.BlockSpec((1, SWA_Q_TILE, qw), lambda kp, bb, n: (bb, n, kp)),
        out_shape=jax.ShapeDtypeStruct((b, s, n_q_heads * HEAD_DIM), BF16),
        scratch_shapes=[pltpu.VMEM((s, LANES), F32), pltpu.VMEM((l, LANES), F32)],
        compiler_params=_params("arbitrary", "arbitrary", "arbitrary"),
        name="windowed_gqa",
    )(p_lat, p_lat, p_lat, p_ctx, p_ctx, cos2, sin2, qn, kn, sink_rows, far)


def _oddeven_merge_sort(n):
    pairs = []

    def merge(lo, size, r):
        step = r * 2
        if step < size:
            merge(lo, size, step)
            merge(lo + r, size, step)
            pairs.extend((i, i + r) for i in range(lo + r, lo + size - r, step))
        else:
            pairs.append((lo, lo + r))

    def sort(lo, size):
        if size > 1:
            sort(lo, size // 2)
            sort(lo + size // 2, size // 2)
            merge(lo, size, 1)

    sort(0, n)
    return pairs


def _top_rows(x, k, scr):
    tiles = [x[8 * i:8 * i + 8, :] for i in range(x.shape[0] // 8)]
    for i, j in _oddeven_merge_sort(len(tiles)):
        tiles[i], tiles[j] = jnp.maximum(tiles[i], tiles[j]), jnp.minimum(tiles[i], tiles[j])
    for r in range(k):
        m = jnp.max(tiles[0], axis=0, keepdims=True)
        scr[r:r + 1, :] = m
        need = k - r - 1
        if need > 0:
            hit = tiles[0] == m
            for d in range(min(need, len(tiles) - 1)):
                tiles[d] = jnp.where(hit, tiles[d + 1], tiles[d])
            if need >= len(tiles):
                tiles[-1] = jnp.where(hit, NEG_INF, tiles[-1])


def _bf16_pair(x):
    bits = lax.bitcast_convert_type(x.astype(BF16).astype(F32), jnp.uint32)
    return bits | (bits >> 16)


def _count_above(sorted_scr, y, strict):
    row = lambda i: sorted_scr[i:i + 1, :]
    above = (lambda r: r > y) if strict else (lambda r: r >= y)
    c8 = above(row(7))
    c4 = above(jnp.where(c8, row(11), row(3)))
    c2 = above(jnp.where(c8, jnp.where(c4, row(13), row(9)), jnp.where(c4, row(5), row(1))))
    hi = jnp.where(c4, jnp.where(c2, row(14), row(12)), jnp.where(c2, row(10), row(8)))
    lo = jnp.where(c4, jnp.where(c2, row(6), row(4)), jnp.where(c2, row(2), row(0)))
    c1 = above(jnp.where(c8, hi, lo))
    count = (jnp.where(c8, 8.0, 0.0) + jnp.where(c4, 4.0, 0.0)) + (jnp.where(c2, 2.0, 0.0) + jnp.where(c1, 1.0, 0.0))
    return count + jnp.where(above(row(15)), 1.0, 0.0)


def _peer_scores_kernel(h_ref, wq_ref, keys_ref, ra_ref, p1_ref, gb_ref, p2_ref,
                        q_scr, s_scr, a_scr, b_scr, c_scr, v_scr, *, n_heads):
    k = PEER_TOPK
    n_blocks = h_ref.shape[0] // LANES

    q = _dot(h_ref[...], wq_ref[...])
    for hp in range(2 * n_heads):
        q_scr[hp] = q[:, hp * PEER_KEYS:(hp + 1) * PEER_KEYS].astype(BF16)

    def lane_block(lb, h, slot):
        tops_a, tops_b, cand, tops_c = a_scr.at[slot], b_scr.at[slot], c_scr.at[slot], v_scr.at[slot]
        s1 = s_scr[0, lb]
        s2 = s_scr[1, lb]
        _top_rows(s1, k + 1, tops_a)
        _top_rows(s2, k + 1, tops_b)
        cand[0:16, :] = tops_a[0:1, :] + tops_b[0:16, :]
        for i in range(1, 8):
            cand[8 + 8 * i:16 + 8 * i, :] = tops_a[i:i + 1, :] + tops_b[0:8, :]
        cand[72:80, :] = tops_a[8:16, :] + tops_b[0:1, :]
        cand[80:81, :] = tops_a[0:1, :] + tops_b[16:17, :]
        cand[81:82, :] = tops_a[16:17, :] + tops_b[0:1, :]
        cand[82:PEER_KEYS, :] = jnp.full((PEER_KEYS - 82, LANES), NEG_INF, F32)
        _top_rows(cand[...], k + 1, tops_c)
        thr = 0.5 * (tops_c[k - 1:k, :] + tops_c[k:k + 1, :])
        z = jnp.sum(jnp.exp(tops_c[0:k, :] - tops_c[0:1, :]), axis=0, keepdims=True)
        ra_ref[lb, h] = _bf16_pair(_count_above(tops_a, s1, strict=True))
        p1_ref[lb, h] = _bf16_pair(jnp.exp(s1 - tops_a[0:1, :]) * (0.5 / z))
        gb_ref[lb, h] = _count_above(tops_a, thr - s2, strict=False).astype(BF16)
        p2_ref[lb, h] = jnp.exp(s2 - tops_b[0:1, :]).astype(BF16)

    def lane_pair(i, h):
        lane_block(2 * i, h, 0)
        lane_block(2 * i + 1, h, 1)
        return h

    def head(h, carry):
        for p in range(2):
            st = _dot_nt(keys_ref[2 * h + p], q_scr[2 * h + p])
            for lb in range(n_blocks):
                s_scr[p, lb] = st[:, lb * LANES:(lb + 1) * LANES]
        lax.fori_loop(0, n_blocks // 2, lane_pair, h)
        return carry

    lax.fori_loop(0, n_heads, head, 0)


def _peer_scores(h2, wq, keys, n_heads):
    t, d = h2.shape
    tt = PEER_TOKEN_TILE
    nb = tt // LANES
    tab_shape = (t // LANES, n_heads, PEER_KEYS, LANES)
    tab_spec = pl.BlockSpec((nb, n_heads, PEER_KEYS, LANES), lambda i: (i, 0, 0, 0))
    return pl.pallas_call(
        functools.partial(_peer_scores_kernel, n_heads=n_heads),
        grid=(t // tt,),
        in_specs=[pl.BlockSpec((tt, d), lambda i: (i, 0)),
                  pl.BlockSpec(wq.shape, lambda i: (0, 0)),
                  pl.BlockSpec(keys.shape, lambda i: (0, 0, 0))],
        out_specs=[tab_spec] * 4,
        out_shape=[jax.ShapeDtypeStruct(tab_shape, dt) for dt in (jnp.uint32, jnp.uint32, BF16, BF16)],
        scratch_shapes=[pltpu.VMEM((2 * n_heads, tt, PEER_KEYS), BF16),
                        pltpu.VMEM((2, nb, PEER_KEYS, LANES), F32),
                        pltpu.VMEM((2, 24, LANES), F32), pltpu.VMEM((2, 24, LANES), F32),
                        pltpu.VMEM((2, PEER_KEYS, LANES), F32), pltpu.VMEM((2, 24, LANES), F32)],
        compiler_params=_params("arbitrary"),
        name="peer_scores",
    )(h2, wq, keys)


def _peer_mix_kernel(h_ref, u_ref, vt_ref, ra_ref, p1_ref, gb_ref, p2_ref, x_ref, g_ref, o_ref,
                     acc_scr, act_scr, a_scr, *, n_heads, n_tiles):
    s = pl.program_id(1)
    tt = h_ref.shape[0]
    rows_per_tile = PEER_EXPERT_TILE // PEER_KEYS
    tile = (PEER_KEYS, LANES)

    def readout():
        acc_scr[...] += _dot(vt_ref[0], a_scr[...])

    def row_tile(ref, lb, h, aa):
        words = jnp.broadcast_to(ref[lb, h, aa:aa + 1, :], (8, LANES))
        packed = pltpu.bitcast(words, BF16)
        return jnp.broadcast_to(packed[None], (PEER_KEYS // 16, 16, LANES)).reshape(tile)

    def gates():
        for aa in range(rows_per_tile):
            rs = slice(aa * PEER_KEYS, (aa + 1) * PEER_KEYS)
            for lb in range(tt // LANES):
                ls = slice(lb * LANES, (lb + 1) * LANES)
                w = None
                for h in range(n_heads):
                    rank = row_tile(ra_ref, lb, h, aa)
                    p1 = row_tile(p1_ref, lb, h, aa)
                    term = jnp.where(rank < gb_ref[lb, h], p2_ref[lb, h], jnp.zeros(tile, BF16)) * p1
                    w = term if w is None else w + term
                act = act_scr[rs, ls]
                gelu2 = act * (1.0 + lax.erf(act * INV_SQRT2))
                a_scr[rs, ls] = w * gelu2.astype(BF16)

    def experts():
        act_scr[...] = _dot_nt(u_ref[0].astype(BF16), h_ref[...])

    @pl.when(s == 0)
    def _():
        acc_scr[...] = jnp.zeros_like(acc_scr)

    i = pl.program_id(0)

    @pl.when(s < n_tiles)
    def _():
        experts()

    @pl.when(i >= 0)
    def _():
        gates()

    @pl.when(s + i >= 0)
    def _():
        readout()

    @pl.when(s == n_tiles - 1)
    def _():
        o_ref[...] = x_ref[...] + g_ref[0] * acc_scr[...].T


def _peer_mix(h2, u_all, layer, vt, tables, x, gate, seg, n_heads):
    t, d = h2.shape
    n_exp = u_all.shape[1]
    tt = min(PEER_MIX_TOKEN_TILE, seg)
    te = PEER_EXPERT_TILE
    n_tiles = n_exp // te
    per_seg = seg // tt
    r = gate.shape[0]
    nb = tt // LANES
    tab_spec = pl.BlockSpec((nb, n_heads, PEER_KEYS, LANES), lambda i, s: (i, 0, 0, 0))
    row_spec = pl.BlockSpec((nb, n_heads, te // PEER_KEYS, LANES), lambda i, s: (i, 0, s, 0))
    return pl.pallas_call(
        functools.partial(_peer_mix_kernel, n_heads=n_heads, n_tiles=n_tiles),
        grid=(t // tt, n_tiles),
        in_specs=[pl.BlockSpec((tt, d), lambda i, s: (i, 0)),
                  pl.BlockSpec((1, te, d), lambda i, s: (layer, s, 0)),
                  pl.BlockSpec((1, d, te), lambda i, s: (s, 0, 0)),
                  row_spec, row_spec, tab_spec, tab_spec,
                  pl.BlockSpec((tt, d), lambda i, s: (i, 0)),
                  pl.BlockSpec((1, 1, d), lambda i, s: (i // per_seg, 0, 0))],
        out_specs=pl.BlockSpec((tt, d), lambda i, s: (i, 0)),
        out_shape=jax.ShapeDtypeStruct((t, d), F32),
        scratch_shapes=[pltpu.VMEM((d, tt), F32), pltpu.VMEM((te, tt), F32), pltpu.VMEM((te, tt), BF16)],
        compiler_params=_params("arbitrary", "arbitrary"),
        name="peer_mix",
    )(h2, u_all, vt, *tables, x, gate.reshape(r, 1, d))


def _peer(h2, x, gate, seg, wq, keys, u_all, layer, vt):
    n_heads = keys.shape[0] // 2
    tables = _peer_scores(h2, wq, keys, n_heads)
    return _peer_mix(h2, u_all, layer, vt, tables, x, gate, seg, n_heads)


def _rope_tables(n_tokens):
    t = jnp.arange(n_tokens)
    row = (t // GRID_W).astype(F32)
    col = (t % GRID_W).astype(F32)
    n_freq = HEAD_DIM // 4
    inv_freq = jnp.power(ROPE_BASE, -jnp.arange(n_freq, dtype=F32) / n_freq)
    ang = jnp.concatenate([row[:, None] * inv_freq, col[:, None] * inv_freq], axis=-1)
    cos, sin = jnp.cos(ang), jnp.sin(ang)
    cos2 = jnp.tile(jnp.concatenate([cos, cos], axis=-1), (1, 2))
    sin2 = jnp.tile(jnp.concatenate([-sin, sin], axis=-1), (1, 2))
    return cos2, sin2


def _pair_lanes(v):
    return jnp.repeat(v, HEAD_DIM, axis=-1).reshape(*v.shape[:-1], v.shape[-1] // 2, LANES)


def kernel(x, c, ctx, c_ctx, mod_w, mod_b, norm1_g, norm2_g, ab_w_in, ab_w_out, na_q_norm, na_k_norm,
           na_rpb, ret_log_decay, swa_w_in, swa_w_out, swa_q_norm, swa_k_norm, swa_sink,
           peer_w_q, peer_sub_keys, peer_u, peer_v):
    b, s, d = x.shape
    l = ctx.shape[1]
    depth = mod_w.shape[0]
    assert depth == 2 and b + 1 <= 8
    assert s % (NA_Q_ROWS * GRID_W) == 0 and s // (NA_Q_ROWS * GRID_W) >= 3
    assert s % SWA_Q_TILE == 0 and s // SWA_Q_TILE >= 3 and s % RET_CHUNK == 0 and l % RET_CHUNK == 0
    for tile in (TOKEN_TILE, PROJ_TOKEN_TILE, PEER_TOKEN_TILE, PEER_MIX_TOKEN_TILE):
        assert s % tile == 0 and (b * l) % min(tile, b * l) == 0 and (b * l) % LANES == 0

    cond = jnp.concatenate([c, c_ctx[None, :], jnp.zeros((8 - b - 1, d), F32)], axis=0)
    mods = _adaln(cond, mod_w, mod_b).reshape(depth, 8, 6, d)
    lat = lambda layer, which: mods[layer, :b, which]
    cx = lambda layer, which: mods[layer, b:b + 1, which]

    cos2, sin2 = _rope_tables(s)
    x_lat = x.reshape(b * s, d)
    x_ctx = ctx.reshape(b * l, d)
    tile2 = lambda g: jnp.tile(g, 2).reshape(1, LANES)

    def peer_weights(layer):
        n_heads = peer_sub_keys.shape[1]
        keys = peer_sub_keys[layer].reshape(2 * n_heads, PEER_KEYS, -1).astype(BF16)
        vt = peer_v[layer].reshape(-1, PEER_EXPERT_TILE, d).transpose(0, 2, 1).astype(BF16)
        return peer_w_q[layer].astype(BF16), keys, peer_u, layer, vt

    n_na = na_rpb.shape[1]
    n_ret = ret_log_decay.shape[2]
    na_pairs, ret_pairs = n_na // 2, n_ret // 2
    w_in = ab_w_in[0].astype(BF16)
    w_out = ab_w_out[0].astype(BF16)
    wa = n_na * HEAD_DIM
    p_lat = _modmm(x_lat, norm1_g[0], lat(0, 0), lat(0, 1), w_in, s).reshape(b, s, -1)
    p_ctx = _modmm(x_ctx, norm1_g[0], cx(0, 0), cx(0, 1), w_in, b * l).reshape(b, l, -1)

    qn, kn = tile2(na_q_norm[0]), tile2(na_k_norm[0])
    oa_lat = _na_attention(p_lat, p_ctx, _na_bias(na_rpb[0], s // GRID_W), qn, kn, na_pairs)
    oa_ctx = _ctx_attention(p_ctx, qn, kn, na_pairs)

    lg = _pair_lanes(ret_log_decay[0]).transpose(1, 0, 2)
    ret_col0 = 3 * wa // LANES
    zeros_state = jnp.zeros((b, ret_pairs, 4, HEAD_DIM, HEAD_DIM), F32)
    ones_tab, zeros_tab = jnp.ones((l, LANES), F32), jnp.zeros((l, LANES), F32)
    ob_ctx, st_ctx = _retention(p_ctx, ones_tab, zeros_tab, lg, zeros_state, ret_col0, ret_pairs, False)
    ob_lat, _ = _retention(p_lat, cos2, sin2, lg, st_ctx, ret_col0, ret_pairs, True)

    w_list = [w_out[:wa], w_out[wa:]]
    x_lat, h_lat = _outproj([oa_lat.reshape(b * s, -1), ob_lat.reshape(b * s, -1)], w_list, x_lat,
                            lat(0, 2), norm2_g[0], lat(0, 3), lat(0, 4), s)
    x_ctx, h_ctx = _outproj([oa_ctx.reshape(b * l, -1), ob_ctx.reshape(b * l, -1)], w_list, x_ctx,
                            cx(0, 2), norm2_g[0], cx(0, 3), cx(0, 4), b * l)
    pw = peer_weights(0)
    x_lat = _peer(h_lat, x_lat, lat(0, 5), s, *pw)
    x_ctx = _peer(h_ctx, x_ctx, cx(0, 5), b * l, *pw)

    n_q = swa_sink.shape[1]
    n_kv = (swa_w_in.shape[2] // HEAD_DIM - n_q) // 2
    w_in = swa_w_in[0].astype(BF16)
    p_lat = _modmm(x_lat, norm1_g[1], lat(1, 0), lat(1, 1), w_in, s).reshape(b, s, -1)
    p_ctx = _modmm(x_ctx, norm1_g[1], cx(1, 0), cx(1, 1), w_in, b * l).reshape(b, l, -1)
    sink_rows = jnp.broadcast_to(swa_sink[0][:, None], (n_q, LANES))
    o_lat = _swa_attention(p_lat, p_ctx, cos2, sin2, tile2(swa_q_norm[0]), tile2(swa_k_norm[0]),
                           sink_rows, n_q, n_kv)
    x_lat, h_lat = _outproj([o_lat.reshape(b * s, -1)], [swa_w_out[0].astype(BF16)], x_lat,
                            lat(1, 2), norm2_g[1], lat(1, 3), lat(1, 4), s)
    x_lat = _peer(h_lat, x_lat, lat(1, 5), s, *peer_weights(1))
    return x_lat.reshape(b, s, d)
```

```python
import functools

import numpy as np
import jax
import jax.numpy as jnp
from jax import lax
from jax.experimental import pallas as pl
from jax.experimental.pallas import tpu as pltpu

F32 = jnp.float32
BF16 = jnp.bfloat16

HEAD_DIM = 64
GRID_W = 64
NA_ROWS = 8
NA_COLS = 16
SWA_WINDOW = 128
PEER_TOPK = 16
PEER_KEYS = 128
ROPE_BASE = 10000.0
NORM_EPS = 1e-6
GN_EPS = 1e-5
NEG_INF = -1e30
ATTN_SCALE = HEAD_DIM ** -0.5
INV_SQRT2 = 0.7071067811865476

LANES = 128
VMEM_LIMIT = 56 * 1024 * 1024

TOKEN_TILE = 1024
PROJ_TOKEN_TILE = 1024
PROJ_N_TILE = 1792
RET_CHUNK = 256
NA_Q_ROWS = 8
NA_BAND_ROWS = 16
SWA_Q_TILE = 256
PEER_TOKEN_TILE = 1024
PEER_MIX_TOKEN_TILE = 1024
PEER_EXPERT_TILE = 1024


def _dot(a, b):
    return jnp.dot(a, b, preferred_element_type=F32)


def _dot_nt(a, b):
    return lax.dot_general(a, b, (((1,), (1,)), ((), ())), preferred_element_type=F32)


def _dot_tn(a, b):
    return lax.dot_general(a, b, (((0,), (0,)), ((), ())), preferred_element_type=F32)


def _params(*sem):
    return pltpu.CompilerParams(dimension_semantics=sem, vmem_limit_bytes=VMEM_LIMIT)


def _rms_rows(x, gain):
    ms = jnp.mean(x * x, axis=-1, keepdims=True)
    return x * lax.rsqrt(ms + NORM_EPS) * gain


def _modulate(x, gain, shift, scale):
    return _rms_rows(x, gain) * (1.0 + scale) + shift


def _head_rms(x, gain):
    lane = lax.broadcasted_iota(jnp.int32, x.shape, 1)
    lo = lane < HEAD_DIM
    ss = x * x
    s_lo = jnp.sum(jnp.where(lo, ss, 0.0), axis=-1, keepdims=True)
    s_hi = jnp.sum(jnp.where(lo, 0.0, ss), axis=-1, keepdims=True)
    ms = jnp.where(lo, s_lo, s_hi) * (1.0 / HEAD_DIM)
    return x * lax.rsqrt(ms + NORM_EPS) * gain


def _rope(x, cos2, sin2):
    lane = lax.broadcasted_iota(jnp.int32, x.shape, 1)
    first_half = (lane & (HEAD_DIM // 2)) == 0
    swapped = jnp.where(first_half, pltpu.roll(x, LANES - HEAD_DIM // 2, axis=1),
                        pltpu.roll(x, HEAD_DIM // 2, axis=1))
    return x * cos2 + swapped * sin2


def _adaln_kernel(c_ref, w_ref, b_ref, o_ref):
    c = c_ref[...]
    s = c * jax.nn.sigmoid(c)
    w = w_ref[0]
    s_hi = s.astype(BF16).astype(F32)
    w_hi = w.astype(BF16)
    w_lo = (w - w_hi.astype(F32)).astype(BF16)
    rows = s.shape[0]
    both = _dot(jnp.concatenate([s_hi, s - s_hi], axis=0).astype(BF16), w_hi)
    o_ref[0] = both[:rows] + both[rows:] + _dot(s_hi.astype(BF16), w_lo) + b_ref[0]


def _adaln(cond, mod_w, mod_b):
    depth, d, n = mod_w.shape
    tn = n // 4
    return pl.pallas_call(
        _adaln_kernel,
        grid=(depth, n // tn),
        in_specs=[pl.BlockSpec((8, d), lambda l, j: (0, 0)),
                  pl.BlockSpec((1, d, tn), lambda l, j: (l, 0, j)),
                  pl.BlockSpec((1, 1, tn), lambda l, j: (l, 0, j))],
        out_specs=pl.BlockSpec((1, 8, tn), lambda l, j: (l, 0, j)),
        out_shape=jax.ShapeDtypeStruct((depth, 8, n), F32),
        compiler_params=_params("arbitrary", "arbitrary"),
        name="adaln",
    )(cond, mod_w, mod_b.reshape(depth, 1, n))


def _modmm_kernel(x_ref, g_ref, sh_ref, sc_ref, w_ref, o_ref, h_scr):
    @pl.when(pl.program_id(1) == 0)
    def _():
        h_scr[...] = _modulate(x_ref[...], g_ref[...], sh_ref[0], sc_ref[0]).astype(BF16)

    o_ref[...] = _dot(h_scr[...], w_ref[...])


def _modmm(x, gain, shift, scale, w, seg):
    t, d = x.shape
    n = w.shape[1]
    tn = n
    tm = min(PROJ_TOKEN_TILE if n <= PROJ_N_TILE else PROJ_TOKEN_TILE // 2, seg)
    per_seg = seg // tm
    r = shift.shape[0]
    return pl.pallas_call(
        _modmm_kernel,
        grid=(t // tm, n // tn),
        in_specs=[pl.BlockSpec((tm, d), lambda i, j: (i, 0)),
                  pl.BlockSpec((1, d), lambda i, j: (0, 0)),
                  pl.BlockSpec((1, 1, d), lambda i, j: (i // per_seg, 0, 0)),
                  pl.BlockSpec((1, 1, d), lambda i, j: (i // per_seg, 0, 0)),
                  pl.BlockSpec((d, tn), lambda i, j: (0, j))],
        out_specs=pl.BlockSpec((tm, tn), lambda i, j: (i, j)),
        out_shape=jax.ShapeDtypeStruct((t, n), F32),
        scratch_shapes=[pltpu.VMEM((tm, d), BF16)],
        compiler_params=_params("arbitrary", "arbitrary"),
        name="modulate_matmul",
    )(x, gain.reshape(1, d), shift.reshape(r, 1, d), scale.reshape(r, 1, d), w)


def _outproj_kernel(*refs, n_in):
    a_refs = refs[:n_in]
    w_refs = refs[n_in:2 * n_in]
    x_ref, gate_ref, g2_ref, sh_ref, sc_ref, xo_ref, h_ref = refs[2 * n_in:]
    acc = None
    for a_ref, w_ref in zip(a_refs, w_refs):
        part = _dot(a_ref[...].astype(BF16), w_ref[...])
        acc = part if acc is None else acc + part
    xn = x_ref[...] + gate_ref[0] * acc
    xo_ref[...] = xn
    h_ref[...] = _modulate(xn, g2_ref[...], sh_ref[0], sc_ref[0]).astype(BF16)


def _outproj(a_list, w_list, x, gate, gain2, shift2, scale2, seg):
    t, d = x.shape
    tm = min(TOKEN_TILE, seg)
    per_seg = seg // tm
    r = gate.shape[0]
    n_in = len(a_list)
    row = lambda i: (i // per_seg, 0, 0)
    in_specs = ([pl.BlockSpec((tm, a.shape[1]), lambda i: (i, 0)) for a in a_list]
                + [pl.BlockSpec(w.shape, lambda i: (0, 0)) for w in w_list]
                + [pl.BlockSpec((tm, d), lambda i: (i, 0)),
                   pl.BlockSpec((1, 1, d), row),
                   pl.BlockSpec((1, d), lambda i: (0, 0)),
                   pl.BlockSpec((1, 1, d), row),
                   pl.BlockSpec((1, 1, d), row)])
    return pl.pallas_call(
        functools.partial(_outproj_kernel, n_in=n_in),
        grid=(t // tm,),
        in_specs=in_specs,
        out_specs=[pl.BlockSpec((tm, d), lambda i: (i, 0)), pl.BlockSpec((tm, d), lambda i: (i, 0))],
        out_shape=[jax.ShapeDtypeStruct((t, d), F32), jax.ShapeDtypeStruct((t, d), BF16)],
        compiler_params=_params("arbitrary"),
        name="out_proj_residual",
    )(*a_list, *w_list, x, gate.reshape(r, 1, d), gain2.reshape(1, d),
      shift2.reshape(r, 1, d), scale2.reshape(r, 1, d))


def _softmax_pv(s_list, v_list, extra=None, fold_lanes=False):
    def lane_chunks(x):
        return [x[:, c:c + LANES] for c in range(0, x.shape[1], LANES)] if fold_lanes else [x]

    def row_reduce(blocks, op, reduce):
        acc = None
        for blk in blocks:
            chunks = lane_chunks(blk)
            part = chunks[0]
            for c in chunks[1:]:
                part = op(part, c)
            if not fold_lanes:
                part = reduce(part, axis=-1, keepdims=True)
            acc = part if acc is None else op(acc, part)
        return reduce(acc, axis=-1, keepdims=True) if fold_lanes else acc

    m = row_reduce(s_list, jnp.maximum, jnp.max)
    if extra is not None:
        m = jnp.maximum(m, extra)
    ps = []
    out = None
    for s, v in zip(s_list, v_list):
        p = jnp.exp(s - m)
        ps.append(p)
        pv = _dot(p.astype(BF16), v)
        out = pv if out is None else out + pv
    denom = row_reduce(ps, jnp.add, jnp.sum)
    if extra is not None:
        denom = denom + jnp.exp(extra - m)
    return out / denom


def _na_kernel(q_ref, k_ref, v_ref, kc_ref, vc_ref, bias_ref, qn_ref, kn_ref, o_ref,
               k_scr, v_scr, kc_scr, vc_scr, *, n_steps):
    i = pl.program_id(2)
    band = NA_BAND_ROWS * GRID_W
    start = jnp.clip(NA_Q_ROWS * i - NA_ROWS // 2, 0, NA_Q_ROWS * n_steps - NA_BAND_ROWS) * GRID_W
    start = pl.multiple_of(start, NA_ROWS // 2 * GRID_W)

    @pl.when(i == 0)
    def _():
        k_scr[...] = _head_rms(k_ref[0], kn_ref[...]).astype(BF16)
        v_scr[...] = v_ref[0].astype(BF16)
        kc_scr[...] = _head_rms(kc_ref[0], kn_ref[...]).astype(BF16)
        vc_scr[...] = vc_ref[0].astype(BF16)

    q = (_head_rms(q_ref[0], qn_ref[...]) * ATTN_SCALE).astype(BF16)
    kb = k_scr[pl.ds(start, band), :]
    vb = v_scr[pl.ds(start, band), :]
    kc = kc_scr[...]
    vc = vc_scr[...]
    outs = []
    for h in range(2):
        sl = slice(h * HEAD_DIM, (h + 1) * HEAD_DIM)
        s_loc = _dot_nt(q[:, sl], kb[:, sl]) + bias_ref[0, h]
        s_ctx = _dot_nt(q[:, sl], kc[:, sl])
        outs.append(_softmax_pv([s_loc, s_ctx], [vb[:, sl], vc[:, sl]]))
    o_ref[0] = jnp.concatenate(outs, axis=-1).astype(o_ref.dtype)


def _na_bias(rpb, rows):
    h = rpb.shape[0]
    col = np.arange(GRID_W)
    c0 = np.clip(col - NA_COLS // 2, 0, GRID_W - NA_COLS)
    dc = col[None, :] - col[:, None] + (NA_COLS - 1)
    ok_c = (col[None, :] >= c0[:, None]) & (col[None, :] < c0[:, None] + NA_COLS)
    pick = ((np.arange(2 * NA_COLS - 1)[:, None, None] == dc[None]) & ok_c[None]).astype(np.float32)
    blocks = jnp.einsum('hrd,dqk->hrqk', rpb, pick, precision=lax.Precision.HIGHEST)
    blocks = jnp.where(ok_c, blocks, NEG_INF)
    qt, band = NA_Q_ROWS * GRID_W, NA_BAND_ROWS * GRID_W
    return pl.pallas_call(
        functools.partial(_na_bias_kernel, rows=rows),
        grid=(3, h),
        in_specs=[pl.BlockSpec((1,) + blocks.shape[1:], lambda c, hh: (hh, 0, 0, 0))],
        out_specs=pl.BlockSpec((1, 1, qt, band), lambda c, hh: (c, hh, 0, 0)),
        out_shape=jax.ShapeDtypeStruct((3, h, qt, band), F32),
        compiler_params=_params("arbitrary", "arbitrary"),
        name="na_bias_tables",
    )(blocks)


def _na_bias_kernel(blk_ref, o_ref, *, rows):
    n_steps = rows // NA_Q_ROWS
    for c, step in enumerate((0, 1, n_steps - 1)):
        @pl.when(pl.program_id(0) == c)
        def _(step=step):
            o_ref[0, 0] = jnp.full(o_ref.shape[2:], NEG_INF, F32)
            band0 = min(max(NA_Q_ROWS * step - NA_ROWS // 2, 0), rows - NA_BAND_ROWS)
            for rq in range(NA_Q_ROWS):
                r = NA_Q_ROWS * step + rq
                r0 = min(max(r - NA_ROWS // 2, 0), rows - NA_ROWS)
                for rk in range(r0, r0 + NA_ROWS):
                    col0 = (rk - band0) * GRID_W
                    o_ref[0, 0, rq * GRID_W:(rq + 1) * GRID_W, col0:col0 + GRID_W] = (
                        blk_ref[0, rk - r + NA_ROWS - 1])


def _na_attention(p_lat, p_ctx, bias, qn, kn, n_pairs):
    b, s, _ = p_lat.shape
    l = p_ctx.shape[1]
    rows = s // GRID_W
    n_steps = rows // NA_Q_ROWS
    qt = NA_Q_ROWS * GRID_W

    def cls(i):
        return jnp.where(i == 0, 0, jnp.where(i == n_steps - 1, 2, 1))

    return pl.pallas_call(
        functools.partial(_na_kernel, n_steps=n_steps),
        grid=(n_pairs, b, n_steps),
        in_specs=[pl.BlockSpec((1, qt, LANES), lambda hp, bb, i: (bb, i, hp)),
                  pl.BlockSpec((1, s, LANES), lambda hp, bb, i: (bb, 0, n_pairs + hp)),
                  pl.BlockSpec((1, s, LANES), lambda hp, bb, i: (bb, 0, 2 * n_pairs + hp)),
                  pl.BlockSpec((1, l, LANES), lambda hp, bb, i: (bb, 0, n_pairs + hp)),
                  pl.BlockSpec((1, l, LANES), lambda hp, bb, i: (bb, 0, 2 * n_pairs + hp)),
                  pl.BlockSpec((1, 2, qt, NA_BAND_ROWS * GRID_W), lambda hp, bb, i: (cls(i), hp, 0, 0)),
                  pl.BlockSpec((1, LANES), lambda hp, bb, i: (0, 0)),
                  pl.BlockSpec((1, LANES), lambda hp, bb, i: (0, 0))],
        out_specs=pl.BlockSpec((1, qt, LANES), lambda hp, bb, i: (bb, i, hp)),
        out_shape=jax.ShapeDtypeStruct((b, s, n_pairs * LANES), BF16),
        scratch_shapes=[pltpu.VMEM((s, LANES), BF16), pltpu.VMEM((s, LANES), BF16),
                        pltpu.VMEM((l, LANES), BF16), pltpu.VMEM((l, LANES), BF16)],
        compiler_params=_params("arbitrary", "arbitrary", "arbitrary"),
        name="neighbourhood_attention",
    )(p_lat, p_lat, p_lat, p_ctx, p_ctx, bias, qn, kn)


def _ctx_attn_kernel(q_ref, k_ref, v_ref, qn_ref, kn_ref, o_ref):
    q = (_head_rms(q_ref[0], qn_ref[...]) * ATTN_SCALE).astype(BF16)
    k = _head_rms(k_ref[0], kn_ref[...]).astype(BF16)
    v = v_ref[0].astype(BF16)
    outs = []
    for h in range(2):
        sl = slice(h * HEAD_DIM, (h + 1) * HEAD_DIM)
        outs.append(_softmax_pv([_dot_nt(q[:, sl], k[:, sl])], [v[:, sl]]))
    o_ref[0] = jnp.concatenate(outs, axis=-1).astype(o_ref.dtype)


def _ctx_attention(p_ctx, qn, kn, n_pairs):
    b, l, _ = p_ctx.shape
    return pl.pallas_call(
        _ctx_attn_kernel,
        grid=(n_pairs, b),
        in_specs=[pl.BlockSpec((1, l, LANES), lambda hp, bb: (bb, 0, hp)),
                  pl.BlockSpec((1, l, LANES), lambda hp, bb: (bb, 0, n_pairs + hp)),
                  pl.BlockSpec((1, l, LANES), lambda hp, bb: (bb, 0, 2 * n_pairs + hp)),
                  pl.BlockSpec((1, LANES), lambda hp, bb: (0, 0)),
                  pl.BlockSpec((1, LANES), lambda hp, bb: (0, 0))],
        out_specs=pl.BlockSpec((1, l, LANES), lambda hp, bb: (bb, 0, hp)),
        out_shape=jax.ShapeDtypeStruct((b, l, n_pairs * LANES), BF16),
        compiler_params=_params("arbitrary", "arbitrary"),
        name="context_attention",
    )(p_ctx, p_ctx, p_ctx, qn, kn)


def _ret_kernel(q_ref, k_ref, v_ref, g_ref, cos_ref, sin_ref, lg_ref, s0_ref, y_ref, st_ref, sf_scr,
                *, n_chunks, use_rope):
    c = RET_CHUNK
    hd = HEAD_DIM
    lg = -jnp.exp(lg_ref[0])
    lgf, lgb = lg[0:1, :], lg[1:2, :]
    ii = lax.broadcasted_iota(jnp.int32, (c, LANES), 0).astype(F32)
    dq_f = jnp.exp(lgf * (ii + 1.0))
    dk_f = jnp.exp(lgf * (c - 1.0 - ii))
    dq_b = jnp.exp(lgb * (c - ii))
    dk_b = jnp.exp(lgb * ii)
    dc_f = jnp.exp(lgf * float(c))
    dc_b = jnp.exp(lgb * float(c))
    diff = (lax.broadcasted_iota(jnp.int32, (c, c), 0) - lax.broadcasted_iota(jnp.int32, (c, c), 1)).astype(F32)
    intra = []
    chunk_f = []
    chunk_b = []
    for h in range(2):
        lf = lgf[:, h * hd:h * hd + 1]
        lb = lgb[:, h * hd:h * hd + 1]
        intra.append(jnp.where(diff >= 0, jnp.exp(lf * jnp.maximum(diff, 0.0)),
                               jnp.exp(lb * jnp.maximum(-diff, 0.0))))
        chunk_f.append(dc_f[:, h * hd:h * hd + 1])
        chunk_b.append(dc_b[:, h * hd:h * hd + 1])

    def load(n):
        r = pl.multiple_of(n * c, c)
        q = q_ref[0, pl.ds(r, c), :]
        k = k_ref[0, pl.ds(r, c), :]
        v = v_ref[0, pl.ds(r, c), :]
        if use_rope:
            cs = cos_ref[pl.ds(r, c), :]
            sn = sin_ref[pl.ds(r, c), :]
            q = _rope(q, cs, sn)
            k = _rope(k, cs, sn)
        return r, q * ATTN_SCALE, k, v.astype(BF16)

    def fwd(n, carry):
        _, _, k, v = load(n)
        kd = (k * dk_f).astype(BF16)
        new = []
        for h in range(2):
            sl = slice(h * hd, (h + 1) * hd)
            sf_scr[n, h] = carry[h]
            new.append(carry[h] * chunk_f[h] + _dot_tn(kd[:, sl], v[:, sl]))
        return tuple(new)

    sf = lax.fori_loop(0, n_chunks, fwd, (s0_ref[0, 0, 0], s0_ref[0, 0, 1]), unroll=2)
    st_ref[0, 0, 0] = sf[0]
    st_ref[0, 0, 1] = sf[1]

    def bwd(jj, carry):
        n = n_chunks - 1 - jj
        r, q, k, v = load(n)
        qb = q.astype(BF16)
        kb = k.astype(BF16)
        qf = (q * dq_f).astype(BF16)
        qr = (q * dq_b).astype(BF16)
        kd = (k * dk_b).astype(BF16)
        outs = []
        new = []
        for h in range(2):
            sl = slice(h * hd, (h + 1) * hd)
            a = (_dot_nt(qb[:, sl], kb[:, sl]) * intra[h]).astype(BF16)
            o = (_dot(a, v[:, sl]) + _dot(qf[:, sl], sf_scr[n, h].astype(BF16))
                 + _dot(qr[:, sl], carry[h].astype(BF16)))
            oc = o - jnp.mean(o, axis=-1, keepdims=True)
            outs.append(oc * lax.rsqrt(jnp.mean(oc * oc, axis=-1, keepdims=True) + GN_EPS))
            new.append(carry[h] * chunk_b[h] + _dot_tn(kd[:, sl], v[:, sl]))
        g = g_ref[0, pl.ds(r, c), :]
        y_ref[0, pl.ds(r, c), :] = (jnp.concatenate(outs, axis=-1) * (g * jax.nn.sigmoid(g))).astype(y_ref.dtype)
        return tuple(new)

    sb = lax.fori_loop(0, n_chunks, bwd, (s0_ref[0, 0, 2], s0_ref[0, 0, 3]), unroll=2)
    st_ref[0, 0, 2] = sb[0]
    st_ref[0, 0, 3] = sb[1]


def _retention(p, cos2, sin2, lg, s0, col0, n_pairs, use_rope):
    b, t, _ = p.shape
    n_chunks = t // RET_CHUNK
    tab = lambda hp, bb: (0, 0)
    return pl.pallas_call(
        functools.partial(_ret_kernel, n_chunks=n_chunks, use_rope=use_rope),
        grid=(n_pairs, b),
        in_specs=[pl.BlockSpec((1, t, LANES), lambda hp, bb: (bb, 0, col0 + hp)),
                  pl.BlockSpec((1, t, LANES), lambda hp, bb: (bb, 0, col0 + n_pairs + hp)),
                  pl.BlockSpec((1, t, LANES), lambda hp, bb: (bb, 0, col0 + 2 * n_pairs + hp)),
                  pl.BlockSpec((1, t, LANES), lambda hp, bb: (bb, 0, col0 + 3 * n_pairs + hp)),
                  pl.BlockSpec(cos2.shape, tab),
                  pl.BlockSpec(sin2.shape, tab),
                  pl.BlockSpec((1, 2, LANES), lambda hp, bb: (hp, 0, 0)),
                  pl.BlockSpec((1, 1, 4, HEAD_DIM, HEAD_DIM), lambda hp, bb: (bb, hp, 0, 0, 0))],
        out_specs=[pl.BlockSpec((1, t, LANES), lambda hp, bb: (bb, 0, hp)),
                   pl.BlockSpec((1, 1, 4, HEAD_DIM, HEAD_DIM), lambda hp, bb: (bb, hp, 0, 0, 0))],
        out_shape=[jax.ShapeDtypeStruct((b, t, n_pairs * LANES), BF16),
                   jax.ShapeDtypeStruct((b, n_pairs, 4, HEAD_DIM, HEAD_DIM), F32)],
        scratch_shapes=[pltpu.VMEM((n_chunks, 2, HEAD_DIM, HEAD_DIM), F32)],
        compiler_params=_params("arbitrary", "arbitrary"),
        name="retention",
    )(p, p, p, p, cos2, sin2, lg, s0)


def _swa_kernel(q_ref, k_ref, v_ref, kc_ref, vc_ref, cos_ref, sin_ref, qn_ref, kn_ref, sink_ref, far_ref,
                o_ref, k_scr, kc_scr, *, seq):
    qt = SWA_Q_TILE
    wk = qt + 2 * SWA_WINDOW
    hd = HEAD_DIM
    n = pl.program_id(2)
    q0 = pl.multiple_of(n * qt, qt)
    ws = pl.multiple_of(jnp.clip(n * qt - SWA_WINDOW, 0, seq - wk), SWA_WINDOW)

    @pl.when(n == 0)
    def _():
        k_scr[...] = _rope(_head_rms(k_ref[0], kn_ref[...]), cos_ref[...], sin_ref[...])
        kc_scr[...] = _head_rms(kc_ref[0], kn_ref[...])

    kw = k_scr[pl.ds(ws, wk), :].astype(BF16)
    vw = v_ref[0, pl.ds(ws, wk), :].astype(BF16)
    kc = kc_scr[...].astype(BF16)
    vc = vc_ref[0].astype(BF16)
    cos_q = cos_ref[pl.ds(q0, qt), :]
    sin_q = sin_ref[pl.ds(q0, qt), :]
    qs = []
    for s in range(4):
        slab = q_ref[0, :, s * LANES:(s + 1) * LANES]
        qs.append((_rope(_head_rms(slab, qn_ref[...]), cos_q, sin_q) * ATTN_SCALE).astype(BF16))
    far = far_ref[0][None]
    for kh in range(2):
        sl = slice(kh * hd, (kh + 1) * hd)
        qstack = jnp.concatenate(
            [qs[kh * 2 + g // 2][:, (g % 2) * hd:(g % 2 + 1) * hd] for g in range(4)], axis=0)
        sink = jnp.concatenate(
            [jnp.broadcast_to(sink_ref[kh * 4 + g:kh * 4 + g + 1, 0:1], (qt, 1)) for g in range(4)], axis=0)
        s_loc = (_dot_nt(qstack, kw[:, sl]).reshape(4, qt, wk) + far).reshape(4 * qt, wk)
        s_ctx = _dot_nt(qstack, kc[:, sl])
        o = _softmax_pv([s_loc, s_ctx], [vw[:, sl], vc[:, sl]], extra=sink, fold_lanes=True)
        for pair in range(2):
            col = (kh * 2 + pair) * LANES
            o_ref[0, :, col:col + LANES] = jnp.concatenate(
                [o[(2 * pair) * qt:(2 * pair + 1) * qt], o[(2 * pair + 1) * qt:(2 * pair + 2) * qt]],
                axis=-1).astype(o_ref.dtype)


def _swa_attention(p_lat, p_ctx, cos2, sin2, qn, kn, sink_rows, n_q_heads, n_kv_heads):
    b, s, _ = p_lat.shape
    l = p_ctx.shape[1]
    kv_pairs = n_kv_heads // 2
    q_blocks = n_q_heads * HEAD_DIM // LANES
    q_per_pair = q_blocks // kv_pairs
    qw = q_per_pair * LANES
    tab = lambda kp, bb, n: (0, 0)
    n_steps = s // SWA_Q_TILE
    wk = SWA_Q_TILE + 2 * SWA_WINDOW
    far = []
    for step in (0, 1, n_steps - 1):
        q0 = step * SWA_Q_TILE
        ws = min(max(q0 - SWA_WINDOW, 0), s - wk)
        dist = np.abs((q0 + np.arange(SWA_Q_TILE))[:, None] - (ws + np.arange(wk))[None, :])
        far.append(np.where(dist <= SWA_WINDOW, 0.0, NEG_INF))
    far = jnp.asarray(np.stack(far), F32)

    def cls(n):
        return jnp.where(n == 0, 0, jnp.where(n == n_steps - 1, 2, 1))

    return pl.pallas_call(
        functools.partial(_swa_kernel, seq=s),
        grid=(kv_pairs, b, n_steps),
        in_specs=[pl.BlockSpec((1, SWA_Q_TILE, qw), lambda kp, bb, n: (bb, n, kp)),
                  pl.BlockSpec((1, s, LANES), lambda kp, bb, n: (bb, 0, q_blocks + kp)),
                  pl.BlockSpec((1, s, LANES), lambda kp, bb, n: (bb, 0, q_blocks + kv_pairs + kp)),
                  pl.BlockSpec((1, l, LANES), lambda kp, bb, n: (bb, 0, q_blocks + kp)),
                  pl.BlockSpec((1, l, LANES), lambda kp, bb, n: (bb, 0, q_blocks + kv_pairs + kp)),
                  pl.BlockSpec(cos2.shape, tab),
                  pl.BlockSpec(sin2.shape, tab),
                  pl.BlockSpec((1, LANES), tab),
                  pl.BlockSpec((1, LANES), tab),
                  pl.BlockSpec((8, LANES), lambda kp, bb, n: (kp, 0)),
                  pl.BlockSpec((1, SWA_Q_TILE, wk), lambda kp, bb, n: (cls(n), 0, 0))],
        out_specs=pl.BlockSpec((1, SWA_Q_TILE, qw), lambda kp, bb, n: (bb, n, kp)),
        out_shape=jax.ShapeDtypeStruct((b, s, n_q_heads * HEAD_DIM), BF16),
        scratch_shapes=[pltpu.VMEM((s, LANES), F32), pltpu.VMEM((l, LANES), F32)],
        compiler_params=_params("arbitrary", "arbitrary", "arbitrary"),
        name="windowed_gqa",
    )(p_lat, p_lat, p_lat, p_ctx, p_ctx, cos2, sin2, qn, kn, sink_rows, far)


def _oddeven_merge_sort(n):
    pairs = []

    def merge(lo, size, r):
        step = r * 2
        if step < size:
            merge(lo, size, step)
            merge(lo + r, size, step)
            pairs.extend((i, i + r) for i in range(lo + r, lo + size - r, step))
        else:
            pairs.append((lo, lo + r))

    def sort(lo, size):
        if size > 1:
            sort(lo, size // 2)
            sort(lo + size // 2, size // 2)
            merge(lo, size, 1)

    sort(0, n)
    return pairs


def _top_rows(x, k, scr):
    tiles = [x[8 * i:8 * i + 8, :] for i in range(x.shape[0] // 8)]
    for i, j in _oddeven_merge_sort(len(tiles)):
        tiles[i], tiles[j] = jnp.maximum(tiles[i], tiles[j]), jnp.minimum(tiles[i], tiles[j])
    for r in range(k):
        m = jnp.max(tiles[0], axis=0, keepdims=True)
        scr[r:r + 1, :] = m
        need = k - r - 1
        if need > 0:
            hit = tiles[0] == m
            for d in range(min(need, len(tiles) - 1)):
                tiles[d] = jnp.where(hit, tiles[d + 1], tiles[d])
            if need >= len(tiles):
                tiles[-1] = jnp.where(hit, NEG_INF, tiles[-1])


def _bf16_pair(x):
    bits = lax.bitcast_convert_type(x.astype(BF16).astype(F32), jnp.uint32)
    return bits | (bits >> 16)


def _count_above(sorted_scr, y, strict):
    row = lambda i: sorted_scr[i:i + 1, :]
    above = (lambda r: r > y) if strict else (lambda r: r >= y)
    c8 = above(row(7))
    c4 = above(jnp.where(c8, row(11), row(3)))
    c2 = above(jnp.where(c8, jnp.where(c4, row(13), row(9)), jnp.where(c4, row(5), row(1))))
    hi = jnp.where(c4, jnp.where(c2, row(14), row(12)), jnp.where(c2, row(10), row(8)))
    lo = jnp.where(c4, jnp.where(c2, row(6), row(4)), jnp.where(c2, row(2), row(0)))
    c1 = above(jnp.where(c8, hi, lo))
    count = (jnp.where(c8, 8.0, 0.0) + jnp.where(c4, 4.0, 0.0)) + (jnp.where(c2, 2.0, 0.0) + jnp.where(c1, 1.0, 0.0))
    return count + jnp.where(above(row(15)), 1.0, 0.0)


def _peer_scores_kernel(h_ref, wq_ref, keys_ref, ra_ref, p1_ref, gb_ref, p2_ref,
                        q_scr, s_scr, a_scr, b_scr, c_scr, v_scr, *, n_heads):
    k = PEER_TOPK
    n_blocks = h_ref.shape[0] // LANES

    q = _dot(h_ref[...], wq_ref[...])
    for hp in range(2 * n_heads):
        q_scr[hp] = q[:, hp * PEER_KEYS:(hp + 1) * PEER_KEYS].astype(BF16)

    def lane_block(lb, h, slot):
        tops_a, tops_b, cand, tops_c = a_scr.at[slot], b_scr.at[slot], c_scr.at[slot], v_scr.at[slot]
        s1 = s_scr[0, lb]
        s2 = s_scr[1, lb]
        _top_rows(s1, k + 1, tops_a)
        _top_rows(s2, k + 1, tops_b)
        cand[0:16, :] = tops_a[0:1, :] + tops_b[0:16, :]
        for i in range(1, 8):
            cand[8 + 8 * i:16 + 8 * i, :] = tops_a[i:i + 1, :] + tops_b[0:8, :]
        cand[72:80, :] = tops_a[8:16, :] + tops_b[0:1, :]
        cand[80:81, :] = tops_a[0:1, :] + tops_b[16:17, :]
        cand[81:82, :] = tops_a[16:17, :] + tops_b[0:1, :]
        cand[82:PEER_KEYS, :] = jnp.full((PEER_KEYS - 82, LANES), NEG_INF, F32)
        _top_rows(cand[...], k + 1, tops_c)
        thr = 0.5 * (tops_c[k - 1:k, :] + tops_c[k:k + 1, :])
        z = jnp.sum(jnp.exp(tops_c[0:k, :] - tops_c[0:1, :]), axis=0, keepdims=True)
        ra_ref[lb, h] = _bf16_pair(_count_above(tops_a, s1, strict=True))
        p1_ref[lb, h] = _bf16_pair(jnp.exp(s1 - tops_a[0:1, :]) * (0.5 / z))
        gb_ref[lb, h] = _count_above(tops_a, thr - s2, strict=False).astype(BF16)
        p2_ref[lb, h] = jnp.exp(s2 - tops_b[0:1, :]).astype(BF16)

    def lane_pair(i, h):
        lane_block(2 * i, h, 0)
        lane_block(2 * i + 1, h, 1)
        return h

    def head(h, carry):
        for p in range(2):
            st = _dot_nt(keys_ref[2 * h + p], q_scr[2 * h + p])
            for lb in range(n_blocks):
                s_scr[p, lb] = st[:, lb * LANES:(lb + 1) * LANES]
        lax.fori_loop(0, n_blocks // 2, lane_pair, h)
        return carry

    lax.fori_loop(0, n_heads, head, 0)


def _peer_scores(h2, wq, keys, n_heads):
    t, d = h2.shape
    tt = PEER_TOKEN_TILE
    nb = tt // LANES
    tab_shape = (t // LANES, n_heads, PEER_KEYS, LANES)
    tab_spec = pl.BlockSpec((nb, n_heads, PEER_KEYS, LANES), lambda i: (i, 0, 0, 0))
    return pl.pallas_call(
        functools.partial(_peer_scores_kernel, n_heads=n_heads),
        grid=(t // tt,),
        in_specs=[pl.BlockSpec((tt, d), lambda i: (i, 0)),
                  pl.BlockSpec(wq.shape, lambda i: (0, 0)),
                  pl.BlockSpec(keys.shape, lambda i: (0, 0, 0))],
        out_specs=[tab_spec] * 4,
        out_shape=[jax.ShapeDtypeStruct(tab_shape, dt) for dt in (jnp.uint32, jnp.uint32, BF16, BF16)],
        scratch_shapes=[pltpu.VMEM((2 * n_heads, tt, PEER_KEYS), BF16),
                        pltpu.VMEM((2, nb, PEER_KEYS, LANES), F32),
                        pltpu.VMEM((2, 24, LANES), F32), pltpu.VMEM((2, 24, LANES), F32),
                        pltpu.VMEM((2, PEER_KEYS, LANES), F32), pltpu.VMEM((2, 24, LANES), F32)],
        compiler_params=_params("arbitrary"),
        name="peer_scores",
    )(h2, wq, keys)


def _peer_mix_kernel(h_ref, u_ref, vt_ref, ra_ref, p1_ref, gb_ref, p2_ref, x_ref, g_ref, o_ref,
                     acc_scr, act_scr, a_scr, *, n_heads, n_tiles):
    s = pl.program_id(1)
    tt = h_ref.shape[0]
    rows_per_tile = PEER_EXPERT_TILE // PEER_KEYS
    tile = (PEER_KEYS, LANES)

    def readout():
        acc_scr[...] += _dot(vt_ref[0], a_scr[...])

    def row_tile(ref, lb, h, aa):
        words = jnp.broadcast_to(ref[lb, h, aa:aa + 1, :], (8, LANES))
        packed = pltpu.bitcast(words, BF16)
        return jnp.broadcast_to(packed[None], (PEER_KEYS // 16, 16, LANES)).reshape(tile)

    def gates():
        for aa in range(rows_per_tile):
            rs = slice(aa * PEER_KEYS, (aa + 1) * PEER_KEYS)
            for lb in range(tt // LANES):
                ls = slice(lb * LANES, (lb + 1) * LANES)
                w = None
                for h in range(n_heads):
                    rank = row_tile(ra_ref, lb, h, aa)
                    p1 = row_tile(p1_ref, lb, h, aa)
                    term = jnp.where(rank < gb_ref[lb, h], p2_ref[lb, h], jnp.zeros(tile, BF16)) * p1
                    w = term if w is None else w + term
                act = act_scr[rs, ls]
                gelu2 = act * (1.0 + lax.erf(act * INV_SQRT2))
                a_scr[rs, ls] = w * gelu2.astype(BF16)

    def experts():
        act_scr[...] = _dot_nt(u_ref[0].astype(BF16), h_ref[...])

    @pl.when(s == 0)
    def _():
        acc_scr[...] = jnp.zeros_like(acc_scr)

    i = pl.program_id(0)

    @pl.when(s < n_tiles)
    def _():
        experts()

    @pl.when(i >= 0)
    def _():
        gates()

    @pl.when(s + i >= 0)
    def _():
        readout()

    @pl.when(s == n_tiles - 1)
    def _():
        o_ref[...] = x_ref[...] + g_ref[0] * acc_scr[...].T


def _peer_mix(h2, u_all, layer, vt, tables, x, gate, seg, n_heads):
    t, d = h2.shape
    n_exp = u_all.shape[1]
    tt = min(PEER_MIX_TOKEN_TILE, seg)
    te = PEER_EXPERT_TILE
    n_tiles = n_exp // te
    per_seg = seg // tt
    r = gate.shape[0]
    nb = tt // LANES
    tab_spec = pl.BlockSpec((nb, n_heads, PEER_KEYS, LANES), lambda i, s: (i, 0, 0, 0))
    row_spec = pl.BlockSpec((nb, n_heads, te // PEER_KEYS, LANES), lambda i, s: (i, 0, s, 0))
    return pl.pallas_call(
        functools.partial(_peer_mix_kernel, n_heads=n_heads, n_tiles=n_tiles),
        grid=(t // tt, n_tiles),
        in_specs=[pl.BlockSpec((tt, d), lambda i, s: (i, 0)),
                  pl.BlockSpec((1, te, d), lambda i, s: (layer, s, 0)),
                  pl.BlockSpec((1, d, te), lambda i, s: (s, 0, 0)),
                  row_spec, row_spec, tab_spec, tab_spec,
                  pl.BlockSpec((tt, d), lambda i, s: (i, 0)),
                  pl.BlockSpec((1, 1, d), lambda i, s: (i // per_seg, 0, 0))],
        out_specs=pl.BlockSpec((tt, d), lambda i, s: (i, 0)),
        out_shape=jax.ShapeDtypeStruct((t, d), F32),
        scratch_shapes=[pltpu.VMEM((d, tt), F32), pltpu.VMEM((te, tt), F32), pltpu.VMEM((te, tt), BF16)],
        compiler_params=_params("arbitrary", "arbitrary"),
        name="peer_mix",
    )(h2, u_all, vt, *tables, x, gate.reshape(r, 1, d))


def _peer(h2, x, gate, seg, wq, keys, u_all, layer, vt):
    n_heads = keys.shape[0] // 2
    tables = _peer_scores(h2, wq, keys, n_heads)
    return _peer_mix(h2, u_all, layer, vt, tables, x, gate, seg, n_heads)


def _rope_tables(n_tokens):
    t = jnp.arange(n_tokens)
    row = (t // GRID_W).astype(F32)
    col = (t % GRID_W).astype(F32)
    n_freq = HEAD_DIM // 4
    inv_freq = jnp.power(ROPE_BASE, -jnp.arange(n_freq, dtype=F32) / n_freq)
    ang = jnp.concatenate([row[:, None] * inv_freq, col[:, None] * inv_freq], axis=-1)
    cos, sin = jnp.cos(ang), jnp.sin(ang)
    cos2 = jnp.tile(jnp.concatenate([cos, cos], axis=-1), (1, 2))
    sin2 = jnp.tile(jnp.concatenate([-sin, sin], axis=-1), (1, 2))
    return cos2, sin2


def _pair_lanes(v):
    return jnp.repeat(v, HEAD_DIM, axis=-1).reshape(*v.shape[:-1], v.shape[-1] // 2, LANES)


def kernel(x, c, ctx, c_ctx, mod_w, mod_b, norm1_g, norm2_g, ab_w_in, ab_w_out, na_q_norm, na_k_norm,
           na_rpb, ret_log_decay, swa_w_in, swa_w_out, swa_q_norm, swa_k_norm, swa_sink,
           peer_w_q, peer_sub_keys, peer_u, peer_v):
    b, s, d = x.shape
    l = ctx.shape[1]
    depth = mod_w.shape[0]
    assert depth == 2 and b + 1 <= 8
    assert s % (NA_Q_ROWS * GRID_W) == 0 and s // (NA_Q_ROWS * GRID_W) >= 3
    assert s % SWA_Q_TILE == 0 and s // SWA_Q_TILE >= 3 and s % RET_CHUNK == 0 and l % RET_CHUNK == 0
    for tile in (TOKEN_TILE, PROJ_TOKEN_TILE, PEER_TOKEN_TILE, PEER_MIX_TOKEN_TILE):
        assert s % tile == 0 and (b * l) % min(tile, b * l) == 0 and (b * l) % LANES == 0

    cond = jnp.concatenate([c, c_ctx[None, :], jnp.zeros((8 - b - 1, d), F32)], axis=0)
    mods = _adaln(cond, mod_w, mod_b).reshape(depth, 8, 6, d)
    lat = lambda layer, which: mods[layer, :b, which]
    cx = lambda layer, which: mods[layer, b:b + 1, which]

    cos2, sin2 = _rope_tables(s)
    x_lat = x.reshape(b * s, d)
    x_ctx = ctx.reshape(b * l, d)
    tile2 = lambda g: jnp.tile(g, 2).reshape(1, LANES)

    def peer_weights(layer):
        n_heads = peer_sub_keys.shape[1]
        keys = peer_sub_keys[layer].reshape(2 * n_heads, PEER_KEYS, -1).astype(BF16)
        vt = peer_v[layer].reshape(-1, PEER_EXPERT_TILE, d).transpose(0, 2, 1).astype(BF16)
        return peer_w_q[layer].astype(BF16), keys, peer_u, layer, vt

    n_na = na_rpb.shape[1]
    n_ret = ret_log_decay.shape[2]
    na_pairs, ret_pairs = n_na // 2, n_ret // 2
    w_in = ab_w_in[0].astype(BF16)
    w_out = ab_w_out[0].astype(BF16)
    wa = n_na * HEAD_DIM
    p_lat = _modmm(x_lat, norm1_g[0], lat(0, 0), lat(0, 1), w_in, s).reshape(b, s, -1)
    p_ctx = _modmm(x_ctx, norm1_g[0], cx(0, 0), cx(0, 1), w_in, b * l).reshape(b, l, -1)

    qn, kn = tile2(na_q_norm[0]), tile2(na_k_norm[0])
    oa_lat = _na_attention(p_lat, p_ctx, _na_bias(na_rpb[0], s // GRID_W), qn, kn, na_pairs)
    oa_ctx = _ctx_attention(p_ctx, qn, kn, na_pairs)

    lg = _pair_lanes(ret_log_decay[0]).transpose(1, 0, 2)
    ret_col0 = 3 * wa // LANES
    zeros_state = jnp.zeros((b, ret_pairs, 4, HEAD_DIM, HEAD_DIM), F32)
    ones_tab, zeros_tab = jnp.ones((l, LANES), F32), jnp.zeros((l, LANES), F32)
    ob_ctx, st_ctx = _retention(p_ctx, ones_tab, zeros_tab, lg, zeros_state, ret_col0, ret_pairs, False)
    ob_lat, _ = _retention(p_lat, cos2, sin2, lg, st_ctx, ret_col0, ret_pairs, True)

    w_list = [w_out[:wa], w_out[wa:]]
    x_lat, h_lat = _outproj([oa_lat.reshape(b * s, -1), ob_lat.reshape(b * s, -1)], w_list, x_lat,
                            lat(0, 2), norm2_g[0], lat(0, 3), lat(0, 4), s)
    x_ctx, h_ctx = _outproj([oa_ctx.reshape(b * l, -1), ob_ctx.reshape(b * l, -1)], w_list, x_ctx,
                            cx(0, 2), norm2_g[0], cx(0, 3), cx(0, 4), b * l)
    pw = peer_weights(0)
    x_lat = _peer(h_lat, x_lat, lat(0, 5), s, *pw)
    x_ctx = _peer(h_ctx, x_ctx, cx(0, 5), b * l, *pw)

    n_q = swa_sink.shape[1]
    n_kv = (swa_w_in.shape[2] // HEAD_DIM - n_q) // 2
    w_in = swa_w_in[0].astype(BF16)
    p_lat = _modmm(x_lat, norm1_g[1], lat(1, 0), lat(1, 1), w_in, s).reshape(b, s, -1)
    p_ctx = _modmm(x_ctx, norm1_g[1], cx(1, 0), cx(1, 1), w_in, b * l).reshape(b, l, -1)
    sink_rows = jnp.broadcast_to(swa_sink[0][:, None], (n_q, LANES))
    o_lat = _swa_attention(p_lat, p_ctx, cos2, sin2, tile2(swa_q_norm[0]), tile2(swa_k_norm[0]),
                           sink_rows, n_q, n_kv)
    x_lat, h_lat = _outproj([o_lat.reshape(b * s, -1)], [swa_w_out[0].astype(BF16)], x_lat,
                            lat(1, 2), norm2_g[1], lat(1, 3), lat(1, 4), s)
    x_lat = _peer(h_lat, x_lat, lat(1, 5), s, *peer_weights(1))
    return x_lat.reshape(b, s, d)
```

```python
import functools

import numpy as np
import jax
import jax.numpy as jnp
from jax import lax
from jax.experimental import pallas as pl
from jax.experimental.pallas import tpu as pltpu

F32 = jnp.float32
BF16 = jnp.bfloat16

HEAD_DIM = 64
GRID_W = 64
NA_ROWS = 8
NA_COLS = 16
SWA_WINDOW = 128
PEER_TOPK = 16
PEER_KEYS = 128
ROPE_BASE = 10000.0
NORM_EPS = 1e-6
GN_EPS = 1e-5
NEG_INF = -1e30
ATTN_SCALE = HEAD_DIM ** -0.5
INV_SQRT2 = 0.7071067811865476

LANES = 128
VMEM_LIMIT = 56 * 1024 * 1024

TOKEN_TILE = 1024
PROJ_TOKEN_TILE = 1024
PROJ_N_TILE = 1792
RET_CHUNK = 256
NA_Q_ROWS = 8
NA_BAND_ROWS = 16
SWA_Q_TILE = 256
PEER_TOKEN_TILE = 1024
PEER_MIX_TOKEN_TILE = 1024
PEER_EXPERT_TILE = 1024


def _dot(a, b):
    return jnp.dot(a, b, preferred_element_type=F32)


def _dot_nt(a, b):
    return lax.dot_general(a, b, (((1,), (1,)), ((), ())), preferred_element_type=F32)


def _dot_tn(a, b):
    return lax.dot_general(a, b, (((0,), (0,)), ((), ())), preferred_element_type=F32)


def _params(*sem):
    return pltpu.CompilerParams(dimension_semantics=sem, vmem_limit_bytes=VMEM_LIMIT)


def _rms_rows(x, gain):
    ms = jnp.mean(x * x, axis=-1, keepdims=True)
    return x * lax.rsqrt(ms + NORM_EPS) * gain


def _modulate(x, gain, shift, scale):
    return _rms_rows(x, gain) * (1.0 + scale) + shift


def _head_rms(x, gain):
    lane = lax.broadcasted_iota(jnp.int32, x.shape, 1)
    lo = lane < HEAD_DIM
    ss = x * x
    s_lo = jnp.sum(jnp.where(lo, ss, 0.0), axis=-1, keepdims=True)
    s_hi = jnp.sum(jnp.where(lo, 0.0, ss), axis=-1, keepdims=True)
    ms = jnp.where(lo, s_lo, s_hi) * (1.0 / HEAD_DIM)
    return x * lax.rsqrt(ms + NORM_EPS) * gain


def _rope(x, cos2, sin2):
    lane = lax.broadcasted_iota(jnp.int32, x.shape, 1)
    first_half = (lane & (HEAD_DIM // 2)) == 0
    swapped = jnp.where(first_half, pltpu.roll(x, LANES - HEAD_DIM // 2, axis=1),
                        pltpu.roll(x, HEAD_DIM // 2, axis=1))
    return x * cos2 + swapped * sin2


def _adaln_kernel(c_ref, w_ref, b_ref, o_ref):
    c = c_ref[...]
    s = c * jax.nn.sigmoid(c)
    w = w_ref[0]
    s_hi = s.astype(BF16).astype(F32)
    w_hi = w.astype(BF16)
    w_lo = (w - w_hi.astype(F32)).astype(BF16)
    rows = s.shape[0]
    both = _dot(jnp.concatenate([s_hi, s - s_hi], axis=0).astype(BF16), w_hi)
    o_ref[0] = both[:rows] + both[rows:] + _dot(s_hi.astype(BF16), w_lo) + b_ref[0]


def _adaln(cond, mod_w, mod_b):
    depth, d, n = mod_w.shape
    tn = n // 4
    return pl.pallas_call(
        _adaln_kernel,
        grid=(depth, n // tn),
        in_specs=[pl.BlockSpec((8, d), lambda l, j: (0, 0)),
                  pl.BlockSpec((1, d, tn), lambda l, j: (l, 0, j)),
                  pl.BlockSpec((1, 1, tn), lambda l, j: (l, 0, j))],
        out_specs=pl.BlockSpec((1, 8, tn), lambda l, j: (l, 0, j)),
        out_shape=jax.ShapeDtypeStruct((depth, 8, n), F32),
        compiler_params=_params("arbitrary", "arbitrary"),
        name="adaln",
    )(cond, mod_w, mod_b.reshape(depth, 1, n))


def _modmm_kernel(x_ref, g_ref, sh_ref, sc_ref, w_ref, o_ref, h_scr):
    @pl.when(pl.program_id(1) == 0)
    def _():
        h_scr[...] = _modulate(x_ref[...], g_ref[...], sh_ref[0], sc_ref[0]).astype(BF16)

    o_ref[...] = _dot(h_scr[...], w_ref[...])


def _modmm(x, gain, shift, scale, w, seg):
    t, d = x.shape
    n = w.shape[1]
    tn = n
    tm = min(PROJ_TOKEN_TILE if n <= PROJ_N_TILE else PROJ_TOKEN_TILE // 2, seg)
    per_seg = seg // tm
    r = shift.shape[0]
    return pl.pallas_call(
        _modmm_kernel,
        grid=(t // tm, n // tn),
        in_specs=[pl.BlockSpec((tm, d), lambda i, j: (i, 0)),
                  pl.BlockSpec((1, d), lambda i, j: (0, 0)),
                  pl.BlockSpec((1, 1, d), lambda i, j: (i // per_seg, 0, 0)),
                  pl.BlockSpec((1, 1, d), lambda i, j: (i // per_seg, 0, 0)),
                  pl.BlockSpec((d, tn), lambda i, j: (0, j))],
        out_specs=pl.BlockSpec((tm, tn), lambda i, j: (i, j)),
        out_shape=jax.ShapeDtypeStruct((t, n), F32),
        scratch_shapes=[pltpu.VMEM((tm, d), BF16)],
        compiler_params=_params("arbitrary", "arbitrary"),
        name="modulate_matmul",
    )(x, gain.reshape(1, d), shift.reshape(r, 1, d), scale.reshape(r, 1, d), w)


def _outproj_kernel(*refs, n_in):
    a_refs = refs[:n_in]
    w_refs = refs[n_in:2 * n_in]
    x_ref, gate_ref, g2_ref, sh_ref, sc_ref, xo_ref, h_ref = refs[2 * n_in:]
    acc = None
    for a_ref, w_ref in zip(a_refs, w_refs):
        part = _dot(a_ref[...].astype(BF16), w_ref[...])
        acc = part if acc is None else acc + part
    xn = x_ref[...] + gate_ref[0] * acc
    xo_ref[...] = xn
    h_ref[...] = _modulate(xn, g2_ref[...], sh_ref[0], sc_ref[0]).astype(BF16)


def _outproj(a_list, w_list, x, gate, gain2, shift2, scale2, seg):
    t, d = x.shape
    tm = min(TOKEN_TILE, seg)
    per_seg = seg // tm
    r = gate.shape[0]
    n_in = len(a_list)
    row = lambda i: (i // per_seg, 0, 0)
    in_specs = ([pl.BlockSpec((tm, a.shape[1]), lambda i: (i, 0)) for a in a_list]
                + [pl.BlockSpec(w.shape, lambda i: (0, 0)) for w in w_list]
                + [pl.BlockSpec((tm, d), lambda i: (i, 0)),
                   pl.BlockSpec((1, 1, d), row),
                   pl.BlockSpec((1, d), lambda i: (0, 0)),
                   pl.BlockSpec((1, 1, d), row),
                   pl.BlockSpec((1, 1, d), row)])
    return pl.pallas_call(
        functools.partial(_outproj_kernel, n_in=n_in),
        grid=(t // tm,),
        in_specs=in_specs,
        out_specs=[pl.BlockSpec((tm, d), lambda i: (i, 0)), pl.BlockSpec((tm, d), lambda i: (i, 0))],
        out_shape=[jax.ShapeDtypeStruct((t, d), F32), jax.ShapeDtypeStruct((t, d), BF16)],
        compiler_params=_params("arbitrary"),
        name="out_proj_residual",
    )(*a_list, *w_list, x, gate.reshape(r, 1, d), gain2.reshape(1, d),
      shift2.reshape(r, 1, d), scale2.reshape(r, 1, d))


def _softmax_pv(s_list, v_list, extra=None, fold_lanes=False):
    def lane_chunks(x):
        return [x[:, c:c + LANES] for c in range(0, x.shape[1], LANES)] if fold_lanes else [x]

    def row_reduce(blocks, op, reduce):
        acc = None
        for blk in blocks:
            chunks = lane_chunks(blk)
            part = chunks[0]
            for c in chunks[1:]:
                part = op(part, c)
            if not fold_lanes:
                part = reduce(part, axis=-1, keepdims=True)
            acc = part if acc is None else op(acc, part)
        return reduce(acc, axis=-1, keepdims=True) if fold_lanes else acc

    m = row_reduce(s_list, jnp.maximum, jnp.max)
    if extra is not None:
        m = jnp.maximum(m, extra)
    ps = []
    out = None
    for s, v in zip(s_list, v_list):
        p = jnp.exp(s - m)
        ps.append(p)
        pv = _dot(p.astype(BF16), v)
        out = pv if out is None else out + pv
    denom = row_reduce(ps, jnp.add, jnp.sum)
    if extra is not None:
        denom = denom + jnp.exp(extra - m)
    return out / denom


def _na_kernel(q_ref, k_ref, v_ref, kc_ref, vc_ref, bias_ref, qn_ref, kn_ref, o_ref,
               k_scr, v_scr, kc_scr, vc_scr, *, n_steps):
    i = pl.program_id(2)
    band = NA_BAND_ROWS * GRID_W
    start = jnp.clip(NA_Q_ROWS * i - NA_ROWS // 2, 0, NA_Q_ROWS * n_steps - NA_BAND_ROWS) * GRID_W
    start = pl.multiple_of(start, NA_ROWS // 2 * GRID_W)

    @pl.when(i == 0)
    def _():
        k_scr[...] = _head_rms(k_ref[0], kn_ref[...]).astype(BF16)
        v_scr[...] = v_ref[0].astype(BF16)
        kc_scr[...] = _head_rms(kc_ref[0], kn_ref[...]).astype(BF16)
        vc_scr[...] = vc_ref[0].astype(BF16)

    q = (_head_rms(q_ref[0], qn_ref[...]) * ATTN_SCALE).astype(BF16)
    kb = k_scr[pl.ds(start, band), :]
    vb = v_scr[pl.ds(start, band), :]
    kc = kc_scr[...]
    vc = vc_scr[...]
    outs = []
    for h in range(2):
        sl = slice(h * HEAD_DIM, (h + 1) * HEAD_DIM)
        s_loc = _dot_nt(q[:, sl], kb[:, sl]) + bias_ref[0, h]
        s_ctx = _dot_nt(q[:, sl], kc[:, sl])
        outs.append(_softmax_pv([s_loc, s_ctx], [vb[:, sl], vc[:, sl]]))
    o_ref[0] = jnp.concatenate(outs, axis=-1).astype(o_ref.dtype)


def _na_bias(rpb, rows):
    h = rpb.shape[0]
    col = np.arange(GRID_W)
    c0 = np.clip(col - NA_COLS // 2, 0, GRID_W - NA_COLS)
    dc = col[None, :] - col[:, None] + (NA_COLS - 1)
    ok_c = (col[None, :] >= c0[:, None]) & (col[None, :] < c0[:, None] + NA_COLS)
    pick = ((np.arange(2 * NA_COLS - 1)[:, None, None] == dc[None]) & ok_c[None]).astype(np.float32)
    blocks = jnp.einsum('hrd,dqk->hrqk', rpb, pick, precision=lax.Precision.HIGHEST)
    blocks = jnp.where(ok_c, blocks, NEG_INF)
    qt, band = NA_Q_ROWS * GRID_W, NA_BAND_ROWS * GRID_W
    return pl.pallas_call(
        functools.partial(_na_bias_kernel, rows=rows),
        grid=(3, h),
        in_specs=[pl.BlockSpec((1,) + blocks.shape[1:], lambda c, hh: (hh, 0, 0, 0))],
        out_specs=pl.BlockSpec((1, 1, qt, band), lambda c, hh: (c, hh, 0, 0)),
        out_shape=jax.ShapeDtypeStruct((3, h, qt, band), F32),
        compiler_params=_params("arbitrary", "arbitrary"),
        name="na_bias_tables",
    )(blocks)


def _na_bias_kernel(blk_ref, o_ref, *, rows):
    n_steps = rows // NA_Q_ROWS
    for c, step in enumerate((0, 1, n_steps - 1)):
        @pl.when(pl.program_id(0) == c)
        def _(step=step):
            o_ref[0, 0] = jnp.full(o_ref.shape[2:], NEG_INF, F32)
            band0 = min(max(NA_Q_ROWS * step - NA_ROWS // 2, 0), rows - NA_BAND_ROWS)
            for rq in range(NA_Q_ROWS):
                r = NA_Q_ROWS * step + rq
                r0 = min(max(r - NA_ROWS // 2, 0), rows - NA_ROWS)
                for rk in range(r0, r0 + NA_ROWS):
                    col0 = (rk - band0) * GRID_W
                    o_ref[0, 0, rq * GRID_W:(rq + 1) * GRID_W, col0:col0 + GRID_W] = (
                        blk_ref[0, rk - r + NA_ROWS - 1])


def _na_attention(p_lat, p_ctx, bias, qn, kn, n_pairs):
    b, s, _ = p_lat.shape
    l = p_ctx.shape[1]
    rows = s // GRID_W
    n_steps = rows // NA_Q_ROWS
    qt = NA_Q_ROWS * GRID_W

    def cls(i):
        return jnp.where(i == 0, 0, jnp.where(i == n_steps - 1, 2, 1))

    return pl.pallas_call(
        functools.partial(_na_kernel, n_steps=n_steps),
        grid=(n_pairs, b, n_steps),
        in_specs=[pl.BlockSpec((1, qt, LANES), lambda hp, bb, i: (bb, i, hp)),
                  pl.BlockSpec((1, s, LANES), lambda hp, bb, i: (bb, 0, n_pairs + hp)),
                  pl.BlockSpec((1, s, LANES), lambda hp, bb, i: (bb, 0, 2 * n_pairs + hp)),
                  pl.BlockSpec((1, l, LANES), lambda hp, bb, i: (bb, 0, n_pairs + hp)),
                  pl.BlockSpec((1, l, LANES), lambda hp, bb, i: (bb, 0, 2 * n_pairs + hp)),
                  pl.BlockSpec((1, 2, qt, NA_BAND_ROWS * GRID_W), lambda hp, bb, i: (cls(i), hp, 0, 0)),
                  pl.BlockSpec((1, LANES), lambda hp, bb, i: (0, 0)),
                  pl.BlockSpec((1, LANES), lambda hp, bb, i: (0, 0))],
        out_specs=pl.BlockSpec((1, qt, LANES), lambda hp, bb, i: (bb, i, hp)),
        out_shape=jax.ShapeDtypeStruct((b, s, n_pairs * LANES), BF16),
        scratch_shapes=[pltpu.VMEM((s, LANES), BF16), pltpu.VMEM((s, LANES), BF16),
                        pltpu.VMEM((l, LANES), BF16), pltpu.VMEM((l, LANES), BF16)],
        compiler_params=_params("arbitrary", "arbitrary", "arbitrary"),
        name="neighbourhood_attention",
    )(p_lat, p_lat, p_lat, p_ctx, p_ctx, bias, qn, kn)


def _ctx_attn_kernel(q_ref, k_ref, v_ref, qn_ref, kn_ref, o_ref):
    q = (_head_rms(q_ref[0], qn_ref[...]) * ATTN_SCALE).astype(BF16)
    k = _head_rms(k_ref[0], kn_ref[...]).astype(BF16)
    v = v_ref[0].astype(BF16)
    outs = []
    for h in range(2):
        sl = slice(h * HEAD_DIM, (h + 1) * HEAD_DIM)
        outs.append(_softmax_pv([_dot_nt(q[:, sl], k[:, sl])], [v[:, sl]]))
    o_ref[0] = jnp.concatenate(outs, axis=-1).astype(o_ref.dtype)


def _ctx_attention(p_ctx, qn, kn, n_pairs):
    b, l, _ = p_ctx.shape
    return pl.pallas_call(
        _ctx_attn_kernel,
        grid=(n_pairs, b),
        in_specs=[pl.BlockSpec((1, l, LANES), lambda hp, bb: (bb, 0, hp)),
                  pl.BlockSpec((1, l, LANES), lambda hp, bb: (bb, 0, n_pairs + hp)),
                  pl.BlockSpec((1, l, LANES), lambda hp, bb: (bb, 0, 2 * n_pairs + hp)),
                  pl.BlockSpec((1, LANES), lambda hp, bb: (0, 0)),
                  pl.BlockSpec((1, LANES), lambda hp, bb: (0, 0))],
        out_specs=pl.BlockSpec((1, l, LANES), lambda hp, bb: (bb, 0, hp)),
        out_shape=jax.ShapeDtypeStruct((b, l, n_pairs * LANES), BF16),
        compiler_params=_params("arbitrary", "arbitrary"),
        name="context_attention",
    )(p_ctx, p_ctx, p_ctx, qn, kn)


def _ret_kernel(q_ref, k_ref, v_ref, g_ref, cos_ref, sin_ref, lg_ref, s0_ref, y_ref, st_ref, sf_scr,
                *, n_chunks, use_rope):
    c = RET_CHUNK
    hd = HEAD_DIM
    lg = -jnp.exp(lg_ref[0])
    lgf, lgb = lg[0:1, :], lg[1:2, :]
    ii = lax.broadcasted_iota(jnp.int32, (c, LANES), 0).astype(F32)
    dq_f = jnp.exp(lgf * (ii + 1.0))
    dk_f = jnp.exp(lgf * (c - 1.0 - ii))
    dq_b = jnp.exp(lgb * (c - ii))
    dk_b = jnp.exp(lgb * ii)
    dc_f = jnp.exp(lgf * float(c))
    dc_b = jnp.exp(lgb * float(c))
    diff = (lax.broadcasted_iota(jnp.int32, (c, c), 0) - lax.broadcasted_iota(jnp.int32, (c, c), 1)).astype(F32)
    intra = []
    chunk_f = []
    chunk_b = []
    for h in range(2):
        lf = lgf[:, h * hd:h * hd + 1]
        lb = lgb[:, h * hd:h * hd + 1]
        intra.append(jnp.where(diff >= 0, jnp.exp(lf * jnp.maximum(diff, 0.0)),
                               jnp.exp(lb * jnp.maximum(-diff, 0.0))))
        chunk_f.append(dc_f[:, h * hd:h * hd + 1])
        chunk_b.append(dc_b[:, h * hd:h * hd + 1])

    def load(n):
        r = pl.multiple_of(n * c, c)
        q = q_ref[0, pl.ds(r, c), :]
        k = k_ref[0, pl.ds(r, c), :]
        v = v_ref[0, pl.ds(r, c), :]
        if use_rope:
            cs = cos_ref[pl.ds(r, c), :]
            sn = sin_ref[pl.ds(r, c), :]
            q = _rope(q, cs, sn)
            k = _rope(k, cs, sn)
        return r, q * ATTN_SCALE, k, v.astype(BF16)

    def fwd(n, carry):
        _, _, k, v = load(n)
        kd = (k * dk_f).astype(BF16)
        new = []
        for h in range(2):
            sl = slice(h * hd, (h + 1) * hd)
            sf_scr[n, h] = carry[h]
            new.append(carry[h] * chunk_f[h] + _dot_tn(kd[:, sl], v[:, sl]))
        return tuple(new)

    sf = lax.fori_loop(0, n_chunks, fwd, (s0_ref[0, 0, 0], s0_ref[0, 0, 1]), unroll=2)
    st_ref[0, 0, 0] = sf[0]
    st_ref[0, 0, 1] = sf[1]

    def bwd(jj, carry):
        n = n_chunks - 1 - jj
        r, q, k, v = load(n)
        qb = q.astype(BF16)
        kb = k.astype(BF16)
        qf = (q * dq_f).astype(BF16)
        qr = (q * dq_b).astype(BF16)
        kd = (k * dk_b).astype(BF16)
        outs = []
        new = []
        for h in range(2):
            sl = slice(h * hd, (h + 1) * hd)
            a = (_dot_nt(qb[:, sl], kb[:, sl]) * intra[h]).astype(BF16)
            o = (_dot(a, v[:, sl]) + _dot(qf[:, sl], sf_scr[n, h].astype(BF16))
                 + _dot(qr[:, sl], carry[h].astype(BF16)))
            oc = o - jnp.mean(o, axis=-1, keepdims=True)
            outs.append(oc * lax.rsqrt(jnp.mean(oc * oc, axis=-1, keepdims=True) + GN_EPS))
            new.append(carry[h] * chunk_b[h] + _dot_tn(kd[:, sl], v[:, sl]))
        g = g_ref[0, pl.ds(r, c), :]
        y_ref[0, pl.ds(r, c), :] = (jnp.concatenate(outs, axis=-1) * (g * jax.nn.sigmoid(g))).astype(y_ref.dtype)
        return tuple(new)

    sb = lax.fori_loop(0, n_chunks, bwd, (s0_ref[0, 0, 2], s0_ref[0, 0, 3]), unroll=2)
    st_ref[0, 0, 2] = sb[0]
    st_ref[0, 0, 3] = sb[1]


def _retention(p, cos2, sin2, lg, s0, col0, n_pairs, use_rope):
    b, t, _ = p.shape
    n_chunks = t // RET_CHUNK
    tab = lambda hp, bb: (0, 0)
    return pl.pallas_call(
        functools.partial(_ret_kernel, n_chunks=n_chunks, use_rope=use_rope),
        grid=(n_pairs, b),
        in_specs=[pl.BlockSpec((1, t, LANES), lambda hp, bb: (bb, 0, col0 + hp)),
                  pl.BlockSpec((1, t, LANES), lambda hp, bb: (bb, 0, col0 + n_pairs + hp)),
                  pl.BlockSpec((1, t, LANES), lambda hp, bb: (bb, 0, col0 + 2 * n_pairs + hp)),
                  pl.BlockSpec((1, t, LANES), lambda hp, bb: (bb, 0, col0 + 3 * n_pairs + hp)),
                  pl.BlockSpec(cos2.shape, tab),
                  pl.BlockSpec(sin2.shape, tab),
                  pl.BlockSpec((1, 2, LANES), lambda hp, bb: (hp, 0, 0)),
                  pl.BlockSpec((1, 1, 4, HEAD_DIM, HEAD_DIM), lambda hp, bb: (bb, hp, 0, 0, 0))],
        out_specs=[pl.BlockSpec((1, t, LANES), lambda hp, bb: (bb, 0, hp)),
                   pl.BlockSpec((1, 1, 4, HEAD_DIM, HEAD_DIM), lambda hp, bb: (bb, hp, 0, 0, 0))],
        out_shape=[jax.ShapeDtypeStruct((b, t, n_pairs * LANES), BF16),
                   jax.ShapeDtypeStruct((b, n_pairs, 4, HEAD_DIM, HEAD_DIM), F32)],
        scratch_shapes=[pltpu.VMEM((n_chunks, 2, HEAD_DIM, HEAD_DIM), F32)],
        compiler_params=_params("arbitrary", "arbitrary"),
        name="retention",
    )(p, p, p, p, cos2, sin2, lg, s0)


def _swa_kernel(q_ref, k_ref, v_ref, kc_ref, vc_ref, cos_ref, sin_ref, qn_ref, kn_ref, sink_ref, far_ref,
                o_ref, k_scr, kc_scr, *, seq):
    qt = SWA_Q_TILE
    wk = qt + 2 * SWA_WINDOW
    hd = HEAD_DIM
    n = pl.program_id(2)
    q0 = pl.multiple_of(n * qt, qt)
    ws = pl.multiple_of(jnp.clip(n * qt - SWA_WINDOW, 0, seq - wk), SWA_WINDOW)

    @pl.when(n == 0)
    def _():
        k_scr[...] = _rope(_head_rms(k_ref[0], kn_ref[...]), cos_ref[...], sin_ref[...])
        kc_scr[...] = _head_rms(kc_ref[0], kn_ref[...])

    kw = k_scr[pl.ds(ws, wk), :].astype(BF16)
    vw = v_ref[0, pl.ds(ws, wk), :].astype(BF16)
    kc = kc_scr[...].astype(BF16)
    vc = vc_ref[0].astype(BF16)
    cos_q = cos_ref[pl.ds(q0, qt), :]
    sin_q = sin_ref[pl.ds(q0, qt), :]
    qs = []
    for s in range(4):
        slab = q_ref[0, :, s * LANES:(s + 1) * LANES]
        qs.append((_rope(_head_rms(slab, qn_ref[...]), cos_q, sin_q) * ATTN_SCALE).astype(BF16))
    far = far_ref[0][None]
    for kh in range(2):
        sl = slice(kh * hd, (kh + 1) * hd)
        qstack = jnp.concatenate(
            [qs[kh * 2 + g // 2][:, (g % 2) * hd:(g % 2 + 1) * hd] for g in range(4)], axis=0)
        sink = jnp.concatenate(
            [jnp.broadcast_to(sink_ref[kh * 4 + g:kh * 4 + g + 1, 0:1], (qt, 1)) for g in range(4)], axis=0)
        s_loc = (_dot_nt(qstack, kw[:, sl]).reshape(4, qt, wk) + far).reshape(4 * qt, wk)
        s_ctx = _dot_nt(qstack, kc[:, sl])
        o = _softmax_pv([s_loc, s_ctx], [vw[:, sl], vc[:, sl]], extra=sink, fold_lanes=True)
        for pair in range(2):
            col = (kh * 2 + pair) * LANES
            o_ref[0, :, col:col + LANES] = jnp.concatenate(
                [o[(2 * pair) * qt:(2 * pair + 1) * qt], o[(2 * pair + 1) * qt:(2 * pair + 2) * qt]],
                axis=-1).astype(o_ref.dtype)


def _swa_attention(p_lat, p_ctx, cos2, sin2, qn, kn, sink_rows, n_q_heads, n_kv_heads):
    b, s, _ = p_lat.shape
    l = p_ctx.shape[1]
    kv_pairs = n_kv_heads // 2
    q_blocks = n_q_heads * HEAD_DIM // LANES
    q_per_pair = q_blocks // kv_pairs
    qw = q_per_pair * LANES
    tab = lambda kp, bb, n: (0, 0)
    n_steps = s // SWA_Q_TILE
    wk = SWA_Q_TILE + 2 * SWA_WINDOW
    far = []
    for step in (0, 1, n_steps - 1):
        q0 = step * SWA_Q_TILE
        ws = min(max(q0 - SWA_WINDOW, 0), s - wk)
        dist = np.abs((q0 + np.arange(SWA_Q_TILE))[:, None] - (ws + np.arange(wk))[None, :])
        far.append(np.where(dist <= SWA_WINDOW, 0.0, NEG_INF))
    far = jnp.asarray(np.stack(far), F32)

    def cls(n):
        return jnp.where(n == 0, 0, jnp.where(n == n_steps - 1, 2, 1))

    return pl.pallas_call(
        functools.partial(_swa_kernel, seq=s),
        grid=(kv_pairs, b, n_steps),
        in_specs=[pl.BlockSpec((1, SWA_Q_TILE, qw), lambda kp, bb, n: (bb, n, kp)),
                  pl.BlockSpec((1, s, LANES), lambda kp, bb, n: (bb, 0, q_blocks + kp)),
                  pl.BlockSpec((1, s, LANES), lambda kp, bb, n: (bb, 0, q_blocks + kv_pairs + kp)),
                  pl.BlockSpec((1, l, LANES), lambda kp, bb, n: (bb, 0, q_blocks + kp)),
                  pl.BlockSpec((1, l, LANES), lambda kp, bb, n: (bb, 0, q_blocks + kv_pairs + kp)),
                  pl.BlockSpec(cos2.shape, tab),
                  pl.BlockSpec(sin2.shape, tab),
                  pl.BlockSpec((1, LANES), tab),
                  pl.BlockSpec((1, LANES), tab),
                  pl.BlockSpec((8, LANES), lambda kp, bb, n: (kp, 0)),
                  pl.BlockSpec((1, SWA_Q_TILE, wk), lambda kp, bb, n: (cls(n), 0, 0))],
        out_specs=pl.BlockSpec((1, SWA_Q_TILE, qw), lambda kp, bb, n: (bb, n, kp)),
        out_shape=jax.ShapeDtypeStruct((b, s, n_q_heads * HEAD_DIM), BF16),
        scratch_shapes=[pltpu.VMEM((s, LANES), F32), pltpu.VMEM((l, LANES), F32)],
        compiler_params=_params("arbitrary", "arbitrary", "arbitrary"),
        name="windowed_gqa",
    )(p_lat, p_lat, p_lat, p_ctx, p_ctx, cos2, sin2, qn, kn, sink_rows, far)


def _oddeven_merge_sort(n):
    pairs = []

    def merge(lo, size, r):
        step = r * 2
        if step < size:
            merge(lo, size, step)
            merge(lo + r, size, step)
            pairs.extend((i, i + r) for i in range(lo + r, lo + size - r, step))
        else:
            pairs.append((lo, lo + r))

    def sort(lo, size):
        if size > 1:
            sort(lo, size // 2)
            sort(lo + size // 2, size // 2)
            merge(lo, size, 1)

    sort(0, n)
    return pairs


def _top_rows(x, k, scr):
    tiles = [x[8 * i:8 * i + 8, :] for i in range(x.shape[0] // 8)]
    for i, j in _oddeven_merge_sort(len(tiles)):
        tiles[i], tiles[j] = jnp.maximum(tiles[i], tiles[j]), jnp.minimum(tiles[i], tiles[j])
    for r in range(k):
        m = jnp.max(tiles[0], axis=0, keepdims=True)
        scr[r:r + 1, :] = m
        need = k - r - 1
        if need > 0:
            hit = tiles[0] == m
            for d in range(min(need, len(tiles) - 1)):
                tiles[d] = jnp.where(hit, tiles[d + 1], tiles[d])
            if need >= len(tiles):
                tiles[-1] = jnp.where(hit, NEG_INF, tiles[-1])


def _bf16_pair(x):
    bits = lax.bitcast_convert_type(x.astype(BF16).astype(F32), jnp.uint32)
    return bits | (bits >> 16)


def _count_above(sorted_scr, y, strict):
    row = lambda i: sorted_scr[i:i + 1, :]
    above = (lambda r: r > y) if strict else (lambda r: r >= y)
    c8 = above(row(7))
    c4 = above(jnp.where(c8, row(11), row(3)))
    c2 = above(jnp.where(c8, jnp.where(c4, row(13), row(9)), jnp.where(c4, row(5), row(1))))
    hi = jnp.where(c4, jnp.where(c2, row(14), row(12)), jnp.where(c2, row(10), row(8)))
    lo = jnp.where(c4, jnp.where(c2, row(6), row(4)), jnp.where(c2, row(2), row(0)))
    c1 = above(jnp.where(c8, hi, lo))
    count = (jnp.where(c8, 8.0, 0.0) + jnp.where(c4, 4.0, 0.0)) + (jnp.where(c2, 2.0, 0.0) + jnp.where(c1, 1.0, 0.0))
    return count + jnp.where(above(row(15)), 1.0, 0.0)


def _peer_scores_kernel(h_ref, wq_ref, keys_ref, ra_ref, p1_ref, gb_ref, p2_ref,
                        q_scr, s_scr, a_scr, b_scr, c_scr, v_scr, *, n_heads):
    k = PEER_TOPK
    n_blocks = h_ref.shape[0] // LANES

    q = _dot(h_ref[...], wq_ref[...])
    for hp in range(2 * n_heads):
        q_scr[hp] = q[:, hp * PEER_KEYS:(hp + 1) * PEER_KEYS].astype(BF16)

    def lane_block(lb, h, slot):
        tops_a, tops_b, cand, tops_c = a_scr.at[slot], b_scr.at[slot], c_scr.at[slot], v_scr.at[slot]
        s1 = s_scr[0, lb]
        s2 = s_scr[1, lb]
        _top_rows(s1, k + 1, tops_a)
        _top_rows(s2, k + 1, tops_b)
        cand[0:16, :] = tops_a[0:1, :] + tops_b[0:16, :]
        for i in range(1, 8):
            cand[8 + 8 * i:16 + 8 * i, :] = tops_a[i:i + 1, :] + tops_b[0:8, :]
        cand[72:80, :] = tops_a[8:16, :] + tops_b[0:1, :]
        cand[80:81, :] = tops_a[0:1, :] + tops_b[16:17, :]
        cand[81:82, :] = tops_a[16:17, :] + tops_b[0:1, :]
        cand[82:PEER_KEYS, :] = jnp.full((PEER_KEYS - 82, LANES), NEG_INF, F32)
        _top_rows(cand[...], k + 1, tops_c)
        thr = 0.5 * (tops_c[k - 1:k, :] + tops_c[k:k + 1, :])
        z = jnp.sum(jnp.exp(tops_c[0:k, :] - tops_c[0:1, :]), axis=0, keepdims=True)
        ra_ref[lb, h] = _bf16_pair(_count_above(tops_a, s1, strict=True))
        p1_ref[lb, h] = _bf16_pair(jnp.exp(s1 - tops_a[0:1, :]) * (INV_SQRT2 / z))
        gb_ref[lb, h] = _count_above(tops_a, thr - s2, strict=False).astype(BF16)
        p2_ref[lb, h] = jnp.exp(s2 - tops_b[0:1, :]).astype(BF16)

    def lane_pair(i, h):
        lane_block(2 * i, h, 0)
        lane_block(2 * i + 1, h, 1)
        return h

    def head(h, carry):
        for p in range(2):
            st = _dot_nt(keys_ref[2 * h + p], q_scr[2 * h + p])
            for lb in range(n_blocks):
                s_scr[p, lb] = st[:, lb * LANES:(lb + 1) * LANES]
        lax.fori_loop(0, n_blocks // 2, lane_pair, h)
        return carry

    lax.fori_loop(0, n_heads, head, 0)


def _peer_scores(h2, wq, keys, n_heads):
    t, d = h2.shape
    tt = PEER_TOKEN_TILE
    nb = tt // LANES
    tab_shape = (t // LANES, n_heads, PEER_KEYS, LANES)
    tab_spec = pl.BlockSpec((nb, n_heads, PEER_KEYS, LANES), lambda i: (i, 0, 0, 0))
    return pl.pallas_call(
        functools.partial(_peer_scores_kernel, n_heads=n_heads),
        grid=(t // tt,),
        in_specs=[pl.BlockSpec((tt, d), lambda i: (i, 0)),
                  pl.BlockSpec(wq.shape, lambda i: (0, 0)),
                  pl.BlockSpec(keys.shape, lambda i: (0, 0, 0))],
        out_specs=[tab_spec] * 4,
        out_shape=[jax.ShapeDtypeStruct(tab_shape, dt) for dt in (jnp.uint32, jnp.uint32, BF16, BF16)],
        scratch_shapes=[pltpu.VMEM((2 * n_heads, tt, PEER_KEYS), BF16),
                        pltpu.VMEM((2, nb, PEER_KEYS, LANES), F32),
                        pltpu.VMEM((2, 24, LANES), F32), pltpu.VMEM((2, 24, LANES), F32),
                        pltpu.VMEM((2, PEER_KEYS, LANES), F32), pltpu.VMEM((2, 24, LANES), F32)],
        compiler_params=_params("arbitrary"),
        name="peer_scores",
    )(h2, wq, keys)


def _peer_mix_kernel(h_ref, u_ref, vt_ref, ra_ref, p1_ref, gb_ref, p2_ref, x_ref, g_ref, o_ref,
                     acc_scr, act_scr, a_scr, *, n_heads, n_tiles):
    s = pl.program_id(1)
    tt = h_ref.shape[0]
    rows_per_tile = PEER_EXPERT_TILE // PEER_KEYS
    tile = (PEER_KEYS, LANES)

    def readout():
        acc_scr[...] += _dot(vt_ref[0], a_scr[...])

    def row_tile(ref, lb, h, aa):
        words = jnp.broadcast_to(ref[lb, h, aa:aa + 1, :], (8, LANES))
        packed = pltpu.bitcast(words, BF16)
        return jnp.broadcast_to(packed[None], (PEER_KEYS // 16, 16, LANES)).reshape(tile)

    def gates():
        for aa in range(rows_per_tile):
            rs = slice(aa * PEER_KEYS, (aa + 1) * PEER_KEYS)
            for lb in range(tt // LANES):
                ls = slice(lb * LANES, (lb + 1) * LANES)
                w = None
                for h in range(n_heads):
                    rank = row_tile(ra_ref, lb, h, aa)
                    p1 = row_tile(p1_ref, lb, h, aa)
                    term = jnp.where(rank < gb_ref[lb, h], p2_ref[lb, h], jnp.zeros(tile, BF16)) * p1
                    w = term if w is None else w + term
                y = act_scr[rs, ls]
                a_scr[rs, ls] = w * (y * (1.0 + lax.erf(y))).astype(BF16)

    def experts():
        act_scr[...] = _dot_nt((u_ref[0] * INV_SQRT2).astype(BF16), h_ref[...])

    @pl.when(s == 0)
    def _():
        acc_scr[...] = jnp.zeros_like(acc_scr)

    i = pl.program_id(0)

    @pl.when(s < n_tiles)
    def _():
        experts()

    @pl.when(i >= 0)
    def _():
        gates()

    @pl.when(s + i >= 0)
    def _():
        readout()

    @pl.when(s == n_tiles - 1)
    def _():
        o_ref[...] = x_ref[...] + g_ref[0] * acc_scr[...].T


def _peer_mix(h2, u_all, layer, vt, tables, x, gate, seg, n_heads):
    t, d = h2.shape
    n_exp = u_all.shape[1]
    tt = min(PEER_MIX_TOKEN_TILE, seg)
    te = PEER_EXPERT_TILE
    n_tiles = n_exp // te
    per_seg = seg // tt
    r = gate.shape[0]
    nb = tt // LANES
    tab_spec = pl.BlockSpec((nb, n_heads, PEER_KEYS, LANES), lambda i, s: (i, 0, 0, 0))
    row_spec = pl.BlockSpec((nb, n_heads, te // PEER_KEYS, LANES), lambda i, s: (i, 0, s, 0))
    return pl.pallas_call(
        functools.partial(_peer_mix_kernel, n_heads=n_heads, n_tiles=n_tiles),
        grid=(t // tt, n_tiles),
        in_specs=[pl.BlockSpec((tt, d), lambda i, s: (i, 0)),
                  pl.BlockSpec((1, te, d), lambda i, s: (layer, s, 0)),
                  pl.BlockSpec((1, d, te), lambda i, s: (s, 0, 0)),
                  row_spec, row_spec, tab_spec, tab_spec,
                  pl.BlockSpec((tt, d), lambda i, s: (i, 0)),
                  pl.BlockSpec((1, 1, d), lambda i, s: (i // per_seg, 0, 0))],
        out_specs=pl.BlockSpec((tt, d), lambda i, s: (i, 0)),
        out_shape=jax.ShapeDtypeStruct((t, d), F32),
        scratch_shapes=[pltpu.VMEM((d, tt), F32), pltpu.VMEM((te, tt), F32), pltpu.VMEM((te, tt), BF16)],
        compiler_params=_params("arbitrary", "arbitrary"),
        name="peer_mix",
    )(h2, u_all, vt, *tables, x, gate.reshape(r, 1, d))


def _peer(h2, x, gate, seg, wq, keys, u_all, layer, vt):
    n_heads = keys.shape[0] // 2
    tables = _peer_scores(h2, wq, keys, n_heads)
    return _peer_mix(h2, u_all, layer, vt, tables, x, gate, seg, n_heads)


def _rope_tables(n_tokens):
    t = jnp.arange(n_tokens)
    row = (t // GRID_W).astype(F32)
    col = (t % GRID_W).astype(F32)
    n_freq = HEAD_DIM // 4
    inv_freq = jnp.power(ROPE_BASE, -jnp.arange(n_freq, dtype=F32) / n_freq)
    ang = jnp.concatenate([row[:, None] * inv_freq, col[:, None] * inv_freq], axis=-1)
    cos, sin = jnp.cos(ang), jnp.sin(ang)
    cos2 = jnp.tile(jnp.concatenate([cos, cos], axis=-1), (1, 2))
    sin2 = jnp.tile(jnp.concatenate([-sin, sin], axis=-1), (1, 2))
    return cos2, sin2


def _pair_lanes(v):
    return jnp.repeat(v, HEAD_DIM, axis=-1).reshape(*v.shape[:-1], v.shape[-1] // 2, LANES)


def kernel(x, c, ctx, c_ctx, mod_w, mod_b, norm1_g, norm2_g, ab_w_in, ab_w_out, na_q_norm, na_k_norm,
           na_rpb, ret_log_decay, swa_w_in, swa_w_out, swa_q_norm, swa_k_norm, swa_sink,
           peer_w_q, peer_sub_keys, peer_u, peer_v):
    b, s, d = x.shape
    l = ctx.shape[1]
    depth = mod_w.shape[0]
    assert depth == 2 and b + 1 <= 8
    assert s % (NA_Q_ROWS * GRID_W) == 0 and s // (NA_Q_ROWS * GRID_W) >= 3
    assert s % SWA_Q_TILE == 0 and s // SWA_Q_TILE >= 3 and s % RET_CHUNK == 0 and l % RET_CHUNK == 0
    for tile in (TOKEN_TILE, PROJ_TOKEN_TILE, PEER_TOKEN_TILE, PEER_MIX_TOKEN_TILE):
        assert s % tile == 0 and (b * l) % min(tile, b * l) == 0 and (b * l) % LANES == 0

    cond = jnp.concatenate([c, c_ctx[None, :], jnp.zeros((8 - b - 1, d), F32)], axis=0)
    mods = _adaln(cond, mod_w, mod_b).reshape(depth, 8, 6, d)
    lat = lambda layer, which: mods[layer, :b, which]
    cx = lambda layer, which: mods[layer, b:b + 1, which]

    cos2, sin2 = _rope_tables(s)
    x_lat = x.reshape(b * s, d)
    x_ctx = ctx.reshape(b * l, d)
    tile2 = lambda g: jnp.tile(g, 2).reshape(1, LANES)

    def peer_weights(layer):
        n_heads = peer_sub_keys.shape[1]
        keys = peer_sub_keys[layer].reshape(2 * n_heads, PEER_KEYS, -1).astype(BF16)
        vt = peer_v[layer].reshape(-1, PEER_EXPERT_TILE, d).transpose(0, 2, 1).astype(BF16)
        return peer_w_q[layer].astype(BF16), keys, peer_u, layer, vt

    n_na = na_rpb.shape[1]
    n_ret = ret_log_decay.shape[2]
    na_pairs, ret_pairs = n_na // 2, n_ret // 2
    w_in = ab_w_in[0].astype(BF16)
    w_out = ab_w_out[0].astype(BF16)
    wa = n_na * HEAD_DIM
    p_lat = _modmm(x_lat, norm1_g[0], lat(0, 0), lat(0, 1), w_in, s).reshape(b, s, -1)
    p_ctx = _modmm(x_ctx, norm1_g[0], cx(0, 0), cx(0, 1), w_in, b * l).reshape(b, l, -1)

    qn, kn = tile2(na_q_norm[0]), tile2(na_k_norm[0])
    oa_lat = _na_attention(p_lat, p_ctx, _na_bias(na_rpb[0], s // GRID_W), qn, kn, na_pairs)
    oa_ctx = _ctx_attention(p_ctx, qn, kn, na_pairs)

    lg = _pair_lanes(ret_log_decay[0]).transpose(1, 0, 2)
    ret_col0 = 3 * wa // LANES
    zeros_state = jnp.zeros((b, ret_pairs, 4, HEAD_DIM, HEAD_DIM), F32)
    ones_tab, zeros_tab = jnp.ones((l, LANES), F32), jnp.zeros((l, LANES), F32)
    ob_ctx, st_ctx = _retention(p_ctx, ones_tab, zeros_tab, lg, zeros_state, ret_col0, ret_pairs, False)
    ob_lat, _ = _retention(p_lat, cos2, sin2, lg, st_ctx, ret_col0, ret_pairs, True)

    w_list = [w_out[:wa], w_out[wa:]]
    x_lat, h_lat = _outproj([oa_lat.reshape(b * s, -1), ob_lat.reshape(b * s, -1)], w_list, x_lat,
                            lat(0, 2), norm2_g[0], lat(0, 3), lat(0, 4), s)
    x_ctx, h_ctx = _outproj([oa_ctx.reshape(b * l, -1), ob_ctx.reshape(b * l, -1)], w_list, x_ctx,
                            cx(0, 2), norm2_g[0], cx(0, 3), cx(0, 4), b * l)
    pw = peer_weights(0)
    x_lat = _peer(h_lat, x_lat, lat(0, 5), s, *pw)
    x_ctx = _peer(h_ctx, x_ctx, cx(0, 5), b * l, *pw)

    n_q = swa_sink.shape[1]
    n_kv = (swa_w_in.shape[2] // HEAD_DIM - n_q) // 2
    w_in = swa_w_in[0].astype(BF16)
    p_lat = _modmm(x_lat, norm1_g[1], lat(1, 0), lat(1, 1), w_in, s).reshape(b, s, -1)
    p_ctx = _modmm(x_ctx, norm1_g[1], cx(1, 0), cx(1, 1), w_in, b * l).reshape(b, l, -1)
    sink_rows = jnp.broadcast_to(swa_sink[0][:, None], (n_q, LANES))
    o_lat = _swa_attention(p_lat, p_ctx, cos2, sin2, tile2(swa_q_norm[0]), tile2(swa_k_norm[0]),
                           sink_rows, n_q, n_kv)
    x_lat, h_lat = _outproj([o_lat.reshape(b * s, -1)], [swa_w_out[0].astype(BF16)], x_lat,
                            lat(1, 2), norm2_g[1], lat(1, 3), lat(1, 4), s)
    x_lat = _peer(h_lat, x_lat, lat(1, 5), s, *peer_weights(1))
    return x_lat.reshape(b, s, d)
```

```python
import functools

import numpy as np
import jax
import jax.numpy as jnp
from jax import lax
from jax.experimental import pallas as pl
from jax.experimental.pallas import tpu as pltpu

F32 = jnp.float32
BF16 = jnp.bfloat16

HEAD_DIM = 64
GRID_W = 64
NA_ROWS = 8
NA_COLS = 16
SWA_WINDOW = 128
PEER_TOPK = 16
PEER_KEYS = 128
ROPE_BASE = 10000.0
NORM_EPS = 1e-6
GN_EPS = 1e-5
NEG_INF = -1e30
ATTN_SCALE = HEAD_DIM ** -0.5
INV_SQRT2 = 0.7071067811865476

LANES = 128
VMEM_LIMIT = 56 * 1024 * 1024

TOKEN_TILE = 1024
PROJ_TOKEN_TILE = 1024
PROJ_N_TILE = 1792
RET_CHUNK = 256
NA_Q_ROWS = 8
NA_BAND_ROWS = 16
SWA_Q_TILE = 256
PEER_TOKEN_TILE = 1024
PEER_MIX_TOKEN_TILE = 1024
PEER_EXPERT_TILE = 1024


def _dot(a, b):
    return jnp.dot(a, b, preferred_element_type=F32)


def _dot_nt(a, b):
    return lax.dot_general(a, b, (((1,), (1,)), ((), ())), preferred_element_type=F32)


def _dot_tn(a, b):
    return lax.dot_general(a, b, (((0,), (0,)), ((), ())), preferred_element_type=F32)


def _params(*sem):
    return pltpu.CompilerParams(dimension_semantics=sem, vmem_limit_bytes=VMEM_LIMIT)


def _rms_rows(x, gain):
    ms = jnp.mean(x * x, axis=-1, keepdims=True)
    return x * lax.rsqrt(ms + NORM_EPS) * gain


def _modulate(x, gain, shift, scale):
    return _rms_rows(x, gain) * (1.0 + scale) + shift


def _head_rms(x, gain):
    lane = lax.broadcasted_iota(jnp.int32, x.shape, 1)
    lo = lane < HEAD_DIM
    ss = x * x
    s_lo = jnp.sum(jnp.where(lo, ss, 0.0), axis=-1, keepdims=True)
    s_hi = jnp.sum(jnp.where(lo, 0.0, ss), axis=-1, keepdims=True)
    ms = jnp.where(lo, s_lo, s_hi) * (1.0 / HEAD_DIM)
    return x * lax.rsqrt(ms + NORM_EPS) * gain


def _rope(x, cos2, sin2):
    lane = lax.broadcasted_iota(jnp.int32, x.shape, 1)
    first_half = (lane & (HEAD_DIM // 2)) == 0
    swapped = jnp.where(first_half, pltpu.roll(x, LANES - HEAD_DIM // 2, axis=1),
                        pltpu.roll(x, HEAD_DIM // 2, axis=1))
    return x * cos2 + swapped * sin2


def _adaln_kernel(c_ref, w_ref, b_ref, o_ref):
    c = c_ref[...]
    s = c * jax.nn.sigmoid(c)
    w = w_ref[0]
    s_hi = s.astype(BF16).astype(F32)
    w_hi = w.astype(BF16)
    w_lo = (w - w_hi.astype(F32)).astype(BF16)
    rows = s.shape[0]
    both = _dot(jnp.concatenate([s_hi, s - s_hi], axis=0).astype(BF16), w_hi)
    o_ref[0] = both[:rows] + both[rows:] + _dot(s_hi.astype(BF16), w_lo) + b_ref[0]


def _adaln(cond, mod_w, mod_b):
    depth, d, n = mod_w.shape
    tn = n // 4
    return pl.pallas_call(
        _adaln_kernel,
        grid=(depth, n // tn),
        in_specs=[pl.BlockSpec((8, d), lambda l, j: (0, 0)),
                  pl.BlockSpec((1, d, tn), lambda l, j: (l, 0, j)),
                  pl.BlockSpec((1, 1, tn), lambda l, j: (l, 0, j))],
        out_specs=pl.BlockSpec((1, 8, tn), lambda l, j: (l, 0, j)),
        out_shape=jax.ShapeDtypeStruct((depth, 8, n), F32),
        compiler_params=_params("arbitrary", "arbitrary"),
        name="adaln",
    )(cond, mod_w, mod_b.reshape(depth, 1, n))


def _modmm_kernel(x_ref, g_ref, sh_ref, sc_ref, w_ref, o_ref, h_scr):
    @pl.when(pl.program_id(1) == 0)
    def _():
        h_scr[...] = _modulate(x_ref[...], g_ref[...], sh_ref[0], sc_ref[0]).astype(BF16)

    o_ref[...] = _dot(h_scr[...], w_ref[...])


def _modmm(x, gain, shift, scale, w, seg):
    t, d = x.shape
    n = w.shape[1]
    tn = n
    tm = min(PROJ_TOKEN_TILE if n <= PROJ_N_TILE else PROJ_TOKEN_TILE // 2, seg)
    per_seg = seg // tm
    r = shift.shape[0]
    return pl.pallas_call(
        _modmm_kernel,
        grid=(t // tm, n // tn),
        in_specs=[pl.BlockSpec((tm, d), lambda i, j: (i, 0)),
                  pl.BlockSpec((1, d), lambda i, j: (0, 0)),
                  pl.BlockSpec((1, 1, d), lambda i, j: (i // per_seg, 0, 0)),
                  pl.BlockSpec((1, 1, d), lambda i, j: (i // per_seg, 0, 0)),
                  pl.BlockSpec((d, tn), lambda i, j: (0, j))],
        out_specs=pl.BlockSpec((tm, tn), lambda i, j: (i, j)),
        out_shape=jax.ShapeDtypeStruct((t, n), F32),
        scratch_shapes=[pltpu.VMEM((tm, d), BF16)],
        compiler_params=_params("arbitrary", "arbitrary"),
        name="modulate_matmul",
    )(x, gain.reshape(1, d), shift.reshape(r, 1, d), scale.reshape(r, 1, d), w)


def _outproj_kernel(*refs, n_in):
    a_refs = refs[:n_in]
    w_refs = refs[n_in:2 * n_in]
    x_ref, gate_ref, g2_ref, sh_ref, sc_ref, xo_ref, h_ref = refs[2 * n_in:]
    acc = None
    for a_ref, w_ref in zip(a_refs, w_refs):
        part = _dot(a_ref[...].astype(BF16), w_ref[...])
        acc = part if acc is None else acc + part
    xn = x_ref[...] + gate_ref[0] * acc
    xo_ref[...] = xn
    h_ref[...] = _modulate(xn, g2_ref[...], sh_ref[0], sc_ref[0]).astype(BF16)


def _outproj(a_list, w_list, x, gate, gain2, shift2, scale2, seg):
    t, d = x.shape
    tm = min(TOKEN_TILE, seg)
    per_seg = seg // tm
    r = gate.shape[0]
    n_in = len(a_list)
    row = lambda i: (i // per_seg, 0, 0)
    in_specs = ([pl.BlockSpec((tm, a.shape[1]), lambda i: (i, 0)) for a in a_list]
                + [pl.BlockSpec(w.shape, lambda i: (0, 0)) for w in w_list]
                + [pl.BlockSpec((tm, d), lambda i: (i, 0)),
                   pl.BlockSpec((1, 1, d), row),
                   pl.BlockSpec((1, d), lambda i: (0, 0)),
                   pl.BlockSpec((1, 1, d), row),
                   pl.BlockSpec((1, 1, d), row)])
    return pl.pallas_call(
        functools.partial(_outproj_kernel, n_in=n_in),
        grid=(t // tm,),
        in_specs=in_specs,
        out_specs=[pl.BlockSpec((tm, d), lambda i: (i, 0)), pl.BlockSpec((tm, d), lambda i: (i, 0))],
        out_shape=[jax.ShapeDtypeStruct((t, d), F32), jax.ShapeDtypeStruct((t, d), BF16)],
        compiler_params=_params("arbitrary"),
        name="out_proj_residual",
    )(*a_list, *w_list, x, gate.reshape(r, 1, d), gain2.reshape(1, d),
      shift2.reshape(r, 1, d), scale2.reshape(r, 1, d))


def _softmax_pv(s_list, v_list, extra=None, fold_lanes=False):
    def lane_chunks(x):
        return [x[:, c:c + LANES] for c in range(0, x.shape[1], LANES)] if fold_lanes else [x]

    def row_reduce(blocks, op, reduce):
        acc = None
        for blk in blocks:
            chunks = lane_chunks(blk)
            part = chunks[0]
            for c in chunks[1:]:
                part = op(part, c)
            if not fold_lanes:
                part = reduce(part, axis=-1, keepdims=True)
            acc = part if acc is None else op(acc, part)
        return reduce(acc, axis=-1, keepdims=True) if fold_lanes else acc

    m = row_reduce(s_list, jnp.maximum, jnp.max)
    if extra is not None:
        m = jnp.maximum(m, extra)
    ps = []
    out = None
    for s, v in zip(s_list, v_list):
        p = jnp.exp(s - m)
        ps.append(p)
        pv = _dot(p.astype(BF16), v)
        out = pv if out is None else out + pv
    denom = row_reduce(ps, jnp.add, jnp.sum)
    if extra is not None:
        denom = denom + jnp.exp(extra - m)
    return out / denom


def _na_kernel(q_ref, k_ref, v_ref, kc_ref, vc_ref, bias_ref, qn_ref, kn_ref, o_ref,
               k_scr, v_scr, kc_scr, vc_scr, *, n_steps):
    i = pl.program_id(2)
    band = NA_BAND_ROWS * GRID_W
    start = jnp.clip(NA_Q_ROWS * i - NA_ROWS // 2, 0, NA_Q_ROWS * n_steps - NA_BAND_ROWS) * GRID_W
    start = pl.multiple_of(start, NA_ROWS // 2 * GRID_W)

    @pl.when(i == 0)
    def _():
        k_scr[...] = _head_rms(k_ref[0], kn_ref[...]).astype(BF16)
        v_scr[...] = v_ref[0].astype(BF16)
        kc_scr[...] = _head_rms(kc_ref[0], kn_ref[...]).astype(BF16)
        vc_scr[...] = vc_ref[0].astype(BF16)

    q = (_head_rms(q_ref[0], qn_ref[...]) * ATTN_SCALE).astype(BF16)
    kb = k_scr[pl.ds(start, band), :]
    vb = v_scr[pl.ds(start, band), :]
    kc = kc_scr[...]
    vc = vc_scr[...]
    outs = []
    for h in range(2):
        sl = slice(h * HEAD_DIM, (h + 1) * HEAD_DIM)
        s_loc = _dot_nt(q[:, sl], kb[:, sl]) + bias_ref[0, h]
        s_ctx = _dot_nt(q[:, sl], kc[:, sl])
        outs.append(_softmax_pv([s_loc, s_ctx], [vb[:, sl], vc[:, sl]]))
    o_ref[0] = jnp.concatenate(outs, axis=-1).astype(o_ref.dtype)


def _na_bias(rpb, rows):
    h = rpb.shape[0]
    col = np.arange(GRID_W)
    c0 = np.clip(col - NA_COLS // 2, 0, GRID_W - NA_COLS)
    dc = col[None, :] - col[:, None] + (NA_COLS - 1)
    ok_c = (col[None, :] >= c0[:, None]) & (col[None, :] < c0[:, None] + NA_COLS)
    pick = ((np.arange(2 * NA_COLS - 1)[:, None, None] == dc[None]) & ok_c[None]).astype(np.float32)
    blocks = jnp.einsum('hrd,dqk->hrqk', rpb, pick, precision=lax.Precision.HIGHEST)
    blocks = jnp.where(ok_c, blocks, NEG_INF)
    qt, band = NA_Q_ROWS * GRID_W, NA_BAND_ROWS * GRID_W
    return pl.pallas_call(
        functools.partial(_na_bias_kernel, rows=rows),
        grid=(3, h),
        in_specs=[pl.BlockSpec((1,) + blocks.shape[1:], lambda c, hh: (hh, 0, 0, 0))],
        out_specs=pl.BlockSpec((1, 1, qt, band), lambda c, hh: (c, hh, 0, 0)),
        out_shape=jax.ShapeDtypeStruct((3, h, qt, band), F32),
        compiler_params=_params("arbitrary", "arbitrary"),
        name="na_bias_tables",
    )(blocks)


def _na_bias_kernel(blk_ref, o_ref, *, rows):
    n_steps = rows // NA_Q_ROWS
    for c, step in enumerate((0, 1, n_steps - 1)):
        @pl.when(pl.program_id(0) == c)
        def _(step=step):
            o_ref[0, 0] = jnp.full(o_ref.shape[2:], NEG_INF, F32)
            band0 = min(max(NA_Q_ROWS * step - NA_ROWS // 2, 0), rows - NA_BAND_ROWS)
            for rq in range(NA_Q_ROWS):
                r = NA_Q_ROWS * step + rq
                r0 = min(max(r - NA_ROWS // 2, 0), rows - NA_ROWS)
                for rk in range(r0, r0 + NA_ROWS):
                    col0 = (rk - band0) * GRID_W
                    o_ref[0, 0, rq * GRID_W:(rq + 1) * GRID_W, col0:col0 + GRID_W] = (
                        blk_ref[0, rk - r + NA_ROWS - 1])


def _na_attention(p_lat, p_ctx, bias, qn, kn, n_pairs):
    b, s, _ = p_lat.shape
    l = p_ctx.shape[1]
    rows = s // GRID_W
    n_steps = rows // NA_Q_ROWS
    qt = NA_Q_ROWS * GRID_W

    def cls(i):
        return jnp.where(i == 0, 0, jnp.where(i == n_steps - 1, 2, 1))

    return pl.pallas_call(
        functools.partial(_na_kernel, n_steps=n_steps),
        grid=(n_pairs, b, n_steps),
        in_specs=[pl.BlockSpec((1, qt, LANES), lambda hp, bb, i: (bb, i, hp)),
                  pl.BlockSpec((1, s, LANES), lambda hp, bb, i: (bb, 0, n_pairs + hp)),
                  pl.BlockSpec((1, s, LANES), lambda hp, bb, i: (bb, 0, 2 * n_pairs + hp)),
                  pl.BlockSpec((1, l, LANES), lambda hp, bb, i: (bb, 0, n_pairs + hp)),
                  pl.BlockSpec((1, l, LANES), lambda hp, bb, i: (bb, 0, 2 * n_pairs + hp)),
                  pl.BlockSpec((1, 2, qt, NA_BAND_ROWS * GRID_W), lambda hp, bb, i: (cls(i), hp, 0, 0)),
                  pl.BlockSpec((1, LANES), lambda hp, bb, i: (0, 0)),
                  pl.BlockSpec((1, LANES), lambda hp, bb, i: (0, 0))],
        out_specs=pl.BlockSpec((1, qt, LANES), lambda hp, bb, i: (bb, i, hp)),
        out_shape=jax.ShapeDtypeStruct((b, s, n_pairs * LANES), BF16),
        scratch_shapes=[pltpu.VMEM((s, LANES), BF16), pltpu.VMEM((s, LANES), BF16),
                        pltpu.VMEM((l, LANES), BF16), pltpu.VMEM((l, LANES), BF16)],
        compiler_params=_params("arbitrary", "arbitrary", "arbitrary"),
        name="neighbourhood_attention",
    )(p_lat, p_lat, p_lat, p_ctx, p_ctx, bias, qn, kn)


def _ctx_attn_kernel(q_ref, k_ref, v_ref, qn_ref, kn_ref, o_ref):
    q = (_head_rms(q_ref[0], qn_ref[...]) * ATTN_SCALE).astype(BF16)
    k = _head_rms(k_ref[0], kn_ref[...]).astype(BF16)
    v = v_ref[0].astype(BF16)
    outs = []
    for h in range(2):
        sl = slice(h * HEAD_DIM, (h + 1) * HEAD_DIM)
        outs.append(_softmax_pv([_dot_nt(q[:, sl], k[:, sl])], [v[:, sl]]))
    o_ref[0] = jnp.concatenate(outs, axis=-1).astype(o_ref.dtype)


def _ctx_attention(p_ctx, qn, kn, n_pairs):
    b, l, _ = p_ctx.shape
    return pl.pallas_call(
        _ctx_attn_kernel,
        grid=(n_pairs, b),
        in_specs=[pl.BlockSpec((1, l, LANES), lambda hp, bb: (bb, 0, hp)),
                  pl.BlockSpec((1, l, LANES), lambda hp, bb: (bb, 0, n_pairs + hp)),
                  pl.BlockSpec((1, l, LANES), lambda hp, bb: (bb, 0, 2 * n_pairs + hp)),
                  pl.BlockSpec((1, LANES), lambda hp, bb: (0, 0)),
                  pl.BlockSpec((1, LANES), lambda hp, bb: (0, 0))],
        out_specs=pl.BlockSpec((1, l, LANES), lambda hp, bb: (bb, 0, hp)),
        out_shape=jax.ShapeDtypeStruct((b, l, n_pairs * LANES), BF16),
        compiler_params=_params("arbitrary", "arbitrary"),
        name="context_attention",
    )(p_ctx, p_ctx, p_ctx, qn, kn)


def _ret_kernel(q_ref, k_ref, v_ref, g_ref, cos_ref, sin_ref, lg_ref, s0_ref, y_ref, st_ref, sf_scr,
                *, n_chunks, use_rope):
    c = RET_CHUNK
    hd = HEAD_DIM
    lg = -jnp.exp(lg_ref[0])
    lgf, lgb = lg[0:1, :], lg[1:2, :]
    ii = lax.broadcasted_iota(jnp.int32, (c, LANES), 0).astype(F32)
    dq_f = jnp.exp(lgf * (ii + 1.0))
    dk_f = jnp.exp(lgf * (c - 1.0 - ii))
    dq_b = jnp.exp(lgb * (c - ii))
    dk_b = jnp.exp(lgb * ii)
    dc_f = jnp.exp(lgf * float(c))
    dc_b = jnp.exp(lgb * float(c))
    diff = (lax.broadcasted_iota(jnp.int32, (c, c), 0) - lax.broadcasted_iota(jnp.int32, (c, c), 1)).astype(F32)
    intra = []
    chunk_f = []
    chunk_b = []
    for h in range(2):
        lf = lgf[:, h * hd:h * hd + 1]
        lb = lgb[:, h * hd:h * hd + 1]
        intra.append(jnp.where(diff >= 0, jnp.exp(lf * jnp.maximum(diff, 0.0)),
                               jnp.exp(lb * jnp.maximum(-diff, 0.0))))
        chunk_f.append(dc_f[:, h * hd:h * hd + 1])
        chunk_b.append(dc_b[:, h * hd:h * hd + 1])

    def load(n):
        r = pl.multiple_of(n * c, c)
        q = q_ref[0, pl.ds(r, c), :]
        k = k_ref[0, pl.ds(r, c), :]
        v = v_ref[0, pl.ds(r, c), :]
        if use_rope:
            cs = cos_ref[pl.ds(r, c), :]
            sn = sin_ref[pl.ds(r, c), :]
            q = _rope(q, cs, sn)
            k = _rope(k, cs, sn)
        return r, q * ATTN_SCALE, k, v.astype(BF16)

    def fwd(n, carry):
        _, _, k, v = load(n)
        kd = (k * dk_f).astype(BF16)
        new = []
        for h in range(2):
            sl = slice(h * hd, (h + 1) * hd)
            sf_scr[n, h] = carry[h]
            new.append(carry[h] * chunk_f[h] + _dot_tn(kd[:, sl], v[:, sl]))
        return tuple(new)

    sf = lax.fori_loop(0, n_chunks, fwd, (s0_ref[0, 0, 0], s0_ref[0, 0, 1]), unroll=2)
    st_ref[0, 0, 0] = sf[0]
    st_ref[0, 0, 1] = sf[1]

    def bwd(jj, carry):
        n = n_chunks - 1 - jj
        r, q, k, v = load(n)
        qb = q.astype(BF16)
        kb = k.astype(BF16)
        qf = (q * dq_f).astype(BF16)
        qr = (q * dq_b).astype(BF16)
        kd = (k * dk_b).astype(BF16)
        outs = []
        new = []
        for h in range(2):
            sl = slice(h * hd, (h + 1) * hd)
            a = (_dot_nt(qb[:, sl], kb[:, sl]) * intra[h]).astype(BF16)
            o = (_dot(a, v[:, sl]) + _dot(qf[:, sl], sf_scr[n, h].astype(BF16))
                 + _dot(qr[:, sl], carry[h].astype(BF16)))
            oc = o - jnp.mean(o, axis=-1, keepdims=True)
            outs.append(oc * lax.rsqrt(jnp.mean(oc * oc, axis=-1, keepdims=True) + GN_EPS))
            new.append(carry[h] * chunk_b[h] + _dot_tn(kd[:, sl], v[:, sl]))
        g = g_ref[0, pl.ds(r, c), :]
        y_ref[0, pl.ds(r, c), :] = (jnp.concatenate(outs, axis=-1) * (g * jax.nn.sigmoid(g))).astype(y_ref.dtype)
        return tuple(new)

    sb = lax.fori_loop(0, n_chunks, bwd, (s0_ref[0, 0, 2], s0_ref[0, 0, 3]), unroll=2)
    st_ref[0, 0, 2] = sb[0]
    st_ref[0, 0, 3] = sb[1]


def _retention(p, cos2, sin2, lg, s0, col0, n_pairs, use_rope):
    b, t, _ = p.shape
    n_chunks = t // RET_CHUNK
    tab = lambda hp, bb: (0, 0)
    return pl.pallas_call(
        functools.partial(_ret_kernel, n_chunks=n_chunks, use_rope=use_rope),
        grid=(n_pairs, b),
        in_specs=[pl.BlockSpec((1, t, LANES), lambda hp, bb: (bb, 0, col0 + hp)),
                  pl.BlockSpec((1, t, LANES), lambda hp, bb: (bb, 0, col0 + n_pairs + hp)),
                  pl.BlockSpec((1, t, LANES), lambda hp, bb: (bb, 0, col0 + 2 * n_pairs + hp)),
                  pl.BlockSpec((1, t, LANES), lambda hp, bb: (bb, 0, col0 + 3 * n_pairs + hp)),
                  pl.BlockSpec(cos2.shape, tab),
                  pl.BlockSpec(sin2.shape, tab),
                  pl.BlockSpec((1, 2, LANES), lambda hp, bb: (hp, 0, 0)),
                  pl.BlockSpec((1, 1, 4, HEAD_DIM, HEAD_DIM), lambda hp, bb: (bb, hp, 0, 0, 0))],
        out_specs=[pl.BlockSpec((1, t, LANES), lambda hp, bb: (bb, 0, hp)),
                   pl.BlockSpec((1, 1, 4, HEAD_DIM, HEAD_DIM), lambda hp, bb: (bb, hp, 0, 0, 0))],
        out_shape=[jax.ShapeDtypeStruct((b, t, n_pairs * LANES), BF16),
                   jax.ShapeDtypeStruct((b, n_pairs, 4, HEAD_DIM, HEAD_DIM), F32)],
        scratch_shapes=[pltpu.VMEM((n_chunks, 2, HEAD_DIM, HEAD_DIM), F32)],
        compiler_params=_params("arbitrary", "arbitrary"),
        name="retention",
    )(p, p, p, p, cos2, sin2, lg, s0)


def _swa_kernel(q_ref, k_ref, v_ref, kc_ref, vc_ref, cos_ref, sin_ref, qn_ref, kn_ref, sink_ref, far_ref,
                o_ref, k_scr, kc_scr, *, seq):
    qt = SWA_Q_TILE
    wk = qt + 2 * SWA_WINDOW
    hd = HEAD_DIM
    n = pl.program_id(2)
    q0 = pl.multiple_of(n * qt, qt)
    ws = pl.multiple_of(jnp.clip(n * qt - SWA_WINDOW, 0, seq - wk), SWA_WINDOW)

    @pl.when(n == 0)
    def _():
        k_scr[...] = _rope(_head_rms(k_ref[0], kn_ref[...]), cos_ref[...], sin_ref[...])
        kc_scr[...] = _head_rms(kc_ref[0], kn_ref[...])

    kw = k_scr[pl.ds(ws, wk), :].astype(BF16)
    vw = v_ref[0, pl.ds(ws, wk), :].astype(BF16)
    kc = kc_scr[...].astype(BF16)
    vc = vc_ref[0].astype(BF16)
    cos_q = cos_ref[pl.ds(q0, qt), :]
    sin_q = sin_ref[pl.ds(q0, qt), :]
    qs = []
    for s in range(4):
        slab = q_ref[0, :, s * LANES:(s + 1) * LANES]
        qs.append((_rope(_head_rms(slab, qn_ref[...]), cos_q, sin_q) * ATTN_SCALE).astype(BF16))
    far = far_ref[0][None]
    for kh in range(2):
        sl = slice(kh * hd, (kh + 1) * hd)
        qstack = jnp.concatenate(
            [qs[kh * 2 + g // 2][:, (g % 2) * hd:(g % 2 + 1) * hd] for g in range(4)], axis=0)
        sink = jnp.concatenate(
            [jnp.broadcast_to(sink_ref[kh * 4 + g:kh * 4 + g + 1, 0:1], (qt, 1)) for g in range(4)], axis=0)
        s_loc = (_dot_nt(qstack, kw[:, sl]).reshape(4, qt, wk) + far).reshape(4 * qt, wk)
        s_ctx = _dot_nt(qstack, kc[:, sl])
        o = _softmax_pv([s_loc, s_ctx], [vw[:, sl], vc[:, sl]], extra=sink, fold_lanes=True)
        for pair in range(2):
            col = (kh * 2 + pair) * LANES
            o_ref[0, :, col:col + LANES] = jnp.concatenate(
                [o[(2 * pair) * qt:(2 * pair + 1) * qt], o[(2 * pair + 1) * qt:(2 * pair + 2) * qt]],
                axis=-1).astype(o_ref.dtype)


def _swa_attention(p_lat, p_ctx, cos2, sin2, qn, kn, sink_rows, n_q_heads, n_kv_heads):
    b, s, _ = p_lat.shape
    l = p_ctx.shape[1]
    kv_pairs = n_kv_heads // 2
    q_blocks = n_q_heads * HEAD_DIM // LANES
    q_per_pair = q_blocks // kv_pairs
    qw = q_per_pair * LANES
    tab = lambda kp, bb, n: (0, 0)
    n_steps = s // SWA_Q_TILE
    wk = SWA_Q_TILE + 2 * SWA_WINDOW
    far = []
    for step in (0, 1, n_steps - 1):
        q0 = step * SWA_Q_TILE
        ws = min(max(q0 - SWA_WINDOW, 0), s - wk)
        dist = np.abs((q0 + np.arange(SWA_Q_TILE))[:, None] - (ws + np.arange(wk))[None, :])
        far.append(np.where(dist <= SWA_WINDOW, 0.0, NEG_INF))
    far = jnp.asarray(np.stack(far), F32)

    def cls(n):
        return jnp.where(n == 0, 0, jnp.where(n == n_steps - 1, 2, 1))

    return pl.pallas_call(
        functools.partial(_swa_kernel, seq=s),
        grid=(kv_pairs, b, n_steps),
        in_specs=[pl.BlockSpec((1, SWA_Q_TILE, qw), lambda kp, bb, n: (bb, n, kp)),
                  pl.BlockSpec((1, s, LANES), lambda kp, bb, n: (bb, 0, q_blocks + kp)),
                  pl.BlockSpec((1, s, LANES), lambda kp, bb, n: (bb, 0, q_blocks + kv_pairs + kp)),
                  pl.BlockSpec((1, l, LANES), lambda kp, bb, n: (bb, 0, q_blocks + kp)),
                  pl.BlockSpec((1, l, LANES), lambda kp, bb, n: (bb, 0, q_blocks + kv_pairs + kp)),
                  pl.BlockSpec(cos2.shape, tab),
                  pl.BlockSpec(sin2.shape, tab),
                  pl.BlockSpec((1, LANES), tab),
                  pl.BlockSpec((1, LANES), tab),
                  pl.BlockSpec((8, LANES), lambda kp, bb, n: (kp, 0)),
                  pl.BlockSpec((1, SWA_Q_TILE, wk), lambda kp, bb, n: (cls(n), 0, 0))],
        out_specs=pl.BlockSpec((1, SWA_Q_TILE, qw), lambda kp, bb, n: (bb, n, kp)),
        out_shape=jax.ShapeDtypeStruct((b, s, n_q_heads * HEAD_DIM), BF16),
        scratch_shapes=[pltpu.VMEM((s, LANES), F32), pltpu.VMEM((l, LANES), F32)],
        compiler_params=_params("arbitrary", "arbitrary", "arbitrary"),
        name="windowed_gqa",
    )(p_lat, p_lat, p_lat, p_ctx, p_ctx, cos2, sin2, qn, kn, sink_rows, far)


def _oddeven_merge_sort(n):
    pairs = []

    def merge(lo, size, r):
        step = r * 2
        if step < size:
            merge(lo, size, step)
            merge(lo + r, size, step)
            pairs.extend((i, i + r) for i in range(lo + r, lo + size - r, step))
        else:
            pairs.append((lo, lo + r))

    def sort(lo, size):
        if size > 1:
            sort(lo, size // 2)
            sort(lo + size // 2, size // 2)
            merge(lo, size, 1)

    sort(0, n)
    return pairs


def _top_rows(x, k, scr):
    tiles = [x[8 * i:8 * i + 8, :] for i in range(x.shape[0] // 8)]
    for i, j in _oddeven_merge_sort(len(tiles)):
        tiles[i], tiles[j] = jnp.maximum(tiles[i], tiles[j]), jnp.minimum(tiles[i], tiles[j])
    for r in range(k):
        m = jnp.max(tiles[0], axis=0, keepdims=True)
        scr[r:r + 1, :] = m
        need = k - r - 1
        if need > 0:
            hit = tiles[0] == m
            for d in range(min(need, len(tiles) - 1)):
                tiles[d] = jnp.where(hit, tiles[d + 1], tiles[d])
            if need >= len(tiles):
                tiles[-1] = jnp.where(hit, NEG_INF, tiles[-1])


def _bf16_pair(x):
    bits = lax.bitcast_convert_type(x.astype(BF16).astype(F32), jnp.uint32)
    return bits | (bits >> 16)


def _count_above(sorted_scr, y, strict):
    row = lambda i: sorted_scr[i:i + 1, :]
    above = (lambda r: r > y) if strict else (lambda r: r >= y)
    c8 = above(row(7))
    c4 = above(jnp.where(c8, row(11), row(3)))
    c2 = above(jnp.where(c8, jnp.where(c4, row(13), row(9)), jnp.where(c4, row(5), row(1))))
    hi = jnp.where(c4, jnp.where(c2, row(14), row(12)), jnp.where(c2, row(10), row(8)))
    lo = jnp.where(c4, jnp.where(c2, row(6), row(4)), jnp.where(c2, row(2), row(0)))
    c1 = above(jnp.where(c8, hi, lo))
    count = (jnp.where(c8, 8.0, 0.0) + jnp.where(c4, 4.0, 0.0)) + (jnp.where(c2, 2.0, 0.0) + jnp.where(c1, 1.0, 0.0))
    return count + jnp.where(above(row(15)), 1.0, 0.0)


def _peer_scores_kernel(h_ref, wq_ref, keys_ref, ra_ref, p1_ref, gb_ref, p2_ref,
                        q_scr, s_scr, a_scr, b_scr, c_scr, v_scr, *, n_heads):
    k = PEER_TOPK
    n_blocks = h_ref.shape[0] // LANES

    q = _dot(h_ref[...], wq_ref[...])
    for hp in range(2 * n_heads):
        q_scr[hp] = q[:, hp * PEER_KEYS:(hp + 1) * PEER_KEYS].astype(BF16)

    def lane_block(lb, h, slot):
        tops_a, tops_b, cand, tops_c = a_scr.at[slot], b_scr.at[slot], c_scr.at[slot], v_scr.at[slot]
        s1 = s_scr[0, lb]
        s2 = s_scr[1, lb]
        _top_rows(s1, k + 1, tops_a)
        _top_rows(s2, k + 1, tops_b)
        cand[0:16, :] = tops_a[0:1, :] + tops_b[0:16, :]
        for i in range(1, 8):
            cand[8 + 8 * i:16 + 8 * i, :] = tops_a[i:i + 1, :] + tops_b[0:8, :]
        cand[72:80, :] = tops_a[8:16, :] + tops_b[0:1, :]
        cand[80:81, :] = tops_a[0:1, :] + tops_b[16:17, :]
        cand[81:82, :] = tops_a[16:17, :] + tops_b[0:1, :]
        cand[82:PEER_KEYS, :] = jnp.full((PEER_KEYS - 82, LANES), NEG_INF, F32)
        _top_rows(cand[...], k + 1, tops_c)
        thr = 0.5 * (tops_c[k - 1:k, :] + tops_c[k:k + 1, :])
        z = jnp.sum(jnp.exp(tops_c[0:k, :] - tops_c[0:1, :]), axis=0, keepdims=True)
        ra_ref[lb, h] = _bf16_pair(_count_above(tops_a, s1, strict=True))
        p1_ref[lb, h] = _bf16_pair(jnp.exp(s1 - tops_a[0:1, :]) * (INV_SQRT2 / z))
        gb_ref[lb, h] = _count_above(tops_a, thr - s2, strict=False).astype(BF16)
        p2_ref[lb, h] = jnp.exp(s2 - tops_b[0:1, :]).astype(BF16)

    def lane_pair(i, h):
        lane_block(2 * i, h, 0)
        lane_block(2 * i + 1, h, 1)
        return h

    def head(h, carry):
        for p in range(2):
            st = _dot_nt(keys_ref[2 * h + p], q_scr[2 * h + p])
            for lb in range(n_blocks):
                s_scr[p, lb] = st[:, lb * LANES:(lb + 1) * LANES]
        lax.fori_loop(0, n_blocks // 2, lane_pair, h)
        return carry

    lax.fori_loop(0, n_heads, head, 0)


def _peer_scores(h2, wq, keys, n_heads):
    t, d = h2.shape
    tt = PEER_TOKEN_TILE
    nb = tt // LANES
    tab_shape = (t // LANES, n_heads, PEER_KEYS, LANES)
    tab_spec = pl.BlockSpec((nb, n_heads, PEER_KEYS, LANES), lambda i: (i, 0, 0, 0))
    return pl.pallas_call(
        functools.partial(_peer_scores_kernel, n_heads=n_heads),
        grid=(t // tt,),
        in_specs=[pl.BlockSpec((tt, d), lambda i: (i, 0)),
                  pl.BlockSpec(wq.shape, lambda i: (0, 0)),
                  pl.BlockSpec(keys.shape, lambda i: (0, 0, 0))],
        out_specs=[tab_spec] * 4,
        out_shape=[jax.ShapeDtypeStruct(tab_shape, dt) for dt in (jnp.uint32, jnp.uint32, BF16, BF16)],
        scratch_shapes=[pltpu.VMEM((2 * n_heads, tt, PEER_KEYS), BF16),
                        pltpu.VMEM((2, nb, PEER_KEYS, LANES), F32),
                        pltpu.VMEM((2, 24, LANES), F32), pltpu.VMEM((2, 24, LANES), F32),
                        pltpu.VMEM((2, PEER_KEYS, LANES), F32), pltpu.VMEM((2, 24, LANES), F32)],
        compiler_params=_params("arbitrary"),
        name="peer_scores",
    )(h2, wq, keys)


def _peer_mix_kernel(h_ref, u_ref, vt_ref, ra_ref, p1_ref, gb_ref, p2_ref, x_ref, g_ref, o_ref,
                     acc_scr, act_scr, a_scr, *, n_heads, n_tiles):
    s = pl.program_id(1)
    tt = h_ref.shape[0]
    rows_per_tile = PEER_EXPERT_TILE // PEER_KEYS
    tile = (PEER_KEYS, LANES)

    def readout():
        acc_scr[...] += _dot(vt_ref[0], a_scr[...])

    def row_tile(ref, lb, h, aa):
        words = jnp.broadcast_to(ref[lb, h, aa:aa + 1, :], (8, LANES))
        packed = pltpu.bitcast(words, BF16)
        return jnp.broadcast_to(packed[None], (PEER_KEYS // 16, 16, LANES)).reshape(tile)

    def gates():
        for aa in range(rows_per_tile):
            rs = slice(aa * PEER_KEYS, (aa + 1) * PEER_KEYS)
            for lb in range(tt // LANES):
                ls = slice(lb * LANES, (lb + 1) * LANES)
                w = None
                for h in range(n_heads):
                    rank = row_tile(ra_ref, lb, h, aa)
                    p1 = row_tile(p1_ref, lb, h, aa)
                    term = jnp.where(rank < gb_ref[lb, h], p2_ref[lb, h], jnp.zeros(tile, BF16)) * p1
                    w = term if w is None else w + term
                a_scr[rs, ls] = w * act_scr[rs, ls]

    def experts():
        y = _dot_nt((u_ref[0] * INV_SQRT2).astype(BF16), h_ref[...])
        act_scr[...] = (y * (1.0 + lax.erf(y))).astype(BF16)

    @pl.when(s == 0)
    def _():
        acc_scr[...] = jnp.zeros_like(acc_scr)

    i = pl.program_id(0)

    @pl.when(s < n_tiles)
    def _():
        experts()

    @pl.when(i >= 0)
    def _():
        gates()

    @pl.when(s + i >= 0)
    def _():
        readout()

    @pl.when(s == n_tiles - 1)
    def _():
        o_ref[...] = x_ref[...] + g_ref[0] * acc_scr[...].T


def _peer_mix(h2, u_all, layer, vt, tables, x, gate, seg, n_heads):
    t, d = h2.shape
    n_exp = u_all.shape[1]
    tt = min(PEER_MIX_TOKEN_TILE, seg)
    te = PEER_EXPERT_TILE
    n_tiles = n_exp // te
    per_seg = seg // tt
    r = gate.shape[0]
    nb = tt // LANES
    tab_spec = pl.BlockSpec((nb, n_heads, PEER_KEYS, LANES), lambda i, s: (i, 0, 0, 0))
    row_spec = pl.BlockSpec((nb, n_heads, te // PEER_KEYS, LANES), lambda i, s: (i, 0, s, 0))
    return pl.pallas_call(
        functools.partial(_peer_mix_kernel, n_heads=n_heads, n_tiles=n_tiles),
        grid=(t // tt, n_tiles),
        in_specs=[pl.BlockSpec((tt, d), lambda i, s: (i, 0)),
                  pl.BlockSpec((1, te, d), lambda i, s: (layer, s, 0)),
                  pl.BlockSpec((1, d, te), lambda i, s: (s, 0, 0)),
                  row_spec, row_spec, tab_spec, tab_spec,
                  pl.BlockSpec((tt, d), lambda i, s: (i, 0)),
                  pl.BlockSpec((1, 1, d), lambda i, s: (i // per_seg, 0, 0))],
        out_specs=pl.BlockSpec((tt, d), lambda i, s: (i, 0)),
        out_shape=jax.ShapeDtypeStruct((t, d), F32),
        scratch_shapes=[pltpu.VMEM((d, tt), F32), pltpu.VMEM((te, tt), BF16), pltpu.VMEM((te, tt), BF16)],
        compiler_params=_params("arbitrary", "arbitrary"),
        name="peer_mix",
    )(h2, u_all, vt, *tables, x, gate.reshape(r, 1, d))


def _peer(h2, x, gate, seg, wq, keys, u_all, layer, vt):
    n_heads = keys.shape[0] // 2
    tables = _peer_scores(h2, wq, keys, n_heads)
    return _peer_mix(h2, u_all, layer, vt, tables, x, gate, seg, n_heads)


def _rope_tables(n_tokens):
    t = jnp.arange(n_tokens)
    row = (t // GRID_W).astype(F32)
    col = (t % GRID_W).astype(F32)
    n_freq = HEAD_DIM // 4
    inv_freq = jnp.power(ROPE_BASE, -jnp.arange(n_freq, dtype=F32) / n_freq)
    ang = jnp.concatenate([row[:, None] * inv_freq, col[:, None] * inv_freq], axis=-1)
    cos, sin = jnp.cos(ang), jnp.sin(ang)
    cos2 = jnp.tile(jnp.concatenate([cos, cos], axis=-1), (1, 2))
    sin2 = jnp.tile(jnp.concatenate([-sin, sin], axis=-1), (1, 2))
    return cos2, sin2


def _pair_lanes(v):
    return jnp.repeat(v, HEAD_DIM, axis=-1).reshape(*v.shape[:-1], v.shape[-1] // 2, LANES)


def kernel(x, c, ctx, c_ctx, mod_w, mod_b, norm1_g, norm2_g, ab_w_in, ab_w_out, na_q_norm, na_k_norm,
           na_rpb, ret_log_decay, swa_w_in, swa_w_out, swa_q_norm, swa_k_norm, swa_sink,
           peer_w_q, peer_sub_keys, peer_u, peer_v):
    b, s, d = x.shape
    l = ctx.shape[1]
    depth = mod_w.shape[0]
    assert depth == 2 and b + 1 <= 8
    assert s % (NA_Q_ROWS * GRID_W) == 0 and s // (NA_Q_ROWS * GRID_W) >= 3
    assert s % SWA_Q_TILE == 0 and s // SWA_Q_TILE >= 3 and s % RET_CHUNK == 0 and l % RET_CHUNK == 0
    for tile in (TOKEN_TILE, PROJ_TOKEN_TILE, PEER_TOKEN_TILE, PEER_MIX_TOKEN_TILE):
        assert s % tile == 0 and (b * l) % min(tile, b * l) == 0 and (b * l) % LANES == 0

    cond = jnp.concatenate([c, c_ctx[None, :], jnp.zeros((8 - b - 1, d), F32)], axis=0)
    mods = _adaln(cond, mod_w, mod_b).reshape(depth, 8, 6, d)
    lat = lambda layer, which: mods[layer, :b, which]
    cx = lambda layer, which: mods[layer, b:b + 1, which]

    cos2, sin2 = _rope_tables(s)
    x_lat = x.reshape(b * s, d)
    x_ctx = ctx.reshape(b * l, d)
    tile2 = lambda g: jnp.tile(g, 2).reshape(1, LANES)

    def peer_weights(layer):
        n_heads = peer_sub_keys.shape[1]
        keys = peer_sub_keys[layer].reshape(2 * n_heads, PEER_KEYS, -1).astype(BF16)
        vt = peer_v[layer].reshape(-1, PEER_EXPERT_TILE, d).transpose(0, 2, 1).astype(BF16)
        return peer_w_q[layer].astype(BF16), keys, peer_u, layer, vt

    n_na = na_rpb.shape[1]
    n_ret = ret_log_decay.shape[2]
    na_pairs, ret_pairs = n_na // 2, n_ret // 2
    w_in = ab_w_in[0].astype(BF16)
    w_out = ab_w_out[0].astype(BF16)
    wa = n_na * HEAD_DIM
    p_lat = _modmm(x_lat, norm1_g[0], lat(0, 0), lat(0, 1), w_in, s).reshape(b, s, -1)
    p_ctx = _modmm(x_ctx, norm1_g[0], cx(0, 0), cx(0, 1), w_in, b * l).reshape(b, l, -1)

    qn, kn = tile2(na_q_norm[0]), tile2(na_k_norm[0])
    oa_lat = _na_attention(p_lat, p_ctx, _na_bias(na_rpb[0], s // GRID_W), qn, kn, na_pairs)
    oa_ctx = _ctx_attention(p_ctx, qn, kn, na_pairs)

    lg = _pair_lanes(ret_log_decay[0]).transpose(1, 0, 2)
    ret_col0 = 3 * wa // LANES
    zeros_state = jnp.zeros((b, ret_pairs, 4, HEAD_DIM, HEAD_DIM), F32)
    ones_tab, zeros_tab = jnp.ones((l, LANES), F32), jnp.zeros((l, LANES), F32)
    ob_ctx, st_ctx = _retention(p_ctx, ones_tab, zeros_tab, lg, zeros_state, ret_col0, ret_pairs, False)
    ob_lat, _ = _retention(p_lat, cos2, sin2, lg, st_ctx, ret_col0, ret_pairs, True)

    w_list = [w_out[:wa], w_out[wa:]]
    x_lat, h_lat = _outproj([oa_lat.reshape(b * s, -1), ob_lat.reshape(b * s, -1)], w_list, x_lat,
                            lat(0, 2), norm2_g[0], lat(0, 3), lat(0, 4), s)
    x_ctx, h_ctx = _outproj([oa_ctx.reshape(b * l, -1), ob_ctx.reshape(b * l, -1)], w_list, x_ctx,
                            cx(0, 2), norm2_g[0], cx(0, 3), cx(0, 4), b * l)
    pw = peer_weights(0)
    x_lat = _peer(h_lat, x_lat, lat(0, 5), s, *pw)
    x_ctx = _peer(h_ctx, x_ctx, cx(0, 5), b * l, *pw)

    n_q = swa_sink.shape[1]
    n_kv = (swa_w_in.shape[2] // HEAD_DIM - n_q) // 2
    w_in = swa_w_in[0].astype(BF16)
    p_lat = _modmm(x_lat, norm1_g[1], lat(1, 0), lat(1, 1), w_in, s).reshape(b, s, -1)
    p_ctx = _modmm(x_ctx, norm1_g[1], cx(1, 0), cx(1, 1), w_in, b * l).reshape(b, l, -1)
    sink_rows = jnp.broadcast_to(swa_sink[0][:, None], (n_q, LANES))
    o_lat = _swa_attention(p_lat, p_ctx, cos2, sin2, tile2(swa_q_norm[0]), tile2(swa_k_norm[0]),
                           sink_rows, n_q, n_kv)
    x_lat, h_lat = _outproj([o_lat.reshape(b * s, -1)], [swa_w_out[0].astype(BF16)], x_lat,
                            lat(1, 2), norm2_g[1], lat(1, 3), lat(1, 4), s)
    x_lat = _peer(h_lat, x_lat, lat(1, 5), s, *peer_weights(1))
    return x_lat.reshape(b, s, d)
```

```python
import functools

import numpy as np
import jax
import jax.numpy as jnp
from jax import lax
from jax.experimental import pallas as pl
from jax.experimental.pallas import tpu as pltpu

F32 = jnp.float32
BF16 = jnp.bfloat16

HEAD_DIM = 64
GRID_W = 64
NA_ROWS = 8
NA_COLS = 16
SWA_WINDOW = 128
PEER_TOPK = 16
PEER_KEYS = 128
ROPE_BASE = 10000.0
NORM_EPS = 1e-6
GN_EPS = 1e-5
NEG_INF = -1e30
ATTN_SCALE = HEAD_DIM ** -0.5
INV_SQRT2 = 0.7071067811865476

LANES = 128
VMEM_LIMIT = 56 * 1024 * 1024

TOKEN_TILE = 1024
PROJ_TOKEN_TILE = 1024
PROJ_N_TILE = 1792
RET_CHUNK = 256
NA_Q_ROWS = 8
NA_BAND_ROWS = 16
SWA_Q_TILE = 256
PEER_TOKEN_TILE = 1024
PEER_MIX_TOKEN_TILE = 1024
PEER_EXPERT_TILE = 1024


def _dot(a, b):
    return jnp.dot(a, b, preferred_element_type=F32)


def _dot_nt(a, b):
    return lax.dot_general(a, b, (((1,), (1,)), ((), ())), preferred_element_type=F32)


def _dot_tn(a, b):
    return lax.dot_general(a, b, (((0,), (0,)), ((), ())), preferred_element_type=F32)


def _params(*sem):
    return pltpu.CompilerParams(dimension_semantics=sem, vmem_limit_bytes=VMEM_LIMIT)


def _rms_rows(x, gain):
    ms = jnp.mean(x * x, axis=-1, keepdims=True)
    return x * lax.rsqrt(ms + NORM_EPS) * gain


def _modulate(x, gain, shift, scale):
    return _rms_rows(x, gain) * (1.0 + scale) + shift


def _head_rms(x, gain):
    lane = lax.broadcasted_iota(jnp.int32, x.shape, 1)
    lo = lane < HEAD_DIM
    ss = x * x
    s_lo = jnp.sum(jnp.where(lo, ss, 0.0), axis=-1, keepdims=True)
    s_hi = jnp.sum(jnp.where(lo, 0.0, ss), axis=-1, keepdims=True)
    ms = jnp.where(lo, s_lo, s_hi) * (1.0 / HEAD_DIM)
    return x * lax.rsqrt(ms + NORM_EPS) * gain


def _rope(x, cos2, sin2):
    lane = lax.broadcasted_iota(jnp.int32, x.shape, 1)
    first_half = (lane & (HEAD_DIM // 2)) == 0
    swapped = jnp.where(first_half, pltpu.roll(x, LANES - HEAD_DIM // 2, axis=1),
                        pltpu.roll(x, HEAD_DIM // 2, axis=1))
    return x * cos2 + swapped * sin2


def _adaln_kernel(c_ref, w_ref, b_ref, o_ref):
    c = c_ref[...]
    s = c * jax.nn.sigmoid(c)
    w = w_ref[0]
    s_hi = s.astype(BF16).astype(F32)
    w_hi = w.astype(BF16)
    w_lo = (w - w_hi.astype(F32)).astype(BF16)
    rows = s.shape[0]
    both = _dot(jnp.concatenate([s_hi, s - s_hi], axis=0).astype(BF16), w_hi)
    o_ref[0] = both[:rows] + both[rows:] + _dot(s_hi.astype(BF16), w_lo) + b_ref[0]


def _adaln(cond, mod_w, mod_b):
    depth, d, n = mod_w.shape
    tn = n // 4
    return pl.pallas_call(
        _adaln_kernel,
        grid=(depth, n // tn),
        in_specs=[pl.BlockSpec((8, d), lambda l, j: (0, 0)),
                  pl.BlockSpec((1, d, tn), lambda l, j: (l, 0, j)),
                  pl.BlockSpec((1, 1, tn), lambda l, j: (l, 0, j))],
        out_specs=pl.BlockSpec((1, 8, tn), lambda l, j: (l, 0, j)),
        out_shape=jax.ShapeDtypeStruct((depth, 8, n), F32),
        compiler_params=_params("arbitrary", "arbitrary"),
        name="adaln",
    )(cond, mod_w, mod_b.reshape(depth, 1, n))


def _modmm_kernel(x_ref, g_ref, sh_ref, sc_ref, w_ref, o_ref, h_scr):
    @pl.when(pl.program_id(1) == 0)
    def _():
        h_scr[...] = _modulate(x_ref[...], g_ref[...], sh_ref[0], sc_ref[0]).astype(BF16)

    o_ref[...] = _dot(h_scr[...], w_ref[...])


def _modmm(x, gain, shift, scale, w, seg):
    t, d = x.shape
    n = w.shape[1]
    tn = n
    tm = min(PROJ_TOKEN_TILE if n <= PROJ_N_TILE else PROJ_TOKEN_TILE // 2, seg)
    per_seg = seg // tm
    r = shift.shape[0]
    return pl.pallas_call(
        _modmm_kernel,
        grid=(t // tm, n // tn),
        in_specs=[pl.BlockSpec((tm, d), lambda i, j: (i, 0)),
                  pl.BlockSpec((1, d), lambda i, j: (0, 0)),
                  pl.BlockSpec((1, 1, d), lambda i, j: (i // per_seg, 0, 0)),
                  pl.BlockSpec((1, 1, d), lambda i, j: (i // per_seg, 0, 0)),
                  pl.BlockSpec((d, tn), lambda i, j: (0, j))],
        out_specs=pl.BlockSpec((tm, tn), lambda i, j: (i, j)),
        out_shape=jax.ShapeDtypeStruct((t, n), F32),
        scratch_shapes=[pltpu.VMEM((tm, d), BF16)],
        compiler_params=_params("arbitrary", "arbitrary"),
        name="modulate_matmul",
    )(x, gain.reshape(1, d), shift.reshape(r, 1, d), scale.reshape(r, 1, d), w)


def _outproj_kernel(*refs, n_in):
    a_refs = refs[:n_in]
    w_refs = refs[n_in:2 * n_in]
    x_ref, gate_ref, g2_ref, sh_ref, sc_ref, xo_ref, h_ref = refs[2 * n_in:]
    acc = None
    for a_ref, w_ref in zip(a_refs, w_refs):
        part = _dot(a_ref[...].astype(BF16), w_ref[...])
        acc = part if acc is None else acc + part
    xn = x_ref[...] + gate_ref[0] * acc
    xo_ref[...] = xn
    h_ref[...] = _modulate(xn, g2_ref[...], sh_ref[0], sc_ref[0]).astype(BF16)


def _outproj(a_list, w_list, x, gate, gain2, shift2, scale2, seg):
    t, d = x.shape
    tm = min(TOKEN_TILE, seg)
    per_seg = seg // tm
    r = gate.shape[0]
    n_in = len(a_list)
    row = lambda i: (i // per_seg, 0, 0)
    in_specs = ([pl.BlockSpec((tm, a.shape[1]), lambda i: (i, 0)) for a in a_list]
                + [pl.BlockSpec(w.shape, lambda i: (0, 0)) for w in w_list]
                + [pl.BlockSpec((tm, d), lambda i: (i, 0)),
                   pl.BlockSpec((1, 1, d), row),
                   pl.BlockSpec((1, d), lambda i: (0, 0)),
                   pl.BlockSpec((1, 1, d), row),
                   pl.BlockSpec((1, 1, d), row)])
    return pl.pallas_call(
        functools.partial(_outproj_kernel, n_in=n_in),
        grid=(t // tm,),
        in_specs=in_specs,
        out_specs=[pl.BlockSpec((tm, d), lambda i: (i, 0)), pl.BlockSpec((tm, d), lambda i: (i, 0))],
        out_shape=[jax.ShapeDtypeStruct((t, d), F32), jax.ShapeDtypeStruct((t, d), BF16)],
        compiler_params=_params("arbitrary"),
        name="out_proj_residual",
    )(*a_list, *w_list, x, gate.reshape(r, 1, d), gain2.reshape(1, d),
      shift2.reshape(r, 1, d), scale2.reshape(r, 1, d))


def _softmax_pv(s_list, v_list, extra=None, fold_lanes=False):
    def lane_chunks(x):
        return [x[:, c:c + LANES] for c in range(0, x.shape[1], LANES)] if fold_lanes else [x]

    def row_reduce(blocks, op, reduce):
        acc = None
        for blk in blocks:
            chunks = lane_chunks(blk)
            part = chunks[0]
            for c in chunks[1:]:
                part = op(part, c)
            if not fold_lanes:
                part = reduce(part, axis=-1, keepdims=True)
            acc = part if acc is None else op(acc, part)
        return reduce(acc, axis=-1, keepdims=True) if fold_lanes else acc

    m = row_reduce(s_list, jnp.maximum, jnp.max)
    if extra is not None:
        m = jnp.maximum(m, extra)
    ps = []
    out = None
    for s, v in zip(s_list, v_list):
        p = jnp.exp(s - m)
        ps.append(p)
        pv = _dot(p.astype(BF16), v)
        out = pv if out is None else out + pv
    denom = row_reduce(ps, jnp.add, jnp.sum)
    if extra is not None:
        denom = denom + jnp.exp(extra - m)
    return out / denom


def _na_kernel(q_ref, k_ref, v_ref, kc_ref, vc_ref, bias_ref, qn_ref, kn_ref, o_ref,
               k_scr, v_scr, kc_scr, vc_scr, *, n_steps):
    i = pl.program_id(2)
    band = NA_BAND_ROWS * GRID_W
    start = jnp.clip(NA_Q_ROWS * i - NA_ROWS // 2, 0, NA_Q_ROWS * n_steps - NA_BAND_ROWS) * GRID_W
    start = pl.multiple_of(start, NA_ROWS // 2 * GRID_W)

    @pl.when(i == 0)
    def _():
        k_scr[...] = _head_rms(k_ref[0], kn_ref[...]).astype(BF16)
        v_scr[...] = v_ref[0].astype(BF16)
        kc_scr[...] = _head_rms(kc_ref[0], kn_ref[...]).astype(BF16)
        vc_scr[...] = vc_ref[0].astype(BF16)

    q = (_head_rms(q_ref[0], qn_ref[...]) * ATTN_SCALE).astype(BF16)
    kb = k_scr[pl.ds(start, band), :]
    vb = v_scr[pl.ds(start, band), :]
    kc = kc_scr[...]
    vc = vc_scr[...]
    outs = []
    for h in range(2):
        sl = slice(h * HEAD_DIM, (h + 1) * HEAD_DIM)
        s_loc = _dot_nt(q[:, sl], kb[:, sl]) + bias_ref[0, h]
        s_ctx = _dot_nt(q[:, sl], kc[:, sl])
        outs.append(_softmax_pv([s_loc, s_ctx], [vb[:, sl], vc[:, sl]]))
    o_ref[0] = jnp.concatenate(outs, axis=-1).astype(o_ref.dtype)


def _na_bias(rpb, rows):
    h = rpb.shape[0]
    col = np.arange(GRID_W)
    c0 = np.clip(col - NA_COLS // 2, 0, GRID_W - NA_COLS)
    dc = col[None, :] - col[:, None] + (NA_COLS - 1)
    ok_c = (col[None, :] >= c0[:, None]) & (col[None, :] < c0[:, None] + NA_COLS)
    pick = ((np.arange(2 * NA_COLS - 1)[:, None, None] == dc[None]) & ok_c[None]).astype(np.float32)
    blocks = jnp.einsum('hrd,dqk->hrqk', rpb, pick, precision=lax.Precision.HIGHEST)
    blocks = jnp.where(ok_c, blocks, NEG_INF)
    qt, band = NA_Q_ROWS * GRID_W, NA_BAND_ROWS * GRID_W
    return pl.pallas_call(
        functools.partial(_na_bias_kernel, rows=rows),
        grid=(3, h),
        in_specs=[pl.BlockSpec((1,) + blocks.shape[1:], lambda c, hh: (hh, 0, 0, 0))],
        out_specs=pl.BlockSpec((1, 1, qt, band), lambda c, hh: (c, hh, 0, 0)),
        out_shape=jax.ShapeDtypeStruct((3, h, qt, band), F32),
        compiler_params=_params("arbitrary", "arbitrary"),
        name="na_bias_tables",
    )(blocks)


def _na_bias_kernel(blk_ref, o_ref, *, rows):
    n_steps = rows // NA_Q_ROWS
    for c, step in enumerate((0, 1, n_steps - 1)):
        @pl.when(pl.program_id(0) == c)
        def _(step=step):
            o_ref[0, 0] = jnp.full(o_ref.shape[2:], NEG_INF, F32)
            band0 = min(max(NA_Q_ROWS * step - NA_ROWS // 2, 0), rows - NA_BAND_ROWS)
            for rq in range(NA_Q_ROWS):
                r = NA_Q_ROWS * step + rq
                r0 = min(max(r - NA_ROWS // 2, 0), rows - NA_ROWS)
                for rk in range(r0, r0 + NA_ROWS):
                    col0 = (rk - band0) * GRID_W
                    o_ref[0, 0, rq * GRID_W:(rq + 1) * GRID_W, col0:col0 + GRID_W] = (
                        blk_ref[0, rk - r + NA_ROWS - 1])


def _na_attention(p_lat, p_ctx, bias, qn, kn, n_pairs):
    b, s, _ = p_lat.shape
    l = p_ctx.shape[1]
    rows = s // GRID_W
    n_steps = rows // NA_Q_ROWS
    qt = NA_Q_ROWS * GRID_W

    def cls(i):
        return jnp.where(i == 0, 0, jnp.where(i == n_steps - 1, 2, 1))

    return pl.pallas_call(
        functools.partial(_na_kernel, n_steps=n_steps),
        grid=(n_pairs, b, n_steps),
        in_specs=[pl.BlockSpec((1, qt, LANES), lambda hp, bb, i: (bb, i, hp)),
                  pl.BlockSpec((1, s, LANES), lambda hp, bb, i: (bb, 0, n_pairs + hp)),
                  pl.BlockSpec((1, s, LANES), lambda hp, bb, i: (bb, 0, 2 * n_pairs + hp)),
                  pl.BlockSpec((1, l, LANES), lambda hp, bb, i: (bb, 0, n_pairs + hp)),
                  pl.BlockSpec((1, l, LANES), lambda hp, bb, i: (bb, 0, 2 * n_pairs + hp)),
                  pl.BlockSpec((1, 2, qt, NA_BAND_ROWS * GRID_W), lambda hp, bb, i: (cls(i), hp, 0, 0)),
                  pl.BlockSpec((1, LANES), lambda hp, bb, i: (0, 0)),
                  pl.BlockSpec((1, LANES), lambda hp, bb, i: (0, 0))],
        out_specs=pl.BlockSpec((1, qt, LANES), lambda hp, bb, i: (bb, i, hp)),
        out_shape=jax.ShapeDtypeStruct((b, s, n_pairs * LANES), BF16),
        scratch_shapes=[pltpu.VMEM((s, LANES), BF16), pltpu.VMEM((s, LANES), BF16),
                        pltpu.VMEM((l, LANES), BF16), pltpu.VMEM((l, LANES), BF16)],
        compiler_params=_params("arbitrary", "arbitrary", "arbitrary"),
        name="neighbourhood_attention",
    )(p_lat, p_lat, p_lat, p_ctx, p_ctx, bias, qn, kn)


def _ctx_attn_kernel(q_ref, k_ref, v_ref, qn_ref, kn_ref, o_ref):
    q = (_head_rms(q_ref[0], qn_ref[...]) * ATTN_SCALE).astype(BF16)
    k = _head_rms(k_ref[0], kn_ref[...]).astype(BF16)
    v = v_ref[0].astype(BF16)
    outs = []
    for h in range(2):
        sl = slice(h * HEAD_DIM, (h + 1) * HEAD_DIM)
        outs.append(_softmax_pv([_dot_nt(q[:, sl], k[:, sl])], [v[:, sl]]))
    o_ref[0] = jnp.concatenate(outs, axis=-1).astype(o_ref.dtype)


def _ctx_attention(p_ctx, qn, kn, n_pairs):
    b, l, _ = p_ctx.shape
    return pl.pallas_call(
        _ctx_attn_kernel,
        grid=(n_pairs, b),
        in_specs=[pl.BlockSpec((1, l, LANES), lambda hp, bb: (bb, 0, hp)),
                  pl.BlockSpec((1, l, LANES), lambda hp, bb: (bb, 0, n_pairs + hp)),
                  pl.BlockSpec((1, l, LANES), lambda hp, bb: (bb, 0, 2 * n_pairs + hp)),
                  pl.BlockSpec((1, LANES), lambda hp, bb: (0, 0)),
                  pl.BlockSpec((1, LANES), lambda hp, bb: (0, 0))],
        out_specs=pl.BlockSpec((1, l, LANES), lambda hp, bb: (bb, 0, hp)),
        out_shape=jax.ShapeDtypeStruct((b, l, n_pairs * LANES), BF16),
        compiler_params=_params("arbitrary", "arbitrary"),
        name="context_attention",
    )(p_ctx, p_ctx, p_ctx, qn, kn)


def _ret_kernel(q_ref, k_ref, v_ref, g_ref, cos_ref, sin_ref, lg_ref, s0_ref, y_ref, st_ref, sf_scr,
                kr_scr, *, n_chunks, use_rope):
    c = RET_CHUNK
    hd = HEAD_DIM
    lg = -jnp.exp(lg_ref[0])
    lgf, lgb = lg[0:1, :], lg[1:2, :]
    ii = lax.broadcasted_iota(jnp.int32, (c, LANES), 0).astype(F32)
    dq_f = jnp.exp(lgf * (ii + 1.0))
    dk_f = jnp.exp(lgf * (c - 1.0 - ii))
    dq_b = jnp.exp(lgb * (c - ii))
    dk_b = jnp.exp(lgb * ii)
    dc_f = jnp.exp(lgf * float(c))
    dc_b = jnp.exp(lgb * float(c))
    diff = (lax.broadcasted_iota(jnp.int32, (c, c), 0) - lax.broadcasted_iota(jnp.int32, (c, c), 1)).astype(F32)
    intra = []
    chunk_f = []
    chunk_b = []
    for h in range(2):
        lf = lgf[:, h * hd:h * hd + 1]
        lb = lgb[:, h * hd:h * hd + 1]
        intra.append(jnp.where(diff >= 0, jnp.exp(lf * jnp.maximum(diff, 0.0)),
                               jnp.exp(lb * jnp.maximum(-diff, 0.0))))
        chunk_f.append(dc_f[:, h * hd:h * hd + 1])
        chunk_b.append(dc_b[:, h * hd:h * hd + 1])

    def load(n, first_sweep):
        r = pl.multiple_of(n * c, c)
        v = v_ref[0, pl.ds(r, c), :].astype(BF16)
        if first_sweep:
            k = k_ref[0, pl.ds(r, c), :]
            if use_rope:
                k = _rope(k, cos_ref[pl.ds(r, c), :], sin_ref[pl.ds(r, c), :])
            kr_scr[pl.ds(r, c), :] = k
            return r, None, k, v
        q = q_ref[0, pl.ds(r, c), :]
        if use_rope:
            q = _rope(q, cos_ref[pl.ds(r, c), :], sin_ref[pl.ds(r, c), :])
        return r, q * ATTN_SCALE, kr_scr[pl.ds(r, c), :], v

    def fwd(n, carry):
        _, _, k, v = load(n, True)
        kd = (k * dk_f).astype(BF16)
        new = []
        for h in range(2):
            sl = slice(h * hd, (h + 1) * hd)
            sf_scr[n, h] = carry[h]
            new.append(carry[h] * chunk_f[h] + _dot_tn(kd[:, sl], v[:, sl]))
        return tuple(new)

    sf = lax.fori_loop(0, n_chunks, fwd, (s0_ref[0, 0, 0], s0_ref[0, 0, 1]), unroll=2)
    st_ref[0, 0, 0] = sf[0]
    st_ref[0, 0, 1] = sf[1]

    def bwd(jj, carry):
        n = n_chunks - 1 - jj
        r, q, k, v = load(n, False)
        qb = q.astype(BF16)
        kb = k.astype(BF16)
        qf = (q * dq_f).astype(BF16)
        qr = (q * dq_b).astype(BF16)
        kd = (k * dk_b).astype(BF16)
        outs = []
        new = []
        for h in range(2):
            sl = slice(h * hd, (h + 1) * hd)
            a = (_dot_nt(qb[:, sl], kb[:, sl]) * intra[h]).astype(BF16)
            o = (_dot(a, v[:, sl]) + _dot(qf[:, sl], sf_scr[n, h].astype(BF16))
                 + _dot(qr[:, sl], carry[h].astype(BF16)))
            oc = o - jnp.mean(o, axis=-1, keepdims=True)
            outs.append(oc * lax.rsqrt(jnp.mean(oc * oc, axis=-1, keepdims=True) + GN_EPS))
            new.append(carry[h] * chunk_b[h] + _dot_tn(kd[:, sl], v[:, sl]))
        g = g_ref[0, pl.ds(r, c), :]
        y_ref[0, pl.ds(r, c), :] = (jnp.concatenate(outs, axis=-1) * (g * jax.nn.sigmoid(g))).astype(y_ref.dtype)
        return tuple(new)

    sb = lax.fori_loop(0, n_chunks, bwd, (s0_ref[0, 0, 2], s0_ref[0, 0, 3]), unroll=2)
    st_ref[0, 0, 2] = sb[0]
    st_ref[0, 0, 3] = sb[1]


def _retention(p, cos2, sin2, lg, s0, col0, n_pairs, use_rope):
    b, t, _ = p.shape
    n_chunks = t // RET_CHUNK
    tab = lambda hp, bb: (0, 0)
    return pl.pallas_call(
        functools.partial(_ret_kernel, n_chunks=n_chunks, use_rope=use_rope),
        grid=(n_pairs, b),
        in_specs=[pl.BlockSpec((1, t, LANES), lambda hp, bb: (bb, 0, col0 + hp)),
                  pl.BlockSpec((1, t, LANES), lambda hp, bb: (bb, 0, col0 + n_pairs + hp)),
                  pl.BlockSpec((1, t, LANES), lambda hp, bb: (bb, 0, col0 + 2 * n_pairs + hp)),
                  pl.BlockSpec((1, t, LANES), lambda hp, bb: (bb, 0, col0 + 3 * n_pairs + hp)),
                  pl.BlockSpec(cos2.shape, tab),
                  pl.BlockSpec(sin2.shape, tab),
                  pl.BlockSpec((1, 2, LANES), lambda hp, bb: (hp, 0, 0)),
                  pl.BlockSpec((1, 1, 4, HEAD_DIM, HEAD_DIM), lambda hp, bb: (bb, hp, 0, 0, 0))],
        out_specs=[pl.BlockSpec((1, t, LANES), lambda hp, bb: (bb, 0, hp)),
                   pl.BlockSpec((1, 1, 4, HEAD_DIM, HEAD_DIM), lambda hp, bb: (bb, hp, 0, 0, 0))],
        out_shape=[jax.ShapeDtypeStruct((b, t, n_pairs * LANES), BF16),
                   jax.ShapeDtypeStruct((b, n_pairs, 4, HEAD_DIM, HEAD_DIM), F32)],
        scratch_shapes=[pltpu.VMEM((n_chunks, 2, HEAD_DIM, HEAD_DIM), F32), pltpu.VMEM((t, LANES), F32)],
        compiler_params=_params("arbitrary", "arbitrary"),
        name="retention",
    )(p, p, p, p, cos2, sin2, lg, s0)


def _swa_kernel(q_ref, k_ref, v_ref, kc_ref, vc_ref, cos_ref, sin_ref, qn_ref, kn_ref, sink_ref, far_ref,
                o_ref, k_scr, kc_scr, *, seq):
    qt = SWA_Q_TILE
    wk = qt + 2 * SWA_WINDOW
    hd = HEAD_DIM
    n = pl.program_id(2)
    q0 = pl.multiple_of(n * qt, qt)
    ws = pl.multiple_of(jnp.clip(n * qt - SWA_WINDOW, 0, seq - wk), SWA_WINDOW)

    @pl.when(n == 0)
    def _():
        k_scr[...] = _rope(_head_rms(k_ref[0], kn_ref[...]), cos_ref[...], sin_ref[...])
        kc_scr[...] = _head_rms(kc_ref[0], kn_ref[...])

    kw = k_scr[pl.ds(ws, wk), :].astype(BF16)
    vw = v_ref[0, pl.ds(ws, wk), :].astype(BF16)
    kc = kc_scr[...].astype(BF16)
    vc = vc_ref[0].astype(BF16)
    cos_q = cos_ref[pl.ds(q0, qt), :]
    sin_q = sin_ref[pl.ds(q0, qt), :]
    qs = []
    for s in range(4):
        slab = q_ref[0, :, s * LANES:(s + 1) * LANES]
        qs.append((_rope(_head_rms(slab, qn_ref[...]), cos_q, sin_q) * ATTN_SCALE).astype(BF16))
    far = far_ref[0][None]
    for kh in range(2):
        sl = slice(kh * hd, (kh + 1) * hd)
        qstack = jnp.concatenate(
            [qs[kh * 2 + g // 2][:, (g % 2) * hd:(g % 2 + 1) * hd] for g in range(4)], axis=0)
        sink = jnp.concatenate(
            [jnp.broadcast_to(sink_ref[kh * 4 + g:kh * 4 + g + 1, 0:1], (qt, 1)) for g in range(4)], axis=0)
        s_loc = (_dot_nt(qstack, kw[:, sl]).reshape(4, qt, wk) + far).reshape(4 * qt, wk)
        s_ctx = _dot_nt(qstack, kc[:, sl])
        o = _softmax_pv([s_loc, s_ctx], [vw[:, sl], vc[:, sl]], extra=sink, fold_lanes=True)
        for pair in range(2):
            col = (kh * 2 + pair) * LANES
            o_ref[0, :, col:col + LANES] = jnp.concatenate(
                [o[(2 * pair) * qt:(2 * pair + 1) * qt], o[(2 * pair + 1) * qt:(2 * pair + 2) * qt]],
                axis=-1).astype(o_ref.dtype)


def _swa_attention(p_lat, p_ctx, cos2, sin2, qn, kn, sink_rows, n_q_heads, n_kv_heads):
    b, s, _ = p_lat.shape
    l = p_ctx.shape[1]
    kv_pairs = n_kv_heads // 2
    q_blocks = n_q_heads * HEAD_DIM // LANES
    q_per_pair = q_blocks // kv_pairs
    qw = q_per_pair * LANES
    tab = lambda kp, bb, n: (0, 0)
    n_steps = s // SWA_Q_TILE
    wk = SWA_Q_TILE + 2 * SWA_WINDOW
    far = []
    for step in (0, 1, n_steps - 1):
        q0 = step * SWA_Q_TILE
        ws = min(max(q0 - SWA_WINDOW, 0), s - wk)
        dist = np.abs((q0 + np.arange(SWA_Q_TILE))[:, None] - (ws + np.arange(wk))[None, :])
        far.append(np.where(dist <= SWA_WINDOW, 0.0, NEG_INF))
    far = jnp.asarray(np.stack(far), F32)

    def cls(n):
        return jnp.where(n == 0, 0, jnp.where(n == n_steps - 1, 2, 1))

    return pl.pallas_call(
        functools.partial(_swa_kernel, seq=s),
        grid=(kv_pairs, b, n_steps),
        in_specs=[pl.BlockSpec((1, SWA_Q_TILE, qw), lambda kp, bb, n: (bb, n, kp)),
                  pl.BlockSpec((1, s, LANES), lambda kp, bb, n: (bb, 0, q_blocks + kp)),
                  pl.BlockSpec((1, s, LANES), lambda kp, bb, n: (bb, 0, q_blocks + kv_pairs + kp)),
                  pl.BlockSpec((1, l, LANES), lambda kp, bb, n: (bb, 0, q_blocks + kp)),
                  pl.BlockSpec((1, l, LANES), lambda kp, bb, n: (bb, 0, q_blocks + kv_pairs + kp)),
                  pl.BlockSpec(cos2.shape, tab),
                  pl.BlockSpec(sin2.shape, tab),
                  pl.BlockSpec((1, LANES), tab),
                  pl.BlockSpec((1, LANES), tab),
                  pl.BlockSpec((8, LANES), lambda kp, bb, n: (kp, 0)),
                  pl.BlockSpec((1, SWA_Q_TILE, wk), lambda kp, bb, n: (cls(n), 0, 0))],
        out_specs=pl.BlockSpec((1, SWA_Q_TILE, qw), lambda kp, bb, n: (bb, n, kp)),
        out_shape=jax.ShapeDtypeStruct((b, s, n_q_heads * HEAD_DIM), BF16),
        scratch_shapes=[pltpu.VMEM((s, LANES), F32), pltpu.VMEM((l, LANES), F32)],
        compiler_params=_params("arbitrary", "arbitrary", "arbitrary"),
        name="windowed_gqa",
    )(p_lat, p_lat, p_lat, p_ctx, p_ctx, cos2, sin2, qn, kn, sink_rows, far)


def _oddeven_merge_sort(n):
    pairs = []

    def merge(lo, size, r):
        step = r * 2
        if step < size:
            merge(lo, size, step)
            merge(lo + r, size, step)
            pairs.extend((i, i + r) for i in range(lo + r, lo + size - r, step))
        else:
            pairs.append((lo, lo + r))

    def sort(lo, size):
        if size > 1:
            sort(lo, size // 2)
            sort(lo + size // 2, size // 2)
            merge(lo, size, 1)

    sort(0, n)
    return pairs


def _top_rows(x, k, scr):
    tiles = [x[8 * i:8 * i + 8, :] for i in range(x.shape[0] // 8)]
    for i, j in _oddeven_merge_sort(len(tiles)):
        tiles[i], tiles[j] = jnp.maximum(tiles[i], tiles[j]), jnp.minimum(tiles[i], tiles[j])
    for r in range(k):
        m = jnp.max(tiles[0], axis=0, keepdims=True)
        scr[r:r + 1, :] = m
        need = k - r - 1
        if need > 0:
            hit = tiles[0] == m
            for d in range(min(need, len(tiles) - 1)):
                tiles[d] = jnp.where(hit, tiles[d + 1], tiles[d])
            if need >= len(tiles):
                tiles[-1] = jnp.where(hit, NEG_INF, tiles[-1])


def _bf16_pair(x):
    bits = lax.bitcast_convert_type(x.astype(BF16).astype(F32), jnp.uint32)
    return bits | (bits >> 16)


def _count_above(sorted_scr, y, strict):
    row = lambda i: sorted_scr[i:i + 1, :]
    above = (lambda r: r > y) if strict else (lambda r: r >= y)
    c8 = above(row(7))
    c4 = above(jnp.where(c8, row(11), row(3)))
    c2 = above(jnp.where(c8, jnp.where(c4, row(13), row(9)), jnp.where(c4, row(5), row(1))))
    hi = jnp.where(c4, jnp.where(c2, row(14), row(12)), jnp.where(c2, row(10), row(8)))
    lo = jnp.where(c4, jnp.where(c2, row(6), row(4)), jnp.where(c2, row(2), row(0)))
    c1 = above(jnp.where(c8, hi, lo))
    count = (jnp.where(c8, 8.0, 0.0) + jnp.where(c4, 4.0, 0.0)) + (jnp.where(c2, 2.0, 0.0) + jnp.where(c1, 1.0, 0.0))
    return count + jnp.where(above(row(15)), 1.0, 0.0)


def _peer_scores_kernel(h_ref, wq_ref, keys_ref, ra_ref, p1_ref, gb_ref, p2_ref,
                        q_scr, s_scr, a_scr, b_scr, c_scr, v_scr, *, n_heads):
    k = PEER_TOPK
    n_blocks = h_ref.shape[0] // LANES

    q = _dot(h_ref[...], wq_ref[...])
    for hp in range(2 * n_heads):
        q_scr[hp] = q[:, hp * PEER_KEYS:(hp + 1) * PEER_KEYS].astype(BF16)

    def lane_block(lb, h, slot):
        tops_a, tops_b, cand, tops_c = a_scr.at[slot], b_scr.at[slot], c_scr.at[slot], v_scr.at[slot]
        s1 = s_scr[0, lb]
        s2 = s_scr[1, lb]
        _top_rows(s1, k + 1, tops_a)
        _top_rows(s2, k + 1, tops_b)
        cand[0:16, :] = tops_a[0:1, :] + tops_b[0:16, :]
        for i in range(1, 8):
            cand[8 + 8 * i:16 + 8 * i, :] = tops_a[i:i + 1, :] + tops_b[0:8, :]
        cand[72:80, :] = tops_a[8:16, :] + tops_b[0:1, :]
        cand[80:81, :] = tops_a[0:1, :] + tops_b[16:17, :]
        cand[81:82, :] = tops_a[16:17, :] + tops_b[0:1, :]
        cand[82:PEER_KEYS, :] = jnp.full((PEER_KEYS - 82, LANES), NEG_INF, F32)
        _top_rows(cand[...], k + 1, tops_c)
        thr = 0.5 * (tops_c[k - 1:k, :] + tops_c[k:k + 1, :])
        z = jnp.sum(jnp.exp(tops_c[0:k, :] - tops_c[0:1, :]), axis=0, keepdims=True)
        ra_ref[lb, h] = _bf16_pair(_count_above(tops_a, s1, strict=True))
        p1_ref[lb, h] = _bf16_pair(jnp.exp(s1 - tops_a[0:1, :]) * (INV_SQRT2 / z))
        gb_ref[lb, h] = _count_above(tops_a, thr - s2, strict=False).astype(BF16)
        p2_ref[lb, h] = jnp.exp(s2 - tops_b[0:1, :]).astype(BF16)

    def lane_pair(i, h):
        lane_block(2 * i, h, 0)
        lane_block(2 * i + 1, h, 1)
        return h

    def head(h, carry):
        for p in range(2):
            st = _dot_nt(keys_ref[2 * h + p], q_scr[2 * h + p])
            for lb in range(n_blocks):
                s_scr[p, lb] = st[:, lb * LANES:(lb + 1) * LANES]
        lax.fori_loop(0, n_blocks // 2, lane_pair, h)
        return carry

    lax.fori_loop(0, n_heads, head, 0)


def _peer_scores(h2, wq, keys, n_heads):
    t, d = h2.shape
    tt = PEER_TOKEN_TILE
    nb = tt // LANES
    tab_shape = (t // LANES, n_heads, PEER_KEYS, LANES)
    tab_spec = pl.BlockSpec((nb, n_heads, PEER_KEYS, LANES), lambda i: (i, 0, 0, 0))
    return pl.pallas_call(
        functools.partial(_peer_scores_kernel, n_heads=n_heads),
        grid=(t // tt,),
        in_specs=[pl.BlockSpec((tt, d), lambda i: (i, 0)),
                  pl.BlockSpec(wq.shape, lambda i: (0, 0)),
                  pl.BlockSpec(keys.shape, lambda i: (0, 0, 0))],
        out_specs=[tab_spec] * 4,
        out_shape=[jax.ShapeDtypeStruct(tab_shape, dt) for dt in (jnp.uint32, jnp.uint32, BF16, BF16)],
        scratch_shapes=[pltpu.VMEM((2 * n_heads, tt, PEER_KEYS), BF16),
                        pltpu.VMEM((2, nb, PEER_KEYS, LANES), F32),
                        pltpu.VMEM((2, 24, LANES), F32), pltpu.VMEM((2, 24, LANES), F32),
                        pltpu.VMEM((2, PEER_KEYS, LANES), F32), pltpu.VMEM((2, 24, LANES), F32)],
        compiler_params=_params("arbitrary"),
        name="peer_scores",
    )(h2, wq, keys)


def _peer_mix_kernel(h_ref, u_ref, vt_ref, ra_ref, p1_ref, gb_ref, p2_ref, x_ref, g_ref, o_ref,
                     acc_scr, act_scr, a_scr, *, n_heads, n_tiles):
    s = pl.program_id(1)
    tt = h_ref.shape[0]
    rows_per_tile = PEER_EXPERT_TILE // PEER_KEYS
    tile = (PEER_KEYS, LANES)

    def readout():
        acc_scr[...] += _dot(vt_ref[0], a_scr[...])

    def row_tile(ref, lb, h, aa):
        words = jnp.broadcast_to(ref[lb, h, aa:aa + 1, :], (8, LANES))
        packed = pltpu.bitcast(words, BF16)
        return jnp.broadcast_to(packed[None], (PEER_KEYS // 16, 16, LANES)).reshape(tile)

    def gates():
        for aa in range(rows_per_tile):
            rs = slice(aa * PEER_KEYS, (aa + 1) * PEER_KEYS)
            for lb in range(tt // LANES):
                ls = slice(lb * LANES, (lb + 1) * LANES)
                w = None
                for h in range(n_heads):
                    rank = row_tile(ra_ref, lb, h, aa)
                    p1 = row_tile(p1_ref, lb, h, aa)
                    term = jnp.where(rank < gb_ref[lb, h], p2_ref[lb, h], jnp.zeros(tile, BF16)) * p1
                    w = term if w is None else w + term
                a_scr[rs, ls] = w * act_scr[rs, ls]

    def experts():
        y = _dot_nt((u_ref[0] * INV_SQRT2).astype(BF16), h_ref[...])
        act_scr[...] = (y * (1.0 + lax.erf(y))).astype(BF16)

    @pl.when(s == 0)
    def _():
        acc_scr[...] = jnp.zeros_like(acc_scr)

    i = pl.program_id(0)

    @pl.when(s < n_tiles)
    def _():
        experts()

    @pl.when(i >= 0)
    def _():
        gates()

    @pl.when(s + i >= 0)
    def _():
        readout()

    @pl.when(s == n_tiles - 1)
    def _():
        o_ref[...] = x_ref[...] + g_ref[0] * acc_scr[...].T


def _peer_mix(h2, u_all, layer, vt, tables, x, gate, seg, n_heads):
    t, d = h2.shape
    n_exp = u_all.shape[1]
    tt = min(PEER_MIX_TOKEN_TILE, seg)
    te = PEER_EXPERT_TILE
    n_tiles = n_exp // te
    per_seg = seg // tt
    r = gate.shape[0]
    nb = tt // LANES
    tab_spec = pl.BlockSpec((nb, n_heads, PEER_KEYS, LANES), lambda i, s: (i, 0, 0, 0))
    row_spec = pl.BlockSpec((nb, n_heads, te // PEER_KEYS, LANES), lambda i, s: (i, 0, s, 0))
    return pl.pallas_call(
        functools.partial(_peer_mix_kernel, n_heads=n_heads, n_tiles=n_tiles),
        grid=(t // tt, n_tiles),
        in_specs=[pl.BlockSpec((tt, d), lambda i, s: (i, 0)),
                  pl.BlockSpec((1, te, d), lambda i, s: (layer, s, 0)),
                  pl.BlockSpec((1, d, te), lambda i, s: (s, 0, 0)),
                  row_spec, row_spec, tab_spec, tab_spec,
                  pl.BlockSpec((tt, d), lambda i, s: (i, 0)),
                  pl.BlockSpec((1, 1, d), lambda i, s: (i // per_seg, 0, 0))],
        out_specs=pl.BlockSpec((tt, d), lambda i, s: (i, 0)),
        out_shape=jax.ShapeDtypeStruct((t, d), F32),
        scratch_shapes=[pltpu.VMEM((d, tt), F32), pltpu.VMEM((te, tt), BF16), pltpu.VMEM((te, tt), BF16)],
        compiler_params=_params("arbitrary", "arbitrary"),
        name="peer_mix",
    )(h2, u_all, vt, *tables, x, gate.reshape(r, 1, d))


def _peer(h2, x, gate, seg, wq, keys, u_all, layer, vt):
    n_heads = keys.shape[0] // 2
    tables = _peer_scores(h2, wq, keys, n_heads)
    return _peer_mix(h2, u_all, layer, vt, tables, x, gate, seg, n_heads)


def _rope_tables(n_tokens):
    t = jnp.arange(n_tokens)
    row = (t // GRID_W).astype(F32)
    col = (t % GRID_W).astype(F32)
    n_freq = HEAD_DIM // 4
    inv_freq = jnp.power(ROPE_BASE, -jnp.arange(n_freq, dtype=F32) / n_freq)
    ang = jnp.concatenate([row[:, None] * inv_freq, col[:, None] * inv_freq], axis=-1)
    cos, sin = jnp.cos(ang), jnp.sin(ang)
    cos2 = jnp.tile(jnp.concatenate([cos, cos], axis=-1), (1, 2))
    sin2 = jnp.tile(jnp.concatenate([-sin, sin], axis=-1), (1, 2))
    return cos2, sin2


def _pair_lanes(v):
    return jnp.repeat(v, HEAD_DIM, axis=-1).reshape(*v.shape[:-1], v.shape[-1] // 2, LANES)


def kernel(x, c, ctx, c_ctx, mod_w, mod_b, norm1_g, norm2_g, ab_w_in, ab_w_out, na_q_norm, na_k_norm,
           na_rpb, ret_log_decay, swa_w_in, swa_w_out, swa_q_norm, swa_k_norm, swa_sink,
           peer_w_q, peer_sub_keys, peer_u, peer_v):
    b, s, d = x.shape
    l = ctx.shape[1]
    depth = mod_w.shape[0]
    assert depth == 2 and b + 1 <= 8
    assert s % (NA_Q_ROWS * GRID_W) == 0 and s // (NA_Q_ROWS * GRID_W) >= 3
    assert s % SWA_Q_TILE == 0 and s // SWA_Q_TILE >= 3 and s % RET_CHUNK == 0 and l % RET_CHUNK == 0
    for tile in (TOKEN_TILE, PROJ_TOKEN_TILE, PEER_TOKEN_TILE, PEER_MIX_TOKEN_TILE):
        assert s % tile == 0 and (b * l) % min(tile, b * l) == 0 and (b * l) % LANES == 0

    cond = jnp.concatenate([c, c_ctx[None, :], jnp.zeros((8 - b - 1, d), F32)], axis=0)
    mods = _adaln(cond, mod_w, mod_b).reshape(depth, 8, 6, d)
    lat = lambda layer, which: mods[layer, :b, which]
    cx = lambda layer, which: mods[layer, b:b + 1, which]

    cos2, sin2 = _rope_tables(s)
    x_lat = x.reshape(b * s, d)
    x_ctx = ctx.reshape(b * l, d)
    tile2 = lambda g: jnp.tile(g, 2).reshape(1, LANES)

    def peer_weights(layer):
        n_heads = peer_sub_keys.shape[1]
        keys = peer_sub_keys[layer].reshape(2 * n_heads, PEER_KEYS, -1).astype(BF16)
        vt = peer_v[layer].reshape(-1, PEER_EXPERT_TILE, d).transpose(0, 2, 1).astype(BF16)
        return peer_w_q[layer].astype(BF16), keys, peer_u, layer, vt

    n_na = na_rpb.shape[1]
    n_ret = ret_log_decay.shape[2]
    na_pairs, ret_pairs = n_na // 2, n_ret // 2
    w_in = ab_w_in[0].astype(BF16)
    w_out = ab_w_out[0].astype(BF16)
    wa = n_na * HEAD_DIM
    p_lat = _modmm(x_lat, norm1_g[0], lat(0, 0), lat(0, 1), w_in, s).reshape(b, s, -1)
    p_ctx = _modmm(x_ctx, norm1_g[0], cx(0, 0), cx(0, 1), w_in, b * l).reshape(b, l, -1)

    qn, kn = tile2(na_q_norm[0]), tile2(na_k_norm[0])
    oa_lat = _na_attention(p_lat, p_ctx, _na_bias(na_rpb[0], s // GRID_W), qn, kn, na_pairs)
    oa_ctx = _ctx_attention(p_ctx, qn, kn, na_pairs)

    lg = _pair_lanes(ret_log_decay[0]).transpose(1, 0, 2)
    ret_col0 = 3 * wa // LANES
    zeros_state = jnp.zeros((b, ret_pairs, 4, HEAD_DIM, HEAD_DIM), F32)
    ones_tab, zeros_tab = jnp.ones((l, LANES), F32), jnp.zeros((l, LANES), F32)
    ob_ctx, st_ctx = _retention(p_ctx, ones_tab, zeros_tab, lg, zeros_state, ret_col0, ret_pairs, False)
    ob_lat, _ = _retention(p_lat, cos2, sin2, lg, st_ctx, ret_col0, ret_pairs, True)

    w_list = [w_out[:wa], w_out[wa:]]
    x_lat, h_lat = _outproj([oa_lat.reshape(b * s, -1), ob_lat.reshape(b * s, -1)], w_list, x_lat,
                            lat(0, 2), norm2_g[0], lat(0, 3), lat(0, 4), s)
    x_ctx, h_ctx = _outproj([oa_ctx.reshape(b * l, -1), ob_ctx.reshape(b * l, -1)], w_list, x_ctx,
                            cx(0, 2), norm2_g[0], cx(0, 3), cx(0, 4), b * l)
    pw = peer_weights(0)
    x_lat = _peer(h_lat, x_lat, lat(0, 5), s, *pw)
    x_ctx = _peer(h_ctx, x_ctx, cx(0, 5), b * l, *pw)

    n_q = swa_sink.shape[1]
    n_kv = (swa_w_in.shape[2] // HEAD_DIM - n_q) // 2
    w_in = swa_w_in[0].astype(BF16)
    p_lat = _modmm(x_lat, norm1_g[1], lat(1, 0), lat(1, 1), w_in, s).reshape(b, s, -1)
    p_ctx = _modmm(x_ctx, norm1_g[1], cx(1, 0), cx(1, 1), w_in, b * l).reshape(b, l, -1)
    sink_rows = jnp.broadcast_to(swa_sink[0][:, None], (n_q, LANES))
    o_lat = _swa_attention(p_lat, p_ctx, cos2, sin2, tile2(swa_q_norm[0]), tile2(swa_k_norm[0]),
                           sink_rows, n_q, n_kv)
    x_lat, h_lat = _outproj([o_lat.reshape(b * s, -1)], [swa_w_out[0].astype(BF16)], x_lat,
                            lat(1, 2), norm2_g[1], lat(1, 3), lat(1, 4), s)
    x_lat = _peer(h_lat, x_lat, lat(1, 5), s, *peer_weights(1))
    return x_lat.reshape(b, s, d)
```
